```python
import math
import numpy as np
import jax
import jax.numpy as jnp
from jax import lax

D_MODEL = 2048
BATCH = 4
SEQ = 4096
DEPTH = 2

CTX_LEN = 256
GRID_W = 64

MIX = D_MODEL
N_GROUPS = 4
GROUP = MIX // N_GROUPS
CHUNK = 64
EPS = 1e-6

GLA_HEADS = 4
GLA_DV = GROUP // GLA_HEADS
GLA_DK = GLA_DV // 2
GLA_LR = 16
GLA_TAU = 16.0

GDN_HEADS = 4
GDN_DK = GROUP // GDN_HEADS
GDN_DV = GDN_DK
CONV_W = 3

RW_HEAD = 64
RW_HEADS = GROUP // RW_HEAD
RW_DECAY_LR = 32
RW_A_LR = 32
RW_GATE_LR = 96
RW_DECAY_SCALE = math.exp(-0.5)
RW_GN_EPS = 64e-5

FFN = -(-(8 * D_MODEL) // (3 * 256)) * 256

PROJ_WIDTHS = (
    GLA_HEADS * GLA_DK, GLA_HEADS * GLA_DK, GROUP, GROUP, GLA_LR,
    3 * GROUP, GROUP, 2 * GDN_HEADS, 2 * GDN_HEADS,
    GROUP, GROUP, GROUP,
    3 * GROUP, RW_DECAY_LR + RW_A_LR, RW_GATE_LR,
)
PROJ = sum(PROJ_WIDTHS)

kernel_name = 'hybrid_parallel_groups_flow_block'


def _rmsnorm(x, g):
    xf = x.astype(jnp.float32)
    y = xf * lax.rsqrt(jnp.mean(xf * xf, axis=-1, keepdims=True) + EPS)
    return (y * g.astype(jnp.float32)).astype(x.dtype)


def _rms_heads(o, g):
    return o * lax.rsqrt(jnp.mean(o * o, axis=-1, keepdims=True) + EPS) * g


def _group_norm(y, w, bias):
    mu = jnp.mean(y, axis=-1, keepdims=True)
    var = jnp.mean(jnp.square(y - mu), axis=-1, keepdims=True)
    yn = (y - mu) * lax.rsqrt(var + RW_GN_EPS)
    return yn.reshape(y.shape[:-2] + (-1,)) * w + bias


def _l2norm(x):
    return x * lax.rsqrt(jnp.sum(x * x, axis=-1, keepdims=True) + EPS)


def _heads(t, n_heads):
    return t.reshape(t.shape[:-1] + (n_heads, t.shape[-1] // n_heads))


def _flip(t):
    return jnp.flip(t, axis=1)


def _to_chunks(t):
    return t.reshape((t.shape[0], t.shape[1] // CHUNK, CHUNK) + t.shape[2:])


def _conv3_rows(x, w, row_len):
    b, t, ch = x.shape
    rows = t // row_len
    xp = jnp.pad(x.reshape(b, rows, row_len, ch), ((0, 0), (0, 0), (1, 1), (0, 0)))
    y = xp[:, :, :-2] * w[0] + xp[:, :, 1:-1] * w[1] + xp[:, :, 2:] * w[2]
    return y.reshape(b, t, ch)


def _token_shift(x, mu):
    prev = jnp.pad(x, ((0, 0), (1, 0), (0, 0)))[:, :-1]
    return x + (prev - x) * mu


def _gla_scan(q, k, v, log_f, s0):
    b, t, h, dv = v.shape
    q, k, v, log_f = (_to_chunks(a) for a in (q, k, v, log_f))
    cum = jnp.cumsum(log_f, axis=2)
    last = cum[:, :, -1:]
    q_dec = q * jnp.exp(cum)
    k_inv = k * jnp.exp(-cum)
    k_end = k * jnp.exp(last - cum)
    lower = jnp.tril(jnp.ones((CHUNK, CHUNK), bool))
    att = jnp.where(lower, jnp.einsum('bnihd,bnjhd->bnhij', q_dec, k_inv), 0.0)
    o_intra = jnp.einsum('bnhij,bnjhv->bnihv', att, v)
    d_state = jnp.einsum('bnjhd,bnjhv->nbhdv', k_end, v)
    chunk_decay = jnp.moveaxis(jnp.exp(last[:, :, 0]), 1, 0)

    def step(s, inp):
        dec, ds = inp
        return dec[..., None] * s + ds, s

    s_fin, s_start = lax.scan(step, s0, (chunk_decay, d_state))
    o_inter = jnp.einsum('bnihd,nbhdv->bnihv', q_dec, s_start)
    return (o_intra + o_inter).reshape(b, t, h, dv), s_fin


def _gdn_scan(q, k, v, log_a, beta, s0):
    b, t, h, dv = v.shape
    dk = q.shape[-1]
    hm = lambda a: jnp.moveaxis(_to_chunks(a), 3, 2)
    q, k, v, log_a, beta = (hm(a) for a in (q, k, v, log_a, beta))
    q = q * dk ** -0.5
    cum = jnp.cumsum(log_a, axis=-1)
    lower = jnp.tril(jnp.ones((CHUNK, CHUNK), bool))
    strict = jnp.tril(jnp.ones((CHUNK, CHUNK), bool), -1)
    decay = jnp.exp(jnp.where(lower, cum[..., :, None] - cum[..., None, :], -jnp.inf))
    kk = jnp.einsum('bnhid,bnhjd->bnhij', k, k)
    lmat = jnp.where(strict, beta[..., None] * kk * decay, 0.0) + jnp.eye(CHUNK, dtype=kk.dtype)
    rhs = jnp.concatenate([v * beta[..., None], k * (beta * jnp.exp(cum))[..., None]], axis=-1)
    sol = lax.linalg.triangular_solve(lmat, rhs, left_side=True, lower=True, unit_diagonal=True)
    u, w = sol[..., :dv], sol[..., dv:]
    a_qk = jnp.einsum('bnhid,bnhjd->bnhij', q, k) * decay
    k_end = k * jnp.exp(cum[..., -1:] - cum)[..., None]
    chunk_decay = jnp.exp(cum[..., -1])
    cm = lambda a: jnp.moveaxis(a, 1, 0)

    def step(s, inp):
        u_i, w_i, ke_i, dec_i = inp
        v_new = u_i - jnp.einsum('bhcd,bhdv->bhcv', w_i, s)
        s_next = dec_i[..., None, None] * s + jnp.einsum('bhcd,bhcv->bhdv', ke_i, v_new)
        return s_next, (s, v_new)

    s_fin, (s_start, v_new) = lax.scan(step, s0, (cm(u), cm(w), cm(k_end), cm(chunk_decay)))
    o = (jnp.einsum('bnhcd,nbhdv->bnhcv', q * jnp.exp(cum)[..., None], s_start)
         + jnp.einsum('bnhij,nbhjv->bnhiv', a_qk, v_new))
    return jnp.moveaxis(o, 2, 3).reshape(b, t, h, dv), s_fin


def _rwkv7_scan(r, w, k, v, kk, a, s0):
    tm = lambda t: jnp.moveaxis(t, 1, 0)

    def step(s, inp):
        r_t, w_t, k_t, v_t, kk_t, a_t = inp
        sa = jnp.einsum('bhvk,bhk->bhv', s, kk_t)
        s = (s * w_t[:, :, None, :] - sa[..., None] * (kk_t * a_t)[:, :, None, :]
             + v_t[..., None] * k_t[:, :, None, :])
        return s, jnp.einsum('bhvk,bhk->bhv', s, r_t)

    s_fin, y = lax.scan(step, s0, tuple(tm(t) for t in (r, w, k, v, kk, a)))
    return tm(y), s_fin


def _rwkv7_direction(p_rkv, p_wa, s0, mu_rkv, mu_wa, w0, w2, a0, a2, k_k, k_a, r_k):
    x_rkv = _token_shift(p_rkv, mu_rkv)
    x_wa = _token_shift(p_wa, mu_wa)
    r, k, v = jnp.split(x_rkv, 3, axis=-1)
    w_lo, a_lo = jnp.split(x_wa, [RW_DECAY_LR], axis=-1)
    decay = jnp.exp(-RW_DECAY_SCALE * jax.nn.sigmoid(w0 + jnp.tanh(w_lo) @ w2))
    a = jax.nn.sigmoid(a0 + a_lo @ a2)
    kk = _l2norm(_heads(k * k_k, RW_HEADS))
    k = k * (1.0 + (a - 1.0) * k_a)
    r, k, v, a, decay = (_heads(t, RW_HEADS) for t in (r, k, v, a, decay))
    y, s_fin = _rwkv7_scan(r, decay, k, v, kk, a, s0)
    bonus = jnp.sum(r * k * r_k, axis=-1, keepdims=True) * v
    return y, bonus, s_fin


def _mix(n, row_len, init, lp):
    b, t, _ = n.shape
    offs = tuple(int(o) for o in np.cumsum(PROJ_WIDTHS)[:-1])
    (g_q, g_k, g_v, g_r, g_lo, d_qkv, d_z, d_a, d_b, c_b, c_c, c_h, r_rkv, r_wa, r_g) = jnp.split(
        (n @ lp['w_in']).astype(jnp.float32), offs, axis=-1)

    q = _heads(g_q, GLA_HEADS) * GLA_DK ** -0.5
    k = _heads(g_k, GLA_HEADS)
    v = _heads(g_v, GLA_HEADS)
    log_f = [_heads(jax.nn.log_sigmoid(g_lo @ lp['gla_a_up'][i] + lp['gla_a_b'][i]) / GLA_TAU, GLA_HEADS)
             for i in range(2)]
    o_f, s_gla_f = _gla_scan(q, k, v, log_f[0], init[0])
    o_b, s_gla_b = _gla_scan(_flip(q), _flip(k), _flip(v), _flip(log_f[1]), init[1])
    y_gla = _rms_heads(o_f + _flip(o_b), lp['gla_norm_g']) * jax.nn.silu(_heads(g_r, GLA_HEADS))

    qkv = jax.nn.silu(_conv3_rows(d_qkv, lp['gdn_conv'], row_len))
    q, k, v = (_heads(a, GDN_HEADS) for a in jnp.split(qkv, 3, axis=-1))
    q, k = _l2norm(q), _l2norm(k)
    log_a = -jnp.exp(lp['gdn_a_log']) * jax.nn.softplus(d_a.reshape(b, t, 2, GDN_HEADS) + lp['gdn_dt_bias'])
    beta = jax.nn.sigmoid(d_b.reshape(b, t, 2, GDN_HEADS))
    o_f, s_gdn_f = _gdn_scan(q, k, v, log_a[:, :, 0], beta[:, :, 0], init[2])
    o_b, s_gdn_b = _gdn_scan(_flip(q), _flip(k), _flip(v), _flip(log_a[:, :, 1]), _flip(beta[:, :, 1]), init[3])
    y_gdn = _rms_heads(o_f + _flip(o_b), lp['gdn_norm_g']) * jax.nn.silu(_heads(d_z, GDN_HEADS))

    y_sc = c_b * _conv3_rows(c_c * c_h, lp['sc_conv'], row_len)

    def dir_args(i):
        return (lp['rw_mu_rkv'][i], lp['rw_mu_wa'][i], lp['rw_w0'][i], lp['rw_w2'][i],
                lp['rw_a0'][i], lp['rw_a2'][i], lp['rw_kk'], lp['rw_ka'], lp['rw_rk'])
    y_f, bo_f, s_rw_f = _rwkv7_direction(r_rkv, r_wa, init[4], *dir_args(0))
    y_b, bo_b, s_rw_b = _rwkv7_direction(_flip(r_rkv), _flip(r_wa), init[5], *dir_args(1))
    y_rw = (_group_norm(y_f + _flip(y_b), lp['rw_gn_w'], lp['rw_gn_b'])
            + (bo_f + _flip(bo_b)).reshape(b, t, GROUP))
    y_rw = y_rw * (jax.nn.sigmoid(r_g) @ lp['rw_g2'])

    y = jnp.concatenate([y_gla.reshape(b, t, GROUP), y_gdn.reshape(b, t, GROUP), y_sc, y_rw],
                        axis=-1).astype(n.dtype)
    return y @ lp['w_out'], (s_gla_f, s_gla_b, s_gdn_f, s_gdn_b, s_rw_f, s_rw_b)


def _modulation(cond, w, bias):
    m = jax.nn.silu(cond) @ w + bias
    return [a[:, None, :] for a in jnp.split(m, 6, axis=-1)]


def _modnorm(h, g, shift, scale):
    return _rmsnorm(h, g) * (1.0 + scale) + shift


def _swiglu(h, w_gu, w_down):
    gate, up = jnp.split(h @ w_gu, 2, axis=-1)
    return (jax.nn.silu(gate) * up) @ w_down


def _zero_states(b):
    z = lambda h, d1, d2: jnp.zeros((b, h, d1, d2), jnp.float32)
    return (z(GLA_HEADS, GLA_DK, GLA_DV), z(GLA_HEADS, GLA_DK, GLA_DV),
            z(GDN_HEADS, GDN_DK, GDN_DV), z(GDN_HEADS, GDN_DK, GDN_DV),
            z(RW_HEADS, RW_HEAD, RW_HEAD), z(RW_HEADS, RW_HEAD, RW_HEAD))


def setup_inputs(seed: int = 0) -> dict:
    key = jax.random.key(seed)
    keys = iter(jax.random.split(key, 48))

    def nrm(shape, std):
        return std * jax.random.normal(next(keys), shape, jnp.float32)

    def uni(shape, lo, hi):
        return jax.random.uniform(next(keys), shape, jnp.float32, lo, hi)

    L, D = DEPTH, D_MODEL
    dt = jnp.exp(uni((L, 2, GDN_HEADS), math.log(1e-3), math.log(1e-1)))
    return {
        'x': nrm((BATCH, SEQ, D), 1.0),
        'c': nrm((BATCH, D), 1.0),
        'ctx': nrm((BATCH, CTX_LEN, D), 1.0),
        'c_ctx': nrm((D,), 1.0),
        'ada_w': nrm((L, D, 6 * D), 0.5 * D ** -0.5),
        'ada_b': nrm((L, 6 * D), 0.02),
        'norm1_g': 1.0 + nrm((L, D), 0.02),
        'norm2_g': 1.0 + nrm((L, D), 0.02),
        'w_in': nrm((L, D, PROJ), D ** -0.5),
        'w_out': nrm((L, MIX, D), MIX ** -0.5),
        'gla_a_up': nrm((L, 2, GLA_LR, GLA_HEADS * GLA_DK), GLA_LR ** -0.5),
        'gla_a_b': nrm((L, 2, GLA_HEADS * GLA_DK), 0.5),
        'gla_norm_g': 1.0 + nrm((L, GLA_DV), 0.02),
        'gdn_conv': nrm((L, CONV_W, 3 * GROUP), 0.5),
        'gdn_a_log': jnp.log(uni((L, 2, GDN_HEADS), 1.0, 16.0)),
        'gdn_dt_bias': dt + jnp.log(-jnp.expm1(-dt)),
        'gdn_norm_g': 1.0 + nrm((L, GDN_DV), 0.02),
        'sc_conv': nrm((L, CONV_W, GROUP), 0.5),
        'rw_mu_rkv': uni((L, 2, 3 * GROUP), 0.0, 1.0),
        'rw_mu_wa': uni((L, 2, RW_DECAY_LR + RW_A_LR), 0.0, 1.0),
        'rw_w0': uni((L, 2, GROUP), -2.0, 2.0),
        'rw_w2': nrm((L, 2, RW_DECAY_LR, GROUP), RW_DECAY_LR ** -0.5),
        'rw_a0': nrm((L, 2, GROUP), 0.5),
        'rw_a2': nrm((L, 2, RW_A_LR, GROUP), RW_A_LR ** -0.5),
        'rw_g2': nrm((L, RW_GATE_LR, GROUP), RW_GATE_LR ** -0.5),
        'rw_kk': 0.85 + nrm((L, GROUP), 0.02),
        'rw_ka': 1.0 + nrm((L, GROUP), 0.02),
        'rw_rk': nrm((L, RW_HEADS, RW_HEAD), 0.1),
        'rw_gn_w': 1.0 + nrm((L, GROUP), 0.02),
        'rw_gn_b': nrm((L, GROUP), 0.02),
        'ffn_w_gu': nrm((L, D, 2 * FFN), D ** -0.5),
        'ffn_w_down': nrm((L, FFN, D), FFN ** -0.5),
        'final_g': 1.0 + nrm((D,), 0.02),
    }


def reference(x, c, ctx, c_ctx, ada_w, ada_b, norm1_g, norm2_g, w_in, w_out,
              gla_a_up, gla_a_b, gla_norm_g, gdn_conv, gdn_a_log, gdn_dt_bias, gdn_norm_g,
              sc_conv, rw_mu_rkv, rw_mu_wa, rw_w0, rw_w2, rw_a0, rw_a2, rw_g2, rw_kk, rw_ka,
              rw_rk, rw_gn_w, rw_gn_b, ffn_w_gu, ffn_w_down, final_g):
    h_x, h_c = x, ctx
    zero = _zero_states(x.shape[0])
    for l in range(DEPTH):
        lp = dict(w_in=w_in[l], w_out=w_out[l], gla_a_up=gla_a_up[l], gla_a_b=gla_a_b[l],
                  gla_norm_g=gla_norm_g[l], gdn_conv=gdn_conv[l], gdn_a_log=gdn_a_log[l],
                  gdn_dt_bias=gdn_dt_bias[l], gdn_norm_g=gdn_norm_g[l], sc_conv=sc_conv[l],
                  rw_mu_rkv=rw_mu_rkv[l], rw_mu_wa=rw_mu_wa[l], rw_w0=rw_w0[l], rw_w2=rw_w2[l],
                  rw_a0=rw_a0[l], rw_a2=rw_a2[l], rw_g2=rw_g2[l], rw_kk=rw_kk[l], rw_ka=rw_ka[l],
                  rw_rk=rw_rk[l], rw_gn_w=rw_gn_w[l], rw_gn_b=rw_gn_b[l])
        m_x = _modulation(c, ada_w[l], ada_b[l])
        m_c = _modulation(c_ctx[None, :], ada_w[l], ada_b[l])
        o_c, ctx_states = _mix(_modnorm(h_c, norm1_g[l], m_c[0], m_c[1]), h_c.shape[1], zero, lp)
        o_x, _ = _mix(_modnorm(h_x, norm1_g[l], m_x[0], m_x[1]), GRID_W, ctx_states, lp)
        h_x = h_x + m_x[2] * o_x
        h_x = h_x + m_x[5] * _swiglu(_modnorm(h_x, norm2_g[l], m_x[3], m_x[4]), ffn_w_gu[l], ffn_w_down[l])
        if l < DEPTH - 1:
            h_c = h_c + m_c[2] * o_c
            h_c = h_c + m_c[5] * _swiglu(_modnorm(h_c, norm2_g[l], m_c[3], m_c[4]), ffn_w_gu[l], ffn_w_down[l])
    return _rmsnorm(h_x, final_g)
```

```python
import functools
import math

import numpy as np
import jax
import jax.numpy as jnp
from jax import lax
from jax.experimental import pallas as pl
from jax.experimental.pallas import tpu as pltpu

F32 = jnp.float32
BF16 = jnp.bfloat16

D_MODEL = 2048
GROUP = D_MODEL // 4
CHUNK = 64
EPS = 1e-6
GRID_W = 64

GLA_HEADS = 4
GLA_DK = 64
GLA_DV = 128
GLA_LR = 16
GLA_TAU = 16.0
GDN_HEADS = 4
GDN_DK = 128
GDN_DV = 128
RW_HEAD = 64
RW_HEADS = GROUP // RW_HEAD
RW_DECAY_LR = 32
RW_A_LR = 32
RW_GATE_LR = 96
RW_DECAY_SCALE = math.exp(-0.5)
RW_GN_EPS = 64e-5
FFN = -(-(8 * D_MODEL) // (3 * 256)) * 256

SUBLANES = 8
LANES = 128
VMEM_LIMIT = 48 * 1024 * 1024

_SEGS = (
    ('g_q', 256), ('g_k', 256), ('g_v', 512), ('g_r', 512), ('g_lo', GLA_LR),
    ('d_qkv', 1536), ('d_z', 512), ('d_ab', 16),
    ('c_b', 512), ('c_c', 512), ('c_h', 512),
    ('r_rkv', 1536), ('r_wa', 64), ('r_g', RW_GATE_LR),
)
_SRC_WIDTHS = (256, 256, 512, 512, 16, 1536, 512, 16, 512, 512, 512, 1536, 64, 96)


def _round_up(n, m):
    return -(-n // m) * m


def _seg_offsets():
    offs, o = {}, 0
    for name, wd in _SEGS:
        offs[name] = (o, wd)
        o += _round_up(wd, LANES)
    return offs, o


SEG, PROJ_PAD = _seg_offsets()


def _pad_w_in(w):
    parts, o = [], 0
    for (name, wd), sw in zip(_SEGS, _SRC_WIDTHS):
        assert wd == sw
        seg = w[:, o:o + wd]
        pad = _round_up(wd, LANES) - wd
        if pad:
            seg = jnp.pad(seg, ((0, 0), (0, pad)))
        parts.append(seg)
        o += wd
    assert o == w.shape[1]
    return jnp.concatenate(parts, axis=1).astype(BF16)


def _mod_kernel(c_ref, w_ref, b_ref, o_ref):
    x = c_ref[...]
    x = (x * jax.nn.sigmoid(x)).astype(BF16)
    o_ref[0] = jnp.dot(x, w_ref[0].astype(BF16), preferred_element_type=F32) + b_ref[0]


def _modulation(cond, ada_w, ada_b):
    nl, d, n = ada_w.shape
    tn = 1024
    return pl.pallas_call(
        _mod_kernel,
        grid=(nl, n // tn),
        in_specs=[pl.BlockSpec((SUBLANES, d), lambda l, j: (0, 0)),
                  pl.BlockSpec((1, d, tn), lambda l, j: (l, 0, j)),
                  pl.BlockSpec((1, 1, tn), lambda l, j: (l, 0, j))],
        out_specs=pl.BlockSpec((1, SUBLANES, tn), lambda l, j: (l, 0, j)),
        out_shape=jax.ShapeDtypeStruct((nl, SUBLANES, n), F32),
        compiler_params=pltpu.CompilerParams(
            dimension_semantics=("arbitrary", "arbitrary"), vmem_limit_bytes=VMEM_LIMIT),
        name="modulation",
    )(cond, ada_w, ada_b.reshape(nl, 1, n))


def _modnorm_rows(h_ref, g_ref, sc_ref, sh_ref):
    x = h_ref[0]
    y = x * lax.rsqrt(jnp.mean(x * x, axis=-1, keepdims=True) + EPS) * g_ref[...]
    return (y * (1.0 + sc_ref[0]) + sh_ref[0]).astype(BF16)


def _norm_proj_kernel(h_ref, g_ref, sc_ref, sh_ref, w_ref, o_ref, a_scr):
    @pl.when(pl.program_id(2) == 0)
    def _():
        a_scr[...] = _modnorm_rows(h_ref, g_ref, sc_ref, sh_ref)

    o_ref[0] = jnp.dot(a_scr[...], w_ref[...], preferred_element_type=F32).astype(o_ref.dtype)


def _norm_proj(h, g, scale, shift, w, tm, tn):
    b, t, d = h.shape
    n = w.shape[1]
    return pl.pallas_call(
        _norm_proj_kernel,
        grid=(b, t // tm, n // tn),
        in_specs=[pl.BlockSpec((1, tm, d), lambda i, m, j: (i, m, 0)),
                  pl.BlockSpec((1, d), lambda i, m, j: (0, 0)),
                  pl.BlockSpec((1, 1, d), lambda i, m, j: (i, 0, 0)),
                  pl.BlockSpec((1, 1, d), lambda i, m, j: (i, 0, 0)),
                  pl.BlockSpec((d, tn), lambda i, m, j: (0, j))],
        out_specs=pl.BlockSpec((1, tm, tn), lambda i, m, j: (i, m, j)),
        out_shape=jax.ShapeDtypeStruct((b, t, n), F32),
        scratch_shapes=[pltpu.VMEM((tm, d), BF16)],
        compiler_params=pltpu.CompilerParams(
            dimension_semantics=("arbitrary", "arbitrary", "arbitrary"), vmem_limit_bytes=VMEM_LIMIT),
        name="norm_proj",
    )(h, g.reshape(1, d), scale, shift, w)


def _norm_swiglu_kernel(h_ref, g_ref, sc_ref, sh_ref, wg_ref, wu_ref, o_ref, a_scr):
    @pl.when(pl.program_id(2) == 0)
    def _():
        a_scr[...] = _modnorm_rows(h_ref, g_ref, sc_ref, sh_ref)

    a = a_scr[...]
    gate = jnp.dot(a, wg_ref[...], preferred_element_type=F32)
    up = jnp.dot(a, wu_ref[...], preferred_element_type=F32)
    o_ref[0] = (gate * jax.nn.sigmoid(gate) * up).astype(o_ref.dtype)


def _norm_swiglu(h, g, scale, shift, w_gu, tm, tn):
    b, t, d = h.shape
    f = w_gu.shape[1] // 2
    nj = f // tn
    return pl.pallas_call(
        _norm_swiglu_kernel,
        grid=(b, t // tm, nj),
        in_specs=[pl.BlockSpec((1, tm, d), lambda i, m, j: (i, m, 0)),
                  pl.BlockSpec((1, d), lambda i, m, j: (0, 0)),
                  pl.BlockSpec((1, 1, d), lambda i, m, j: (i, 0, 0)),
                  pl.BlockSpec((1, 1, d), lambda i, m, j: (i, 0, 0)),
                  pl.BlockSpec((d, tn), lambda i, m, j: (0, j)),
                  pl.BlockSpec((d, tn), lambda i, m, j: (0, j + nj))],
        out_specs=pl.BlockSpec((1, tm, tn), lambda i, m, j: (i, m, j)),
        out_shape=jax.ShapeDtypeStruct((b, t, f), BF16),
        scratch_shapes=[pltpu.VMEM((tm, d), BF16)],
        compiler_params=pltpu.CompilerParams(
            dimension_semantics=("arbitrary", "arbitrary", "arbitrary"), vmem_limit_bytes=VMEM_LIMIT),
        name="norm_swiglu",
    )(h, g.reshape(1, d), scale, shift, w_gu, w_gu)


def _proj_residual_kernel(a_ref, w_ref, h_ref, gate_ref, o_ref):
    o_ref[0] = h_ref[0] + gate_ref[0] * jnp.dot(a_ref[0], w_ref[...], preferred_element_type=F32)


def _proj_residual(a, w, h, gate, tm, tn):
    b, t, k = a.shape
    d = w.shape[1]
    return pl.pallas_call(
        _proj_residual_kernel,
        grid=(b, t // tm, d // tn),
        in_specs=[pl.BlockSpec((1, tm, k), lambda i, m, j: (i, m, 0)),
                  pl.BlockSpec((k, tn), lambda i, m, j: (0, j)),
                  pl.BlockSpec((1, tm, tn), lambda i, m, j: (i, m, j)),
                  pl.BlockSpec((1, 1, tn), lambda i, m, j: (i, 0, j))],
        out_specs=pl.BlockSpec((1, tm, tn), lambda i, m, j: (i, m, j)),
        out_shape=jax.ShapeDtypeStruct((b, t, d), F32),
        compiler_params=pltpu.CompilerParams(
            dimension_semantics=("arbitrary", "arbitrary", "arbitrary"), vmem_limit_bytes=VMEM_LIMIT),
        name="proj_residual",
    )(a, w, h, gate)


def _final_norm_kernel(h_ref, g_ref, o_ref):
    x = h_ref[0]
    o_ref[0] = x * lax.rsqrt(jnp.mean(x * x, axis=-1, keepdims=True) + EPS) * g_ref[...]


def _final_norm(h, g, tm):
    b, t, d = h.shape
    return pl.pallas_call(
        _final_norm_kernel,
        grid=(b, t // tm),
        in_specs=[pl.BlockSpec((1, tm, d), lambda i, m: (i, m, 0)),
                  pl.BlockSpec((1, d), lambda i, m: (0, 0))],
        out_specs=pl.BlockSpec((1, tm, d), lambda i, m: (i, m, 0)),
        out_shape=jax.ShapeDtypeStruct((b, t, d), F32),
        compiler_params=pltpu.CompilerParams(
            dimension_semantics=("arbitrary", "arbitrary"), vmem_limit_bytes=VMEM_LIMIT),
        name="final_norm",
    )(h, g.reshape(1, d))


RW_VM = RW_HEAD // (2 * SUBLANES)
RW_STEP_BLOCK = 16
OP_W, OP_B, OP_K, OP_R, OP_KK = range(5)


def _rwkv_kernel(ops_ref, v_ref, y_ref, s_ref):
    tb = ops_ref.shape[0]

    @pl.when(pl.program_id(0) == 0)
    def _():
        s_ref[...] = jnp.zeros_like(s_ref)

    zero = jnp.zeros((SUBLANES, LANES), F32)
    sa0 = [zero] * RW_VM
    for k in range(RW_HEAD):
        kk = ops_ref[0, OP_KK, k:k + 1, :]
        for m in range(RW_VM):
            sa0[m] = sa0[m] + s_ref[k, m] * kk

    def step(t, sa):
        tn = jnp.minimum(t + 1, tb - 1)
        vt = [v_ref[t, m] for m in range(RW_VM)]
        acc_y = [zero] * RW_VM
        acc_s = [zero] * RW_VM
        for k in range(RW_HEAD):
            w = ops_ref[t, OP_W, k:k + 1, :]
            b = ops_ref[t, OP_B, k:k + 1, :]
            kt = ops_ref[t, OP_K, k:k + 1, :]
            r = ops_ref[t, OP_R, k:k + 1, :]
            kkn = ops_ref[tn, OP_KK, k:k + 1, :]
            for m in range(RW_VM):
                s = s_ref[k, m] * w - sa[m] * b + vt[m] * kt
                s_ref[k, m] = s
                acc_y[m] = acc_y[m] + s * r
                acc_s[m] = acc_s[m] + s * kkn
        for m in range(RW_VM):
            y_ref[t, m] = acc_y[m]
        return tuple(acc_s)

    lax.fori_loop(0, tb, step, tuple(sa0))


def _rwkv_scan(ops, vt):
    t = ops.shape[0]
    tb = RW_STEP_BLOCK
    return pl.pallas_call(
        _rwkv_kernel,
        grid=(t // tb,),
        in_specs=[pl.BlockSpec((tb, 5, RW_HEAD, LANES), lambda i: (i, 0, 0, 0)),
                  pl.BlockSpec((tb, RW_VM, SUBLANES, LANES), lambda i: (i, 0, 0, 0))],
        out_specs=pl.BlockSpec((tb, RW_VM, SUBLANES, LANES), lambda i: (i, 0, 0, 0)),
        out_shape=jax.ShapeDtypeStruct((t, RW_VM, SUBLANES, LANES), F32),
        scratch_shapes=[pltpu.VMEM((RW_HEAD, RW_VM, SUBLANES, LANES), F32)],
        compiler_params=pltpu.CompilerParams(
            dimension_semantics=("arbitrary",), vmem_limit_bytes=VMEM_LIMIT),
        name="rwkv_scan",
    )(ops, vt)


def _token_shift(x, mu):
    prev = jnp.pad(x, ((0, 0), (1, 0), (0, 0)))[:, :-1]
    return x + (prev - x) * mu


def _heads(t, n_heads):
    return t.reshape(t.shape[:-1] + (n_heads, t.shape[-1] // n_heads))


def _flip(t):
    return jnp.flip(t, axis=1)


def _l2norm(x):
    return x * lax.rsqrt(jnp.sum(x * x, axis=-1, keepdims=True) + EPS)


def _rw_features(p_rkv, p_wa, mu_rkv, mu_wa, w0, w2, a0, a2, k_k, k_a):
    x_rkv = _token_shift(p_rkv, mu_rkv)
    x_wa = _token_shift(p_wa, mu_wa)
    r, k, v = jnp.split(x_rkv, 3, axis=-1)
    w_lo, a_lo = jnp.split(x_wa, [RW_DECAY_LR], axis=-1)
    decay = jnp.exp(-RW_DECAY_SCALE * jax.nn.sigmoid(w0 + jnp.tanh(w_lo) @ w2))
    a = jax.nn.sigmoid(a0 + a_lo @ a2)
    kk = _l2norm(_heads(k * k_k, RW_HEADS))
    k = k * (1.0 + (a - 1.0) * k_a)
    r, k, v, a, decay = (_heads(t, RW_HEADS) for t in (r, k, v, a, decay))
    return r, decay, k, v, kk, kk * a


def _rw_mixer(p_c, p_x, lp):
    b = p_x[0].shape[0]
    nchain = 2 * b * RW_HEADS
    assert 2 * nchain == LANES, "chain-on-lanes layout needs 2 * batch * heads == 64"
    feats = []
    for i in range(2):
        per_seq = []
        for (rkv, wa, _) in (p_c, p_x):
            if i == 1:
                rkv, wa = _flip(rkv), _flip(wa)
            per_seq.append(_rw_features(rkv, wa, lp['rw_mu_rkv'][i], lp['rw_mu_wa'][i], lp['rw_w0'][i],
                                        lp['rw_w2'][i], lp['rw_a0'][i], lp['rw_a2'][i], lp['rw_kk'], lp['rw_ka']))
        feats.append(per_seq)

    def chain_major(idx):
        per_dir = [jnp.concatenate([feats[i][0][idx], feats[i][1][idx]], axis=1) for i in range(2)]
        x = jnp.stack(per_dir, axis=0)
        return jnp.transpose(x, (2, 4, 0, 1, 3)).reshape(x.shape[2], RW_HEAD, nchain)

    r_i, w_i, k_i, v_i, kk_i, b_i = range(6)
    ops = jnp.stack([jnp.repeat(chain_major(j), 2, axis=-1) for j in (w_i, b_i, k_i, r_i, kk_i)], axis=1)
    v_cm = chain_major(v_i)
    tt = v_cm.shape[0]
    vt = v_cm.reshape(tt, RW_VM, SUBLANES, 2, nchain)
    vt = jnp.transpose(vt, (0, 1, 2, 4, 3)).reshape(tt, RW_VM, SUBLANES, LANES)
    y = _rwkv_scan(ops, vt)
    y = y.reshape(tt, RW_VM, SUBLANES, nchain, 2)
    y = jnp.transpose(y, (3, 0, 1, 2, 4)).reshape(2, b, RW_HEADS, tt, RW_HEAD)
    y = jnp.transpose(y, (0, 1, 3, 2, 4))
    tc = p_c[0].shape[1]
    outs = []
    for s, (sl, p) in enumerate(((slice(0, tc), p_c), (slice(tc, tt), p_x))):
        y_f, y_b = y[0][:, sl], y[1][:, sl]
        bo = []
        for i in range(2):
            r, _, k, v = feats[i][s][:4]
            bo.append(jnp.sum(r * k * lp['rw_rk'], axis=-1, keepdims=True) * v)
        yy = y_f + _flip(y_b)
        mu = jnp.mean(yy, axis=-1, keepdims=True)
        var = jnp.mean(jnp.square(yy - mu), axis=-1, keepdims=True)
        yn = (yy - mu) * lax.rsqrt(var + RW_GN_EPS)
        bt = yy.shape[:2]
        y_rw = yn.reshape(bt + (GROUP,)) * lp['rw_gn_w'] + lp['rw_gn_b'] + (bo[0] + _flip(bo[1])).reshape(bt + (GROUP,))
        outs.append(y_rw * (jax.nn.sigmoid(p[2]) @ lp['rw_g2']))
    return outs


def _to_chunks(t):
    return t.reshape((t.shape[0], t.shape[1] // CHUNK, CHUNK) + t.shape[2:])


def _rms_heads(o, g):
    return o * lax.rsqrt(jnp.mean(o * o, axis=-1, keepdims=True) + EPS) * g


def _conv3_rows(x, w, row_len):
    b, t, ch = x.shape
    rows = t // row_len
    xp = jnp.pad(x.reshape(b, rows, row_len, ch), ((0, 0), (0, 0), (1, 1), (0, 0)))
    y = xp[:, :, :-2] * w[0] + xp[:, :, 1:-1] * w[1] + xp[:, :, 2:] * w[2]
    return y.reshape(b, t, ch)


def _gla_scan(q, k, v, log_f, s0):
    b, t, h, dv = v.shape
    q, k, v, log_f = (_to_chunks(a) for a in (q, k, v, log_f))
    cum = jnp.cumsum(log_f, axis=2)
    last = cum[:, :, -1:]
    q_dec = q * jnp.exp(cum)
    k_inv = k * jnp.exp(-cum)
    k_end = k * jnp.exp(last - cum)
    lower = jnp.tril(jnp.ones((CHUNK, CHUNK), bool))
    att = jnp.where(lower, jnp.einsum('bnihd,bnjhd->bnhij', q_dec, k_inv), 0.0)
    o_intra = jnp.einsum('bnhij,bnjhv->bnihv', att, v)
    d_state = jnp.einsum('bnjhd,bnjhv->nbhdv', k_end, v)
    chunk_decay = jnp.moveaxis(jnp.exp(last[:, :, 0]), 1, 0)

    def step(s, inp):
        dec, ds = inp
        return dec[..., None] * s + ds, s

    s_fin, s_start = lax.scan(step, s0, (chunk_decay, d_state))
    o_inter = jnp.einsum('bnihd,nbhdv->bnihv', q_dec, s_start)
    return (o_intra + o_inter).reshape(b, t, h, dv), s_fin


def _gdn_scan(q, k, v, log_a, beta, s0):
    b, t, h, dv = v.shape
    dk = q.shape[-1]
    hm = lambda a: jnp.moveaxis(_to_chunks(a), 3, 2)
    q, k, v, log_a, beta = (hm(a) for a in (q, k, v, log_a, beta))
    q = q * dk ** -0.5
    cum = jnp.cumsum(log_a, axis=-1)
    lower = jnp.tril(jnp.ones((CHUNK, CHUNK), bool))
    strict = jnp.tril(jnp.ones((CHUNK, CHUNK), bool), -1)
    decay = jnp.exp(jnp.where(lower, cum[..., :, None] - cum[..., None, :], -jnp.inf))
    kk = jnp.einsum('bnhid,bnhjd->bnhij', k, k)
    lmat = jnp.where(strict, beta[..., None] * kk * decay, 0.0) + jnp.eye(CHUNK, dtype=kk.dtype)
    rhs = jnp.concatenate([v * beta[..., None], k * (beta * jnp.exp(cum))[..., None]], axis=-1)
    sol = lax.linalg.triangular_solve(lmat, rhs, left_side=True, lower=True, unit_diagonal=True)
    u, w = sol[..., :dv], sol[..., dv:]
    a_qk = jnp.einsum('bnhid,bnhjd->bnhij', q, k) * decay
    k_end = k * jnp.exp(cum[..., -1:] - cum)[..., None]
    chunk_decay = jnp.exp(cum[..., -1])
    cm = lambda a: jnp.moveaxis(a, 1, 0)

    def step(s, inp):
        u_i, w_i, ke_i, dec_i = inp
        v_new = u_i - jnp.einsum('bhcd,bhdv->bhcv', w_i, s)
        s_next = dec_i[..., None, None] * s + jnp.einsum('bhcd,bhcv->bhdv', ke_i, v_new)
        return s_next, (s, v_new)

    s_fin, (s_start, v_new) = lax.scan(step, s0, (cm(u), cm(w), cm(k_end), cm(chunk_decay)))
    o = (jnp.einsum('bnhcd,nbhdv->bnhcv', q * jnp.exp(cum)[..., None], s_start)
         + jnp.einsum('bnhij,nbhjv->bnhiv', a_qk, v_new))
    return jnp.moveaxis(o, 2, 3).reshape(b, t, h, dv), s_fin


def _seg(proj, name):
    o, wd = SEG[name]
    return proj[..., o:o + wd]


def _gla_group(proj, init, lp):
    g_q, g_k, g_v, g_r, g_lo = (_seg(proj, n) for n in ('g_q', 'g_k', 'g_v', 'g_r', 'g_lo'))
    q = _heads(g_q, GLA_HEADS) * GLA_DK ** -0.5
    k = _heads(g_k, GLA_HEADS)
    v = _heads(g_v, GLA_HEADS)
    log_f = [_heads(jax.nn.log_sigmoid(g_lo @ lp['gla_a_up'][i] + lp['gla_a_b'][i]) / GLA_TAU, GLA_HEADS)
             for i in range(2)]
    o_f, s_f = _gla_scan(q, k, v, log_f[0], init[0])
    o_b, s_b = _gla_scan(_flip(q), _flip(k), _flip(v), _flip(log_f[1]), init[1])
    y = _rms_heads(o_f + _flip(o_b), lp['gla_norm_g']) * jax.nn.silu(_heads(g_r, GLA_HEADS))
    return y.reshape(y.shape[:2] + (GROUP,)), (s_f, s_b)


def _gdn_group(proj, row_len, init, lp):
    b, t = proj.shape[:2]
    d_qkv, d_z, d_ab = (_seg(proj, n) for n in ('d_qkv', 'd_z', 'd_ab'))
    d_a, d_b = d_ab[..., :2 * GDN_HEADS], d_ab[..., 2 * GDN_HEADS:]
    qkv = jax.nn.silu(_conv3_rows(d_qkv, lp['gdn_conv'], row_len))
    q, k, v = (_heads(a, GDN_HEADS) for a in jnp.split(qkv, 3, axis=-1))
    q, k = _l2norm(q), _l2norm(k)
    log_a = -jnp.exp(lp['gdn_a_log']) * jax.nn.softplus(d_a.reshape(b, t, 2, GDN_HEADS) + lp['gdn_dt_bias'])
    beta = jax.nn.sigmoid(d_b.reshape(b, t, 2, GDN_HEADS))
    o_f, s_f = _gdn_scan(q, k, v, log_a[:, :, 0], beta[:, :, 0], init[0])
    o_b, s_b = _gdn_scan(_flip(q), _flip(k), _flip(v), _flip(log_a[:, :, 1]), _flip(beta[:, :, 1]), init[1])
    y = _rms_heads(o_f + _flip(o_b), lp['gdn_norm_g']) * jax.nn.silu(_heads(d_z, GDN_HEADS))
    return y.reshape(b, t, GROUP), (s_f, s_b)


def _sc_group(proj, row_len, lp):
    c_b, c_c, c_h = (_seg(proj, n) for n in ('c_b', 'c_c', 'c_h'))
    return c_b * _conv3_rows(c_c * c_h, lp['sc_conv'], row_len)


def _mixers(proj_c, proj_x, lp):
    b = proj_x.shape[0]
    z = lambda h, d1, d2: jnp.zeros((b, h, d1, d2), F32)
    gla0 = (z(GLA_HEADS, GLA_DK, GLA_DV),) * 2
    gdn0 = (z(GDN_HEADS, GDN_DK, GDN_DV),) * 2
    tc = proj_c.shape[1]
    y_gla_c, gla_s = _gla_group(proj_c, gla0, lp)
    y_gla_x, _ = _gla_group(proj_x, gla_s, lp)
    y_gdn_c, gdn_s = _gdn_group(proj_c, tc, gdn0, lp)
    y_gdn_x, _ = _gdn_group(proj_x, GRID_W, gdn_s, lp)
    y_sc_c = _sc_group(proj_c, tc, lp)
    y_sc_x = _sc_group(proj_x, GRID_W, lp)
    rw = lambda p: tuple(_seg(p, n) for n in ('r_rkv', 'r_wa', 'r_g'))
    y_rw_c, y_rw_x = _rw_mixer(rw(proj_c), rw(proj_x), lp)
    y_c = jnp.concatenate([y_gla_c, y_gdn_c, y_sc_c, y_rw_c], axis=-1).astype(BF16)
    y_x = jnp.concatenate([y_gla_x, y_gdn_x, y_sc_x, y_rw_x], axis=-1).astype(BF16)
    return y_c, y_x


def kernel(x, c, ctx, c_ctx, ada_w, ada_b, norm1_g, norm2_g, w_in, w_out, gla_a_up, gla_a_b, gla_norm_g, gdn_conv, gdn_a_log, gdn_dt_bias, gdn_norm_g, sc_conv, rw_mu_rkv, rw_mu_wa, rw_w0, rw_w2, rw_a0, rw_a2, rw_g2, rw_kk, rw_ka, rw_rk, rw_gn_w, rw_gn_b, ffn_w_gu, ffn_w_down, final_g):
    nb, t, d = x.shape
    tc = ctx.shape[1]
    depth = w_in.shape[0]
    assert d == D_MODEL and t % 512 == 0 and tc % CHUNK == 0 and nb + 1 <= SUBLANES

    cond = jnp.concatenate([c, c_ctx[None, :], jnp.zeros((SUBLANES - nb - 1, d), F32)], axis=0)
    mod = _modulation(cond, ada_w, ada_b)

    tm_x = 512
    tm_c = tc
    h_x, h_c = x, ctx
    for l in range(depth):
        lp = dict(gla_a_up=gla_a_up[l], gla_a_b=gla_a_b[l],
                  gla_norm_g=gla_norm_g[l], gdn_conv=gdn_conv[l], gdn_a_log=gdn_a_log[l],
                  gdn_dt_bias=gdn_dt_bias[l], gdn_norm_g=gdn_norm_g[l], sc_conv=sc_conv[l],
                  rw_mu_rkv=rw_mu_rkv[l], rw_mu_wa=rw_mu_wa[l], rw_w0=rw_w0[l], rw_w2=rw_w2[l],
                  rw_a0=rw_a0[l], rw_a2=rw_a2[l], rw_g2=rw_g2[l], rw_kk=rw_kk[l], rw_ka=rw_ka[l],
                  rw_rk=rw_rk[l], rw_gn_w=rw_gn_w[l], rw_gn_b=rw_gn_b[l])
        m_x = [mod[l, :nb, i * d:(i + 1) * d][:, None, :] for i in range(6)]
        m_c = [jnp.broadcast_to(mod[l, nb, i * d:(i + 1) * d][None, None, :], (nb, 1, d)) for i in range(6)]
        w_in_l = _pad_w_in(w_in[l])
        w_out_l = w_out[l].astype(BF16)
        w_gu_l = ffn_w_gu[l].astype(BF16)
        w_dn_l = ffn_w_down[l].astype(BF16)

        proj_c = _norm_proj(h_c, norm1_g[l], m_c[1], m_c[0], w_in_l, tm_c, 1024)
        proj_x = _norm_proj(h_x, norm1_g[l], m_x[1], m_x[0], w_in_l, tm_x, 1024)
        y_c, y_x = _mixers(proj_c, proj_x, lp)

        h_x = _proj_residual(y_x, w_out_l, h_x, m_x[2], tm_x, 512)
        a_x = _norm_swiglu(h_x, norm2_g[l], m_x[4], m_x[3], w_gu_l, tm_x, 512)
        h_x = _proj_residual(a_x, w_dn_l, h_x, m_x[5], tm_x, 512)
        if l < depth - 1:
            h_c = _proj_residual(y_c, w_out_l, h_c, m_c[2], tm_c, 512)
            a_c = _norm_swiglu(h_c, norm2_g[l], m_c[4], m_c[3], w_gu_l, tm_c, 512)
            h_c = _proj_residual(a_c, w_dn_l, h_c, m_c[5], tm_c, 512)
    return _final_norm(h_x, final_g, tm_x)
```

```python
import functools
import math

import numpy as np
import jax
import jax.numpy as jnp
from jax import lax
from jax.experimental import pallas as pl
from jax.experimental.pallas import tpu as pltpu

F32 = jnp.float32
BF16 = jnp.bfloat16

D_MODEL = 2048
GROUP = D_MODEL // 4
CHUNK = 64
EPS = 1e-6
GRID_W = 64

GLA_HEADS = 4
GLA_DK = 64
GLA_DV = 128
GLA_LR = 16
GLA_TAU = 16.0
GDN_HEADS = 4
GDN_DK = 128
GDN_DV = 128
RW_HEAD = 64
RW_HEADS = GROUP // RW_HEAD
RW_DECAY_LR = 32
RW_A_LR = 32
RW_GATE_LR = 96
RW_DECAY_SCALE = math.exp(-0.5)
RW_GN_EPS = 64e-5
FFN = -(-(8 * D_MODEL) // (3 * 256)) * 256

SUBLANES = 8
LANES = 128
VMEM_LIMIT = 48 * 1024 * 1024

_SRC_SEGS = (
    ('g_q', 256), ('g_k', 256), ('g_v', 512), ('g_r', 512), ('g_lo', GLA_LR),
    ('d_qkv', 1536), ('d_z', 512), ('d_ab', 16),
    ('c_b', 512), ('c_c', 512), ('c_h', 512),
    ('r_rkv', 1536), ('r_wa', 64), ('r_g', RW_GATE_LR),
)
_DST_ORDER = ('g_q', 'g_k', 'g_v', 'g_r', 'd_qkv', 'r_rkv', 'd_z', 'c_b', 'c_c', 'c_h',
              'g_lo', 'd_ab', 'r_wa', 'r_g')


def _round_up(n, m):
    return -(-n // m) * m


def _seg_layout():
    width = dict(_SRC_SEGS)
    src, o = {}, 0
    for name, wd in _SRC_SEGS:
        src[name] = o
        o += wd
    dst, o = {}, 0
    for name in _DST_ORDER:
        pw = _round_up(width[name], LANES)
        assert o % pw == 0, name
        dst[name] = (o, width[name], pw)
        o += pw
    return src, dst, o


SEG_SRC, SEG, PROJ_PAD = _seg_layout()


def _pad_w_in(w):
    parts = []
    for name in _DST_ORDER:
        _, wd, pw = SEG[name]
        seg = w[:, SEG_SRC[name]:SEG_SRC[name] + wd]
        if pw != wd:
            seg = jnp.pad(seg, ((0, 0), (0, pw - wd)))
        parts.append(seg)
    return jnp.concatenate(parts, axis=1).astype(BF16)


def _col_block(name):
    o, _, pw = SEG[name]
    return o // pw


def _mod_kernel(c_ref, w_ref, b_ref, o_ref):
    x = c_ref[...]
    x = (x * jax.nn.sigmoid(x)).astype(BF16)
    o_ref[0] = jnp.dot(x, w_ref[0].astype(BF16), preferred_element_type=F32) + b_ref[0]


def _modulation(cond, ada_w, ada_b):
    nl, d, n = ada_w.shape
    tn = 1024
    return pl.pallas_call(
        _mod_kernel,
        grid=(nl, n // tn),
        in_specs=[pl.BlockSpec((SUBLANES, d), lambda l, j: (0, 0)),
                  pl.BlockSpec((1, d, tn), lambda l, j: (l, 0, j)),
                  pl.BlockSpec((1, 1, tn), lambda l, j: (l, 0, j))],
        out_specs=pl.BlockSpec((1, SUBLANES, tn), lambda l, j: (l, 0, j)),
        out_shape=jax.ShapeDtypeStruct((nl, SUBLANES, n), F32),
        compiler_params=pltpu.CompilerParams(
            dimension_semantics=("arbitrary", "arbitrary"), vmem_limit_bytes=VMEM_LIMIT),
        name="modulation",
    )(cond, ada_w, ada_b.reshape(nl, 1, n))


def _modnorm_rows(h_ref, g_ref, sc_ref, sh_ref):
    x = h_ref[0]
    y = x * lax.rsqrt(jnp.mean(x * x, axis=-1, keepdims=True) + EPS) * g_ref[...]
    return (y * (1.0 + sc_ref[0]) + sh_ref[0]).astype(BF16)


def _norm_proj_kernel(h_ref, g_ref, sc_ref, sh_ref, w_ref, o_ref, a_scr):
    @pl.when(pl.program_id(2) == 0)
    def _():
        a_scr[...] = _modnorm_rows(h_ref, g_ref, sc_ref, sh_ref)

    o_ref[0] = jnp.dot(a_scr[...], w_ref[...], preferred_element_type=F32).astype(o_ref.dtype)


def _norm_proj(h, g, scale, shift, w, tm, tn):
    b, t, d = h.shape
    n = w.shape[1]
    return pl.pallas_call(
        _norm_proj_kernel,
        grid=(b, t // tm, n // tn),
        in_specs=[pl.BlockSpec((1, tm, d), lambda i, m, j: (i, m, 0)),
                  pl.BlockSpec((1, d), lambda i, m, j: (0, 0)),
                  pl.BlockSpec((1, 1, d), lambda i, m, j: (i, 0, 0)),
                  pl.BlockSpec((1, 1, d), lambda i, m, j: (i, 0, 0)),
                  pl.BlockSpec((d, tn), lambda i, m, j: (0, j))],
        out_specs=pl.BlockSpec((1, tm, tn), lambda i, m, j: (i, m, j)),
        out_shape=jax.ShapeDtypeStruct((b, t, n), F32),
        scratch_shapes=[pltpu.VMEM((tm, d), BF16)],
        compiler_params=pltpu.CompilerParams(
            dimension_semantics=("arbitrary", "arbitrary", "arbitrary"), vmem_limit_bytes=VMEM_LIMIT),
        name="norm_proj",
    )(h, g.reshape(1, d), scale, shift, w)


def _norm_swiglu_kernel(h_ref, g_ref, sc_ref, sh_ref, wg_ref, wu_ref, o_ref, a_scr):
    @pl.when(pl.program_id(2) == 0)
    def _():
        a_scr[...] = _modnorm_rows(h_ref, g_ref, sc_ref, sh_ref)

    a = a_scr[...]
    gate = jnp.dot(a, wg_ref[...], preferred_element_type=F32)
    up = jnp.dot(a, wu_ref[...], preferred_element_type=F32)
    o_ref[0] = (gate * jax.nn.sigmoid(gate) * up).astype(o_ref.dtype)


def _norm_swiglu(h, g, scale, shift, w_gu, tm, tn):
    b, t, d = h.shape
    f = w_gu.shape[1] // 2
    nj = f // tn
    return pl.pallas_call(
        _norm_swiglu_kernel,
        grid=(b, t // tm, nj),
        in_specs=[pl.BlockSpec((1, tm, d), lambda i, m, j: (i, m, 0)),
                  pl.BlockSpec((1, d), lambda i, m, j: (0, 0)),
                  pl.BlockSpec((1, 1, d), lambda i, m, j: (i, 0, 0)),
                  pl.BlockSpec((1, 1, d), lambda i, m, j: (i, 0, 0)),
                  pl.BlockSpec((d, tn), lambda i, m, j: (0, j)),
                  pl.BlockSpec((d, tn), lambda i, m, j: (0, j + nj))],
        out_specs=pl.BlockSpec((1, tm, tn), lambda i, m, j: (i, m, j)),
        out_shape=jax.ShapeDtypeStruct((b, t, f), BF16),
        scratch_shapes=[pltpu.VMEM((tm, d), BF16)],
        compiler_params=pltpu.CompilerParams(
            dimension_semantics=("arbitrary", "arbitrary", "arbitrary"), vmem_limit_bytes=VMEM_LIMIT),
        name="norm_swiglu",
    )(h, g.reshape(1, d), scale, shift, w_gu, w_gu)


def _proj_residual_kernel(a_ref, w_ref, h_ref, gate_ref, o_ref):
    o_ref[0] = h_ref[0] + gate_ref[0] * jnp.dot(a_ref[0], w_ref[...], preferred_element_type=F32)


def _proj_residual(a, w, h, gate, tm, tn):
    b, t, k = a.shape
    d = w.shape[1]
    return pl.pallas_call(
        _proj_residual_kernel,
        grid=(b, t // tm, d // tn),
        in_specs=[pl.BlockSpec((1, tm, k), lambda i, m, j: (i, m, 0)),
                  pl.BlockSpec((k, tn), lambda i, m, j: (0, j)),
                  pl.BlockSpec((1, tm, tn), lambda i, m, j: (i, m, j)),
                  pl.BlockSpec((1, 1, tn), lambda i, m, j: (i, 0, j))],
        out_specs=pl.BlockSpec((1, tm, tn), lambda i, m, j: (i, m, j)),
        out_shape=jax.ShapeDtypeStruct((b, t, d), F32),
        compiler_params=pltpu.CompilerParams(
            dimension_semantics=("arbitrary", "arbitrary", "arbitrary"), vmem_limit_bytes=VMEM_LIMIT),
        name="proj_residual",
    )(a, w, h, gate)


def _final_norm_kernel(h_ref, g_ref, o_ref):
    x = h_ref[0]
    o_ref[0] = x * lax.rsqrt(jnp.mean(x * x, axis=-1, keepdims=True) + EPS) * g_ref[...]


def _final_norm(h, g, tm):
    b, t, d = h.shape
    return pl.pallas_call(
        _final_norm_kernel,
        grid=(b, t // tm),
        in_specs=[pl.BlockSpec((1, tm, d), lambda i, m: (i, m, 0)),
                  pl.BlockSpec((1, d), lambda i, m: (0, 0))],
        out_specs=pl.BlockSpec((1, tm, d), lambda i, m: (i, m, 0)),
        out_shape=jax.ShapeDtypeStruct((b, t, d), F32),
        compiler_params=pltpu.CompilerParams(
            dimension_semantics=("arbitrary", "arbitrary"), vmem_limit_bytes=VMEM_LIMIT),
        name="final_norm",
    )(h, g.reshape(1, d))


RW_VM = RW_HEAD // (2 * SUBLANES)
RW_STEP_BLOCK = 16
OP_W, OP_B, OP_K, OP_R, OP_KK = range(5)


def _rwkv_kernel(ops_ref, v_ref, y_ref, s_ref):
    tb = ops_ref.shape[0]

    @pl.when(pl.program_id(0) == 0)
    def _():
        s_ref[...] = jnp.zeros_like(s_ref)

    zero = jnp.zeros((SUBLANES, LANES), F32)
    sa0 = [zero] * RW_VM
    for k in range(RW_HEAD):
        kk = ops_ref[0, OP_KK, k:k + 1, :]
        for m in range(RW_VM):
            sa0[m] = sa0[m] + s_ref[k, m] * kk

    def step(t, sa):
        tn = jnp.minimum(t + 1, tb - 1)
        vt = [v_ref[t, m] for m in range(RW_VM)]
        acc_y = [zero] * RW_VM
        acc_s = [zero] * RW_VM
        for k in range(RW_HEAD):
            w = ops_ref[t, OP_W, k:k + 1, :]
            b = ops_ref[t, OP_B, k:k + 1, :]
            kt = ops_ref[t, OP_K, k:k + 1, :]
            r = ops_ref[t, OP_R, k:k + 1, :]
            kkn = ops_ref[tn, OP_KK, k:k + 1, :]
            for m in range(RW_VM):
                s = s_ref[k, m] * w - sa[m] * b + vt[m] * kt
                s_ref[k, m] = s
                acc_y[m] = acc_y[m] + s * r
                acc_s[m] = acc_s[m] + s * kkn
        for m in range(RW_VM):
            y_ref[t, m] = acc_y[m]
        return tuple(acc_s)

    lax.fori_loop(0, tb, step, tuple(sa0))


def _rwkv_scan(ops, vt):
    t = ops.shape[0]
    tb = RW_STEP_BLOCK
    return pl.pallas_call(
        _rwkv_kernel,
        grid=(t // tb,),
        in_specs=[pl.BlockSpec((tb, 5, RW_HEAD, LANES), lambda i: (i, 0, 0, 0)),
                  pl.BlockSpec((tb, RW_VM, SUBLANES, LANES), lambda i: (i, 0, 0, 0))],
        out_specs=pl.BlockSpec((tb, RW_VM, SUBLANES, LANES), lambda i: (i, 0, 0, 0)),
        out_shape=jax.ShapeDtypeStruct((t, RW_VM, SUBLANES, LANES), F32),
        scratch_shapes=[pltpu.VMEM((RW_HEAD, RW_VM, SUBLANES, LANES), F32)],
        compiler_params=pltpu.CompilerParams(
            dimension_semantics=("arbitrary",), vmem_limit_bytes=VMEM_LIMIT),
        name="rwkv_scan",
    )(ops, vt)


def _token_shift(x, mu):
    prev = jnp.pad(x, ((0, 0), (1, 0), (0, 0)))[:, :-1]
    return x + (prev - x) * mu


def _heads(t, n_heads):
    return t.reshape(t.shape[:-1] + (n_heads, t.shape[-1] // n_heads))


def _flip(t):
    return jnp.flip(t, axis=1)


def _l2norm(x):
    return x * lax.rsqrt(jnp.sum(x * x, axis=-1, keepdims=True) + EPS)


def _rw_features(p_rkv, p_wa, mu_rkv, mu_wa, w0, w2, a0, a2, k_k, k_a):
    x_rkv = _token_shift(p_rkv, mu_rkv)
    x_wa = _token_shift(p_wa, mu_wa)
    r, k, v = jnp.split(x_rkv, 3, axis=-1)
    w_lo, a_lo = jnp.split(x_wa, [RW_DECAY_LR], axis=-1)
    decay = jnp.exp(-RW_DECAY_SCALE * jax.nn.sigmoid(w0 + jnp.tanh(w_lo) @ w2))
    a = jax.nn.sigmoid(a0 + a_lo @ a2)
    kk = _l2norm(_heads(k * k_k, RW_HEADS))
    k = k * (1.0 + (a - 1.0) * k_a)
    r, k, v, a, decay = (_heads(t, RW_HEADS) for t in (r, k, v, a, decay))
    return r, decay, k, v, kk, kk * a


def _rw_mixer(p_c, p_x, lp):
    b = p_x[0].shape[0]
    nchain = 2 * b * RW_HEADS
    assert 2 * nchain == LANES, "chain-on-lanes layout needs 2 * batch * heads == 64"
    feats = []
    for i in range(2):
        per_seq = []
        for (rkv, wa, _) in (p_c, p_x):
            if i == 1:
                rkv, wa = _flip(rkv), _flip(wa)
            per_seq.append(_rw_features(rkv, wa, lp['rw_mu_rkv'][i], lp['rw_mu_wa'][i], lp['rw_w0'][i],
                                        lp['rw_w2'][i], lp['rw_a0'][i], lp['rw_a2'][i], lp['rw_kk'], lp['rw_ka']))
        feats.append(per_seq)

    def chain_major(idx):
        per_dir = [jnp.concatenate([feats[i][0][idx], feats[i][1][idx]], axis=1) for i in range(2)]
        x = jnp.stack(per_dir, axis=0)
        return jnp.transpose(x, (2, 4, 0, 1, 3)).reshape(x.shape[2], RW_HEAD, nchain)

    r_i, w_i, k_i, v_i, kk_i, b_i = range(6)
    ops = jnp.stack([jnp.repeat(chain_major(j), 2, axis=-1) for j in (w_i, b_i, k_i, r_i, kk_i)], axis=1)
    v_cm = chain_major(v_i)
    tt = v_cm.shape[0]
    vt = v_cm.reshape(tt, RW_VM, SUBLANES, 2, nchain)
    vt = jnp.transpose(vt, (0, 1, 2, 4, 3)).reshape(tt, RW_VM, SUBLANES, LANES)
    y = _rwkv_scan(ops, vt)
    y = y.reshape(tt, RW_VM, SUBLANES, nchain, 2)
    y = jnp.transpose(y, (3, 0, 1, 2, 4)).reshape(2, b, RW_HEADS, tt, RW_HEAD)
    y = jnp.transpose(y, (0, 1, 3, 2, 4))
    tc = p_c[0].shape[1]
    outs = []
    for s, (sl, p) in enumerate(((slice(0, tc), p_c), (slice(tc, tt), p_x))):
        y_f, y_b = y[0][:, sl], y[1][:, sl]
        bo = []
        for i in range(2):
            r, _, k, v = feats[i][s][:4]
            bo.append(jnp.sum(r * k * lp['rw_rk'], axis=-1, keepdims=True) * v)
        yy = y_f + _flip(y_b)
        mu = jnp.mean(yy, axis=-1, keepdims=True)
        var = jnp.mean(jnp.square(yy - mu), axis=-1, keepdims=True)
        yn = (yy - mu) * lax.rsqrt(var + RW_GN_EPS)
        bt = yy.shape[:2]
        y_rw = yn.reshape(bt + (GROUP,)) * lp['rw_gn_w'] + lp['rw_gn_b'] + (bo[0] + _flip(bo[1])).reshape(bt + (GROUP,))
        outs.append(y_rw * (jax.nn.sigmoid(p[2]) @ lp['rw_g2']))
    return outs


def _to_chunks(t):
    return t.reshape((t.shape[0], t.shape[1] // CHUNK, CHUNK) + t.shape[2:])


def _rms_heads(o, g):
    return o * lax.rsqrt(jnp.mean(o * o, axis=-1, keepdims=True) + EPS) * g


def _conv3_rows(x, w, row_len):
    b, t, ch = x.shape
    rows = t // row_len
    xp = jnp.pad(x.reshape(b, rows, row_len, ch), ((0, 0), (0, 0), (1, 1), (0, 0)))
    y = xp[:, :, :-2] * w[0] + xp[:, :, 1:-1] * w[1] + xp[:, :, 2:] * w[2]
    return y.reshape(b, t, ch)


def _gla_scan(q, k, v, log_f, s0):
    b, t, h, dv = v.shape
    q, k, v, log_f = (_to_chunks(a) for a in (q, k, v, log_f))
    cum = jnp.cumsum(log_f, axis=2)
    last = cum[:, :, -1:]
    q_dec = q * jnp.exp(cum)
    k_inv = k * jnp.exp(-cum)
    k_end = k * jnp.exp(last - cum)
    lower = jnp.tril(jnp.ones((CHUNK, CHUNK), bool))
    att = jnp.where(lower, jnp.einsum('bnihd,bnjhd->bnhij', q_dec, k_inv), 0.0)
    o_intra = jnp.einsum('bnhij,bnjhv->bnihv', att, v)
    d_state = jnp.einsum('bnjhd,bnjhv->nbhdv', k_end, v)
    chunk_decay = jnp.moveaxis(jnp.exp(last[:, :, 0]), 1, 0)

    def step(s, inp):
        dec, ds = inp
        return dec[..., None] * s + ds, s

    s_fin, s_start = lax.scan(step, s0, (chunk_decay, d_state))
    o_inter = jnp.einsum('bnihd,nbhdv->bnihv', q_dec, s_start)
    return (o_intra + o_inter).reshape(b, t, h, dv), s_fin


def _gdn_scan(q, k, v, log_a, beta, s0):
    b, t, h, dv = v.shape
    dk = q.shape[-1]
    hm = lambda a: jnp.moveaxis(_to_chunks(a), 3, 2)
    q, k, v, log_a, beta = (hm(a) for a in (q, k, v, log_a, beta))
    q = q * dk ** -0.5
    cum = jnp.cumsum(log_a, axis=-1)
    lower = jnp.tril(jnp.ones((CHUNK, CHUNK), bool))
    strict = jnp.tril(jnp.ones((CHUNK, CHUNK), bool), -1)
    decay = jnp.exp(jnp.where(lower, cum[..., :, None] - cum[..., None, :], -jnp.inf))
    kk = jnp.einsum('bnhid,bnhjd->bnhij', k, k)
    lmat = jnp.where(strict, beta[..., None] * kk * decay, 0.0) + jnp.eye(CHUNK, dtype=kk.dtype)
    rhs = jnp.concatenate([v * beta[..., None], k * (beta * jnp.exp(cum))[..., None]], axis=-1)
    sol = lax.linalg.triangular_solve(lmat, rhs, left_side=True, lower=True, unit_diagonal=True)
    u, w = sol[..., :dv], sol[..., dv:]
    a_qk = jnp.einsum('bnhid,bnhjd->bnhij', q, k) * decay
    k_end = k * jnp.exp(cum[..., -1:] - cum)[..., None]
    chunk_decay = jnp.exp(cum[..., -1])
    cm = lambda a: jnp.moveaxis(a, 1, 0)

    def step(s, inp):
        u_i, w_i, ke_i, dec_i = inp
        v_new = u_i - jnp.einsum('bhcd,bhdv->bhcv', w_i, s)
        s_next = dec_i[..., None, None] * s + jnp.einsum('bhcd,bhcv->bhdv', ke_i, v_new)
        return s_next, (s, v_new)

    s_fin, (s_start, v_new) = lax.scan(step, s0, (cm(u), cm(w), cm(k_end), cm(chunk_decay)))
    o = (jnp.einsum('bnhcd,nbhdv->bnhcv', q * jnp.exp(cum)[..., None], s_start)
         + jnp.einsum('bnhij,nbhjv->bnhiv', a_qk, v_new))
    return jnp.moveaxis(o, 2, 3).reshape(b, t, h, dv), s_fin


def _seg(proj, name):
    o, wd, _ = SEG[name]
    return proj[..., o:o + wd]


TOKEN_BLOCK = 256
HI = lax.Precision.HIGHEST


def _dot(a, b, precision=None):
    return jnp.dot(a, b, preferred_element_type=F32, precision=precision)


def _dot_nt(a, b):
    return lax.dot_general(a, b, (((1,), (1,)), ((), ())), preferred_element_type=F32)


def _dot_tn(a, b):
    return lax.dot_general(a, b, (((0,), (0,)), ((), ())), preferred_element_type=F32)


def _silu(x):
    return x * jax.nn.sigmoid(x)


def _chunk_masks(reverse):
    ri = lax.broadcasted_iota(jnp.int32, (CHUNK, CHUNK), 0)
    ci = lax.broadcasted_iota(jnp.int32, (CHUNK, CHUNK), 1)
    incl = (ci >= ri) if reverse else (ci <= ri)
    strict = (ci > ri) if reverse else (ci < ri)
    return incl, strict, (ri == ci).astype(F32)


def _conv3_block(x, cw, row_len):
    tb = x.shape[0]
    pos = lax.broadcasted_iota(jnp.int32, (tb, 1), 0) % row_len
    x_prev = jnp.where(pos == 0, 0.0, pltpu.roll(x, 1, 0))
    x_next = jnp.where(pos == row_len - 1, 0.0, pltpu.roll(x, tb - 1, 0))
    return x_prev * cw[0:1] + x * cw[1:2] + x_next * cw[2:3]


def _gdn_kernel(*refs, reverse, finalize, row_len, dirn):
    if finalize:
        (qkv_ref, ab_ref, z_ref, ob_ref, cw_ref, alog_ref, dtb_ref, g_ref, s0_ref,
         o_ref, sfin_ref, s_scr) = refs
    else:
        qkv_ref, ab_ref, cw_ref, alog_ref, dtb_ref, s0_ref, o_ref, sfin_ref, s_scr = refs
    n = pl.program_id(1)

    @pl.when(n == 0)
    def _():
        s_scr[...] = s0_ref[0]

    tb = qkv_ref.shape[1]
    qkv = _silu(_conv3_block(qkv_ref[0], cw_ref[...], row_len))
    ab = ab_ref[0]
    log_a = -jnp.exp(alog_ref[...]) * jax.nn.softplus(ab + dtb_ref[...])
    beta = jax.nn.sigmoid(ab)
    incl, strict, eye = _chunk_masks(reverse)
    tri = incl.astype(F32)
    last = 0 if reverse else CHUNK - 1
    hd = GDN_DK
    nchunk = tb // CHUNK
    order = range(nchunk - 1, -1, -1) if reverse else range(nchunk)
    for c in order:
        sl = slice(c * CHUNK, (c + 1) * CHUNK)
        cum_all = _dot(tri, log_a[sl], HI)
        for h in range(GDN_HEADS):
            col = dirn * GDN_HEADS + h
            cum = cum_all[:, col:col + 1]
            bet = beta[sl, 2 * GDN_HEADS + col:2 * GDN_HEADS + col + 1]
            q = _l2norm(qkv[sl, h * hd:(h + 1) * hd]) * hd ** -0.5
            k = _l2norm(qkv[sl, GROUP + h * hd:GROUP + (h + 1) * hd])
            v = qkv[sl, 2 * GROUP + h * hd:2 * GROUP + (h + 1) * hd]
            cum_row = jnp.sum(eye * cum, axis=0, keepdims=True)
            decay = jnp.exp(jnp.where(incl, cum - cum_row, -jnp.inf))
            kb = k.astype(BF16)
            a = jnp.where(strict, bet * _dot_nt(kb, kb) * decay, 0.0)
            x = -a
            inv = eye + x
            for _ in range(5):
                x = _dot(x, x, HI)
                inv = inv + _dot(inv, x, HI)
            ecum = jnp.exp(cum)
            rhs = jnp.concatenate([v * bet, k * (bet * ecum)], axis=1)
            sol = _dot(inv, rhs, HI)
            u, w = sol[:, :GDN_DV], sol[:, GDN_DV:]
            a_qk = _dot_nt(q.astype(BF16), kb) * decay
            total = cum[last:last + 1]
            k_end = k * jnp.exp(total - cum)
            s = s_scr[h]
            sb = s.astype(BF16)
            v_new = u - _dot(w.astype(BF16), sb)
            vb = v_new.astype(BF16)
            o = _dot((q * ecum).astype(BF16), sb) + _dot(a_qk.astype(BF16), vb)
            s_scr[h] = jnp.exp(total) * s + _dot_tn(k_end.astype(BF16), vb)
            cols = slice(h * GDN_DV, (h + 1) * GDN_DV)
            if finalize:
                o = o + ob_ref[0, sl, cols]
                o = o * lax.rsqrt(jnp.mean(o * o, axis=-1, keepdims=True) + EPS) * g_ref[...]
                o = o * _silu(z_ref[0, sl, cols])
            o_ref[0, sl, cols] = o.astype(o_ref.dtype)

    @pl.when(n == pl.num_programs(1) - 1)
    def _():
        sfin_ref[0] = s_scr[...]


def _gdn_pass(proj, o_other, s0, lp, *, dirn, row_len):
    b, t, _ = proj.shape
    tb = min(TOKEN_BLOCK, t)
    assert t % tb == 0 and tb % row_len == 0
    nblk = t // tb
    reverse = dirn == 1
    finalize = o_other is not None
    tok = (lambda i, n: (i, nblk - 1 - n)) if reverse else (lambda i, n: (i, n))
    seg = lambda name, wd: pl.BlockSpec((1, tb, wd), lambda i, n: tok(i, n) + (_col_block(name),))
    full = lambda a: pl.BlockSpec(a.shape, lambda i, n: (0,) * a.ndim)
    lane_row = lambda vals: jnp.pad(vals.reshape(1, -1), ((0, 0), (0, LANES - vals.size)))
    cw = lp['gdn_conv']
    alog = lane_row(lp['gdn_a_log'])
    dtb = lane_row(lp['gdn_dt_bias'])
    g = lp['gdn_norm_g'].reshape(1, GDN_DV)
    state_spec = pl.BlockSpec((1, GDN_HEADS, GDN_DK, GDN_DV), lambda i, n: (i, 0, 0, 0))
    if finalize:
        args = (proj, proj, proj, o_other, cw, alog, dtb, g, s0)
        in_specs = [seg('d_qkv', 3 * GROUP), seg('d_ab', LANES), seg('d_z', GROUP),
                    pl.BlockSpec((1, tb, GROUP), lambda i, n: tok(i, n) + (0,)),
                    full(cw), full(alog), full(dtb), full(g), state_spec]
    else:
        args = (proj, proj, cw, alog, dtb, s0)
        in_specs = [seg('d_qkv', 3 * GROUP), seg('d_ab', LANES), full(cw), full(alog), full(dtb), state_spec]
    return pl.pallas_call(
        functools.partial(_gdn_kernel, reverse=reverse, finalize=finalize, row_len=row_len, dirn=dirn),
        grid=(b, nblk),
        in_specs=in_specs,
        out_specs=[pl.BlockSpec((1, tb, GROUP), lambda i, n: tok(i, n) + (0,)), state_spec],
        out_shape=[jax.ShapeDtypeStruct((b, t, GROUP), BF16 if finalize else F32),
                   jax.ShapeDtypeStruct((b, GDN_HEADS, GDN_DK, GDN_DV), F32)],
        scratch_shapes=[pltpu.VMEM((GDN_HEADS, GDN_DK, GDN_DV), F32)],
        compiler_params=pltpu.CompilerParams(
            dimension_semantics=("arbitrary", "arbitrary"), vmem_limit_bytes=VMEM_LIMIT),
        name="gdn_fwd" if finalize else "gdn_bwd",
    )(*args)


def _gla_kernel(*refs, reverse, finalize):
    if finalize:
        (q_ref, k_ref, v_ref, lo_ref, r_ref, ob_ref, up_ref, ab_ref, g_ref, s0_ref,
         o_ref, sfin_ref, s_scr) = refs
    else:
        q_ref, k_ref, v_ref, lo_ref, up_ref, ab_ref, s0_ref, o_ref, sfin_ref, s_scr = refs
    n = pl.program_id(1)

    @pl.when(n == 0)
    def _():
        s_scr[...] = s0_ref[0]

    tb = q_ref.shape[1]
    gate = _dot(lo_ref[0].astype(BF16), up_ref[...].astype(BF16)) + ab_ref[...]
    log_f = jax.nn.log_sigmoid(gate) / GLA_TAU
    incl, _, eye = _chunk_masks(reverse)
    tri = incl.astype(F32)
    last = 0 if reverse else CHUNK - 1
    nchunk = tb // CHUNK
    order = range(nchunk - 1, -1, -1) if reverse else range(nchunk)
    for c in order:
        sl = slice(c * CHUNK, (c + 1) * CHUNK)
        cum = _dot(tri, log_f[sl], HI)
        total = cum[last:last + 1]
        q_dec = q_ref[0, sl, :] * GLA_DK ** -0.5 * jnp.exp(cum)
        k = k_ref[0, sl, :]
        k_inv = (k * jnp.exp(-cum)).astype(BF16)
        k_end = (k * jnp.exp(total - cum)).astype(BF16)
        q_dec = q_dec.astype(BF16)
        dec_row = jnp.exp(total)
        for h in range(GLA_HEADS):
            kc = slice(h * GLA_DK, (h + 1) * GLA_DK)
            vc = slice(h * GLA_DV, (h + 1) * GLA_DV)
            vb = v_ref[0, sl, vc].astype(BF16)
            att = jnp.where(incl, _dot_nt(q_dec[:, kc], k_inv[:, kc]), 0.0)
            s = s_scr[h]
            o = _dot(att.astype(BF16), vb) + _dot(q_dec[:, kc], s.astype(BF16))
            dec_col = jnp.sum(eye * dec_row[:, kc], axis=1, keepdims=True)
            s_scr[h] = dec_col * s + _dot_tn(k_end[:, kc], vb)
            if finalize:
                o = o + ob_ref[0, sl, vc]
                o = o * lax.rsqrt(jnp.mean(o * o, axis=-1, keepdims=True) + EPS) * g_ref[...]
                o = o * _silu(r_ref[0, sl, vc])
            o_ref[0, sl, vc] = o.astype(o_ref.dtype)

    @pl.when(n == pl.num_programs(1) - 1)
    def _():
        sfin_ref[0] = s_scr[...]


def _gla_pass(proj, o_other, s0, lp, *, dirn):
    b, t, _ = proj.shape
    tb = min(TOKEN_BLOCK, t)
    assert t % tb == 0
    nblk = t // tb
    reverse = dirn == 1
    finalize = o_other is not None
    tok = (lambda i, n: (i, nblk - 1 - n)) if reverse else (lambda i, n: (i, n))
    seg = lambda name: pl.BlockSpec((1, tb, SEG[name][2]), lambda i, n: tok(i, n) + (_col_block(name),))
    full = lambda a: pl.BlockSpec(a.shape, lambda i, n: (0,) * a.ndim)
    hk = GLA_HEADS * GLA_DK
    up = jnp.pad(lp['gla_a_up'][dirn], ((0, LANES - GLA_LR), (0, 0)))
    ab = lp['gla_a_b'][dirn].reshape(1, hk)
    g = lp['gla_norm_g'].reshape(1, GLA_DV)
    state_spec = pl.BlockSpec((1, GLA_HEADS, GLA_DK, GLA_DV), lambda i, n: (i, 0, 0, 0))
    if finalize:
        args = (proj, proj, proj, proj, proj, o_other, up, ab, g, s0)
        in_specs = [seg('g_q'), seg('g_k'), seg('g_v'), seg('g_lo'), seg('g_r'),
                    pl.BlockSpec((1, tb, GROUP), lambda i, n: tok(i, n) + (0,)),
                    full(up), full(ab), full(g), state_spec]
    else:
        args = (proj, proj, proj, proj, up, ab, s0)
        in_specs = [seg('g_q'), seg('g_k'), seg('g_v'), seg('g_lo'), full(up), full(ab), state_spec]
    return pl.pallas_call(
        functools.partial(_gla_kernel, reverse=reverse, finalize=finalize),
        grid=(b, nblk),
        in_specs=in_specs,
        out_specs=[pl.BlockSpec((1, tb, GROUP), lambda i, n: tok(i, n) + (0,)), state_spec],
        out_shape=[jax.ShapeDtypeStruct((b, t, GROUP), BF16 if finalize else F32),
                   jax.ShapeDtypeStruct((b, GLA_HEADS, GLA_DK, GLA_DV), F32)],
        scratch_shapes=[pltpu.VMEM((GLA_HEADS, GLA_DK, GLA_DV), F32)],
        compiler_params=pltpu.CompilerParams(
            dimension_semantics=("arbitrary", "arbitrary"), vmem_limit_bytes=VMEM_LIMIT),
        name="gla_fwd" if finalize else "gla_bwd",
    )(*args)


def _two_direction_group(pass_fn, state_shape, proj_c, proj_x, **kw):
    zero = jnp.zeros((proj_x[0].shape[0],) + state_shape, F32)
    ob_c, sb_c = pass_fn(proj_c[0], None, zero, dirn=1, **proj_c[1], **kw)
    ob_x, _ = pass_fn(proj_x[0], None, sb_c, dirn=1, **proj_x[1], **kw)
    y_c, sf_c = pass_fn(proj_c[0], ob_c, zero, dirn=0, **proj_c[1], **kw)
    y_x, _ = pass_fn(proj_x[0], ob_x, sf_c, dirn=0, **proj_x[1], **kw)
    return y_c, y_x


def _gla_group_pallas(proj_c, proj_x, lp):
    return _two_direction_group(_gla_pass, (GLA_HEADS, GLA_DK, GLA_DV), (proj_c, {}), (proj_x, {}), lp=lp)


def _gdn_group_pallas(proj_c, proj_x, lp):
    return _two_direction_group(_gdn_pass, (GDN_HEADS, GDN_DK, GDN_DV),
                                (proj_c, dict(row_len=proj_c.shape[1])), (proj_x, dict(row_len=GRID_W)), lp=lp)


def _gla_group(proj, init, lp):
    g_q, g_k, g_v, g_r, g_lo = (_seg(proj, n) for n in ('g_q', 'g_k', 'g_v', 'g_r', 'g_lo'))
    q = _heads(g_q, GLA_HEADS) * GLA_DK ** -0.5
    k = _heads(g_k, GLA_HEADS)
    v = _heads(g_v, GLA_HEADS)
    log_f = [_heads(jax.nn.log_sigmoid(g_lo @ lp['gla_a_up'][i] + lp['gla_a_b'][i]) / GLA_TAU, GLA_HEADS)
             for i in range(2)]
    o_f, s_f = _gla_scan(q, k, v, log_f[0], init[0])
    o_b, s_b = _gla_scan(_flip(q), _flip(k), _flip(v), _flip(log_f[1]), init[1])
    y = _rms_heads(o_f + _flip(o_b), lp['gla_norm_g']) * jax.nn.silu(_heads(g_r, GLA_HEADS))
    return y.reshape(y.shape[:2] + (GROUP,)), (s_f, s_b)


def _gdn_group(proj, row_len, init, lp):
    b, t = proj.shape[:2]
    d_qkv, d_z, d_ab = (_seg(proj, n) for n in ('d_qkv', 'd_z', 'd_ab'))
    d_a, d_b = d_ab[..., :2 * GDN_HEADS], d_ab[..., 2 * GDN_HEADS:]
    qkv = jax.nn.silu(_conv3_rows(d_qkv, lp['gdn_conv'], row_len))
    q, k, v = (_heads(a, GDN_HEADS) for a in jnp.split(qkv, 3, axis=-1))
    q, k = _l2norm(q), _l2norm(k)
    log_a = -jnp.exp(lp['gdn_a_log']) * jax.nn.softplus(d_a.reshape(b, t, 2, GDN_HEADS) + lp['gdn_dt_bias'])
    beta = jax.nn.sigmoid(d_b.reshape(b, t, 2, GDN_HEADS))
    o_f, s_f = _gdn_scan(q, k, v, log_a[:, :, 0], beta[:, :, 0], init[0])
    o_b, s_b = _gdn_scan(_flip(q), _flip(k), _flip(v), _flip(log_a[:, :, 1]), _flip(beta[:, :, 1]), init[1])
    y = _rms_heads(o_f + _flip(o_b), lp['gdn_norm_g']) * jax.nn.silu(_heads(d_z, GDN_HEADS))
    return y.reshape(b, t, GROUP), (s_f, s_b)


def _sc_group(proj, row_len, lp):
    c_b, c_c, c_h = (_seg(proj, n) for n in ('c_b', 'c_c', 'c_h'))
    return c_b * _conv3_rows(c_c * c_h, lp['sc_conv'], row_len)


def _mixers(proj_c, proj_x, lp):
    b = proj_x.shape[0]
    z = lambda h, d1, d2: jnp.zeros((b, h, d1, d2), F32)
    gla0 = (z(GLA_HEADS, GLA_DK, GLA_DV),) * 2
    gdn0 = (z(GDN_HEADS, GDN_DK, GDN_DV),) * 2
    tc = proj_c.shape[1]
    y_gla_c, y_gla_x = _gla_group_pallas(proj_c, proj_x, lp)
    y_gdn_c, y_gdn_x = _gdn_group_pallas(proj_c, proj_x, lp)
    y_sc_c = _sc_group(proj_c, tc, lp)
    y_sc_x = _sc_group(proj_x, GRID_W, lp)
    rw = lambda p: tuple(_seg(p, n) for n in ('r_rkv', 'r_wa', 'r_g'))
    y_rw_c, y_rw_x = _rw_mixer(rw(proj_c), rw(proj_x), lp)
    y_c = jnp.concatenate([y_gla_c, y_gdn_c, y_sc_c, y_rw_c], axis=-1).astype(BF16)
    y_x = jnp.concatenate([y_gla_x, y_gdn_x, y_sc_x, y_rw_x], axis=-1).astype(BF16)
    return y_c, y_x


def kernel(x, c, ctx, c_ctx, ada_w, ada_b, norm1_g, norm2_g, w_in, w_out, gla_a_up, gla_a_b, gla_norm_g, gdn_conv, gdn_a_log, gdn_dt_bias, gdn_norm_g, sc_conv, rw_mu_rkv, rw_mu_wa, rw_w0, rw_w2, rw_a0, rw_a2, rw_g2, rw_kk, rw_ka, rw_rk, rw_gn_w, rw_gn_b, ffn_w_gu, ffn_w_down, final_g):
    nb, t, d = x.shape
    tc = ctx.shape[1]
    depth = w_in.shape[0]
    assert d == D_MODEL and t % 512 == 0 and tc % CHUNK == 0 and nb + 1 <= SUBLANES

    cond = jnp.concatenate([c, c_ctx[None, :], jnp.zeros((SUBLANES - nb - 1, d), F32)], axis=0)
    mod = _modulation(cond, ada_w, ada_b)

    tm_x = 512
    tm_c = tc
    h_x, h_c = x, ctx
    for l in range(depth):
        lp = dict(gla_a_up=gla_a_up[l], gla_a_b=gla_a_b[l],
                  gla_norm_g=gla_norm_g[l], gdn_conv=gdn_conv[l], gdn_a_log=gdn_a_log[l],
                  gdn_dt_bias=gdn_dt_bias[l], gdn_norm_g=gdn_norm_g[l], sc_conv=sc_conv[l],
                  rw_mu_rkv=rw_mu_rkv[l], rw_mu_wa=rw_mu_wa[l], rw_w0=rw_w0[l], rw_w2=rw_w2[l],
                  rw_a0=rw_a0[l], rw_a2=rw_a2[l], rw_g2=rw_g2[l], rw_kk=rw_kk[l], rw_ka=rw_ka[l],
                  rw_rk=rw_rk[l], rw_gn_w=rw_gn_w[l], rw_gn_b=rw_gn_b[l])
        m_x = [mod[l, :nb, i * d:(i + 1) * d][:, None, :] for i in range(6)]
        m_c = [jnp.broadcast_to(mod[l, nb, i * d:(i + 1) * d][None, None, :], (nb, 1, d)) for i in range(6)]
        w_in_l = _pad_w_in(w_in[l])
        w_out_l = w_out[l].astype(BF16)
        w_gu_l = ffn_w_gu[l].astype(BF16)
        w_dn_l = ffn_w_down[l].astype(BF16)

        proj_c = _norm_proj(h_c, norm1_g[l], m_c[1], m_c[0], w_in_l, tm_c, 1024)
        proj_x = _norm_proj(h_x, norm1_g[l], m_x[1], m_x[0], w_in_l, tm_x, 1024)
        y_c, y_x = _mixers(proj_c, proj_x, lp)

        h_x = _proj_residual(y_x, w_out_l, h_x, m_x[2], tm_x, 512)
        a_x = _norm_swiglu(h_x, norm2_g[l], m_x[4], m_x[3], w_gu_l, tm_x, 512)
        h_x = _proj_residual(a_x, w_dn_l, h_x, m_x[5], tm_x, 512)
        if l < depth - 1:
            h_c = _proj_residual(y_c, w_out_l, h_c, m_c[2], tm_c, 512)
            a_c = _norm_swiglu(h_c, norm2_g[l], m_c[4], m_c[3], w_gu_l, tm_c, 512)
            h_c = _proj_residual(a_c, w_dn_l, h_c, m_c[5], tm_c, 512)
    return _final_norm(h_x, final_g, tm_x)
```

```python
import functools
import math

import numpy as np
import jax
import jax.numpy as jnp
from jax import lax
from jax.experimental import pallas as pl
from jax.experimental.pallas import tpu as pltpu

F32 = jnp.float32
BF16 = jnp.bfloat16

D_MODEL = 2048
GROUP = D_MODEL // 4
CHUNK = 64
EPS = 1e-6
GRID_W = 64

GLA_HEADS = 4
GLA_DK = 64
GLA_DV = 128
GLA_LR = 16
GLA_TAU = 16.0
GDN_HEADS = 4
GDN_DK = 128
GDN_DV = 128
RW_HEAD = 64
RW_HEADS = GROUP // RW_HEAD
RW_DECAY_LR = 32
RW_A_LR = 32
RW_GATE_LR = 96
RW_DECAY_SCALE = math.exp(-0.5)
RW_GN_EPS = 64e-5
FFN = -(-(8 * D_MODEL) // (3 * 256)) * 256

SUBLANES = 8
LANES = 128
VMEM_LIMIT = 48 * 1024 * 1024

_SRC_SEGS = (
    ('g_q', 256), ('g_k', 256), ('g_v', 512), ('g_r', 512), ('g_lo', GLA_LR),
    ('d_qkv', 1536), ('d_z', 512), ('d_ab', 16),
    ('c_b', 512), ('c_c', 512), ('c_h', 512),
    ('r_rkv', 1536), ('r_wa', 64), ('r_g', RW_GATE_LR),
)
_DST_ORDER = ('g_q', 'g_k', 'g_v', 'g_r', 'd_qkv', 'r_rkv', 'd_z', 'c_b', 'c_c', 'c_h',
              'g_lo', 'd_ab', 'r_wa', 'r_g')


def _round_up(n, m):
    return -(-n // m) * m


def _seg_layout():
    width = dict(_SRC_SEGS)
    src, o = {}, 0
    for name, wd in _SRC_SEGS:
        src[name] = o
        o += wd
    dst, o = {}, 0
    for name in _DST_ORDER:
        pw = _round_up(width[name], LANES)
        assert o % pw == 0, name
        dst[name] = (o, width[name], pw)
        o += pw
    return src, dst, o


SEG_SRC, SEG, PROJ_PAD = _seg_layout()


def _pad_w_in(w):
    parts = []
    for name in _DST_ORDER:
        _, wd, pw = SEG[name]
        seg = w[:, SEG_SRC[name]:SEG_SRC[name] + wd]
        if pw != wd:
            seg = jnp.pad(seg, ((0, 0), (0, pw - wd)))
        parts.append(seg)
    return jnp.concatenate(parts, axis=1).astype(BF16)


def _col_block(name):
    o, _, pw = SEG[name]
    return o // pw


def _mod_kernel(c_ref, w_ref, b_ref, o_ref):
    x = c_ref[...]
    x = (x * jax.nn.sigmoid(x)).astype(BF16)
    o_ref[0] = jnp.dot(x, w_ref[0].astype(BF16), preferred_element_type=F32) + b_ref[0]


def _modulation(cond, ada_w, ada_b):
    nl, d, n = ada_w.shape
    tn = 1024
    return pl.pallas_call(
        _mod_kernel,
        grid=(nl, n // tn),
        in_specs=[pl.BlockSpec((SUBLANES, d), lambda l, j: (0, 0)),
                  pl.BlockSpec((1, d, tn), lambda l, j: (l, 0, j)),
                  pl.BlockSpec((1, 1, tn), lambda l, j: (l, 0, j))],
        out_specs=pl.BlockSpec((1, SUBLANES, tn), lambda l, j: (l, 0, j)),
        out_shape=jax.ShapeDtypeStruct((nl, SUBLANES, n), F32),
        compiler_params=pltpu.CompilerParams(
            dimension_semantics=("arbitrary", "arbitrary"), vmem_limit_bytes=VMEM_LIMIT),
        name="modulation",
    )(cond, ada_w, ada_b.reshape(nl, 1, n))


def _modnorm_rows(h_ref, g_ref, sc_ref, sh_ref):
    x = h_ref[0]
    y = x * lax.rsqrt(jnp.mean(x * x, axis=-1, keepdims=True) + EPS) * g_ref[...]
    return (y * (1.0 + sc_ref[0]) + sh_ref[0]).astype(BF16)


def _norm_proj_kernel(h_ref, g_ref, sc_ref, sh_ref, w_ref, o_ref, a_scr):
    @pl.when(pl.program_id(2) == 0)
    def _():
        a_scr[...] = _modnorm_rows(h_ref, g_ref, sc_ref, sh_ref)

    o_ref[0] = jnp.dot(a_scr[...], w_ref[...], preferred_element_type=F32).astype(o_ref.dtype)


def _norm_proj(h, g, scale, shift, w, tm, tn):
    b, t, d = h.shape
    n = w.shape[1]
    return pl.pallas_call(
        _norm_proj_kernel,
        grid=(b, t // tm, n // tn),
        in_specs=[pl.BlockSpec((1, tm, d), lambda i, m, j: (i, m, 0)),
                  pl.BlockSpec((1, d), lambda i, m, j: (0, 0)),
                  pl.BlockSpec((1, 1, d), lambda i, m, j: (i, 0, 0)),
                  pl.BlockSpec((1, 1, d), lambda i, m, j: (i, 0, 0)),
                  pl.BlockSpec((d, tn), lambda i, m, j: (0, j))],
        out_specs=pl.BlockSpec((1, tm, tn), lambda i, m, j: (i, m, j)),
        out_shape=jax.ShapeDtypeStruct((b, t, n), F32),
        scratch_shapes=[pltpu.VMEM((tm, d), BF16)],
        compiler_params=pltpu.CompilerParams(
            dimension_semantics=("arbitrary", "arbitrary", "arbitrary"), vmem_limit_bytes=VMEM_LIMIT),
        name="norm_proj",
    )(h, g.reshape(1, d), scale, shift, w)


def _norm_swiglu_kernel(h_ref, g_ref, sc_ref, sh_ref, wg_ref, wu_ref, o_ref, a_scr):
    @pl.when(pl.program_id(2) == 0)
    def _():
        a_scr[...] = _modnorm_rows(h_ref, g_ref, sc_ref, sh_ref)

    a = a_scr[...]
    gate = jnp.dot(a, wg_ref[...], preferred_element_type=F32)
    up = jnp.dot(a, wu_ref[...], preferred_element_type=F32)
    o_ref[0] = (gate * jax.nn.sigmoid(gate) * up).astype(o_ref.dtype)


def _norm_swiglu(h, g, scale, shift, w_gu, tm, tn):
    b, t, d = h.shape
    f = w_gu.shape[1] // 2
    nj = f // tn
    return pl.pallas_call(
        _norm_swiglu_kernel,
        grid=(b, t // tm, nj),
        in_specs=[pl.BlockSpec((1, tm, d), lambda i, m, j: (i, m, 0)),
                  pl.BlockSpec((1, d), lambda i, m, j: (0, 0)),
                  pl.BlockSpec((1, 1, d), lambda i, m, j: (i, 0, 0)),
                  pl.BlockSpec((1, 1, d), lambda i, m, j: (i, 0, 0)),
                  pl.BlockSpec((d, tn), lambda i, m, j: (0, j)),
                  pl.BlockSpec((d, tn), lambda i, m, j: (0, j + nj))],
        out_specs=pl.BlockSpec((1, tm, tn), lambda i, m, j: (i, m, j)),
        out_shape=jax.ShapeDtypeStruct((b, t, f), BF16),
        scratch_shapes=[pltpu.VMEM((tm, d), BF16)],
        compiler_params=pltpu.CompilerParams(
            dimension_semantics=("arbitrary", "arbitrary", "arbitrary"), vmem_limit_bytes=VMEM_LIMIT),
        name="norm_swiglu",
    )(h, g.reshape(1, d), scale, shift, w_gu, w_gu)


def _proj_residual_kernel(a_ref, w_ref, h_ref, gate_ref, o_ref):
    o_ref[0] = h_ref[0] + gate_ref[0] * jnp.dot(a_ref[0], w_ref[...], preferred_element_type=F32)


def _proj_residual(a, w, h, gate, tm, tn):
    b, t, k = a.shape
    d = w.shape[1]
    return pl.pallas_call(
        _proj_residual_kernel,
        grid=(b, t // tm, d // tn),
        in_specs=[pl.BlockSpec((1, tm, k), lambda i, m, j: (i, m, 0)),
                  pl.BlockSpec((k, tn), lambda i, m, j: (0, j)),
                  pl.BlockSpec((1, tm, tn), lambda i, m, j: (i, m, j)),
                  pl.BlockSpec((1, 1, tn), lambda i, m, j: (i, 0, j))],
        out_specs=pl.BlockSpec((1, tm, tn), lambda i, m, j: (i, m, j)),
        out_shape=jax.ShapeDtypeStruct((b, t, d), F32),
        compiler_params=pltpu.CompilerParams(
            dimension_semantics=("arbitrary", "arbitrary", "arbitrary"), vmem_limit_bytes=VMEM_LIMIT),
        name="proj_residual",
    )(a, w, h, gate)


def _final_norm_kernel(h_ref, g_ref, o_ref):
    x = h_ref[0]
    o_ref[0] = x * lax.rsqrt(jnp.mean(x * x, axis=-1, keepdims=True) + EPS) * g_ref[...]


def _final_norm(h, g, tm):
    b, t, d = h.shape
    return pl.pallas_call(
        _final_norm_kernel,
        grid=(b, t // tm),
        in_specs=[pl.BlockSpec((1, tm, d), lambda i, m: (i, m, 0)),
                  pl.BlockSpec((1, d), lambda i, m: (0, 0))],
        out_specs=pl.BlockSpec((1, tm, d), lambda i, m: (i, m, 0)),
        out_shape=jax.ShapeDtypeStruct((b, t, d), F32),
        compiler_params=pltpu.CompilerParams(
            dimension_semantics=("arbitrary", "arbitrary"), vmem_limit_bytes=VMEM_LIMIT),
        name="final_norm",
    )(h, g.reshape(1, d))


RW_CHAINS = 32
RW_VLO = LANES // RW_CHAINS
RW_VM = RW_HEAD // (SUBLANES * RW_VLO)
RW_STEP_BLOCK = 16
OP_W, OP_B, OP_K, OP_R, OP_KK = range(5)


def _rwkv_kernel(opsf_ref, opsb_ref, vf_ref, vb_ref, yf_ref, yb_ref, s_ref):
    tb = opsf_ref.shape[0]
    dirs = ((opsf_ref, vf_ref, yf_ref), (opsb_ref, vb_ref, yb_ref))

    @pl.when(pl.program_id(0) == 0)
    def _():
        s_ref[...] = jnp.zeros_like(s_ref)

    zero = jnp.zeros((SUBLANES, LANES), F32)
    sa0 = []
    for d, (ops_ref, _, _) in enumerate(dirs):
        first = 0 if d == 0 else tb - 1
        acc = [zero] * RW_VM
        for k in range(RW_HEAD):
            kk = ops_ref[first, OP_KK, k:k + 1, :]
            for m in range(RW_VM):
                acc[m] = acc[m] + s_ref[d, k, m] * kk
        sa0 += acc

    def step(i, sa):
        sa_next = []
        for d, (ops_ref, v_ref, y_ref) in enumerate(dirs):
            t = i if d == 0 else tb - 1 - i
            tn = jnp.minimum(i + 1, tb - 1) if d == 0 else jnp.maximum(tb - 2 - i, 0)
            vt = [v_ref[t, m] for m in range(RW_VM)]
            acc_y = [zero] * RW_VM
            acc_s = [zero] * RW_VM
            for k in range(RW_HEAD):
                w = ops_ref[t, OP_W, k:k + 1, :]
                b = ops_ref[t, OP_B, k:k + 1, :]
                kt = ops_ref[t, OP_K, k:k + 1, :]
                r = ops_ref[t, OP_R, k:k + 1, :]
                kkn = ops_ref[tn, OP_KK, k:k + 1, :]
                for m in range(RW_VM):
                    s = s_ref[d, k, m] * w - sa[d * RW_VM + m] * b + vt[m] * kt
                    s_ref[d, k, m] = s
                    acc_y[m] = acc_y[m] + s * r
                    acc_s[m] = acc_s[m] + s * kkn
            for m in range(RW_VM):
                y_ref[t, m] = acc_y[m]
            sa_next += acc_s
        return tuple(sa_next)

    lax.fori_loop(0, tb, step, tuple(sa0))


def _rwkv_scan(ops_f, ops_b, v_f, v_b):
    t = ops_f.shape[0]
    tb = RW_STEP_BLOCK
    nblk = t // tb
    fwd = lambda i: (i, 0, 0, 0)
    bwd = lambda i: (nblk - 1 - i, 0, 0, 0)
    ops_block = (tb, 5, RW_HEAD, LANES)
    v_block = (tb, RW_VM, SUBLANES, LANES)
    y_shape = jax.ShapeDtypeStruct((t, RW_VM, SUBLANES, LANES), F32)
    return pl.pallas_call(
        _rwkv_kernel,
        grid=(nblk,),
        in_specs=[pl.BlockSpec(ops_block, fwd), pl.BlockSpec(ops_block, bwd),
                  pl.BlockSpec(v_block, fwd), pl.BlockSpec(v_block, bwd)],
        out_specs=[pl.BlockSpec(v_block, fwd), pl.BlockSpec(v_block, bwd)],
        out_shape=[y_shape, y_shape],
        scratch_shapes=[pltpu.VMEM((2, RW_HEAD, RW_VM, SUBLANES, LANES), F32)],
        compiler_params=pltpu.CompilerParams(
            dimension_semantics=("arbitrary",), vmem_limit_bytes=VMEM_LIMIT),
        name="rwkv_scan",
    )(ops_f, ops_b, v_f, v_b)


def _token_shift(x, mu, backward):
    if backward:
        prev = jnp.pad(x, ((0, 0), (0, 1), (0, 0)))[:, 1:]
    else:
        prev = jnp.pad(x, ((0, 0), (1, 0), (0, 0)))[:, :-1]
    return x + (prev - x) * mu


def _heads(t, n_heads):
    return t.reshape(t.shape[:-1] + (n_heads, t.shape[-1] // n_heads))


def _flip(t):
    return jnp.flip(t, axis=1)


def _l2norm(x):
    return x * lax.rsqrt(jnp.sum(x * x, axis=-1, keepdims=True) + EPS)


def _rw_features(p_rkv, p_wa, backward, mu_rkv, mu_wa, w0, w2, a0, a2, k_k, k_a):
    x_rkv = _token_shift(p_rkv, mu_rkv, backward)
    x_wa = _token_shift(p_wa, mu_wa, backward)
    r, k, v = jnp.split(x_rkv, 3, axis=-1)
    w_lo, a_lo = jnp.split(x_wa, [RW_DECAY_LR], axis=-1)
    decay = jnp.exp(-RW_DECAY_SCALE * jax.nn.sigmoid(w0 + jnp.tanh(w_lo) @ w2))
    a = jax.nn.sigmoid(a0 + a_lo @ a2)
    kk = _l2norm(_heads(k * k_k, RW_HEADS))
    k = k * (1.0 + (a - 1.0) * k_a)
    r, k, v, a, decay = (_heads(t, RW_HEADS) for t in (r, k, v, a, decay))
    return r, decay, k, v, kk, kk * a


def _rw_mixer(p_c, p_x, lp):
    b = p_x[0].shape[0]
    assert b * RW_HEADS == RW_CHAINS, "chain-on-lanes layout needs batch * heads == 32"
    tc, tx = p_c[0].shape[1], p_x[0].shape[1]
    tt = tc + tx
    feats = []
    for i in range(2):
        feats.append([_rw_features(rkv, wa, i == 1, lp['rw_mu_rkv'][i], lp['rw_mu_wa'][i], lp['rw_w0'][i],
                                   lp['rw_w2'][i], lp['rw_a0'][i], lp['rw_a2'][i], lp['rw_kk'], lp['rw_ka'])
                      for (rkv, wa, _) in (p_c, p_x)])

    def timeline(i, idx):
        c_part, x_part = feats[i][0][idx], feats[i][1][idx]
        return jnp.concatenate([c_part, x_part] if i == 0 else [x_part, c_part], axis=1)

    def k_rows(i, idx):
        x = jnp.transpose(timeline(i, idx), (1, 3, 0, 2)).reshape(tt, RW_HEAD, RW_CHAINS)
        return jnp.repeat(x, RW_VLO, axis=-1)

    def v_tiles(i, idx):
        x = timeline(i, idx).reshape(b, tt, RW_HEADS, RW_VM, SUBLANES, RW_VLO)
        return jnp.transpose(x, (1, 3, 4, 0, 2, 5)).reshape(tt, RW_VM, SUBLANES, LANES)

    r_i, w_i, k_i, v_i, kk_i, b_i = range(6)
    ops = [jnp.stack([k_rows(i, j) for j in (w_i, b_i, k_i, r_i, kk_i)], axis=1) for i in range(2)]
    y_f, y_b = _rwkv_scan(ops[0], ops[1], v_tiles(0, v_i), v_tiles(1, v_i))

    def token_major(y):
        y = y.reshape(tt, RW_VM, SUBLANES, b, RW_HEADS, RW_VLO)
        return jnp.transpose(y, (3, 0, 4, 1, 2, 5)).reshape(b, tt, RW_HEADS, RW_HEAD)

    y_f, y_b = token_major(y_f), token_major(y_b)
    outs = []
    for s, (sl_f, sl_b, p) in enumerate(((slice(0, tc), slice(tx, tt), p_c), (slice(tc, tt), slice(0, tx), p_x))):
        bo = []
        for i in range(2):
            r, _, k, v = feats[i][s][:4]
            bo.append(jnp.sum(r * k * lp['rw_rk'], axis=-1, keepdims=True) * v)
        yy = y_f[:, sl_f] + y_b[:, sl_b]
        mu = jnp.mean(yy, axis=-1, keepdims=True)
        var = jnp.mean(jnp.square(yy - mu), axis=-1, keepdims=True)
        yn = (yy - mu) * lax.rsqrt(var + RW_GN_EPS)
        bt = yy.shape[:2]
        y_rw = yn.reshape(bt + (GROUP,)) * lp['rw_gn_w'] + lp['rw_gn_b'] + (bo[0] + bo[1]).reshape(bt + (GROUP,))
        outs.append(y_rw * (jax.nn.sigmoid(p[2]) @ lp['rw_g2']))
    return outs


def _to_chunks(t):
    return t.reshape((t.shape[0], t.shape[1] // CHUNK, CHUNK) + t.shape[2:])


def _rms_heads(o, g):
    return o * lax.rsqrt(jnp.mean(o * o, axis=-1, keepdims=True) + EPS) * g


def _conv3_rows(x, w, row_len):
    b, t, ch = x.shape
    rows = t // row_len
    xp = jnp.pad(x.reshape(b, rows, row_len, ch), ((0, 0), (0, 0), (1, 1), (0, 0)))
    y = xp[:, :, :-2] * w[0] + xp[:, :, 1:-1] * w[1] + xp[:, :, 2:] * w[2]
    return y.reshape(b, t, ch)


def _gla_scan(q, k, v, log_f, s0):
    b, t, h, dv = v.shape
    q, k, v, log_f = (_to_chunks(a) for a in (q, k, v, log_f))
    cum = jnp.cumsum(log_f, axis=2)
    last = cum[:, :, -1:]
    q_dec = q * jnp.exp(cum)
    k_inv = k * jnp.exp(-cum)
    k_end = k * jnp.exp(last - cum)
    lower = jnp.tril(jnp.ones((CHUNK, CHUNK), bool))
    att = jnp.where(lower, jnp.einsum('bnihd,bnjhd->bnhij', q_dec, k_inv), 0.0)
    o_intra = jnp.einsum('bnhij,bnjhv->bnihv', att, v)
    d_state = jnp.einsum('bnjhd,bnjhv->nbhdv', k_end, v)
    chunk_decay = jnp.moveaxis(jnp.exp(last[:, :, 0]), 1, 0)

    def step(s, inp):
        dec, ds = inp
        return dec[..., None] * s + ds, s

    s_fin, s_start = lax.scan(step, s0, (chunk_decay, d_state))
    o_inter = jnp.einsum('bnihd,nbhdv->bnihv', q_dec, s_start)
    return (o_intra + o_inter).reshape(b, t, h, dv), s_fin


def _gdn_scan(q, k, v, log_a, beta, s0):
    b, t, h, dv = v.shape
    dk = q.shape[-1]
    hm = lambda a: jnp.moveaxis(_to_chunks(a), 3, 2)
    q, k, v, log_a, beta = (hm(a) for a in (q, k, v, log_a, beta))
    q = q * dk ** -0.5
    cum = jnp.cumsum(log_a, axis=-1)
    lower = jnp.tril(jnp.ones((CHUNK, CHUNK), bool))
    strict = jnp.tril(jnp.ones((CHUNK, CHUNK), bool), -1)
    decay = jnp.exp(jnp.where(lower, cum[..., :, None] - cum[..., None, :], -jnp.inf))
    kk = jnp.einsum('bnhid,bnhjd->bnhij', k, k)
    lmat = jnp.where(strict, beta[..., None] * kk * decay, 0.0) + jnp.eye(CHUNK, dtype=kk.dtype)
    rhs = jnp.concatenate([v * beta[..., None], k * (beta * jnp.exp(cum))[..., None]], axis=-1)
    sol = lax.linalg.triangular_solve(lmat, rhs, left_side=True, lower=True, unit_diagonal=True)
    u, w = sol[..., :dv], sol[..., dv:]
    a_qk = jnp.einsum('bnhid,bnhjd->bnhij', q, k) * decay
    k_end = k * jnp.exp(cum[..., -1:] - cum)[..., None]
    chunk_decay = jnp.exp(cum[..., -1])
    cm = lambda a: jnp.moveaxis(a, 1, 0)

    def step(s, inp):
        u_i, w_i, ke_i, dec_i = inp
        v_new = u_i - jnp.einsum('bhcd,bhdv->bhcv', w_i, s)
        s_next = dec_i[..., None, None] * s + jnp.einsum('bhcd,bhcv->bhdv', ke_i, v_new)
        return s_next, (s, v_new)

    s_fin, (s_start, v_new) = lax.scan(step, s0, (cm(u), cm(w), cm(k_end), cm(chunk_decay)))
    o = (jnp.einsum('bnhcd,nbhdv->bnhcv', q * jnp.exp(cum)[..., None], s_start)
         + jnp.einsum('bnhij,nbhjv->bnhiv', a_qk, v_new))
    return jnp.moveaxis(o, 2, 3).reshape(b, t, h, dv), s_fin


def _seg(proj, name):
    o, wd, _ = SEG[name]
    return proj[..., o:o + wd]


TOKEN_BLOCK = 256
HI = lax.Precision.HIGHEST


def _dot(a, b, precision=None):
    return jnp.dot(a, b, preferred_element_type=F32, precision=precision)


def _dot_nt(a, b):
    return lax.dot_general(a, b, (((1,), (1,)), ((), ())), preferred_element_type=F32)


def _dot_tn(a, b):
    return lax.dot_general(a, b, (((0,), (0,)), ((), ())), preferred_element_type=F32)


def _split_bf16(x):
    hi = x.astype(BF16)
    return hi, (x - hi.astype(F32)).astype(BF16)


def _dot3(a_hi, a_lo, b_hi, b_lo):
    return _dot(a_hi, b_hi) + (_dot(a_hi, b_lo) + _dot(a_lo, b_hi))


def _silu(x):
    return x * jax.nn.sigmoid(x)


def _chunk_masks(reverse):
    ri = lax.broadcasted_iota(jnp.int32, (CHUNK, CHUNK), 0)
    ci = lax.broadcasted_iota(jnp.int32, (CHUNK, CHUNK), 1)
    incl = (ci >= ri) if reverse else (ci <= ri)
    strict = (ci > ri) if reverse else (ci < ri)
    return incl, strict, (ri == ci).astype(F32)


def _conv3_block(x, cw, row_len):
    tb = x.shape[0]
    pos = lax.broadcasted_iota(jnp.int32, (tb, 1), 0) % row_len
    x_prev = jnp.where(pos == 0, 0.0, pltpu.roll(x, 1, 0))
    x_next = jnp.where(pos == row_len - 1, 0.0, pltpu.roll(x, tb - 1, 0))
    return x_prev * cw[0:1] + x * cw[1:2] + x_next * cw[2:3]


def _gdn_kernel(*refs, reverse, finalize, row_len, dirn):
    if finalize:
        (qkv_ref, ab_ref, z_ref, ob_ref, cw_ref, alog_ref, dtb_ref, g_ref, s0_ref,
         o_ref, sfin_ref, s_scr) = refs
    else:
        qkv_ref, ab_ref, cw_ref, alog_ref, dtb_ref, s0_ref, o_ref, sfin_ref, s_scr = refs
    n = pl.program_id(1)

    @pl.when(n == 0)
    def _():
        s_scr[...] = s0_ref[0]

    tb = qkv_ref.shape[1]
    qkv = _silu(_conv3_block(qkv_ref[0], cw_ref[...], row_len))
    ab = ab_ref[0]
    log_a = -jnp.exp(alog_ref[...]) * jax.nn.softplus(ab + dtb_ref[...])
    beta = jax.nn.sigmoid(ab)
    incl, strict, eye = _chunk_masks(reverse)
    tri = incl.astype(F32)
    last = 0 if reverse else CHUNK - 1
    hd = GDN_DK
    nchunk = tb // CHUNK
    order = list(range(nchunk - 1, -1, -1) if reverse else range(nchunk))
    rows = lambda c: slice(c * CHUNK, (c + 1) * CHUNK)
    cum_all = _dot(tri, jnp.concatenate([log_a[rows(c)] for c in range(nchunk)], axis=1), HI)

    items = [(c, h) for c in order for h in range(GDN_HEADS)]
    pre = {}
    for c, h in items:
        col = dirn * GDN_HEADS + h
        cum = cum_all[:, c * LANES + col:c * LANES + col + 1]
        bet = beta[rows(c), 2 * GDN_HEADS + col:2 * GDN_HEADS + col + 1]
        q = _l2norm(qkv[rows(c), h * hd:(h + 1) * hd]) * hd ** -0.5
        k = _l2norm(qkv[rows(c), GROUP + h * hd:GROUP + (h + 1) * hd])
        v = qkv[rows(c), 2 * GROUP + h * hd:2 * GROUP + (h + 1) * hd]
        cum_row = jnp.sum(eye * cum, axis=0, keepdims=True)
        decay = jnp.exp(jnp.where(incl, cum - cum_row, -jnp.inf))
        kb = k.astype(BF16)
        a = jnp.where(strict, bet * _dot_nt(kb, kb) * decay, 0.0)
        ecum = jnp.exp(cum)
        total = cum[last:last + 1]
        pre[c, h] = dict(
            x=-a, rhs=jnp.concatenate([v * bet, k * (bet * ecum)], axis=1),
            a_qk=(_dot_nt(q.astype(BF16), kb) * decay).astype(BF16),
            qe=(q * ecum).astype(BF16), k_end=(k * jnp.exp(total - cum)).astype(BF16),
            dec=jnp.exp(total))
    for p in pre.values():
        p['inv'] = eye + p['x']
    for level in range(6):
        for p in pre.values():
            x_hi, x_lo = _split_bf16(p['x'])
            if level == 0:
                p['x'] = _dot3(x_hi, x_lo, x_hi, x_lo)
            elif level < 5:
                i_hi, i_lo = _split_bf16(p['inv'])
                both = _dot3(jnp.concatenate([x_hi, i_hi], axis=0), jnp.concatenate([x_lo, i_lo], axis=0),
                             x_hi, x_lo)
                p['x'] = both[:CHUNK]
                p['inv'] = p['inv'] + both[CHUNK:]
            else:
                i_hi, i_lo = _split_bf16(p['inv'])
                p['inv'] = p['inv'] + _dot3(i_hi, i_lo, x_hi, x_lo)
    for p in pre.values():
        i_hi, i_lo = _split_bf16(p['inv'])
        r_hi, r_lo = _split_bf16(p['rhs'])
        sol = _dot3(i_hi, i_lo, r_hi, r_lo)
        p['u'], p['w'] = sol[:, :GDN_DV], sol[:, GDN_DV:].astype(BF16)

    for c, h in items:
        p = pre[c, h]
        s = s_scr[h]
        sb = s.astype(BF16)
        v_new = p['u'] - _dot(p['w'], sb)
        vb = v_new.astype(BF16)
        o = _dot(p['qe'], sb) + _dot(p['a_qk'], vb)
        s_scr[h] = p['dec'] * s + _dot_tn(p['k_end'], vb)
        cols = slice(h * GDN_DV, (h + 1) * GDN_DV)
        if finalize:
            o = o + ob_ref[0, rows(c), cols]
            o = o * lax.rsqrt(jnp.mean(o * o, axis=-1, keepdims=True) + EPS) * g_ref[...]
            o = o * _silu(z_ref[0, rows(c), cols])
        o_ref[0, rows(c), cols] = o.astype(o_ref.dtype)

    @pl.when(n == pl.num_programs(1) - 1)
    def _():
        sfin_ref[0] = s_scr[...]


def _gdn_pass(proj, o_other, s0, lp, *, dirn, row_len):
    b, t, _ = proj.shape
    tb = min(TOKEN_BLOCK, t)
    assert t % tb == 0 and tb % row_len == 0
    nblk = t // tb
    reverse = dirn == 1
    finalize = o_other is not None
    tok = (lambda i, n: (i, nblk - 1 - n)) if reverse else (lambda i, n: (i, n))
    seg = lambda name, wd: pl.BlockSpec((1, tb, wd), lambda i, n: tok(i, n) + (_col_block(name),))
    full = lambda a: pl.BlockSpec(a.shape, lambda i, n: (0,) * a.ndim)
    lane_row = lambda vals: jnp.pad(vals.reshape(1, -1), ((0, 0), (0, LANES - vals.size)))
    cw = lp['gdn_conv']
    alog = lane_row(lp['gdn_a_log'])
    dtb = lane_row(lp['gdn_dt_bias'])
    g = lp['gdn_norm_g'].reshape(1, GDN_DV)
    state_spec = pl.BlockSpec((1, GDN_HEADS, GDN_DK, GDN_DV), lambda i, n: (i, 0, 0, 0))
    if finalize:
        args = (proj, proj, proj, o_other, cw, alog, dtb, g, s0)
        in_specs = [seg('d_qkv', 3 * GROUP), seg('d_ab', LANES), seg('d_z', GROUP),
                    pl.BlockSpec((1, tb, GROUP), lambda i, n: tok(i, n) + (0,)),
                    full(cw), full(alog), full(dtb), full(g), state_spec]
    else:
        args = (proj, proj, cw, alog, dtb, s0)
        in_specs = [seg('d_qkv', 3 * GROUP), seg('d_ab', LANES), full(cw), full(alog), full(dtb), state_spec]
    return pl.pallas_call(
        functools.partial(_gdn_kernel, reverse=reverse, finalize=finalize, row_len=row_len, dirn=dirn),
        grid=(b, nblk),
        in_specs=in_specs,
        out_specs=[pl.BlockSpec((1, tb, GROUP), lambda i, n: tok(i, n) + (0,)), state_spec],
        out_shape=[jax.ShapeDtypeStruct((b, t, GROUP), BF16 if finalize else F32),
                   jax.ShapeDtypeStruct((b, GDN_HEADS, GDN_DK, GDN_DV), F32)],
        scratch_shapes=[pltpu.VMEM((GDN_HEADS, GDN_DK, GDN_DV), F32)],
        compiler_params=pltpu.CompilerParams(
            dimension_semantics=("arbitrary", "arbitrary"), vmem_limit_bytes=VMEM_LIMIT),
        name="gdn_fwd" if finalize else "gdn_bwd",
    )(*args)


def _gla_kernel(*refs, reverse, finalize):
    if finalize:
        (q_ref, k_ref, v_ref, lo_ref, r_ref, ob_ref, up_ref, ab_ref, g_ref, s0_ref,
         o_ref, sfin_ref, s_scr) = refs
    else:
        q_ref, k_ref, v_ref, lo_ref, up_ref, ab_ref, s0_ref, o_ref, sfin_ref, s_scr = refs
    n = pl.program_id(1)

    @pl.when(n == 0)
    def _():
        s_scr[...] = s0_ref[0]

    tb = q_ref.shape[1]
    gate = _dot(lo_ref[0].astype(BF16), up_ref[...].astype(BF16)) + ab_ref[...]
    log_f = jax.nn.log_sigmoid(gate) / GLA_TAU
    incl, _, eye = _chunk_masks(reverse)
    tri = incl.astype(F32)
    last = 0 if reverse else CHUNK - 1
    nchunk = tb // CHUNK
    order = range(nchunk - 1, -1, -1) if reverse else range(nchunk)
    for c in order:
        sl = slice(c * CHUNK, (c + 1) * CHUNK)
        cum = _dot(tri, log_f[sl], HI)
        total = cum[last:last + 1]
        q_dec = q_ref[0, sl, :] * GLA_DK ** -0.5 * jnp.exp(cum)
        k = k_ref[0, sl, :]
        k_inv = (k * jnp.exp(-cum)).astype(BF16)
        k_end = (k * jnp.exp(total - cum)).astype(BF16)
        q_dec = q_dec.astype(BF16)
        dec_row = jnp.exp(total)
        for h in range(GLA_HEADS):
            kc = slice(h * GLA_DK, (h + 1) * GLA_DK)
            vc = slice(h * GLA_DV, (h + 1) * GLA_DV)
            vb = v_ref[0, sl, vc].astype(BF16)
            att = jnp.where(incl, _dot_nt(q_dec[:, kc], k_inv[:, kc]), 0.0)
            s = s_scr[h]
            o = _dot(att.astype(BF16), vb) + _dot(q_dec[:, kc], s.astype(BF16))
            dec_col = jnp.sum(eye * dec_row[:, kc], axis=1, keepdims=True)
            s_scr[h] = dec_col * s + _dot_tn(k_end[:, kc], vb)
            if finalize:
                o = o + ob_ref[0, sl, vc]
                o = o * lax.rsqrt(jnp.mean(o * o, axis=-1, keepdims=True) + EPS) * g_ref[...]
                o = o * _silu(r_ref[0, sl, vc])
            o_ref[0, sl, vc] = o.astype(o_ref.dtype)

    @pl.when(n == pl.num_programs(1) - 1)
    def _():
        sfin_ref[0] = s_scr[...]


def _gla_pass(proj, o_other, s0, lp, *, dirn):
    b, t, _ = proj.shape
    tb = min(TOKEN_BLOCK, t)
    assert t % tb == 0
    nblk = t // tb
    reverse = dirn == 1
    finalize = o_other is not None
    tok = (lambda i, n: (i, nblk - 1 - n)) if reverse else (lambda i, n: (i, n))
    seg = lambda name: pl.BlockSpec((1, tb, SEG[name][2]), lambda i, n: tok(i, n) + (_col_block(name),))
    full = lambda a: pl.BlockSpec(a.shape, lambda i, n: (0,) * a.ndim)
    hk = GLA_HEADS * GLA_DK
    up = jnp.pad(lp['gla_a_up'][dirn], ((0, LANES - GLA_LR), (0, 0)))
    ab = lp['gla_a_b'][dirn].reshape(1, hk)
    g = lp['gla_norm_g'].reshape(1, GLA_DV)
    state_spec = pl.BlockSpec((1, GLA_HEADS, GLA_DK, GLA_DV), lambda i, n: (i, 0, 0, 0))
    if finalize:
        args = (proj, proj, proj, proj, proj, o_other, up, ab, g, s0)
        in_specs = [seg('g_q'), seg('g_k'), seg('g_v'), seg('g_lo'), seg('g_r'),
                    pl.BlockSpec((1, tb, GROUP), lambda i, n: tok(i, n) + (0,)),
                    full(up), full(ab), full(g), state_spec]
    else:
        args = (proj, proj, proj, proj, up, ab, s0)
        in_specs = [seg('g_q'), seg('g_k'), seg('g_v'), seg('g_lo'), full(up), full(ab), state_spec]
    return pl.pallas_call(
        functools.partial(_gla_kernel, reverse=reverse, finalize=finalize),
        grid=(b, nblk),
        in_specs=in_specs,
        out_specs=[pl.BlockSpec((1, tb, GROUP), lambda i, n: tok(i, n) + (0,)), state_spec],
        out_shape=[jax.ShapeDtypeStruct((b, t, GROUP), BF16 if finalize else F32),
                   jax.ShapeDtypeStruct((b, GLA_HEADS, GLA_DK, GLA_DV), F32)],
        scratch_shapes=[pltpu.VMEM((GLA_HEADS, GLA_DK, GLA_DV), F32)],
        compiler_params=pltpu.CompilerParams(
            dimension_semantics=("arbitrary", "arbitrary"), vmem_limit_bytes=VMEM_LIMIT),
        name="gla_fwd" if finalize else "gla_bwd",
    )(*args)


def _two_direction_group(pass_fn, state_shape, proj_c, proj_x, **kw):
    zero = jnp.zeros((proj_x[0].shape[0],) + state_shape, F32)
    ob_c, sb_c = pass_fn(proj_c[0], None, zero, dirn=1, **proj_c[1], **kw)
    ob_x, _ = pass_fn(proj_x[0], None, sb_c, dirn=1, **proj_x[1], **kw)
    y_c, sf_c = pass_fn(proj_c[0], ob_c, zero, dirn=0, **proj_c[1], **kw)
    y_x, _ = pass_fn(proj_x[0], ob_x, sf_c, dirn=0, **proj_x[1], **kw)
    return y_c, y_x


def _gla_group_pallas(proj_c, proj_x, lp):
    return _two_direction_group(_gla_pass, (GLA_HEADS, GLA_DK, GLA_DV), (proj_c, {}), (proj_x, {}), lp=lp)


def _gdn_group_pallas(proj_c, proj_x, lp):
    return _two_direction_group(_gdn_pass, (GDN_HEADS, GDN_DK, GDN_DV),
                                (proj_c, dict(row_len=proj_c.shape[1])), (proj_x, dict(row_len=GRID_W)), lp=lp)


def _gla_group(proj, init, lp):
    g_q, g_k, g_v, g_r, g_lo = (_seg(proj, n) for n in ('g_q', 'g_k', 'g_v', 'g_r', 'g_lo'))
    q = _heads(g_q, GLA_HEADS) * GLA_DK ** -0.5
    k = _heads(g_k, GLA_HEADS)
    v = _heads(g_v, GLA_HEADS)
    log_f = [_heads(jax.nn.log_sigmoid(g_lo @ lp['gla_a_up'][i] + lp['gla_a_b'][i]) / GLA_TAU, GLA_HEADS)
             for i in range(2)]
    o_f, s_f = _gla_scan(q, k, v, log_f[0], init[0])
    o_b, s_b = _gla_scan(_flip(q), _flip(k), _flip(v), _flip(log_f[1]), init[1])
    y = _rms_heads(o_f + _flip(o_b), lp['gla_norm_g']) * jax.nn.silu(_heads(g_r, GLA_HEADS))
    return y.reshape(y.shape[:2] + (GROUP,)), (s_f, s_b)


def _gdn_group(proj, row_len, init, lp):
    b, t = proj.shape[:2]
    d_qkv, d_z, d_ab = (_seg(proj, n) for n in ('d_qkv', 'd_z', 'd_ab'))
    d_a, d_b = d_ab[..., :2 * GDN_HEADS], d_ab[..., 2 * GDN_HEADS:]
    qkv = jax.nn.silu(_conv3_rows(d_qkv, lp['gdn_conv'], row_len))
    q, k, v = (_heads(a, GDN_HEADS) for a in jnp.split(qkv, 3, axis=-1))
    q, k = _l2norm(q), _l2norm(k)
    log_a = -jnp.exp(lp['gdn_a_log']) * jax.nn.softplus(d_a.reshape(b, t, 2, GDN_HEADS) + lp['gdn_dt_bias'])
    beta = jax.nn.sigmoid(d_b.reshape(b, t, 2, GDN_HEADS))
    o_f, s_f = _gdn_scan(q, k, v, log_a[:, :, 0], beta[:, :, 0], init[0])
    o_b, s_b = _gdn_scan(_flip(q), _flip(k), _flip(v), _flip(log_a[:, :, 1]), _flip(beta[:, :, 1]), init[1])
    y = _rms_heads(o_f + _flip(o_b), lp['gdn_norm_g']) * jax.nn.silu(_heads(d_z, GDN_HEADS))
    return y.reshape(b, t, GROUP), (s_f, s_b)


def _sc_group(proj, row_len, lp):
    c_b, c_c, c_h = (_seg(proj, n) for n in ('c_b', 'c_c', 'c_h'))
    return c_b * _conv3_rows(c_c * c_h, lp['sc_conv'], row_len)


def _mixers(proj_c, proj_x, lp):
    b = proj_x.shape[0]
    z = lambda h, d1, d2: jnp.zeros((b, h, d1, d2), F32)
    gla0 = (z(GLA_HEADS, GLA_DK, GLA_DV),) * 2
    gdn0 = (z(GDN_HEADS, GDN_DK, GDN_DV),) * 2
    tc = proj_c.shape[1]
    y_gla_c, y_gla_x = _gla_group_pallas(proj_c, proj_x, lp)
    y_gdn_c, y_gdn_x = _gdn_group_pallas(proj_c, proj_x, lp)
    y_sc_c = _sc_group(proj_c, tc, lp)
    y_sc_x = _sc_group(proj_x, GRID_W, lp)
    rw = lambda p: tuple(_seg(p, n) for n in ('r_rkv', 'r_wa', 'r_g'))
    y_rw_c, y_rw_x = _rw_mixer(rw(proj_c), rw(proj_x), lp)
    y_c = jnp.concatenate([y_gla_c, y_gdn_c, y_sc_c, y_rw_c], axis=-1).astype(BF16)
    y_x = jnp.concatenate([y_gla_x, y_gdn_x, y_sc_x, y_rw_x], axis=-1).astype(BF16)
    return y_c, y_x


def kernel(x, c, ctx, c_ctx, ada_w, ada_b, norm1_g, norm2_g, w_in, w_out, gla_a_up, gla_a_b, gla_norm_g, gdn_conv, gdn_a_log, gdn_dt_bias, gdn_norm_g, sc_conv, rw_mu_rkv, rw_mu_wa, rw_w0, rw_w2, rw_a0, rw_a2, rw_g2, rw_kk, rw_ka, rw_rk, rw_gn_w, rw_gn_b, ffn_w_gu, ffn_w_down, final_g):
    nb, t, d = x.shape
    tc = ctx.shape[1]
    depth = w_in.shape[0]
    assert d == D_MODEL and t % 512 == 0 and tc % CHUNK == 0 and nb + 1 <= SUBLANES

    cond = jnp.concatenate([c, c_ctx[None, :], jnp.zeros((SUBLANES - nb - 1, d), F32)], axis=0)
    mod = _modulation(cond, ada_w, ada_b)

    tm_x = 512
    tm_c = tc
    h_x, h_c = x, ctx
    for l in range(depth):
        lp = dict(gla_a_up=gla_a_up[l], gla_a_b=gla_a_b[l],
                  gla_norm_g=gla_norm_g[l], gdn_conv=gdn_conv[l], gdn_a_log=gdn_a_log[l],
                  gdn_dt_bias=gdn_dt_bias[l], gdn_norm_g=gdn_norm_g[l], sc_conv=sc_conv[l],
                  rw_mu_rkv=rw_mu_rkv[l], rw_mu_wa=rw_mu_wa[l], rw_w0=rw_w0[l], rw_w2=rw_w2[l],
                  rw_a0=rw_a0[l], rw_a2=rw_a2[l], rw_g2=rw_g2[l], rw_kk=rw_kk[l], rw_ka=rw_ka[l],
                  rw_rk=rw_rk[l], rw_gn_w=rw_gn_w[l], rw_gn_b=rw_gn_b[l])
        m_x = [mod[l, :nb, i * d:(i + 1) * d][:, None, :] for i in range(6)]
        m_c = [jnp.broadcast_to(mod[l, nb, i * d:(i + 1) * d][None, None, :], (nb, 1, d)) for i in range(6)]
        w_in_l = _pad_w_in(w_in[l])
        w_out_l = w_out[l].astype(BF16)
        w_gu_l = ffn_w_gu[l].astype(BF16)
        w_dn_l = ffn_w_down[l].astype(BF16)

        proj_c = _norm_proj(h_c, norm1_g[l], m_c[1], m_c[0], w_in_l, tm_c, 1024)
        proj_x = _norm_proj(h_x, norm1_g[l], m_x[1], m_x[0], w_in_l, tm_x, 1024)
        y_c, y_x = _mixers(proj_c, proj_x, lp)

        h_x = _proj_residual(y_x, w_out_l, h_x, m_x[2], tm_x, 512)
        a_x = _norm_swiglu(h_x, norm2_g[l], m_x[4], m_x[3], w_gu_l, tm_x, 512)
        h_x = _proj_residual(a_x, w_dn_l, h_x, m_x[5], tm_x, 512)
        if l < depth - 1:
            h_c = _proj_residual(y_c, w_out_l, h_c, m_c[2], tm_c, 512)
            a_c = _norm_swiglu(h_c, norm2_g[l], m_c[4], m_c[3], w_gu_l, tm_c, 512)
            h_c = _proj_residual(a_c, w_dn_l, h_c, m_c[5], tm_c, 512)
    return _final_norm(h_x, final_g, tm_x)
```

```python
import functools
import math

import numpy as np
import jax
import jax.numpy as jnp
from jax import lax
from jax.experimental import pallas as pl
from jax.experimental.pallas import tpu as pltpu

F32 = jnp.float32
BF16 = jnp.bfloat16

D_MODEL = 2048
GROUP = D_MODEL // 4
CHUNK = 64
EPS = 1e-6
GRID_W = 64

GLA_HEADS = 4
GLA_DK = 64
GLA_DV = 128
GLA_LR = 16
GLA_TAU = 16.0
GDN_HEADS = 4
GDN_DK = 128
GDN_DV = 128
RW_HEAD = 64
RW_HEADS = GROUP // RW_HEAD
RW_DECAY_LR = 32
RW_A_LR = 32
RW_GATE_LR = 96
RW_DECAY_SCALE = math.exp(-0.5)
RW_GN_EPS = 64e-5
FFN = -(-(8 * D_MODEL) // (3 * 256)) * 256

SUBLANES = 8
LANES = 128
VMEM_LIMIT = 48 * 1024 * 1024

_SRC_SEGS = (
    ('g_q', 256), ('g_k', 256), ('g_v', 512), ('g_r', 512), ('g_lo', GLA_LR),
    ('d_qkv', 1536), ('d_z', 512), ('d_ab', 16),
    ('c_b', 512), ('c_c', 512), ('c_h', 512),
    ('r_rkv', 1536), ('r_wa', 64), ('r_g', RW_GATE_LR),
)
_DST_ORDER = ('g_q', 'g_k', 'g_v', 'g_r', 'd_qkv', 'r_rkv', 'd_z', 'c_b', 'c_c', 'c_h',
              'g_lo', 'd_ab', 'r_wa', 'r_g')


def _round_up(n, m):
    return -(-n // m) * m


def _seg_layout():
    width = dict(_SRC_SEGS)
    src, o = {}, 0
    for name, wd in _SRC_SEGS:
        src[name] = o
        o += wd
    dst, o = {}, 0
    for name in _DST_ORDER:
        pw = _round_up(width[name], LANES)
        assert o % pw == 0, name
        dst[name] = (o, width[name], pw)
        o += pw
    return src, dst, o


SEG_SRC, SEG, PROJ_PAD = _seg_layout()


def _pad_w_in(w):
    parts = []
    for name in _DST_ORDER:
        _, wd, pw = SEG[name]
        seg = w[:, SEG_SRC[name]:SEG_SRC[name] + wd]
        if pw != wd:
            seg = jnp.pad(seg, ((0, 0), (0, pw - wd)))
        parts.append(seg)
    return jnp.concatenate(parts, axis=1).astype(BF16)


def _col_block(name):
    o, _, pw = SEG[name]
    return o // pw


def _mod_kernel(c_ref, w_ref, b_ref, o_ref):
    x = c_ref[...]
    x = (x * jax.nn.sigmoid(x)).astype(BF16)
    o_ref[0] = jnp.dot(x, w_ref[0].astype(BF16), preferred_element_type=F32) + b_ref[0]


def _modulation(cond, ada_w, ada_b):
    nl, d, n = ada_w.shape
    tn = 1024
    return pl.pallas_call(
        _mod_kernel,
        grid=(nl, n // tn),
        in_specs=[pl.BlockSpec((SUBLANES, d), lambda l, j: (0, 0)),
                  pl.BlockSpec((1, d, tn), lambda l, j: (l, 0, j)),
                  pl.BlockSpec((1, 1, tn), lambda l, j: (l, 0, j))],
        out_specs=pl.BlockSpec((1, SUBLANES, tn), lambda l, j: (l, 0, j)),
        out_shape=jax.ShapeDtypeStruct((nl, SUBLANES, n), F32),
        compiler_params=pltpu.CompilerParams(
            dimension_semantics=("arbitrary", "arbitrary"), vmem_limit_bytes=VMEM_LIMIT),
        name="modulation",
    )(cond, ada_w, ada_b.reshape(nl, 1, n))


def _modnorm_rows(h_ref, g_ref, sc_ref, sh_ref):
    x = h_ref[0]
    y = x * lax.rsqrt(jnp.mean(x * x, axis=-1, keepdims=True) + EPS) * g_ref[...]
    return (y * (1.0 + sc_ref[0]) + sh_ref[0]).astype(BF16)


def _norm_proj_kernel(h_ref, g_ref, sc_ref, sh_ref, w_ref, o_ref, a_scr):
    @pl.when(pl.program_id(2) == 0)
    def _():
        a_scr[...] = _modnorm_rows(h_ref, g_ref, sc_ref, sh_ref)

    o_ref[0] = jnp.dot(a_scr[...], w_ref[...], preferred_element_type=F32).astype(o_ref.dtype)


def _norm_proj(h, g, scale, shift, w, tm, tn):
    b, t, d = h.shape
    n = w.shape[1]
    return pl.pallas_call(
        _norm_proj_kernel,
        grid=(b, t // tm, n // tn),
        in_specs=[pl.BlockSpec((1, tm, d), lambda i, m, j: (i, m, 0)),
                  pl.BlockSpec((1, d), lambda i, m, j: (0, 0)),
                  pl.BlockSpec((1, 1, d), lambda i, m, j: (i, 0, 0)),
                  pl.BlockSpec((1, 1, d), lambda i, m, j: (i, 0, 0)),
                  pl.BlockSpec((d, tn), lambda i, m, j: (0, j))],
        out_specs=pl.BlockSpec((1, tm, tn), lambda i, m, j: (i, m, j)),
        out_shape=jax.ShapeDtypeStruct((b, t, n), F32),
        scratch_shapes=[pltpu.VMEM((tm, d), BF16)],
        compiler_params=pltpu.CompilerParams(
            dimension_semantics=("arbitrary", "arbitrary", "arbitrary"), vmem_limit_bytes=VMEM_LIMIT),
        name="norm_proj",
    )(h, g.reshape(1, d), scale, shift, w)


def _norm_swiglu_kernel(h_ref, g_ref, sc_ref, sh_ref, wg_ref, wu_ref, o_ref, a_scr):
    @pl.when(pl.program_id(2) == 0)
    def _():
        a_scr[...] = _modnorm_rows(h_ref, g_ref, sc_ref, sh_ref)

    a = a_scr[...]
    gate = jnp.dot(a, wg_ref[...], preferred_element_type=F32)
    up = jnp.dot(a, wu_ref[...], preferred_element_type=F32)
    o_ref[0] = (gate * jax.nn.sigmoid(gate) * up).astype(o_ref.dtype)


def _norm_swiglu(h, g, scale, shift, w_gu, tm, tn):
    b, t, d = h.shape
    f = w_gu.shape[1] // 2
    nj = f // tn
    return pl.pallas_call(
        _norm_swiglu_kernel,
        grid=(b, t // tm, nj),
        in_specs=[pl.BlockSpec((1, tm, d), lambda i, m, j: (i, m, 0)),
                  pl.BlockSpec((1, d), lambda i, m, j: (0, 0)),
                  pl.BlockSpec((1, 1, d), lambda i, m, j: (i, 0, 0)),
                  pl.BlockSpec((1, 1, d), lambda i, m, j: (i, 0, 0)),
                  pl.BlockSpec((d, tn), lambda i, m, j: (0, j)),
                  pl.BlockSpec((d, tn), lambda i, m, j: (0, j + nj))],
        out_specs=pl.BlockSpec((1, tm, tn), lambda i, m, j: (i, m, j)),
        out_shape=jax.ShapeDtypeStruct((b, t, f), BF16),
        scratch_shapes=[pltpu.VMEM((tm, d), BF16)],
        compiler_params=pltpu.CompilerParams(
            dimension_semantics=("arbitrary", "arbitrary", "arbitrary"), vmem_limit_bytes=VMEM_LIMIT),
        name="norm_swiglu",
    )(h, g.reshape(1, d), scale, shift, w_gu, w_gu)


def _proj_residual_kernel(a_ref, w_ref, h_ref, gate_ref, o_ref):
    o_ref[0] = h_ref[0] + gate_ref[0] * jnp.dot(a_ref[0], w_ref[...], preferred_element_type=F32)


def _proj_residual(a, w, h, gate, tm, tn):
    b, t, k = a.shape
    d = w.shape[1]
    return pl.pallas_call(
        _proj_residual_kernel,
        grid=(b, t // tm, d // tn),
        in_specs=[pl.BlockSpec((1, tm, k), lambda i, m, j: (i, m, 0)),
                  pl.BlockSpec((k, tn), lambda i, m, j: (0, j)),
                  pl.BlockSpec((1, tm, tn), lambda i, m, j: (i, m, j)),
                  pl.BlockSpec((1, 1, tn), lambda i, m, j: (i, 0, j))],
        out_specs=pl.BlockSpec((1, tm, tn), lambda i, m, j: (i, m, j)),
        out_shape=jax.ShapeDtypeStruct((b, t, d), F32),
        compiler_params=pltpu.CompilerParams(
            dimension_semantics=("arbitrary", "arbitrary", "arbitrary"), vmem_limit_bytes=VMEM_LIMIT),
        name="proj_residual",
    )(a, w, h, gate)


def _final_norm_kernel(h_ref, g_ref, o_ref):
    x = h_ref[0]
    o_ref[0] = x * lax.rsqrt(jnp.mean(x * x, axis=-1, keepdims=True) + EPS) * g_ref[...]


def _final_norm(h, g, tm):
    b, t, d = h.shape
    return pl.pallas_call(
        _final_norm_kernel,
        grid=(b, t // tm),
        in_specs=[pl.BlockSpec((1, tm, d), lambda i, m: (i, m, 0)),
                  pl.BlockSpec((1, d), lambda i, m: (0, 0))],
        out_specs=pl.BlockSpec((1, tm, d), lambda i, m: (i, m, 0)),
        out_shape=jax.ShapeDtypeStruct((b, t, d), F32),
        compiler_params=pltpu.CompilerParams(
            dimension_semantics=("arbitrary", "arbitrary"), vmem_limit_bytes=VMEM_LIMIT),
        name="final_norm",
    )(h, g.reshape(1, d))


RW_CHAINS = 32
RW_KLO = LANES // RW_CHAINS
RW_KQ = RW_HEAD // RW_KLO
RW_VM = RW_HEAD // SUBLANES
RW_STEP_BLOCK = 16
OP_W, OP_B, OP_K, OP_R, OP_KK = range(5)


def _lane_group_sum(p):
    out = p
    for g in range(1, RW_KLO):
        out = out + pltpu.roll(p, g * RW_CHAINS, 1)
    return out


def _rwkv_kernel(opsf_ref, opsb_ref, vf_ref, vb_ref, yf_ref, yb_ref, s_ref):
    tb = opsf_ref.shape[0]
    dirs = ((opsf_ref, vf_ref, yf_ref), (opsb_ref, vb_ref, yb_ref))

    @pl.when(pl.program_id(0) == 0)
    def _():
        s_ref[...] = jnp.zeros_like(s_ref)

    zero = jnp.zeros((SUBLANES, LANES), F32)
    part0 = []
    for d, (ops_ref, _, _) in enumerate(dirs):
        first = 0 if d == 0 else tb - 1
        acc = [zero] * RW_VM
        for q in range(RW_KQ):
            kk = ops_ref[first, OP_KK, q:q + 1, :]
            for m in range(RW_VM):
                acc[m] = acc[m] + s_ref[d, q, m] * kk
        part0 += acc

    def advance(d, i, sa):
        ops_ref, v_ref, y_ref = dirs[d]
        t = i if d == 0 else tb - 1 - i
        tn = jnp.minimum(i + 1, tb - 1) if d == 0 else jnp.maximum(tb - 2 - i, 0)
        vt = [v_ref[t, m] for m in range(RW_VM)]
        acc_y = [zero] * RW_VM
        acc_s = [zero] * RW_VM
        for q in range(RW_KQ):
            w = ops_ref[t, OP_W, q:q + 1, :]
            b = ops_ref[t, OP_B, q:q + 1, :]
            kt = ops_ref[t, OP_K, q:q + 1, :]
            r = ops_ref[t, OP_R, q:q + 1, :]
            kkn = ops_ref[tn, OP_KK, q:q + 1, :]
            for m in range(RW_VM):
                s = s_ref[d, q, m] * w - sa[m] * b + vt[m] * kt
                s_ref[d, q, m] = s
                acc_y[m] = acc_y[m] + s * r
                acc_s[m] = acc_s[m] + s * kkn
        for m in range(RW_VM):
            y_ref[t, m] = acc_y[m]
        return acc_s

    def step(i, carry):
        sa_f, part_b = carry[:RW_VM], carry[RW_VM:]
        sa_b = [_lane_group_sum(p) for p in part_b]
        sa_f_next = [_lane_group_sum(p) for p in advance(0, i, sa_f)]
        return tuple(sa_f_next) + tuple(advance(1, i, sa_b))

    carry0 = [_lane_group_sum(p) for p in part0[:RW_VM]] + part0[RW_VM:]
    lax.fori_loop(0, tb, step, tuple(carry0))


def _rwkv_scan(ops_f, ops_b, v_f, v_b):
    t = ops_f.shape[0]
    tb = RW_STEP_BLOCK
    nblk = t // tb
    fwd = lambda i: (i, 0, 0, 0)
    bwd = lambda i: (nblk - 1 - i, 0, 0, 0)
    ops_block = (tb, 5, RW_KQ, LANES)
    v_block = (tb, RW_VM, SUBLANES, LANES)
    y_shape = jax.ShapeDtypeStruct((t, RW_VM, SUBLANES, LANES), F32)
    return pl.pallas_call(
        _rwkv_kernel,
        grid=(nblk,),
        in_specs=[pl.BlockSpec(ops_block, fwd), pl.BlockSpec(ops_block, bwd),
                  pl.BlockSpec(v_block, fwd), pl.BlockSpec(v_block, bwd)],
        out_specs=[pl.BlockSpec(v_block, fwd), pl.BlockSpec(v_block, bwd)],
        out_shape=[y_shape, y_shape],
        scratch_shapes=[pltpu.VMEM((2, RW_KQ, RW_VM, SUBLANES, LANES), F32)],
        compiler_params=pltpu.CompilerParams(
            dimension_semantics=("arbitrary",), vmem_limit_bytes=VMEM_LIMIT),
        name="rwkv_scan",
    )(ops_f, ops_b, v_f, v_b)


def _token_shift(x, mu, backward):
    if backward:
        prev = jnp.pad(x, ((0, 0), (0, 1), (0, 0)))[:, 1:]
    else:
        prev = jnp.pad(x, ((0, 0), (1, 0), (0, 0)))[:, :-1]
    return x + (prev - x) * mu


def _heads(t, n_heads):
    return t.reshape(t.shape[:-1] + (n_heads, t.shape[-1] // n_heads))


def _flip(t):
    return jnp.flip(t, axis=1)


def _l2norm(x):
    return x * lax.rsqrt(jnp.sum(x * x, axis=-1, keepdims=True) + EPS)


def _rw_features(p_rkv, p_wa, backward, mu_rkv, mu_wa, w0, w2, a0, a2, k_k, k_a):
    x_rkv = _token_shift(p_rkv, mu_rkv, backward)
    x_wa = _token_shift(p_wa, mu_wa, backward)
    r, k, v = jnp.split(x_rkv, 3, axis=-1)
    w_lo, a_lo = jnp.split(x_wa, [RW_DECAY_LR], axis=-1)
    decay = jnp.exp(-RW_DECAY_SCALE * jax.nn.sigmoid(w0 + jnp.tanh(w_lo) @ w2))
    a = jax.nn.sigmoid(a0 + a_lo @ a2)
    kk = _l2norm(_heads(k * k_k, RW_HEADS))
    k = k * (1.0 + (a - 1.0) * k_a)
    r, k, v, a, decay = (_heads(t, RW_HEADS) for t in (r, k, v, a, decay))
    return r, decay, k, v, kk, kk * a


def _rw_mixer(p_c, p_x, lp):
    b = p_x[0].shape[0]
    assert b * RW_HEADS == RW_CHAINS, "chain-on-lanes layout needs batch * heads == 32"
    tc, tx = p_c[0].shape[1], p_x[0].shape[1]
    tt = tc + tx
    feats = []
    for i in range(2):
        feats.append([_rw_features(rkv, wa, i == 1, lp['rw_mu_rkv'][i], lp['rw_mu_wa'][i], lp['rw_w0'][i],
                                   lp['rw_w2'][i], lp['rw_a0'][i], lp['rw_a2'][i], lp['rw_kk'], lp['rw_ka'])
                      for (rkv, wa, _) in (p_c, p_x)])

    def timeline(i, idx):
        c_part, x_part = feats[i][0][idx], feats[i][1][idx]
        return jnp.concatenate([c_part, x_part] if i == 0 else [x_part, c_part], axis=1)

    def k_rows(i, idxs):
        x = jnp.stack([timeline(i, j) for j in idxs], axis=0)
        x = x.reshape(len(idxs), b, tt, RW_HEADS, RW_KQ, RW_KLO)
        return jnp.transpose(x, (2, 0, 4, 5, 1, 3)).reshape(tt, len(idxs), RW_KQ, LANES)

    def v_tiles(i, idx):
        x = jnp.transpose(timeline(i, idx), (1, 3, 0, 2)).reshape(tt, RW_VM, SUBLANES, 1, RW_CHAINS)
        return jnp.broadcast_to(x, (tt, RW_VM, SUBLANES, RW_KLO, RW_CHAINS)).reshape(tt, RW_VM, SUBLANES, LANES)

    r_i, w_i, k_i, v_i, kk_i, b_i = range(6)
    ops = [k_rows(i, (w_i, b_i, k_i, r_i, kk_i)) for i in range(2)]
    y_f, y_b = _rwkv_scan(ops[0], ops[1], v_tiles(0, v_i), v_tiles(1, v_i))

    def token_major(y):
        y = jnp.sum(y.reshape(tt, RW_HEAD, RW_KLO, b, RW_HEADS), axis=2)
        return jnp.transpose(y, (2, 0, 3, 1))

    y_f, y_b = token_major(y_f), token_major(y_b)
    outs = []
    for s, (sl_f, sl_b, p) in enumerate(((slice(0, tc), slice(tx, tt), p_c), (slice(tc, tt), slice(0, tx), p_x))):
        bo = []
        for i in range(2):
            r, _, k, v = feats[i][s][:4]
            bo.append(jnp.sum(r * k * lp['rw_rk'], axis=-1, keepdims=True) * v)
        yy = y_f[:, sl_f] + y_b[:, sl_b]
        mu = jnp.mean(yy, axis=-1, keepdims=True)
        var = jnp.mean(jnp.square(yy - mu), axis=-1, keepdims=True)
        yn = (yy - mu) * lax.rsqrt(var + RW_GN_EPS)
        bt = yy.shape[:2]
        y_rw = yn.reshape(bt + (GROUP,)) * lp['rw_gn_w'] + lp['rw_gn_b'] + (bo[0] + bo[1]).reshape(bt + (GROUP,))
        outs.append(y_rw * (jax.nn.sigmoid(p[2]) @ lp['rw_g2']))
    return outs


def _to_chunks(t):
    return t.reshape((t.shape[0], t.shape[1] // CHUNK, CHUNK) + t.shape[2:])


def _rms_heads(o, g):
    return o * lax.rsqrt(jnp.mean(o * o, axis=-1, keepdims=True) + EPS) * g


def _conv3_rows(x, w, row_len):
    b, t, ch = x.shape
    rows = t // row_len
    xp = jnp.pad(x.reshape(b, rows, row_len, ch), ((0, 0), (0, 0), (1, 1), (0, 0)))
    y = xp[:, :, :-2] * w[0] + xp[:, :, 1:-1] * w[1] + xp[:, :, 2:] * w[2]
    return y.reshape(b, t, ch)


def _gla_scan(q, k, v, log_f, s0):
    b, t, h, dv = v.shape
    q, k, v, log_f = (_to_chunks(a) for a in (q, k, v, log_f))
    cum = jnp.cumsum(log_f, axis=2)
    last = cum[:, :, -1:]
    q_dec = q * jnp.exp(cum)
    k_inv = k * jnp.exp(-cum)
    k_end = k * jnp.exp(last - cum)
    lower = jnp.tril(jnp.ones((CHUNK, CHUNK), bool))
    att = jnp.where(lower, jnp.einsum('bnihd,bnjhd->bnhij', q_dec, k_inv), 0.0)
    o_intra = jnp.einsum('bnhij,bnjhv->bnihv', att, v)
    d_state = jnp.einsum('bnjhd,bnjhv->nbhdv', k_end, v)
    chunk_decay = jnp.moveaxis(jnp.exp(last[:, :, 0]), 1, 0)

    def step(s, inp):
        dec, ds = inp
        return dec[..., None] * s + ds, s

    s_fin, s_start = lax.scan(step, s0, (chunk_decay, d_state))
    o_inter = jnp.einsum('bnihd,nbhdv->bnihv', q_dec, s_start)
    return (o_intra + o_inter).reshape(b, t, h, dv), s_fin


def _gdn_scan(q, k, v, log_a, beta, s0):
    b, t, h, dv = v.shape
    dk = q.shape[-1]
    hm = lambda a: jnp.moveaxis(_to_chunks(a), 3, 2)
    q, k, v, log_a, beta = (hm(a) for a in (q, k, v, log_a, beta))
    q = q * dk ** -0.5
    cum = jnp.cumsum(log_a, axis=-1)
    lower = jnp.tril(jnp.ones((CHUNK, CHUNK), bool))
    strict = jnp.tril(jnp.ones((CHUNK, CHUNK), bool), -1)
    decay = jnp.exp(jnp.where(lower, cum[..., :, None] - cum[..., None, :], -jnp.inf))
    kk = jnp.einsum('bnhid,bnhjd->bnhij', k, k)
    lmat = jnp.where(strict, beta[..., None] * kk * decay, 0.0) + jnp.eye(CHUNK, dtype=kk.dtype)
    rhs = jnp.concatenate([v * beta[..., None], k * (beta * jnp.exp(cum))[..., None]], axis=-1)
    sol = lax.linalg.triangular_solve(lmat, rhs, left_side=True, lower=True, unit_diagonal=True)
    u, w = sol[..., :dv], sol[..., dv:]
    a_qk = jnp.einsum('bnhid,bnhjd->bnhij', q, k) * decay
    k_end = k * jnp.exp(cum[..., -1:] - cum)[..., None]
    chunk_decay = jnp.exp(cum[..., -1])
    cm = lambda a: jnp.moveaxis(a, 1, 0)

    def step(s, inp):
        u_i, w_i, ke_i, dec_i = inp
        v_new = u_i - jnp.einsum('bhcd,bhdv->bhcv', w_i, s)
        s_next = dec_i[..., None, None] * s + jnp.einsum('bhcd,bhcv->bhdv', ke_i, v_new)
        return s_next, (s, v_new)

    s_fin, (s_start, v_new) = lax.scan(step, s0, (cm(u), cm(w), cm(k_end), cm(chunk_decay)))
    o = (jnp.einsum('bnhcd,nbhdv->bnhcv', q * jnp.exp(cum)[..., None], s_start)
         + jnp.einsum('bnhij,nbhjv->bnhiv', a_qk, v_new))
    return jnp.moveaxis(o, 2, 3).reshape(b, t, h, dv), s_fin


def _seg(proj, name):
    o, wd, _ = SEG[name]
    return proj[..., o:o + wd]


TOKEN_BLOCK = 256
HI = lax.Precision.HIGHEST


def _dot(a, b, precision=None):
    return jnp.dot(a, b, preferred_element_type=F32, precision=precision)


def _dot_nt(a, b):
    return lax.dot_general(a, b, (((1,), (1,)), ((), ())), preferred_element_type=F32)


def _dot_tn(a, b):
    return lax.dot_general(a, b, (((0,), (0,)), ((), ())), preferred_element_type=F32)


def _split_bf16(x):
    hi = x.astype(BF16)
    return hi, (x - hi.astype(F32)).astype(BF16)


def _dot3(a_hi, a_lo, b_hi, b_lo):
    return _dot(a_hi, b_hi) + (_dot(a_hi, b_lo) + _dot(a_lo, b_hi))


def _silu(x):
    return x * jax.nn.sigmoid(x)


def _chunk_masks(reverse):
    ri = lax.broadcasted_iota(jnp.int32, (CHUNK, CHUNK), 0)
    ci = lax.broadcasted_iota(jnp.int32, (CHUNK, CHUNK), 1)
    incl = (ci >= ri) if reverse else (ci <= ri)
    strict = (ci > ri) if reverse else (ci < ri)
    return incl, strict, (ri == ci).astype(F32)


def _conv3_block(x, cw, row_len):
    tb = x.shape[0]
    pos = lax.broadcasted_iota(jnp.int32, (tb, 1), 0) % row_len
    x_prev = jnp.where(pos == 0, 0.0, pltpu.roll(x, 1, 0))
    x_next = jnp.where(pos == row_len - 1, 0.0, pltpu.roll(x, tb - 1, 0))
    return x_prev * cw[0:1] + x * cw[1:2] + x_next * cw[2:3]


def _gdn_kernel(*refs, reverse, finalize, row_len, dirn):
    if finalize:
        (qkv_ref, ab_ref, z_ref, ob_ref, cw_ref, alog_ref, dtb_ref, g_ref, s0_ref,
         o_ref, sfin_ref, s_scr) = refs
    else:
        qkv_ref, ab_ref, cw_ref, alog_ref, dtb_ref, s0_ref, o_ref, sfin_ref, s_scr = refs
    n = pl.program_id(1)

    @pl.when(n == 0)
    def _():
        s_scr[...] = s0_ref[0]

    tb = qkv_ref.shape[1]
    qkv = _silu(_conv3_block(qkv_ref[0], cw_ref[...], row_len))
    ab = ab_ref[0]
    log_a = -jnp.exp(alog_ref[...]) * jax.nn.softplus(ab + dtb_ref[...])
    beta = jax.nn.sigmoid(ab)
    incl, strict, eye = _chunk_masks(reverse)
    tri = incl.astype(F32)
    last = 0 if reverse else CHUNK - 1
    hd = GDN_DK
    nchunk = tb // CHUNK
    order = list(range(nchunk - 1, -1, -1) if reverse else range(nchunk))
    rows = lambda c: slice(c * CHUNK, (c + 1) * CHUNK)
    cum_all = _dot(tri, jnp.concatenate([log_a[rows(c)] for c in range(nchunk)], axis=1), HI)

    items = [(c, h) for c in order for h in range(GDN_HEADS)]
    pre = {}
    for c, h in items:
        col = dirn * GDN_HEADS + h
        cum = cum_all[:, c * LANES + col:c * LANES + col + 1]
        bet = beta[rows(c), 2 * GDN_HEADS + col:2 * GDN_HEADS + col + 1]
        q = _l2norm(qkv[rows(c), h * hd:(h + 1) * hd]) * hd ** -0.5
        k = _l2norm(qkv[rows(c), GROUP + h * hd:GROUP + (h + 1) * hd])
        v = qkv[rows(c), 2 * GROUP + h * hd:2 * GROUP + (h + 1) * hd]
        cum_row = jnp.sum(eye * cum, axis=0, keepdims=True)
        decay = jnp.exp(jnp.where(incl, cum - cum_row, -jnp.inf))
        kb = k.astype(BF16)
        a = jnp.where(strict, bet * _dot_nt(kb, kb) * decay, 0.0)
        ecum = jnp.exp(cum)
        total = cum[last:last + 1]
        pre[c, h] = dict(
            x=-a, rhs=jnp.concatenate([v * bet, k * (bet * ecum)], axis=1),
            a_qk=(_dot_nt(q.astype(BF16), kb) * decay).astype(BF16),
            qe=(q * ecum).astype(BF16), k_end=(k * jnp.exp(total - cum)).astype(BF16),
            dec=jnp.exp(total))
    for p in pre.values():
        p['inv'] = eye + p['x']
    for level in range(6):
        for p in pre.values():
            x_hi, x_lo = _split_bf16(p['x'])
            if level == 0:
                p['x'] = _dot3(x_hi, x_lo, x_hi, x_lo)
            elif level < 5:
                i_hi, i_lo = _split_bf16(p['inv'])
                both = _dot3(jnp.concatenate([x_hi, i_hi], axis=0), jnp.concatenate([x_lo, i_lo], axis=0),
                             x_hi, x_lo)
                p['x'] = both[:CHUNK]
                p['inv'] = p['inv'] + both[CHUNK:]
            else:
                i_hi, i_lo = _split_bf16(p['inv'])
                p['inv'] = p['inv'] + _dot3(i_hi, i_lo, x_hi, x_lo)
    for p in pre.values():
        i_hi, i_lo = _split_bf16(p['inv'])
        r_hi, r_lo = _split_bf16(p['rhs'])
        sol = _dot3(i_hi, i_lo, r_hi, r_lo)
        p['u'], p['w'] = sol[:, :GDN_DV], sol[:, GDN_DV:].astype(BF16)

    for c, h in items:
        p = pre[c, h]
        s = s_scr[h]
        sb = s.astype(BF16)
        v_new = p['u'] - _dot(p['w'], sb)
        vb = v_new.astype(BF16)
        o = _dot(p['qe'], sb) + _dot(p['a_qk'], vb)
        s_scr[h] = p['dec'] * s + _dot_tn(p['k_end'], vb)
        cols = slice(h * GDN_DV, (h + 1) * GDN_DV)
        if finalize:
            o = o + ob_ref[0, rows(c), cols]
            o = o * lax.rsqrt(jnp.mean(o * o, axis=-1, keepdims=True) + EPS) * g_ref[...]
            o = o * _silu(z_ref[0, rows(c), cols])
        o_ref[0, rows(c), cols] = o.astype(o_ref.dtype)

    @pl.when(n == pl.num_programs(1) - 1)
    def _():
        sfin_ref[0] = s_scr[...]


def _gdn_pass(proj, o_other, s0, lp, *, dirn, row_len):
    b, t, _ = proj.shape
    tb = min(TOKEN_BLOCK, t)
    assert t % tb == 0 and tb % row_len == 0
    nblk = t // tb
    reverse = dirn == 1
    finalize = o_other is not None
    tok = (lambda i, n: (i, nblk - 1 - n)) if reverse else (lambda i, n: (i, n))
    seg = lambda name, wd: pl.BlockSpec((1, tb, wd), lambda i, n: tok(i, n) + (_col_block(name),))
    full = lambda a: pl.BlockSpec(a.shape, lambda i, n: (0,) * a.ndim)
    lane_row = lambda vals: jnp.pad(vals.reshape(1, -1), ((0, 0), (0, LANES - vals.size)))
    cw = lp['gdn_conv']
    alog = lane_row(lp['gdn_a_log'])
    dtb = lane_row(lp['gdn_dt_bias'])
    g = lp['gdn_norm_g'].reshape(1, GDN_DV)
    state_spec = pl.BlockSpec((1, GDN_HEADS, GDN_DK, GDN_DV), lambda i, n: (i, 0, 0, 0))
    if finalize:
        args = (proj, proj, proj, o_other, cw, alog, dtb, g, s0)
        in_specs = [seg('d_qkv', 3 * GROUP), seg('d_ab', LANES), seg('d_z', GROUP),
                    pl.BlockSpec((1, tb, GROUP), lambda i, n: tok(i, n) + (0,)),
                    full(cw), full(alog), full(dtb), full(g), state_spec]
    else:
        args = (proj, proj, cw, alog, dtb, s0)
        in_specs = [seg('d_qkv', 3 * GROUP), seg('d_ab', LANES), full(cw), full(alog), full(dtb), state_spec]
    return pl.pallas_call(
        functools.partial(_gdn_kernel, reverse=reverse, finalize=finalize, row_len=row_len, dirn=dirn),
        grid=(b, nblk),
        in_specs=in_specs,
        out_specs=[pl.BlockSpec((1, tb, GROUP), lambda i, n: tok(i, n) + (0,)), state_spec],
        out_shape=[jax.ShapeDtypeStruct((b, t, GROUP), BF16 if finalize else F32),
                   jax.ShapeDtypeStruct((b, GDN_HEADS, GDN_DK, GDN_DV), F32)],
        scratch_shapes=[pltpu.VMEM((GDN_HEADS, GDN_DK, GDN_DV), F32)],
        compiler_params=pltpu.CompilerParams(
            dimension_semantics=("arbitrary", "arbitrary"), vmem_limit_bytes=VMEM_LIMIT),
        name="gdn_fwd" if finalize else "gdn_bwd",
    )(*args)


def _gla_kernel(*refs, reverse, finalize):
    if finalize:
        (q_ref, k_ref, v_ref, lo_ref, r_ref, ob_ref, up_ref, ab_ref, g_ref, s0_ref,
         o_ref, sfin_ref, s_scr) = refs
    else:
        q_ref, k_ref, v_ref, lo_ref, up_ref, ab_ref, s0_ref, o_ref, sfin_ref, s_scr = refs
    n = pl.program_id(1)

    @pl.when(n == 0)
    def _():
        s_scr[...] = s0_ref[0]

    tb = q_ref.shape[1]
    gate = _dot(lo_ref[0].astype(BF16), up_ref[...].astype(BF16)) + ab_ref[...]
    log_f = jax.nn.log_sigmoid(gate) / GLA_TAU
    incl, _, eye = _chunk_masks(reverse)
    tri = incl.astype(F32)
    last = 0 if reverse else CHUNK - 1
    nchunk = tb // CHUNK
    order = range(nchunk - 1, -1, -1) if reverse else range(nchunk)
    for c in order:
        sl = slice(c * CHUNK, (c + 1) * CHUNK)
        cum = _dot(tri, log_f[sl], HI)
        total = cum[last:last + 1]
        q_dec = q_ref[0, sl, :] * GLA_DK ** -0.5 * jnp.exp(cum)
        k = k_ref[0, sl, :]
        k_inv = (k * jnp.exp(-cum)).astype(BF16)
        k_end = (k * jnp.exp(total - cum)).astype(BF16)
        q_dec = q_dec.astype(BF16)
        dec_row = jnp.exp(total)
        for h in range(GLA_HEADS):
            kc = slice(h * GLA_DK, (h + 1) * GLA_DK)
            vc = slice(h * GLA_DV, (h + 1) * GLA_DV)
            vb = v_ref[0, sl, vc].astype(BF16)
            att = jnp.where(incl, _dot_nt(q_dec[:, kc], k_inv[:, kc]), 0.0)
            s = s_scr[h]
            o = _dot(att.astype(BF16), vb) + _dot(q_dec[:, kc], s.astype(BF16))
            dec_col = jnp.sum(eye * dec_row[:, kc], axis=1, keepdims=True)
            s_scr[h] = dec_col * s + _dot_tn(k_end[:, kc], vb)
            if finalize:
                o = o + ob_ref[0, sl, vc]
                o = o * lax.rsqrt(jnp.mean(o * o, axis=-1, keepdims=True) + EPS) * g_ref[...]
                o = o * _silu(r_ref[0, sl, vc])
            o_ref[0, sl, vc] = o.astype(o_ref.dtype)

    @pl.when(n == pl.num_programs(1) - 1)
    def _():
        sfin_ref[0] = s_scr[...]


def _gla_pass(proj, o_other, s0, lp, *, dirn):
    b, t, _ = proj.shape
    tb = min(TOKEN_BLOCK, t)
    assert t % tb == 0
    nblk = t // tb
    reverse = dirn == 1
    finalize = o_other is not None
    tok = (lambda i, n: (i, nblk - 1 - n)) if reverse else (lambda i, n: (i, n))
    seg = lambda name: pl.BlockSpec((1, tb, SEG[name][2]), lambda i, n: tok(i, n) + (_col_block(name),))
    full = lambda a: pl.BlockSpec(a.shape, lambda i, n: (0,) * a.ndim)
    hk = GLA_HEADS * GLA_DK
    up = jnp.pad(lp['gla_a_up'][dirn], ((0, LANES - GLA_LR), (0, 0)))
    ab = lp['gla_a_b'][dirn].reshape(1, hk)
    g = lp['gla_norm_g'].reshape(1, GLA_DV)
    state_spec = pl.BlockSpec((1, GLA_HEADS, GLA_DK, GLA_DV), lambda i, n: (i, 0, 0, 0))
    if finalize:
        args = (proj, proj, proj, proj, proj, o_other, up, ab, g, s0)
        in_specs = [seg('g_q'), seg('g_k'), seg('g_v'), seg('g_lo'), seg('g_r'),
                    pl.BlockSpec((1, tb, GROUP), lambda i, n: tok(i, n) + (0,)),
                    full(up), full(ab), full(g), state_spec]
    else:
        args = (proj, proj, proj, proj, up, ab, s0)
        in_specs = [seg('g_q'), seg('g_k'), seg('g_v'), seg('g_lo'), full(up), full(ab), state_spec]
    return pl.pallas_call(
        functools.partial(_gla_kernel, reverse=reverse, finalize=finalize),
        grid=(b, nblk),
        in_specs=in_specs,
        out_specs=[pl.BlockSpec((1, tb, GROUP), lambda i, n: tok(i, n) + (0,)), state_spec],
        out_shape=[jax.ShapeDtypeStruct((b, t, GROUP), BF16 if finalize else F32),
                   jax.ShapeDtypeStruct((b, GLA_HEADS, GLA_DK, GLA_DV), F32)],
        scratch_shapes=[pltpu.VMEM((GLA_HEADS, GLA_DK, GLA_DV), F32)],
        compiler_params=pltpu.CompilerParams(
            dimension_semantics=("arbitrary", "arbitrary"), vmem_limit_bytes=VMEM_LIMIT),
        name="gla_fwd" if finalize else "gla_bwd",
    )(*args)


def _two_direction_group(pass_fn, state_shape, proj_c, proj_x, **kw):
    zero = jnp.zeros((proj_x[0].shape[0],) + state_shape, F32)
    ob_c, sb_c = pass_fn(proj_c[0], None, zero, dirn=1, **proj_c[1], **kw)
    ob_x, _ = pass_fn(proj_x[0], None, sb_c, dirn=1, **proj_x[1], **kw)
    y_c, sf_c = pass_fn(proj_c[0], ob_c, zero, dirn=0, **proj_c[1], **kw)
    y_x, _ = pass_fn(proj_x[0], ob_x, sf_c, dirn=0, **proj_x[1], **kw)
    return y_c, y_x


def _gla_group_pallas(proj_c, proj_x, lp):
    return _two_direction_group(_gla_pass, (GLA_HEADS, GLA_DK, GLA_DV), (proj_c, {}), (proj_x, {}), lp=lp)


def _gdn_group_pallas(proj_c, proj_x, lp):
    return _two_direction_group(_gdn_pass, (GDN_HEADS, GDN_DK, GDN_DV),
                                (proj_c, dict(row_len=proj_c.shape[1])), (proj_x, dict(row_len=GRID_W)), lp=lp)


def _gla_group(proj, init, lp):
    g_q, g_k, g_v, g_r, g_lo = (_seg(proj, n) for n in ('g_q', 'g_k', 'g_v', 'g_r', 'g_lo'))
    q = _heads(g_q, GLA_HEADS) * GLA_DK ** -0.5
    k = _heads(g_k, GLA_HEADS)
    v = _heads(g_v, GLA_HEADS)
    log_f = [_heads(jax.nn.log_sigmoid(g_lo @ lp['gla_a_up'][i] + lp['gla_a_b'][i]) / GLA_TAU, GLA_HEADS)
             for i in range(2)]
    o_f, s_f = _gla_scan(q, k, v, log_f[0], init[0])
    o_b, s_b = _gla_scan(_flip(q), _flip(k), _flip(v), _flip(log_f[1]), init[1])
    y = _rms_heads(o_f + _flip(o_b), lp['gla_norm_g']) * jax.nn.silu(_heads(g_r, GLA_HEADS))
    return y.reshape(y.shape[:2] + (GROUP,)), (s_f, s_b)


def _gdn_group(proj, row_len, init, lp):
    b, t = proj.shape[:2]
    d_qkv, d_z, d_ab = (_seg(proj, n) for n in ('d_qkv', 'd_z', 'd_ab'))
    d_a, d_b = d_ab[..., :2 * GDN_HEADS], d_ab[..., 2 * GDN_HEADS:]
    qkv = jax.nn.silu(_conv3_rows(d_qkv, lp['gdn_conv'], row_len))
    q, k, v = (_heads(a, GDN_HEADS) for a in jnp.split(qkv, 3, axis=-1))
    q, k = _l2norm(q), _l2norm(k)
    log_a = -jnp.exp(lp['gdn_a_log']) * jax.nn.softplus(d_a.reshape(b, t, 2, GDN_HEADS) + lp['gdn_dt_bias'])
    beta = jax.nn.sigmoid(d_b.reshape(b, t, 2, GDN_HEADS))
    o_f, s_f = _gdn_scan(q, k, v, log_a[:, :, 0], beta[:, :, 0], init[0])
    o_b, s_b = _gdn_scan(_flip(q), _flip(k), _flip(v), _flip(log_a[:, :, 1]), _flip(beta[:, :, 1]), init[1])
    y = _rms_heads(o_f + _flip(o_b), lp['gdn_norm_g']) * jax.nn.silu(_heads(d_z, GDN_HEADS))
    return y.reshape(b, t, GROUP), (s_f, s_b)


def _sc_group(proj, row_len, lp):
    c_b, c_c, c_h = (_seg(proj, n) for n in ('c_b', 'c_c', 'c_h'))
    return c_b * _conv3_rows(c_c * c_h, lp['sc_conv'], row_len)


def _mixers(proj_c, proj_x, lp):
    b = proj_x.shape[0]
    z = lambda h, d1, d2: jnp.zeros((b, h, d1, d2), F32)
    gla0 = (z(GLA_HEADS, GLA_DK, GLA_DV),) * 2
    gdn0 = (z(GDN_HEADS, GDN_DK, GDN_DV),) * 2
    tc = proj_c.shape[1]
    y_gla_c, y_gla_x = _gla_group_pallas(proj_c, proj_x, lp)
    y_gdn_c, y_gdn_x = _gdn_group_pallas(proj_c, proj_x, lp)
    y_sc_c = _sc_group(proj_c, tc, lp)
    y_sc_x = _sc_group(proj_x, GRID_W, lp)
    rw = lambda p: tuple(_seg(p, n) for n in ('r_rkv', 'r_wa', 'r_g'))
    y_rw_c, y_rw_x = _rw_mixer(rw(proj_c), rw(proj_x), lp)
    y_c = jnp.concatenate([y_gla_c, y_gdn_c, y_sc_c, y_rw_c], axis=-1).astype(BF16)
    y_x = jnp.concatenate([y_gla_x, y_gdn_x, y_sc_x, y_rw_x], axis=-1).astype(BF16)
    return y_c, y_x


def kernel(x, c, ctx, c_ctx, ada_w, ada_b, norm1_g, norm2_g, w_in, w_out, gla_a_up, gla_a_b, gla_norm_g, gdn_conv, gdn_a_log, gdn_dt_bias, gdn_norm_g, sc_conv, rw_mu_rkv, rw_mu_wa, rw_w0, rw_w2, rw_a0, rw_a2, rw_g2, rw_kk, rw_ka, rw_rk, rw_gn_w, rw_gn_b, ffn_w_gu, ffn_w_down, final_g):
    nb, t, d = x.shape
    tc = ctx.shape[1]
    depth = w_in.shape[0]
    assert d == D_MODEL and t % 512 == 0 and tc % CHUNK == 0 and nb + 1 <= SUBLANES

    cond = jnp.concatenate([c, c_ctx[None, :], jnp.zeros((SUBLANES - nb - 1, d), F32)], axis=0)
    mod = _modulation(cond, ada_w, ada_b)

    tm_x = 512
    tm_c = tc
    h_x, h_c = x, ctx
    for l in range(depth):
        lp = dict(gla_a_up=gla_a_up[l], gla_a_b=gla_a_b[l],
                  gla_norm_g=gla_norm_g[l], gdn_conv=gdn_conv[l], gdn_a_log=gdn_a_log[l],
                  gdn_dt_bias=gdn_dt_bias[l], gdn_norm_g=gdn_norm_g[l], sc_conv=sc_conv[l],
                  rw_mu_rkv=rw_mu_rkv[l], rw_mu_wa=rw_mu_wa[l], rw_w0=rw_w0[l], rw_w2=rw_w2[l],
                  rw_a0=rw_a0[l], rw_a2=rw_a2[l], rw_g2=rw_g2[l], rw_kk=rw_kk[l], rw_ka=rw_ka[l],
                  rw_rk=rw_rk[l], rw_gn_w=rw_gn_w[l], rw_gn_b=rw_gn_b[l])
        m_x = [mod[l, :nb, i * d:(i + 1) * d][:, None, :] for i in range(6)]
        m_c = [jnp.broadcast_to(mod[l, nb, i * d:(i + 1) * d][None, None, :], (nb, 1, d)) for i in range(6)]
        w_in_l = _pad_w_in(w_in[l])
        w_out_l = w_out[l].astype(BF16)
        w_gu_l = ffn_w_gu[l].astype(BF16)
        w_dn_l = ffn_w_down[l].astype(BF16)

        proj_c = _norm_proj(h_c, norm1_g[l], m_c[1], m_c[0], w_in_l, tm_c, 1024)
        proj_x = _norm_proj(h_x, norm1_g[l], m_x[1], m_x[0], w_in_l, tm_x, 1024)
        y_c, y_x = _mixers(proj_c, proj_x, lp)

        h_x = _proj_residual(y_x, w_out_l, h_x, m_x[2], tm_x, 512)
        a_x = _norm_swiglu(h_x, norm2_g[l], m_x[4], m_x[3], w_gu_l, tm_x, 512)
        h_x = _proj_residual(a_x, w_dn_l, h_x, m_x[5], tm_x, 512)
        if l < depth - 1:
            h_c = _proj_residual(y_c, w_out_l, h_c, m_c[2], tm_c, 512)
            a_c = _norm_swiglu(h_c, norm2_g[l], m_c[4], m_c[3], w_gu_l, tm_c, 512)
            h_c = _proj_residual(a_c, w_dn_l, h_c, m_c[5], tm_c, 512)
    return _final_norm(h_x, final_g, tm_x)
```

```python
import functools
import math

import numpy as np
import jax
import jax.numpy as jnp
from jax import lax
from jax.experimental import pallas as pl
from jax.experimental.pallas import tpu as pltpu

F32 = jnp.float32
BF16 = jnp.bfloat16

D_MODEL = 2048
GROUP = D_MODEL // 4
CHUNK = 64
EPS = 1e-6
GRID_W = 64

GLA_HEADS = 4
GLA_DK = 64
GLA_DV = 128
GLA_LR = 16
GLA_TAU = 16.0
GDN_HEADS = 4
GDN_DK = 128
GDN_DV = 128
RW_HEAD = 64
RW_HEADS = GROUP // RW_HEAD
RW_DECAY_LR = 32
RW_A_LR = 32
RW_GATE_LR = 96
RW_DECAY_SCALE = math.exp(-0.5)
RW_GN_EPS = 64e-5
FFN = -(-(8 * D_MODEL) // (3 * 256)) * 256

SUBLANES = 8
LANES = 128
VMEM_LIMIT = 48 * 1024 * 1024

_SRC_SEGS = (
    ('g_q', 256), ('g_k', 256), ('g_v', 512), ('g_r', 512), ('g_lo', GLA_LR),
    ('d_qkv', 1536), ('d_z', 512), ('d_ab', 16),
    ('c_b', 512), ('c_c', 512), ('c_h', 512),
    ('r_rkv', 1536), ('r_wa', 64), ('r_g', RW_GATE_LR),
)
_DST_ORDER = ('g_q', 'g_k', 'g_v', 'g_r', 'd_qkv', 'r_rkv', 'd_z', 'c_b', 'c_c', 'c_h',
              'g_lo', 'd_ab', 'r_wa', 'r_g')


def _round_up(n, m):
    return -(-n // m) * m


def _seg_layout():
    width = dict(_SRC_SEGS)
    src, o = {}, 0
    for name, wd in _SRC_SEGS:
        src[name] = o
        o += wd
    dst, o = {}, 0
    for name in _DST_ORDER:
        pw = _round_up(width[name], LANES)
        assert o % pw == 0, name
        dst[name] = (o, width[name], pw)
        o += pw
    return src, dst, o


SEG_SRC, SEG, PROJ_PAD = _seg_layout()


def _pad_w_in(w):
    parts = []
    for name in _DST_ORDER:
        _, wd, pw = SEG[name]
        seg = w[:, SEG_SRC[name]:SEG_SRC[name] + wd]
        if pw != wd:
            seg = jnp.pad(seg, ((0, 0), (0, pw - wd)))
        parts.append(seg)
    return jnp.concatenate(parts, axis=1).astype(BF16)


def _col_block(name):
    o, _, pw = SEG[name]
    return o // pw


def _mod_kernel(c_ref, w_ref, b_ref, o_ref):
    x = c_ref[...]
    x = (x * jax.nn.sigmoid(x)).astype(BF16)
    o_ref[0] = jnp.dot(x, w_ref[0].astype(BF16), preferred_element_type=F32) + b_ref[0]


def _modulation(cond, ada_w, ada_b):
    nl, d, n = ada_w.shape
    tn = 1024
    return pl.pallas_call(
        _mod_kernel,
        grid=(nl, n // tn),
        in_specs=[pl.BlockSpec((SUBLANES, d), lambda l, j: (0, 0)),
                  pl.BlockSpec((1, d, tn), lambda l, j: (l, 0, j)),
                  pl.BlockSpec((1, 1, tn), lambda l, j: (l, 0, j))],
        out_specs=pl.BlockSpec((1, SUBLANES, tn), lambda l, j: (l, 0, j)),
        out_shape=jax.ShapeDtypeStruct((nl, SUBLANES, n), F32),
        compiler_params=pltpu.CompilerParams(
            dimension_semantics=("arbitrary", "arbitrary"), vmem_limit_bytes=VMEM_LIMIT),
        name="modulation",
    )(cond, ada_w, ada_b.reshape(nl, 1, n))


def _modnorm_rows(h_ref, g_ref, sc_ref, sh_ref):
    x = h_ref[0]
    y = x * lax.rsqrt(jnp.mean(x * x, axis=-1, keepdims=True) + EPS) * g_ref[...]
    return (y * (1.0 + sc_ref[0]) + sh_ref[0]).astype(BF16)


def _norm_proj_kernel(h_ref, g_ref, sc_ref, sh_ref, w_ref, o_ref, a_scr):
    @pl.when(pl.program_id(2) == 0)
    def _():
        a_scr[...] = _modnorm_rows(h_ref, g_ref, sc_ref, sh_ref)

    o_ref[0] = jnp.dot(a_scr[...], w_ref[...], preferred_element_type=F32).astype(o_ref.dtype)


def _norm_proj(h, g, scale, shift, w, tm, tn):
    b, t, d = h.shape
    n = w.shape[1]
    return pl.pallas_call(
        _norm_proj_kernel,
        grid=(b, t // tm, n // tn),
        in_specs=[pl.BlockSpec((1, tm, d), lambda i, m, j: (i, m, 0)),
                  pl.BlockSpec((1, d), lambda i, m, j: (0, 0)),
                  pl.BlockSpec((1, 1, d), lambda i, m, j: (i, 0, 0)),
                  pl.BlockSpec((1, 1, d), lambda i, m, j: (i, 0, 0)),
                  pl.BlockSpec((d, tn), lambda i, m, j: (0, j))],
        out_specs=pl.BlockSpec((1, tm, tn), lambda i, m, j: (i, m, j)),
        out_shape=jax.ShapeDtypeStruct((b, t, n), F32),
        scratch_shapes=[pltpu.VMEM((tm, d), BF16)],
        compiler_params=pltpu.CompilerParams(
            dimension_semantics=("arbitrary", "arbitrary", "arbitrary"), vmem_limit_bytes=VMEM_LIMIT),
        name="norm_proj",
    )(h, g.reshape(1, d), scale, shift, w)


def _norm_swiglu_kernel(h_ref, g_ref, sc_ref, sh_ref, wg_ref, wu_ref, o_ref, a_scr):
    @pl.when(pl.program_id(2) == 0)
    def _():
        a_scr[...] = _modnorm_rows(h_ref, g_ref, sc_ref, sh_ref)

    a = a_scr[...]
    gate = jnp.dot(a, wg_ref[...], preferred_element_type=F32)
    up = jnp.dot(a, wu_ref[...], preferred_element_type=F32)
    o_ref[0] = (gate * jax.nn.sigmoid(gate) * up).astype(o_ref.dtype)


def _norm_swiglu(h, g, scale, shift, w_gu, tm, tn):
    b, t, d = h.shape
    f = w_gu.shape[1] // 2
    nj = f // tn
    return pl.pallas_call(
        _norm_swiglu_kernel,
        grid=(b, t // tm, nj),
        in_specs=[pl.BlockSpec((1, tm, d), lambda i, m, j: (i, m, 0)),
                  pl.BlockSpec((1, d), lambda i, m, j: (0, 0)),
                  pl.BlockSpec((1, 1, d), lambda i, m, j: (i, 0, 0)),
                  pl.BlockSpec((1, 1, d), lambda i, m, j: (i, 0, 0)),
                  pl.BlockSpec((d, tn), lambda i, m, j: (0, j)),
                  pl.BlockSpec((d, tn), lambda i, m, j: (0, j + nj))],
        out_specs=pl.BlockSpec((1, tm, tn), lambda i, m, j: (i, m, j)),
        out_shape=jax.ShapeDtypeStruct((b, t, f), BF16),
        scratch_shapes=[pltpu.VMEM((tm, d), BF16)],
        compiler_params=pltpu.CompilerParams(
            dimension_semantics=("arbitrary", "arbitrary", "arbitrary"), vmem_limit_bytes=VMEM_LIMIT),
        name="norm_swiglu",
    )(h, g.reshape(1, d), scale, shift, w_gu, w_gu)


def _proj_residual_kernel(a_ref, w_ref, h_ref, gate_ref, o_ref):
    o_ref[0] = h_ref[0] + gate_ref[0] * jnp.dot(a_ref[0], w_ref[...], preferred_element_type=F32)


def _proj_residual(a, w, h, gate, tm, tn):
    b, t, k = a.shape
    d = w.shape[1]
    return pl.pallas_call(
        _proj_residual_kernel,
        grid=(b, t // tm, d // tn),
        in_specs=[pl.BlockSpec((1, tm, k), lambda i, m, j: (i, m, 0)),
                  pl.BlockSpec((k, tn), lambda i, m, j: (0, j)),
                  pl.BlockSpec((1, tm, tn), lambda i, m, j: (i, m, j)),
                  pl.BlockSpec((1, 1, tn), lambda i, m, j: (i, 0, j))],
        out_specs=pl.BlockSpec((1, tm, tn), lambda i, m, j: (i, m, j)),
        out_shape=jax.ShapeDtypeStruct((b, t, d), F32),
        compiler_params=pltpu.CompilerParams(
            dimension_semantics=("arbitrary", "arbitrary", "arbitrary"), vmem_limit_bytes=VMEM_LIMIT),
        name="proj_residual",
    )(a, w, h, gate)


def _final_norm_kernel(h_ref, g_ref, o_ref):
    x = h_ref[0]
    o_ref[0] = x * lax.rsqrt(jnp.mean(x * x, axis=-1, keepdims=True) + EPS) * g_ref[...]


def _final_norm(h, g, tm):
    b, t, d = h.shape
    return pl.pallas_call(
        _final_norm_kernel,
        grid=(b, t // tm),
        in_specs=[pl.BlockSpec((1, tm, d), lambda i, m: (i, m, 0)),
                  pl.BlockSpec((1, d), lambda i, m: (0, 0))],
        out_specs=pl.BlockSpec((1, tm, d), lambda i, m: (i, m, 0)),
        out_shape=jax.ShapeDtypeStruct((b, t, d), F32),
        compiler_params=pltpu.CompilerParams(
            dimension_semantics=("arbitrary", "arbitrary"), vmem_limit_bytes=VMEM_LIMIT),
        name="final_norm",
    )(h, g.reshape(1, d))


RW_CHAINS = 32
RW_KLO = LANES // RW_CHAINS
RW_KQ = RW_HEAD // RW_KLO
RW_VM = RW_HEAD // SUBLANES
RW_VGROUP = 4
RW_STEP_BLOCK = 16
OP_W, OP_B, OP_K, OP_R, OP_KK = range(5)


def _lane_group_sum(p):
    out = p
    for g in range(1, RW_KLO):
        out = out + pltpu.roll(p, g * RW_CHAINS, 1)
    return out


def _rwkv_kernel(opsf_ref, opsb_ref, vf_ref, vb_ref, yf_ref, yb_ref, s_ref, saf_ref, sab_ref, pb_ref):
    tb = opsf_ref.shape[0]
    dirs = ((opsf_ref, vf_ref, yf_ref), (opsb_ref, vb_ref, yb_ref))

    @pl.when(pl.program_id(0) == 0)
    def _():
        s_ref[...] = jnp.zeros_like(s_ref)

    zero = jnp.zeros((SUBLANES, LANES), F32)
    groups = [range(m0, m0 + RW_VGROUP) for m0 in range(0, RW_VM, RW_VGROUP)]

    for d, (ops_ref, _, _) in enumerate(dirs):
        first = 0 if d == 0 else tb - 1
        for ms in groups:
            acc = {m: zero for m in ms}
            for q in range(RW_KQ):
                kk = ops_ref[first, OP_KK, q:q + 1, :]
                for m in ms:
                    acc[m] = acc[m] + s_ref[d, q, m] * kk
            for m in ms:
                if d == 0:
                    saf_ref[m] = _lane_group_sum(acc[m])
                else:
                    pb_ref[m] = acc[m]

    def advance(d, i, sa_ref, next_ref, reduce_next):
        ops_ref, v_ref, y_ref = dirs[d]
        t = i if d == 0 else tb - 1 - i
        tn = jnp.minimum(i + 1, tb - 1) if d == 0 else jnp.maximum(tb - 2 - i, 0)
        for ms in groups:
            sa = {m: sa_ref[m] for m in ms}
            vt = {m: v_ref[t, m] for m in ms}
            acc_y = {m: zero for m in ms}
            acc_s = {m: zero for m in ms}
            for q in range(RW_KQ):
                w = ops_ref[t, OP_W, q:q + 1, :]
                b = ops_ref[t, OP_B, q:q + 1, :]
                kt = ops_ref[t, OP_K, q:q + 1, :]
                r = ops_ref[t, OP_R, q:q + 1, :]
                kkn = ops_ref[tn, OP_KK, q:q + 1, :]
                for m in ms:
                    s = s_ref[d, q, m] * w - sa[m] * b + vt[m] * kt
                    s_ref[d, q, m] = s
                    acc_y[m] = acc_y[m] + s * r
                    acc_s[m] = acc_s[m] + s * kkn
            for m in ms:
                y_ref[t, m] = acc_y[m]
                next_ref[m] = _lane_group_sum(acc_s[m]) if reduce_next else acc_s[m]

    def step(i, carry):
        for m in range(RW_VM):
            sab_ref[m] = _lane_group_sum(pb_ref[m])
        advance(0, i, saf_ref, saf_ref, True)
        advance(1, i, sab_ref, pb_ref, False)
        return carry

    lax.fori_loop(0, tb, step, 0)


def _rwkv_scan(ops_f, ops_b, v_f, v_b):
    t = ops_f.shape[0]
    tb = RW_STEP_BLOCK
    nblk = t // tb
    fwd = lambda i: (i, 0, 0, 0)
    bwd = lambda i: (nblk - 1 - i, 0, 0, 0)
    ops_block = (tb, 5, RW_KQ, LANES)
    v_block = (tb, RW_VM, SUBLANES, LANES)
    y_shape = jax.ShapeDtypeStruct((t, RW_VM, SUBLANES, LANES), F32)
    return pl.pallas_call(
        _rwkv_kernel,
        grid=(nblk,),
        in_specs=[pl.BlockSpec(ops_block, fwd), pl.BlockSpec(ops_block, bwd),
                  pl.BlockSpec(v_block, fwd), pl.BlockSpec(v_block, bwd)],
        out_specs=[pl.BlockSpec(v_block, fwd), pl.BlockSpec(v_block, bwd)],
        out_shape=[y_shape, y_shape],
        scratch_shapes=[pltpu.VMEM((2, RW_KQ, RW_VM, SUBLANES, LANES), F32)]
        + [pltpu.VMEM((RW_VM, SUBLANES, LANES), F32)] * 3,
        compiler_params=pltpu.CompilerParams(
            dimension_semantics=("arbitrary",), vmem_limit_bytes=VMEM_LIMIT),
        name="rwkv_scan",
    )(ops_f, ops_b, v_f, v_b)


def _token_shift(x, mu, backward):
    if backward:
        prev = jnp.pad(x, ((0, 0), (0, 1), (0, 0)))[:, 1:]
    else:
        prev = jnp.pad(x, ((0, 0), (1, 0), (0, 0)))[:, :-1]
    return x + (prev - x) * mu


def _heads(t, n_heads):
    return t.reshape(t.shape[:-1] + (n_heads, t.shape[-1] // n_heads))


def _flip(t):
    return jnp.flip(t, axis=1)


def _l2norm(x):
    return x * lax.rsqrt(jnp.sum(x * x, axis=-1, keepdims=True) + EPS)


def _rw_features(p_rkv, p_wa, backward, mu_rkv, mu_wa, w0, w2, a0, a2, k_k, k_a):
    x_rkv = _token_shift(p_rkv, mu_rkv, backward)
    x_wa = _token_shift(p_wa, mu_wa, backward)
    r, k, v = jnp.split(x_rkv, 3, axis=-1)
    w_lo, a_lo = jnp.split(x_wa, [RW_DECAY_LR], axis=-1)
    decay = jnp.exp(-RW_DECAY_SCALE * jax.nn.sigmoid(w0 + jnp.tanh(w_lo) @ w2))
    a = jax.nn.sigmoid(a0 + a_lo @ a2)
    kk = _l2norm(_heads(k * k_k, RW_HEADS))
    k = k * (1.0 + (a - 1.0) * k_a)
    r, k, v, a, decay = (_heads(t, RW_HEADS) for t in (r, k, v, a, decay))
    return r, decay, k, v, kk, kk * a


def _rw_mixer(p_c, p_x, lp):
    b = p_x[0].shape[0]
    assert b * RW_HEADS == RW_CHAINS, "chain-on-lanes layout needs batch * heads == 32"
    tc, tx = p_c[0].shape[1], p_x[0].shape[1]
    tt = tc + tx
    feats = []
    for i in range(2):
        feats.append([_rw_features(rkv, wa, i == 1, lp['rw_mu_rkv'][i], lp['rw_mu_wa'][i], lp['rw_w0'][i],
                                   lp['rw_w2'][i], lp['rw_a0'][i], lp['rw_a2'][i], lp['rw_kk'], lp['rw_ka'])
                      for (rkv, wa, _) in (p_c, p_x)])

    def timeline(i, idx):
        c_part, x_part = feats[i][0][idx], feats[i][1][idx]
        return jnp.concatenate([c_part, x_part] if i == 0 else [x_part, c_part], axis=1)

    def k_rows(i, idxs):
        x = jnp.stack([timeline(i, j) for j in idxs], axis=0)
        x = x.reshape(len(idxs), b, tt, RW_HEADS, RW_KQ, RW_KLO)
        return jnp.transpose(x, (2, 0, 4, 5, 1, 3)).reshape(tt, len(idxs), RW_KQ, LANES)

    def v_tiles(i, idx):
        x = jnp.transpose(timeline(i, idx), (1, 3, 0, 2)).reshape(tt, RW_VM, SUBLANES, 1, RW_CHAINS)
        return jnp.broadcast_to(x, (tt, RW_VM, SUBLANES, RW_KLO, RW_CHAINS)).reshape(tt, RW_VM, SUBLANES, LANES)

    r_i, w_i, k_i, v_i, kk_i, b_i = range(6)
    ops = [k_rows(i, (w_i, b_i, k_i, r_i, kk_i)) for i in range(2)]
    y_f, y_b = _rwkv_scan(ops[0], ops[1], v_tiles(0, v_i), v_tiles(1, v_i))

    def token_major(y):
        y = jnp.sum(y.reshape(tt, RW_HEAD, RW_KLO, b, RW_HEADS), axis=2)
        return jnp.transpose(y, (2, 0, 3, 1))

    y_f, y_b = token_major(y_f), token_major(y_b)
    outs = []
    for s, (sl_f, sl_b, p) in enumerate(((slice(0, tc), slice(tx, tt), p_c), (slice(tc, tt), slice(0, tx), p_x))):
        bo = []
        for i in range(2):
            r, _, k, v = feats[i][s][:4]
            bo.append(jnp.sum(r * k * lp['rw_rk'], axis=-1, keepdims=True) * v)
        yy = y_f[:, sl_f] + y_b[:, sl_b]
        mu = jnp.mean(yy, axis=-1, keepdims=True)
        var = jnp.mean(jnp.square(yy - mu), axis=-1, keepdims=True)
        yn = (yy - mu) * lax.rsqrt(var + RW_GN_EPS)
        bt = yy.shape[:2]
        y_rw = yn.reshape(bt + (GROUP,)) * lp['rw_gn_w'] + lp['rw_gn_b'] + (bo[0] + bo[1]).reshape(bt + (GROUP,))
        outs.append(y_rw * (jax.nn.sigmoid(p[2]) @ lp['rw_g2']))
    return outs


def _to_chunks(t):
    return t.reshape((t.shape[0], t.shape[1] // CHUNK, CHUNK) + t.shape[2:])


def _rms_heads(o, g):
    return o * lax.rsqrt(jnp.mean(o * o, axis=-1, keepdims=True) + EPS) * g


def _conv3_rows(x, w, row_len):
    b, t, ch = x.shape
    rows = t // row_len
    xp = jnp.pad(x.reshape(b, rows, row_len, ch), ((0, 0), (0, 0), (1, 1), (0, 0)))
    y = xp[:, :, :-2] * w[0] + xp[:, :, 1:-1] * w[1] + xp[:, :, 2:] * w[2]
    return y.reshape(b, t, ch)


def _gla_scan(q, k, v, log_f, s0):
    b, t, h, dv = v.shape
    q, k, v, log_f = (_to_chunks(a) for a in (q, k, v, log_f))
    cum = jnp.cumsum(log_f, axis=2)
    last = cum[:, :, -1:]
    q_dec = q * jnp.exp(cum)
    k_inv = k * jnp.exp(-cum)
    k_end = k * jnp.exp(last - cum)
    lower = jnp.tril(jnp.ones((CHUNK, CHUNK), bool))
    att = jnp.where(lower, jnp.einsum('bnihd,bnjhd->bnhij', q_dec, k_inv), 0.0)
    o_intra = jnp.einsum('bnhij,bnjhv->bnihv', att, v)
    d_state = jnp.einsum('bnjhd,bnjhv->nbhdv', k_end, v)
    chunk_decay = jnp.moveaxis(jnp.exp(last[:, :, 0]), 1, 0)

    def step(s, inp):
        dec, ds = inp
        return dec[..., None] * s + ds, s

    s_fin, s_start = lax.scan(step, s0, (chunk_decay, d_state))
    o_inter = jnp.einsum('bnihd,nbhdv->bnihv', q_dec, s_start)
    return (o_intra + o_inter).reshape(b, t, h, dv), s_fin


def _gdn_scan(q, k, v, log_a, beta, s0):
    b, t, h, dv = v.shape
    dk = q.shape[-1]
    hm = lambda a: jnp.moveaxis(_to_chunks(a), 3, 2)
    q, k, v, log_a, beta = (hm(a) for a in (q, k, v, log_a, beta))
    q = q * dk ** -0.5
    cum = jnp.cumsum(log_a, axis=-1)
    lower = jnp.tril(jnp.ones((CHUNK, CHUNK), bool))
    strict = jnp.tril(jnp.ones((CHUNK, CHUNK), bool), -1)
    decay = jnp.exp(jnp.where(lower, cum[..., :, None] - cum[..., None, :], -jnp.inf))
    kk = jnp.einsum('bnhid,bnhjd->bnhij', k, k)
    lmat = jnp.where(strict, beta[..., None] * kk * decay, 0.0) + jnp.eye(CHUNK, dtype=kk.dtype)
    rhs = jnp.concatenate([v * beta[..., None], k * (beta * jnp.exp(cum))[..., None]], axis=-1)
    sol = lax.linalg.triangular_solve(lmat, rhs, left_side=True, lower=True, unit_diagonal=True)
    u, w = sol[..., :dv], sol[..., dv:]
    a_qk = jnp.einsum('bnhid,bnhjd->bnhij', q, k) * decay
    k_end = k * jnp.exp(cum[..., -1:] - cum)[..., None]
    chunk_decay = jnp.exp(cum[..., -1])
    cm = lambda a: jnp.moveaxis(a, 1, 0)

    def step(s, inp):
        u_i, w_i, ke_i, dec_i = inp
        v_new = u_i - jnp.einsum('bhcd,bhdv->bhcv', w_i, s)
        s_next = dec_i[..., None, None] * s + jnp.einsum('bhcd,bhcv->bhdv', ke_i, v_new)
        return s_next, (s, v_new)

    s_fin, (s_start, v_new) = lax.scan(step, s0, (cm(u), cm(w), cm(k_end), cm(chunk_decay)))
    o = (jnp.einsum('bnhcd,nbhdv->bnhcv', q * jnp.exp(cum)[..., None], s_start)
         + jnp.einsum('bnhij,nbhjv->bnhiv', a_qk, v_new))
    return jnp.moveaxis(o, 2, 3).reshape(b, t, h, dv), s_fin


def _seg(proj, name):
    o, wd, _ = SEG[name]
    return proj[..., o:o + wd]


TOKEN_BLOCK = 256
HI = lax.Precision.HIGHEST


def _dot(a, b, precision=None):
    return jnp.dot(a, b, preferred_element_type=F32, precision=precision)


def _dot_nt(a, b):
    return lax.dot_general(a, b, (((1,), (1,)), ((), ())), preferred_element_type=F32)


def _dot_tn(a, b):
    return lax.dot_general(a, b, (((0,), (0,)), ((), ())), preferred_element_type=F32)


def _split_bf16(x):
    hi = x.astype(BF16)
    return hi, (x - hi.astype(F32)).astype(BF16)


def _dot3(a_hi, a_lo, b_hi, b_lo):
    return _dot(a_hi, b_hi) + (_dot(a_hi, b_lo) + _dot(a_lo, b_hi))


def _silu(x):
    return x * jax.nn.sigmoid(x)


def _chunk_masks(reverse):
    ri = lax.broadcasted_iota(jnp.int32, (CHUNK, CHUNK), 0)
    ci = lax.broadcasted_iota(jnp.int32, (CHUNK, CHUNK), 1)
    incl = (ci >= ri) if reverse else (ci <= ri)
    strict = (ci > ri) if reverse else (ci < ri)
    return incl, strict, (ri == ci).astype(F32)


def _conv3_block(x, cw, row_len):
    tb = x.shape[0]
    pos = lax.broadcasted_iota(jnp.int32, (tb, 1), 0) % row_len
    x_prev = jnp.where(pos == 0, 0.0, pltpu.roll(x, 1, 0))
    x_next = jnp.where(pos == row_len - 1, 0.0, pltpu.roll(x, tb - 1, 0))
    return x_prev * cw[0:1] + x * cw[1:2] + x_next * cw[2:3]


def _gdn_kernel(*refs, reverse, finalize, row_len, dirn):
    if finalize:
        (qkv_ref, ab_ref, z_ref, ob_ref, cw_ref, alog_ref, dtb_ref, g_ref, s0_ref,
         o_ref, sfin_ref, s_scr) = refs
    else:
        qkv_ref, ab_ref, cw_ref, alog_ref, dtb_ref, s0_ref, o_ref, sfin_ref, s_scr = refs
    n = pl.program_id(1)

    @pl.when(n == 0)
    def _():
        s_scr[...] = s0_ref[0]

    tb = qkv_ref.shape[1]
    qkv = _silu(_conv3_block(qkv_ref[0], cw_ref[...], row_len))
    ab = ab_ref[0]
    log_a = -jnp.exp(alog_ref[...]) * jax.nn.softplus(ab + dtb_ref[...])
    beta = jax.nn.sigmoid(ab)
    incl, strict, eye = _chunk_masks(reverse)
    tri = incl.astype(F32)
    last = 0 if reverse else CHUNK - 1
    hd = GDN_DK
    nchunk = tb // CHUNK
    order = list(range(nchunk - 1, -1, -1) if reverse else range(nchunk))
    rows = lambda c: slice(c * CHUNK, (c + 1) * CHUNK)
    cum_all = _dot(tri, jnp.concatenate([log_a[rows(c)] for c in range(nchunk)], axis=1), HI)

    items = [(c, h) for c in order for h in range(GDN_HEADS)]
    pre = {}
    for c, h in items:
        col = dirn * GDN_HEADS + h
        cum = cum_all[:, c * LANES + col:c * LANES + col + 1]
        bet = beta[rows(c), 2 * GDN_HEADS + col:2 * GDN_HEADS + col + 1]
        q = _l2norm(qkv[rows(c), h * hd:(h + 1) * hd]) * hd ** -0.5
        k = _l2norm(qkv[rows(c), GROUP + h * hd:GROUP + (h + 1) * hd])
        v = qkv[rows(c), 2 * GROUP + h * hd:2 * GROUP + (h + 1) * hd]
        cum_row = jnp.sum(eye * cum, axis=0, keepdims=True)
        decay = jnp.exp(jnp.where(incl, cum - cum_row, -jnp.inf))
        kb = k.astype(BF16)
        a = jnp.where(strict, bet * _dot_nt(kb, kb) * decay, 0.0)
        ecum = jnp.exp(cum)
        total = cum[last:last + 1]
        pre[c, h] = dict(
            x=-a, rhs=jnp.concatenate([v * bet, k * (bet * ecum)], axis=1),
            a_qk=(_dot_nt(q.astype(BF16), kb) * decay).astype(BF16),
            qe=(q * ecum).astype(BF16), k_end=(k * jnp.exp(total - cum)).astype(BF16),
            dec=jnp.exp(total))
    for p in pre.values():
        p['inv'] = eye + p['x']
    for level in range(6):
        for p in pre.values():
            x_hi, x_lo = _split_bf16(p['x'])
            if level == 0:
                p['x'] = _dot3(x_hi, x_lo, x_hi, x_lo)
            elif level < 5:
                i_hi, i_lo = _split_bf16(p['inv'])
                both = _dot3(jnp.concatenate([x_hi, i_hi], axis=0), jnp.concatenate([x_lo, i_lo], axis=0),
                             x_hi, x_lo)
                p['x'] = both[:CHUNK]
                p['inv'] = p['inv'] + both[CHUNK:]
            else:
                i_hi, i_lo = _split_bf16(p['inv'])
                p['inv'] = p['inv'] + _dot3(i_hi, i_lo, x_hi, x_lo)
    for p in pre.values():
        i_hi, i_lo = _split_bf16(p['inv'])
        r_hi, r_lo = _split_bf16(p['rhs'])
        sol = _dot3(i_hi, i_lo, r_hi, r_lo)
        p['u'], p['w'] = sol[:, :GDN_DV], sol[:, GDN_DV:].astype(BF16)

    for c, h in items:
        p = pre[c, h]
        s = s_scr[h]
        sb = s.astype(BF16)
        v_new = p['u'] - _dot(p['w'], sb)
        vb = v_new.astype(BF16)
        o = _dot(p['qe'], sb) + _dot(p['a_qk'], vb)
        s_scr[h] = p['dec'] * s + _dot_tn(p['k_end'], vb)
        cols = slice(h * GDN_DV, (h + 1) * GDN_DV)
        if finalize:
            o = o + ob_ref[0, rows(c), cols]
            o = o * lax.rsqrt(jnp.mean(o * o, axis=-1, keepdims=True) + EPS) * g_ref[...]
            o = o * _silu(z_ref[0, rows(c), cols])
        o_ref[0, rows(c), cols] = o.astype(o_ref.dtype)

    @pl.when(n == pl.num_programs(1) - 1)
    def _():
        sfin_ref[0] = s_scr[...]


def _gdn_pass(proj, o_other, s0, lp, *, dirn, row_len):
    b, t, _ = proj.shape
    tb = min(TOKEN_BLOCK, t)
    assert t % tb == 0 and tb % row_len == 0
    nblk = t // tb
    reverse = dirn == 1
    finalize = o_other is not None
    tok = (lambda i, n: (i, nblk - 1 - n)) if reverse else (lambda i, n: (i, n))
    seg = lambda name, wd: pl.BlockSpec((1, tb, wd), lambda i, n: tok(i, n) + (_col_block(name),))
    full = lambda a: pl.BlockSpec(a.shape, lambda i, n: (0,) * a.ndim)
    lane_row = lambda vals: jnp.pad(vals.reshape(1, -1), ((0, 0), (0, LANES - vals.size)))
    cw = lp['gdn_conv']
    alog = lane_row(lp['gdn_a_log'])
    dtb = lane_row(lp['gdn_dt_bias'])
    g = lp['gdn_norm_g'].reshape(1, GDN_DV)
    state_spec = pl.BlockSpec((1, GDN_HEADS, GDN_DK, GDN_DV), lambda i, n: (i, 0, 0, 0))
    if finalize:
        args = (proj, proj, proj, o_other, cw, alog, dtb, g, s0)
        in_specs = [seg('d_qkv', 3 * GROUP), seg('d_ab', LANES), seg('d_z', GROUP),
                    pl.BlockSpec((1, tb, GROUP), lambda i, n: tok(i, n) + (0,)),
                    full(cw), full(alog), full(dtb), full(g), state_spec]
    else:
        args = (proj, proj, cw, alog, dtb, s0)
        in_specs = [seg('d_qkv', 3 * GROUP), seg('d_ab', LANES), full(cw), full(alog), full(dtb), state_spec]
    return pl.pallas_call(
        functools.partial(_gdn_kernel, reverse=reverse, finalize=finalize, row_len=row_len, dirn=dirn),
        grid=(b, nblk),
        in_specs=in_specs,
        out_specs=[pl.BlockSpec((1, tb, GROUP), lambda i, n: tok(i, n) + (0,)), state_spec],
        out_shape=[jax.ShapeDtypeStruct((b, t, GROUP), BF16 if finalize else F32),
                   jax.ShapeDtypeStruct((b, GDN_HEADS, GDN_DK, GDN_DV), F32)],
        scratch_shapes=[pltpu.VMEM((GDN_HEADS, GDN_DK, GDN_DV), F32)],
        compiler_params=pltpu.CompilerParams(
            dimension_semantics=("arbitrary", "arbitrary"), vmem_limit_bytes=VMEM_LIMIT),
        name="gdn_fwd" if finalize else "gdn_bwd",
    )(*args)


def _gla_kernel(*refs, reverse, finalize):
    if finalize:
        (q_ref, k_ref, v_ref, lo_ref, r_ref, ob_ref, up_ref, ab_ref, g_ref, s0_ref,
         o_ref, sfin_ref, s_scr) = refs
    else:
        q_ref, k_ref, v_ref, lo_ref, up_ref, ab_ref, s0_ref, o_ref, sfin_ref, s_scr = refs
    n = pl.program_id(1)

    @pl.when(n == 0)
    def _():
        s_scr[...] = s0_ref[0]

    tb = q_ref.shape[1]
    gate = _dot(lo_ref[0].astype(BF16), up_ref[...].astype(BF16)) + ab_ref[...]
    log_f = jax.nn.log_sigmoid(gate) / GLA_TAU
    incl, _, eye = _chunk_masks(reverse)
    tri = incl.astype(F32)
    last = 0 if reverse else CHUNK - 1
    nchunk = tb // CHUNK
    order = range(nchunk - 1, -1, -1) if reverse else range(nchunk)
    for c in order:
        sl = slice(c * CHUNK, (c + 1) * CHUNK)
        cum = _dot(tri, log_f[sl], HI)
        total = cum[last:last + 1]
        q_dec = q_ref[0, sl, :] * GLA_DK ** -0.5 * jnp.exp(cum)
        k = k_ref[0, sl, :]
        k_inv = (k * jnp.exp(-cum)).astype(BF16)
        k_end = (k * jnp.exp(total - cum)).astype(BF16)
        q_dec = q_dec.astype(BF16)
        dec_row = jnp.exp(total)
        for h in range(GLA_HEADS):
            kc = slice(h * GLA_DK, (h + 1) * GLA_DK)
            vc = slice(h * GLA_DV, (h + 1) * GLA_DV)
            vb = v_ref[0, sl, vc].astype(BF16)
            att = jnp.where(incl, _dot_nt(q_dec[:, kc], k_inv[:, kc]), 0.0)
            s = s_scr[h]
            o = _dot(att.astype(BF16), vb) + _dot(q_dec[:, kc], s.astype(BF16))
            dec_col = jnp.sum(eye * dec_row[:, kc], axis=1, keepdims=True)
            s_scr[h] = dec_col * s + _dot_tn(k_end[:, kc], vb)
            if finalize:
                o = o + ob_ref[0, sl, vc]
                o = o * lax.rsqrt(jnp.mean(o * o, axis=-1, keepdims=True) + EPS) * g_ref[...]
                o = o * _silu(r_ref[0, sl, vc])
            o_ref[0, sl, vc] = o.astype(o_ref.dtype)

    @pl.when(n == pl.num_programs(1) - 1)
    def _():
        sfin_ref[0] = s_scr[...]


def _gla_pass(proj, o_other, s0, lp, *, dirn):
    b, t, _ = proj.shape
    tb = min(TOKEN_BLOCK, t)
    assert t % tb == 0
    nblk = t // tb
    reverse = dirn == 1
    finalize = o_other is not None
    tok = (lambda i, n: (i, nblk - 1 - n)) if reverse else (lambda i, n: (i, n))
    seg = lambda name: pl.BlockSpec((1, tb, SEG[name][2]), lambda i, n: tok(i, n) + (_col_block(name),))
    full = lambda a: pl.BlockSpec(a.shape, lambda i, n: (0,) * a.ndim)
    hk = GLA_HEADS * GLA_DK
    up = jnp.pad(lp['gla_a_up'][dirn], ((0, LANES - GLA_LR), (0, 0)))
    ab = lp['gla_a_b'][dirn].reshape(1, hk)
    g = lp['gla_norm_g'].reshape(1, GLA_DV)
    state_spec = pl.BlockSpec((1, GLA_HEADS, GLA_DK, GLA_DV), lambda i, n: (i, 0, 0, 0))
    if finalize:
        args = (proj, proj, proj, proj, proj, o_other, up, ab, g, s0)
        in_specs = [seg('g_q'), seg('g_k'), seg('g_v'), seg('g_lo'), seg('g_r'),
                    pl.BlockSpec((1, tb, GROUP), lambda i, n: tok(i, n) + (0,)),
                    full(up), full(ab), full(g), state_spec]
    else:
        args = (proj, proj, proj, proj, up, ab, s0)
        in_specs = [seg('g_q'), seg('g_k'), seg('g_v'), seg('g_lo'), full(up), full(ab), state_spec]
    return pl.pallas_call(
        functools.partial(_gla_kernel, reverse=reverse, finalize=finalize),
        grid=(b, nblk),
        in_specs=in_specs,
        out_specs=[pl.BlockSpec((1, tb, GROUP), lambda i, n: tok(i, n) + (0,)), state_spec],
        out_shape=[jax.ShapeDtypeStruct((b, t, GROUP), BF16 if finalize else F32),
                   jax.ShapeDtypeStruct((b, GLA_HEADS, GLA_DK, GLA_DV), F32)],
        scratch_shapes=[pltpu.VMEM((GLA_HEADS, GLA_DK, GLA_DV), F32)],
        compiler_params=pltpu.CompilerParams(
            dimension_semantics=("arbitrary", "arbitrary"), vmem_limit_bytes=VMEM_LIMIT),
        name="gla_fwd" if finalize else "gla_bwd",
    )(*args)


def _two_direction_group(pass_fn, state_shape, proj_c, proj_x, **kw):
    zero = jnp.zeros((proj_x[0].shape[0],) + state_shape, F32)
    ob_c, sb_c = pass_fn(proj_c[0], None, zero, dirn=1, **proj_c[1], **kw)
    ob_x, _ = pass_fn(proj_x[0], None, sb_c, dirn=1, **proj_x[1], **kw)
    y_c, sf_c = pass_fn(proj_c[0], ob_c, zero, dirn=0, **proj_c[1], **kw)
    y_x, _ = pass_fn(proj_x[0], ob_x, sf_c, dirn=0, **proj_x[1], **kw)
    return y_c, y_x


def _gla_group_pallas(proj_c, proj_x, lp):
    return _two_direction_group(_gla_pass, (GLA_HEADS, GLA_DK, GLA_DV), (proj_c, {}), (proj_x, {}), lp=lp)


def _gdn_group_pallas(proj_c, proj_x, lp):
    return _two_direction_group(_gdn_pass, (GDN_HEADS, GDN_DK, GDN_DV),
                                (proj_c, dict(row_len=proj_c.shape[1])), (proj_x, dict(row_len=GRID_W)), lp=lp)


def _gla_group(proj, init, lp):
    g_q, g_k, g_v, g_r, g_lo = (_seg(proj, n) for n in ('g_q', 'g_k', 'g_v', 'g_r', 'g_lo'))
    q = _heads(g_q, GLA_HEADS) * GLA_DK ** -0.5
    k = _heads(g_k, GLA_HEADS)
    v = _heads(g_v, GLA_HEADS)
    log_f = [_heads(jax.nn.log_sigmoid(g_lo @ lp['gla_a_up'][i] + lp['gla_a_b'][i]) / GLA_TAU, GLA_HEADS)
             for i in range(2)]
    o_f, s_f = _gla_scan(q, k, v, log_f[0], init[0])
    o_b, s_b = _gla_scan(_flip(q), _flip(k), _flip(v), _flip(log_f[1]), init[1])
    y = _rms_heads(o_f + _flip(o_b), lp['gla_norm_g']) * jax.nn.silu(_heads(g_r, GLA_HEADS))
    return y.reshape(y.shape[:2] + (GROUP,)), (s_f, s_b)


def _gdn_group(proj, row_len, init, lp):
    b, t = proj.shape[:2]
    d_qkv, d_z, d_ab = (_seg(proj, n) for n in ('d_qkv', 'd_z', 'd_ab'))
    d_a, d_b = d_ab[..., :2 * GDN_HEADS], d_ab[..., 2 * GDN_HEADS:]
    qkv = jax.nn.silu(_conv3_rows(d_qkv, lp['gdn_conv'], row_len))
    q, k, v = (_heads(a, GDN_HEADS) for a in jnp.split(qkv, 3, axis=-1))
    q, k = _l2norm(q), _l2norm(k)
    log_a = -jnp.exp(lp['gdn_a_log']) * jax.nn.softplus(d_a.reshape(b, t, 2, GDN_HEADS) + lp['gdn_dt_bias'])
    beta = jax.nn.sigmoid(d_b.reshape(b, t, 2, GDN_HEADS))
    o_f, s_f = _gdn_scan(q, k, v, log_a[:, :, 0], beta[:, :, 0], init[0])
    o_b, s_b = _gdn_scan(_flip(q), _flip(k), _flip(v), _flip(log_a[:, :, 1]), _flip(beta[:, :, 1]), init[1])
    y = _rms_heads(o_f + _flip(o_b), lp['gdn_norm_g']) * jax.nn.silu(_heads(d_z, GDN_HEADS))
    return y.reshape(b, t, GROUP), (s_f, s_b)


def _sc_group(proj, row_len, lp):
    c_b, c_c, c_h = (_seg(proj, n) for n in ('c_b', 'c_c', 'c_h'))
    return c_b * _conv3_rows(c_c * c_h, lp['sc_conv'], row_len)


def _mixers(proj_c, proj_x, lp):
    b = proj_x.shape[0]
    z = lambda h, d1, d2: jnp.zeros((b, h, d1, d2), F32)
    gla0 = (z(GLA_HEADS, GLA_DK, GLA_DV),) * 2
    gdn0 = (z(GDN_HEADS, GDN_DK, GDN_DV),) * 2
    tc = proj_c.shape[1]
    y_gla_c, y_gla_x = _gla_group_pallas(proj_c, proj_x, lp)
    y_gdn_c, y_gdn_x = _gdn_group_pallas(proj_c, proj_x, lp)
    y_sc_c = _sc_group(proj_c, tc, lp)
    y_sc_x = _sc_group(proj_x, GRID_W, lp)
    rw = lambda p: tuple(_seg(p, n) for n in ('r_rkv', 'r_wa', 'r_g'))
    y_rw_c, y_rw_x = _rw_mixer(rw(proj_c), rw(proj_x), lp)
    y_c = jnp.concatenate([y_gla_c, y_gdn_c, y_sc_c, y_rw_c], axis=-1).astype(BF16)
    y_x = jnp.concatenate([y_gla_x, y_gdn_x, y_sc_x, y_rw_x], axis=-1).astype(BF16)
    return y_c, y_x


def kernel(x, c, ctx, c_ctx, ada_w, ada_b, norm1_g, norm2_g, w_in, w_out, gla_a_up, gla_a_b, gla_norm_g, gdn_conv, gdn_a_log, gdn_dt_bias, gdn_norm_g, sc_conv, rw_mu_rkv, rw_mu_wa, rw_w0, rw_w2, rw_a0, rw_a2, rw_g2, rw_kk, rw_ka, rw_rk, rw_gn_w, rw_gn_b, ffn_w_gu, ffn_w_down, final_g):
    nb, t, d = x.shape
    tc = ctx.shape[1]
    depth = w_in.shape[0]
    assert d == D_MODEL and t % 512 == 0 and tc % CHUNK == 0 and nb + 1 <= SUBLANES

    cond = jnp.concatenate([c, c_ctx[None, :], jnp.zeros((SUBLANES - nb - 1, d), F32)], axis=0)
    mod = _modulation(cond, ada_w, ada_b)

    tm_x = 512
    tm_c = tc
    h_x, h_c = x, ctx
    for l in range(depth):
        lp = dict(gla_a_up=gla_a_up[l], gla_a_b=gla_a_b[l],
                  gla_norm_g=gla_norm_g[l], gdn_conv=gdn_conv[l], gdn_a_log=gdn_a_log[l],
                  gdn_dt_bias=gdn_dt_bias[l], gdn_norm_g=gdn_norm_g[l], sc_conv=sc_conv[l],
                  rw_mu_rkv=rw_mu_rkv[l], rw_mu_wa=rw_mu_wa[l], rw_w0=rw_w0[l], rw_w2=rw_w2[l],
                  rw_a0=rw_a0[l], rw_a2=rw_a2[l], rw_g2=rw_g2[l], rw_kk=rw_kk[l], rw_ka=rw_ka[l],
                  rw_rk=rw_rk[l], rw_gn_w=rw_gn_w[l], rw_gn_b=rw_gn_b[l])
        m_x = [mod[l, :nb, i * d:(i + 1) * d][:, None, :] for i in range(6)]
        m_c = [jnp.broadcast_to(mod[l, nb, i * d:(i + 1) * d][None, None, :], (nb, 1, d)) for i in range(6)]
        w_in_l = _pad_w_in(w_in[l])
        w_out_l = w_out[l].astype(BF16)
        w_gu_l = ffn_w_gu[l].astype(BF16)
        w_dn_l = ffn_w_down[l].astype(BF16)

        proj_c = _norm_proj(h_c, norm1_g[l], m_c[1], m_c[0], w_in_l, tm_c, 1024)
        proj_x = _norm_proj(h_x, norm1_g[l], m_x[1], m_x[0], w_in_l, 2 * tm_x, 1024)
        y_c, y_x = _mixers(proj_c, proj_x, lp)

        h_x = _proj_residual(y_x, w_out_l, h_x, m_x[2], 2 * tm_x, 1024)
        a_x = _norm_swiglu(h_x, norm2_g[l], m_x[4], m_x[3], w_gu_l, 2 * tm_x, 512)
        h_x = _proj_residual(a_x, w_dn_l, h_x, m_x[5], tm_x, 512)
        if l < depth - 1:
            h_c = _proj_residual(y_c, w_out_l, h_c, m_c[2], tm_c, 512)
            a_c = _norm_swiglu(h_c, norm2_g[l], m_c[4], m_c[3], w_gu_l, tm_c, 512)
            h_c = _proj_residual(a_c, w_dn_l, h_c, m_c[5], tm_c, 512)
    return _final_norm(h_x, final_g, tm_x)
```

```python
import functools
import math

import numpy as np
import jax
import jax.numpy as jnp
from jax import lax
from jax.experimental import pallas as pl
from jax.experimental.pallas import tpu as pltpu

F32 = jnp.float32
BF16 = jnp.bfloat16

D_MODEL = 2048
GROUP = D_MODEL // 4
CHUNK = 64
EPS = 1e-6
GRID_W = 64

GLA_HEADS = 4
GLA_DK = 64
GLA_DV = 128
GLA_LR = 16
GLA_TAU = 16.0
GDN_HEADS = 4
GDN_DK = 128
GDN_DV = 128
RW_HEAD = 64
RW_HEADS = GROUP // RW_HEAD
RW_DECAY_LR = 32
RW_A_LR = 32
RW_GATE_LR = 96
RW_DECAY_SCALE = math.exp(-0.5)
RW_GN_EPS = 64e-5
FFN = -(-(8 * D_MODEL) // (3 * 256)) * 256

SUBLANES = 8
LANES = 128
VMEM_LIMIT = 48 * 1024 * 1024

_SRC_SEGS = (
    ('g_q', 256), ('g_k', 256), ('g_v', 512), ('g_r', 512), ('g_lo', GLA_LR),
    ('d_qkv', 1536), ('d_z', 512), ('d_ab', 16),
    ('c_b', 512), ('c_c', 512), ('c_h', 512),
    ('r_rkv', 1536), ('r_wa', 64), ('r_g', RW_GATE_LR),
)
_DST_ORDER = ('g_q', 'g_k', 'g_v', 'g_r', 'd_qkv', 'r_rkv', 'd_z', 'c_b', 'c_c', 'c_h',
              'g_lo', 'd_ab', 'r_wa', 'r_g')


def _round_up(n, m):
    return -(-n // m) * m


def _seg_layout():
    width = dict(_SRC_SEGS)
    src, o = {}, 0
    for name, wd in _SRC_SEGS:
        src[name] = o
        o += wd
    dst, o = {}, 0
    for name in _DST_ORDER:
        pw = _round_up(width[name], LANES)
        assert o % pw == 0, name
        dst[name] = (o, width[name], pw)
        o += pw
    return src, dst, o


SEG_SRC, SEG, PROJ_PAD = _seg_layout()


def _pad_w_in(w):
    parts = []
    for name in _DST_ORDER:
        _, wd, pw = SEG[name]
        seg = w[:, SEG_SRC[name]:SEG_SRC[name] + wd]
        if name == 'r_rkv':
            perm = _rw_channel_perm()
            seg = jnp.concatenate([seg[:, i * GROUP:(i + 1) * GROUP][:, perm] for i in range(3)], axis=1)
        if pw != wd:
            seg = jnp.pad(seg, ((0, 0), (0, pw - wd)))
        parts.append(seg)
    return jnp.concatenate(parts, axis=1).astype(BF16)


def _col_block(name):
    o, _, pw = SEG[name]
    return o // pw


def _mod_kernel(c_ref, w_ref, b_ref, o_ref):
    x = c_ref[...]
    x = (x * jax.nn.sigmoid(x)).astype(BF16)
    o_ref[0] = jnp.dot(x, w_ref[0].astype(BF16), preferred_element_type=F32) + b_ref[0]


def _modulation(cond, ada_w, ada_b):
    nl, d, n = ada_w.shape
    tn = 1024
    return pl.pallas_call(
        _mod_kernel,
        grid=(nl, n // tn),
        in_specs=[pl.BlockSpec((SUBLANES, d), lambda l, j: (0, 0)),
                  pl.BlockSpec((1, d, tn), lambda l, j: (l, 0, j)),
                  pl.BlockSpec((1, 1, tn), lambda l, j: (l, 0, j))],
        out_specs=pl.BlockSpec((1, SUBLANES, tn), lambda l, j: (l, 0, j)),
        out_shape=jax.ShapeDtypeStruct((nl, SUBLANES, n), F32),
        compiler_params=pltpu.CompilerParams(
            dimension_semantics=("arbitrary", "arbitrary"), vmem_limit_bytes=VMEM_LIMIT),
        name="modulation",
    )(cond, ada_w, ada_b.reshape(nl, 1, n))


def _modnorm_rows(h_ref, g_ref, sc_ref, sh_ref):
    x = h_ref[0]
    y = x * lax.rsqrt(jnp.mean(x * x, axis=-1, keepdims=True) + EPS) * g_ref[...]
    return (y * (1.0 + sc_ref[0]) + sh_ref[0]).astype(BF16)


def _norm_proj_kernel(h_ref, g_ref, sc_ref, sh_ref, w_ref, o_ref, a_scr):
    @pl.when(pl.program_id(2) == 0)
    def _():
        a_scr[...] = _modnorm_rows(h_ref, g_ref, sc_ref, sh_ref)

    o_ref[0] = jnp.dot(a_scr[...], w_ref[...], preferred_element_type=F32).astype(o_ref.dtype)


def _norm_proj(h, g, scale, shift, w, tm, tn):
    b, t, d = h.shape
    n = w.shape[1]
    return pl.pallas_call(
        _norm_proj_kernel,
        grid=(b, t // tm, n // tn),
        in_specs=[pl.BlockSpec((1, tm, d), lambda i, m, j: (i, m, 0)),
                  pl.BlockSpec((1, d), lambda i, m, j: (0, 0)),
                  pl.BlockSpec((1, 1, d), lambda i, m, j: (i, 0, 0)),
                  pl.BlockSpec((1, 1, d), lambda i, m, j: (i, 0, 0)),
                  pl.BlockSpec((d, tn), lambda i, m, j: (0, j))],
        out_specs=pl.BlockSpec((1, tm, tn), lambda i, m, j: (i, m, j)),
        out_shape=jax.ShapeDtypeStruct((b, t, n), F32),
        scratch_shapes=[pltpu.VMEM((tm, d), BF16)],
        compiler_params=pltpu.CompilerParams(
            dimension_semantics=("arbitrary", "arbitrary", "arbitrary"), vmem_limit_bytes=VMEM_LIMIT),
        name="norm_proj",
    )(h, g.reshape(1, d), scale, shift, w)


def _norm_swiglu_kernel(h_ref, g_ref, sc_ref, sh_ref, wg_ref, wu_ref, o_ref, a_scr):
    @pl.when(pl.program_id(2) == 0)
    def _():
        a_scr[...] = _modnorm_rows(h_ref, g_ref, sc_ref, sh_ref)

    a = a_scr[...]
    gate = jnp.dot(a, wg_ref[...], preferred_element_type=F32)
    up = jnp.dot(a, wu_ref[...], preferred_element_type=F32)
    o_ref[0] = (gate * jax.nn.sigmoid(gate) * up).astype(o_ref.dtype)


def _norm_swiglu(h, g, scale, shift, w_gu, tm, tn):
    b, t, d = h.shape
    f = w_gu.shape[1] // 2
    nj = f // tn
    return pl.pallas_call(
        _norm_swiglu_kernel,
        grid=(b, t // tm, nj),
        in_specs=[pl.BlockSpec((1, tm, d), lambda i, m, j: (i, m, 0)),
                  pl.BlockSpec((1, d), lambda i, m, j: (0, 0)),
                  pl.BlockSpec((1, 1, d), lambda i, m, j: (i, 0, 0)),
                  pl.BlockSpec((1, 1, d), lambda i, m, j: (i, 0, 0)),
                  pl.BlockSpec((d, tn), lambda i, m, j: (0, j)),
                  pl.BlockSpec((d, tn), lambda i, m, j: (0, j + nj))],
        out_specs=pl.BlockSpec((1, tm, tn), lambda i, m, j: (i, m, j)),
        out_shape=jax.ShapeDtypeStruct((b, t, f), BF16),
        scratch_shapes=[pltpu.VMEM((tm, d), BF16)],
        compiler_params=pltpu.CompilerParams(
            dimension_semantics=("arbitrary", "arbitrary", "arbitrary"), vmem_limit_bytes=VMEM_LIMIT),
        name="norm_swiglu",
    )(h, g.reshape(1, d), scale, shift, w_gu, w_gu)


def _proj_residual_kernel(a_ref, w_ref, h_ref, gate_ref, o_ref):
    o_ref[0] = h_ref[0] + gate_ref[0] * jnp.dot(a_ref[0], w_ref[...], preferred_element_type=F32)


def _proj_residual(a, w, h, gate, tm, tn):
    b, t, k = a.shape
    d = w.shape[1]
    return pl.pallas_call(
        _proj_residual_kernel,
        grid=(b, t // tm, d // tn),
        in_specs=[pl.BlockSpec((1, tm, k), lambda i, m, j: (i, m, 0)),
                  pl.BlockSpec((k, tn), lambda i, m, j: (0, j)),
                  pl.BlockSpec((1, tm, tn), lambda i, m, j: (i, m, j)),
                  pl.BlockSpec((1, 1, tn), lambda i, m, j: (i, 0, j))],
        out_specs=pl.BlockSpec((1, tm, tn), lambda i, m, j: (i, m, j)),
        out_shape=jax.ShapeDtypeStruct((b, t, d), F32),
        compiler_params=pltpu.CompilerParams(
            dimension_semantics=("arbitrary", "arbitrary", "arbitrary"), vmem_limit_bytes=VMEM_LIMIT),
        name="proj_residual",
    )(a, w, h, gate)


def _final_norm_kernel(h_ref, g_ref, o_ref):
    x = h_ref[0]
    o_ref[0] = x * lax.rsqrt(jnp.mean(x * x, axis=-1, keepdims=True) + EPS) * g_ref[...]


def _final_norm(h, g, tm):
    b, t, d = h.shape
    return pl.pallas_call(
        _final_norm_kernel,
        grid=(b, t // tm),
        in_specs=[pl.BlockSpec((1, tm, d), lambda i, m: (i, m, 0)),
                  pl.BlockSpec((1, d), lambda i, m: (0, 0))],
        out_specs=pl.BlockSpec((1, tm, d), lambda i, m: (i, m, 0)),
        out_shape=jax.ShapeDtypeStruct((b, t, d), F32),
        compiler_params=pltpu.CompilerParams(
            dimension_semantics=("arbitrary", "arbitrary"), vmem_limit_bytes=VMEM_LIMIT),
        name="final_norm",
    )(h, g.reshape(1, d))


RW_CHAINS = 32
RW_KLO = LANES // RW_CHAINS
RW_KQ = RW_HEAD // RW_KLO
RW_VM = RW_HEAD // SUBLANES
RW_VGROUP = 4
RW_STEP_BLOCK = 16
OP_W, OP_B, OP_K, OP_R, OP_KK = range(5)


def _lane_group_sum(p):
    assert RW_KLO == 4
    half = p + pltpu.roll(p, 2 * RW_CHAINS, 1)
    return half + pltpu.roll(half, RW_CHAINS, 1)


def _rwkv_kernel(opsf_ref, opsb_ref, vf_ref, vb_ref, s0_ref, yf_ref, yb_ref, sfin_ref,
                 s_ref, saf_ref, sab_ref, pb_ref):
    tb = opsf_ref.shape[0]
    dirs = ((opsf_ref, vf_ref, yf_ref), (opsb_ref, vb_ref, yb_ref))

    @pl.when(pl.program_id(0) == 0)
    def _():
        s_ref[...] = s0_ref[...]

    zero = jnp.zeros((SUBLANES, LANES), F32)
    groups = [range(m0, m0 + RW_VGROUP) for m0 in range(0, RW_VM, RW_VGROUP)]

    for d, (ops_ref, _, _) in enumerate(dirs):
        first = 0 if d == 0 else tb - 1
        for ms in groups:
            acc = {m: zero for m in ms}
            for q in range(RW_KQ):
                kk = ops_ref[first, OP_KK, q:q + 1, :]
                for m in ms:
                    acc[m] = acc[m] + s_ref[d, q, m] * kk
            for m in ms:
                if d == 0:
                    saf_ref[m] = _lane_group_sum(acc[m])
                else:
                    pb_ref[m] = acc[m]

    def advance(d, i, sa_ref, next_ref, reduce_next):
        ops_ref, v_ref, y_ref = dirs[d]
        t = i if d == 0 else tb - 1 - i
        tn = jnp.minimum(i + 1, tb - 1) if d == 0 else jnp.maximum(tb - 2 - i, 0)
        for ms in groups:
            sa = {m: sa_ref[m] for m in ms}
            vt = {m: v_ref[t, m] for m in ms}
            acc_y = {m: zero for m in ms}
            acc_s = {m: zero for m in ms}
            for q in range(RW_KQ):
                w = ops_ref[t, OP_W, q:q + 1, :]
                b = ops_ref[t, OP_B, q:q + 1, :]
                kt = ops_ref[t, OP_K, q:q + 1, :]
                r = ops_ref[t, OP_R, q:q + 1, :]
                kkn = ops_ref[tn, OP_KK, q:q + 1, :]
                for m in ms:
                    s = s_ref[d, q, m] * w - sa[m] * b + vt[m] * kt
                    s_ref[d, q, m] = s
                    acc_y[m] = acc_y[m] + s * r
                    acc_s[m] = acc_s[m] + s * kkn
            for m in ms:
                y_ref[t, m] = acc_y[m]
                next_ref[m] = _lane_group_sum(acc_s[m]) if reduce_next else acc_s[m]

    def step(i, carry):
        for m in range(RW_VM):
            sab_ref[m] = _lane_group_sum(pb_ref[m])
        advance(0, i, saf_ref, saf_ref, True)
        advance(1, i, sab_ref, pb_ref, False)
        return carry

    lax.fori_loop(0, tb, step, 0)

    @pl.when(pl.program_id(0) == pl.num_programs(0) - 1)
    def _():
        sfin_ref[...] = s_ref[...]


RW_STATE_SHAPE = (2, RW_KQ, RW_VM, SUBLANES, LANES)


def _rwkv_scan(ops_f, ops_b, v_f, v_b, s0):
    t = ops_f.shape[0]
    tb = RW_STEP_BLOCK
    nblk = t // tb
    fwd = lambda i: (i, 0, 0, 0)
    bwd = lambda i: (nblk - 1 - i, 0, 0, 0)
    ops_block = (tb, 5, RW_KQ, LANES)
    v_block = (tb, RW_VM, SUBLANES, LANES)
    y_shape = jax.ShapeDtypeStruct((t, RW_VM, SUBLANES, LANES), F32)
    state_spec = pl.BlockSpec(RW_STATE_SHAPE, lambda i: (0,) * len(RW_STATE_SHAPE))
    return pl.pallas_call(
        _rwkv_kernel,
        grid=(nblk,),
        in_specs=[pl.BlockSpec(ops_block, fwd), pl.BlockSpec(ops_block, bwd),
                  pl.BlockSpec(v_block, fwd), pl.BlockSpec(v_block, bwd), state_spec],
        out_specs=[pl.BlockSpec(v_block, fwd), pl.BlockSpec(v_block, bwd), state_spec],
        out_shape=[y_shape, y_shape, jax.ShapeDtypeStruct(RW_STATE_SHAPE, F32)],
        scratch_shapes=[pltpu.VMEM(RW_STATE_SHAPE, F32)]
        + [pltpu.VMEM((RW_VM, SUBLANES, LANES), F32)] * 3,
        compiler_params=pltpu.CompilerParams(
            dimension_semantics=("arbitrary",), vmem_limit_bytes=VMEM_LIMIT),
        name="rwkv_scan",
    )(ops_f, ops_b, v_f, v_b, s0)


RW_PREP_BLOCK = 128
RW_PREP_VMEM_LIMIT = 56 * 1024 * 1024
RW_NPIECE = 3


def _rw_channel_perm():
    return np.arange(GROUP).reshape(RW_HEADS, RW_HEAD).T.reshape(-1)


def _rw_relayout_matrices(nb):
    slab = LANES
    ch_per_slab = slab // RW_HEADS
    pk = np.zeros((nb, RW_NPIECE, slab, ch_per_slab // RW_KLO, RW_KLO, nb, RW_HEADS), np.float32)
    pv = np.zeros((nb, RW_NPIECE, slab, ch_per_slab, RW_KLO, nb, RW_HEADS), np.float32)
    for b in range(nb):
        for ch in range(ch_per_slab):
            for h in range(RW_HEADS):
                pk[b, :, ch * RW_HEADS + h, ch // RW_KLO, ch % RW_KLO, b, h] = 1.0
                pv[b, :, ch * RW_HEADS + h, ch, :, b, h] = 1.0
    rows = nb * RW_NPIECE * slab
    return (jnp.asarray(pk.reshape(rows, -1), BF16), jnp.asarray(pv.reshape(rows, -1), BF16))


def _split3(x):
    hi = x.astype(BF16)
    r1 = x - hi.astype(F32)
    mid = r1.astype(BF16)
    lo = (r1 - mid.astype(F32)).astype(BF16)
    return hi, mid, lo


def _rw_prep_kernel(rkv_ref, wa_ref, rkv_nb_ref, wa_nb_ref, mu_rkv_ref, mu_wa_ref, w0_ref, w2_ref,
                    a0_ref, a2_ref, kk_ref, ka_ref, rk_ref, hsum_ref, pk_ref, pv_ref,
                    ops_ref, v_ref, bonus_ref, *, backward):
    nb, tb, _ = rkv_ref.shape
    n = pl.program_id(0)
    edge = (n == pl.num_programs(0) - 1) if backward else (n == 0)
    nb_row = SUBLANES - 1 if not backward else 0
    row = lax.broadcasted_iota(jnp.int32, (tb, 1), 0)
    edge_row = (row == tb - 1) if backward else (row == 0)

    def shifted(x, neighbour):
        inner = pltpu.roll(x, tb - 1 if backward else 1, 0)
        outer = jnp.where(edge, 0.0, neighbour)
        return jnp.where(edge_row, outer, inner)

    def head_sum(z):
        hi, lo = _split_bf16(z)
        return _dot(hi, hsum_ref[...]) + _dot(lo, hsum_ref[...])

    feats = []
    for b in range(nb):
        x = rkv_ref[b]
        x = x + (shifted(x, rkv_nb_ref[b, nb_row:nb_row + 1, :]) - x) * mu_rkv_ref[...]
        xw = wa_ref[b]
        xw = xw + (shifted(xw, wa_nb_ref[b, nb_row:nb_row + 1, :]) - xw) * mu_wa_ref[...]
        r, k, v = x[:, :GROUP], x[:, GROUP:2 * GROUP], x[:, 2 * GROUP:]
        decay = jnp.exp(-RW_DECAY_SCALE * jax.nn.sigmoid(
            w0_ref[...] + _dot(jnp.tanh(xw).astype(BF16), w2_ref[...])))
        a = jax.nn.sigmoid(a0_ref[...] + _dot(xw.astype(BF16), a2_ref[...]))
        kq = k * kk_ref[...]
        kk = kq * lax.rsqrt(head_sum(kq * kq) + EPS)
        kt = k * (1.0 + (a - 1.0) * ka_ref[...])
        bonus_ref[b] = head_sum(r * kt * rk_ref[...]) * v
        feats.append([decay, kk * a, kt, r, kk, v])

    nslab = GROUP // LANES
    kw = pk_ref.shape[1]
    vw = pv_ref.shape[1]
    for j in range(6):
        pieces = [_split3(feats[b][j]) for b in range(nb)]
        lhs = jnp.concatenate(
            [jnp.concatenate([p[:, g * LANES:(g + 1) * LANES] for bp in pieces for p in bp], axis=1)
             for g in range(nslab)], axis=0)
        if j < 5:
            out = _dot(lhs, pk_ref[...])
            for g in range(nslab):
                ops_ref[:, (j * nslab + g) * kw:(j * nslab + g + 1) * kw] = out[g * tb:(g + 1) * tb]
        else:
            out = _dot(lhs, pv_ref[...])
            for g in range(nslab):
                v_ref[:, g * vw:(g + 1) * vw] = out[g * tb:(g + 1) * tb]


def _rw_prep(proj, lp, dirn):
    nb, t, _ = proj.shape
    tb = min(RW_PREP_BLOCK, t)
    assert t % tb == 0 and nb * RW_HEADS == RW_CHAINS
    nblk = t // tb
    backward = dirn == 1
    per8 = tb // SUBLANES
    last8 = t // SUBLANES - 1
    if backward:
        nb_idx = lambda n: jnp.minimum((n + 1) * per8, last8)
    else:
        nb_idx = lambda n: jnp.maximum(n * per8 - 1, 0)
    perm = _rw_channel_perm()
    lane_row = lambda v: v.reshape(1, -1)
    pad_rows = lambda w, lo: jnp.zeros((LANES, GROUP), F32).at[lo:lo + w.shape[0]].set(w)
    mu_rkv = lane_row(jnp.concatenate([lp['rw_mu_rkv'][dirn][i * GROUP:(i + 1) * GROUP][perm] for i in range(3)]))
    mu_wa = lane_row(jnp.pad(lp['rw_mu_wa'][dirn], (0, LANES - RW_DECAY_LR - RW_A_LR)))
    w0 = lane_row(lp['rw_w0'][dirn][perm])
    w2 = pad_rows(lp['rw_w2'][dirn][:, perm], 0).astype(BF16)
    a0 = lane_row(lp['rw_a0'][dirn][perm])
    a2 = pad_rows(lp['rw_a2'][dirn][:, perm], RW_DECAY_LR).astype(BF16)
    k_k = lane_row(lp['rw_kk'][perm])
    k_a = lane_row(lp['rw_ka'][perm])
    r_k = lane_row(lp['rw_rk'].reshape(-1)[perm])
    lane = np.arange(GROUP)
    hsum = jnp.asarray((lane[:, None] % RW_HEADS) == (lane[None, :] % RW_HEADS), BF16)
    pk, pv = _rw_relayout_matrices(nb)
    params = (mu_rkv, mu_wa, w0, w2, a0, a2, k_k, k_a, r_k, hsum, pk, pv)
    full = lambda a: pl.BlockSpec(a.shape, lambda n: (0,) * a.ndim)
    ops_w = 5 * RW_KQ * LANES
    v_w = RW_VM * SUBLANES * LANES
    ops, vt, bonus = pl.pallas_call(
        functools.partial(_rw_prep_kernel, backward=backward),
        grid=(nblk,),
        in_specs=[pl.BlockSpec((nb, tb, 3 * GROUP), lambda n: (0, n, _col_block('r_rkv'))),
                  pl.BlockSpec((nb, tb, LANES), lambda n: (0, n, _col_block('r_wa'))),
                  pl.BlockSpec((nb, SUBLANES, 3 * GROUP), lambda n: (0, nb_idx(n), _col_block('r_rkv'))),
                  pl.BlockSpec((nb, SUBLANES, LANES), lambda n: (0, nb_idx(n), _col_block('r_wa')))]
        + [full(p) for p in params],
        out_specs=[pl.BlockSpec((tb, ops_w), lambda n: (n, 0)),
                   pl.BlockSpec((tb, v_w), lambda n: (n, 0)),
                   pl.BlockSpec((nb, tb, GROUP), lambda n: (0, n, 0))],
        out_shape=[jax.ShapeDtypeStruct((t, ops_w), F32), jax.ShapeDtypeStruct((t, v_w), F32),
                   jax.ShapeDtypeStruct((nb, t, GROUP), F32)],
        compiler_params=pltpu.CompilerParams(
            dimension_semantics=("arbitrary",), vmem_limit_bytes=RW_PREP_VMEM_LIMIT),
        name="rw_prep",
    )(proj, proj, proj, proj, *params)
    return ops.reshape(t, 5, RW_KQ, LANES), vt.reshape(t, RW_VM, SUBLANES, LANES), bonus


def _heads(t, n_heads):
    return t.reshape(t.shape[:-1] + (n_heads, t.shape[-1] // n_heads))


def _flip(t):
    return jnp.flip(t, axis=1)


def _l2norm(x):
    return x * lax.rsqrt(jnp.sum(x * x, axis=-1, keepdims=True) + EPS)


def _rw_group(proj_c, proj_x, lp):
    nb = proj_x.shape[0]
    perm = _rw_channel_perm()
    s = jnp.zeros(RW_STATE_SHAPE, F32)
    outs = []
    for proj in (proj_c, proj_x):
        t = proj.shape[1]
        ops_f, v_f, bonus_f = _rw_prep(proj, lp, 0)
        ops_b, v_b, bonus_b = _rw_prep(proj, lp, 1)
        y_f, y_b, s = _rwkv_scan(ops_f, ops_b, v_f, v_b, s)
        yy = jnp.sum((y_f + y_b).reshape(t, RW_HEAD, RW_KLO, nb, RW_HEADS), axis=2)
        yy = jnp.transpose(yy, (2, 0, 1, 3))
        mu = jnp.mean(yy, axis=2, keepdims=True)
        var = jnp.mean(jnp.square(yy - mu), axis=2, keepdims=True)
        yn = ((yy - mu) * lax.rsqrt(var + RW_GN_EPS)).reshape(nb, t, GROUP)
        y_rw = yn * lp['rw_gn_w'][perm] + lp['rw_gn_b'][perm] + (bonus_f + bonus_b)
        outs.append(y_rw * (jax.nn.sigmoid(_seg(proj, 'r_g')) @ lp['rw_g2'][:, perm]))
    return outs


def _to_chunks(t):
    return t.reshape((t.shape[0], t.shape[1] // CHUNK, CHUNK) + t.shape[2:])


def _rms_heads(o, g):
    return o * lax.rsqrt(jnp.mean(o * o, axis=-1, keepdims=True) + EPS) * g


def _conv3_rows(x, w, row_len):
    b, t, ch = x.shape
    rows = t // row_len
    xp = jnp.pad(x.reshape(b, rows, row_len, ch), ((0, 0), (0, 0), (1, 1), (0, 0)))
    y = xp[:, :, :-2] * w[0] + xp[:, :, 1:-1] * w[1] + xp[:, :, 2:] * w[2]
    return y.reshape(b, t, ch)


def _gla_scan(q, k, v, log_f, s0):
    b, t, h, dv = v.shape
    q, k, v, log_f = (_to_chunks(a) for a in (q, k, v, log_f))
    cum = jnp.cumsum(log_f, axis=2)
    last = cum[:, :, -1:]
    q_dec = q * jnp.exp(cum)
    k_inv = k * jnp.exp(-cum)
    k_end = k * jnp.exp(last - cum)
    lower = jnp.tril(jnp.ones((CHUNK, CHUNK), bool))
    att = jnp.where(lower, jnp.einsum('bnihd,bnjhd->bnhij', q_dec, k_inv), 0.0)
    o_intra = jnp.einsum('bnhij,bnjhv->bnihv', att, v)
    d_state = jnp.einsum('bnjhd,bnjhv->nbhdv', k_end, v)
    chunk_decay = jnp.moveaxis(jnp.exp(last[:, :, 0]), 1, 0)

    def step(s, inp):
        dec, ds = inp
        return dec[..., None] * s + ds, s

    s_fin, s_start = lax.scan(step, s0, (chunk_decay, d_state))
    o_inter = jnp.einsum('bnihd,nbhdv->bnihv', q_dec, s_start)
    return (o_intra + o_inter).reshape(b, t, h, dv), s_fin


def _gdn_scan(q, k, v, log_a, beta, s0):
    b, t, h, dv = v.shape
    dk = q.shape[-1]
    hm = lambda a: jnp.moveaxis(_to_chunks(a), 3, 2)
    q, k, v, log_a, beta = (hm(a) for a in (q, k, v, log_a, beta))
    q = q * dk ** -0.5
    cum = jnp.cumsum(log_a, axis=-1)
    lower = jnp.tril(jnp.ones((CHUNK, CHUNK), bool))
    strict = jnp.tril(jnp.ones((CHUNK, CHUNK), bool), -1)
    decay = jnp.exp(jnp.where(lower, cum[..., :, None] - cum[..., None, :], -jnp.inf))
    kk = jnp.einsum('bnhid,bnhjd->bnhij', k, k)
    lmat = jnp.where(strict, beta[..., None] * kk * decay, 0.0) + jnp.eye(CHUNK, dtype=kk.dtype)
    rhs = jnp.concatenate([v * beta[..., None], k * (beta * jnp.exp(cum))[..., None]], axis=-1)
    sol = lax.linalg.triangular_solve(lmat, rhs, left_side=True, lower=True, unit_diagonal=True)
    u, w = sol[..., :dv], sol[..., dv:]
    a_qk = jnp.einsum('bnhid,bnhjd->bnhij', q, k) * decay
    k_end = k * jnp.exp(cum[..., -1:] - cum)[..., None]
    chunk_decay = jnp.exp(cum[..., -1])
    cm = lambda a: jnp.moveaxis(a, 1, 0)

    def step(s, inp):
        u_i, w_i, ke_i, dec_i = inp
        v_new = u_i - jnp.einsum('bhcd,bhdv->bhcv', w_i, s)
        s_next = dec_i[..., None, None] * s + jnp.einsum('bhcd,bhcv->bhdv', ke_i, v_new)
        return s_next, (s, v_new)

    s_fin, (s_start, v_new) = lax.scan(step, s0, (cm(u), cm(w), cm(k_end), cm(chunk_decay)))
    o = (jnp.einsum('bnhcd,nbhdv->bnhcv', q * jnp.exp(cum)[..., None], s_start)
         + jnp.einsum('bnhij,nbhjv->bnhiv', a_qk, v_new))
    return jnp.moveaxis(o, 2, 3).reshape(b, t, h, dv), s_fin


def _seg(proj, name):
    o, wd, _ = SEG[name]
    return proj[..., o:o + wd]


TOKEN_BLOCK = 256
HI = lax.Precision.HIGHEST


def _dot(a, b, precision=None):
    return jnp.dot(a, b, preferred_element_type=F32, precision=precision)


def _dot_nt(a, b):
    return lax.dot_general(a, b, (((1,), (1,)), ((), ())), preferred_element_type=F32)


def _dot_tn(a, b):
    return lax.dot_general(a, b, (((0,), (0,)), ((), ())), preferred_element_type=F32)


def _split_bf16(x):
    hi = x.astype(BF16)
    return hi, (x - hi.astype(F32)).astype(BF16)


def _dot3(a_hi, a_lo, b_hi, b_lo):
    return _dot(a_hi, b_hi) + (_dot(a_hi, b_lo) + _dot(a_lo, b_hi))


def _silu(x):
    return x * jax.nn.sigmoid(x)


def _chunk_masks(reverse):
    ri = lax.broadcasted_iota(jnp.int32, (CHUNK, CHUNK), 0)
    ci = lax.broadcasted_iota(jnp.int32, (CHUNK, CHUNK), 1)
    incl = (ci >= ri) if reverse else (ci <= ri)
    strict = (ci > ri) if reverse else (ci < ri)
    return incl, strict, (ri == ci).astype(F32)


def _conv3_block(x, cw, row_len):
    tb = x.shape[0]
    pos = lax.broadcasted_iota(jnp.int32, (tb, 1), 0) % row_len
    x_prev = jnp.where(pos == 0, 0.0, pltpu.roll(x, 1, 0))
    x_next = jnp.where(pos == row_len - 1, 0.0, pltpu.roll(x, tb - 1, 0))
    return x_prev * cw[0:1] + x * cw[1:2] + x_next * cw[2:3]


def _gdn_kernel(*refs, reverse, finalize, row_len, dirn):
    if finalize:
        (qkv_ref, ab_ref, z_ref, ob_ref, cw_ref, alog_ref, dtb_ref, g_ref, s0_ref,
         o_ref, sfin_ref, s_scr) = refs
    else:
        qkv_ref, ab_ref, cw_ref, alog_ref, dtb_ref, s0_ref, o_ref, sfin_ref, s_scr = refs
    n = pl.program_id(1)

    @pl.when(n == 0)
    def _():
        s_scr[...] = s0_ref[0]

    tb = qkv_ref.shape[1]
    qkv = _silu(_conv3_block(qkv_ref[0], cw_ref[...], row_len))
    ab = ab_ref[0]
    log_a = -jnp.exp(alog_ref[...]) * jax.nn.softplus(ab + dtb_ref[...])
    beta = jax.nn.sigmoid(ab)
    incl, strict, eye = _chunk_masks(reverse)
    tri = incl.astype(F32)
    last = 0 if reverse else CHUNK - 1
    hd = GDN_DK
    nchunk = tb // CHUNK
    order = list(range(nchunk - 1, -1, -1) if reverse else range(nchunk))
    rows = lambda c: slice(c * CHUNK, (c + 1) * CHUNK)
    cum_all = _dot(tri, jnp.concatenate([log_a[rows(c)] for c in range(nchunk)], axis=1), HI)

    items = [(c, h) for c in order for h in range(GDN_HEADS)]
    pre = {}
    for c, h in items:
        col = dirn * GDN_HEADS + h
        cum = cum_all[:, c * LANES + col:c * LANES + col + 1]
        bet = beta[rows(c), 2 * GDN_HEADS + col:2 * GDN_HEADS + col + 1]
        q = _l2norm(qkv[rows(c), h * hd:(h + 1) * hd]) * hd ** -0.5
        k = _l2norm(qkv[rows(c), GROUP + h * hd:GROUP + (h + 1) * hd])
        v = qkv[rows(c), 2 * GROUP + h * hd:2 * GROUP + (h + 1) * hd]
        cum_row = jnp.sum(eye * cum, axis=0, keepdims=True)
        decay = jnp.exp(jnp.where(incl, cum - cum_row, -jnp.inf))
        kb = k.astype(BF16)
        a = jnp.where(strict, bet * _dot_nt(kb, kb) * decay, 0.0)
        ecum = jnp.exp(cum)
        total = cum[last:last + 1]
        pre[c, h] = dict(
            x=-a, rhs=jnp.concatenate([v * bet, k * (bet * ecum)], axis=1),
            a_qk=(_dot_nt(q.astype(BF16), kb) * decay).astype(BF16),
            qe=(q * ecum).astype(BF16), k_end=(k * jnp.exp(total - cum)).astype(BF16),
            dec=jnp.exp(total))
    for p in pre.values():
        p['inv'] = eye + p['x']
    for level in range(6):
        for p in pre.values():
            x_hi, x_lo = _split_bf16(p['x'])
            if level == 0:
                p['x'] = _dot3(x_hi, x_lo, x_hi, x_lo)
            elif level < 5:
                i_hi, i_lo = _split_bf16(p['inv'])
                both = _dot3(jnp.concatenate([x_hi, i_hi], axis=0), jnp.concatenate([x_lo, i_lo], axis=0),
                             x_hi, x_lo)
                p['x'] = both[:CHUNK]
                p['inv'] = p['inv'] + both[CHUNK:]
            else:
                i_hi, i_lo = _split_bf16(p['inv'])
                p['inv'] = p['inv'] + _dot3(i_hi, i_lo, x_hi, x_lo)
    for p in pre.values():
        i_hi, i_lo = _split_bf16(p['inv'])
        r_hi, r_lo = _split_bf16(p['rhs'])
        sol = _dot3(i_hi, i_lo, r_hi, r_lo)
        p['u'], p['w'] = sol[:, :GDN_DV], sol[:, GDN_DV:].astype(BF16)

    for c, h in items:
        p = pre[c, h]
        s = s_scr[h]
        sb = s.astype(BF16)
        v_new = p['u'] - _dot(p['w'], sb)
        vb = v_new.astype(BF16)
        o = _dot(p['qe'], sb) + _dot(p['a_qk'], vb)
        s_scr[h] = p['dec'] * s + _dot_tn(p['k_end'], vb)
        cols = slice(h * GDN_DV, (h + 1) * GDN_DV)
        if finalize:
            o = o + ob_ref[0, rows(c), cols]
            o = o * lax.rsqrt(jnp.mean(o * o, axis=-1, keepdims=True) + EPS) * g_ref[...]
            o = o * _silu(z_ref[0, rows(c), cols])
        o_ref[0, rows(c), cols] = o.astype(o_ref.dtype)

    @pl.when(n == pl.num_programs(1) - 1)
    def _():
        sfin_ref[0] = s_scr[...]


def _gdn_pass(proj, o_other, s0, lp, *, dirn, row_len):
    b, t, _ = proj.shape
    tb = min(TOKEN_BLOCK, t)
    assert t % tb == 0 and tb % row_len == 0
    nblk = t // tb
    reverse = dirn == 1
    finalize = o_other is not None
    tok = (lambda i, n: (i, nblk - 1 - n)) if reverse else (lambda i, n: (i, n))
    seg = lambda name, wd: pl.BlockSpec((1, tb, wd), lambda i, n: tok(i, n) + (_col_block(name),))
    full = lambda a: pl.BlockSpec(a.shape, lambda i, n: (0,) * a.ndim)
    lane_row = lambda vals: jnp.pad(vals.reshape(1, -1), ((0, 0), (0, LANES - vals.size)))
    cw = lp['gdn_conv']
    alog = lane_row(lp['gdn_a_log'])
    dtb = lane_row(lp['gdn_dt_bias'])
    g = lp['gdn_norm_g'].reshape(1, GDN_DV)
    state_spec = pl.BlockSpec((1, GDN_HEADS, GDN_DK, GDN_DV), lambda i, n: (i, 0, 0, 0))
    if finalize:
        args = (proj, proj, proj, o_other, cw, alog, dtb, g, s0)
        in_specs = [seg('d_qkv', 3 * GROUP), seg('d_ab', LANES), seg('d_z', GROUP),
                    pl.BlockSpec((1, tb, GROUP), lambda i, n: tok(i, n) + (0,)),
                    full(cw), full(alog), full(dtb), full(g), state_spec]
    else:
        args = (proj, proj, cw, alog, dtb, s0)
        in_specs = [seg('d_qkv', 3 * GROUP), seg('d_ab', LANES), full(cw), full(alog), full(dtb), state_spec]
    return pl.pallas_call(
        functools.partial(_gdn_kernel, reverse=reverse, finalize=finalize, row_len=row_len, dirn=dirn),
        grid=(b, nblk),
        in_specs=in_specs,
        out_specs=[pl.BlockSpec((1, tb, GROUP), lambda i, n: tok(i, n) + (0,)), state_spec],
        out_shape=[jax.ShapeDtypeStruct((b, t, GROUP), BF16 if finalize else F32),
                   jax.ShapeDtypeStruct((b, GDN_HEADS, GDN_DK, GDN_DV), F32)],
        scratch_shapes=[pltpu.VMEM((GDN_HEADS, GDN_DK, GDN_DV), F32)],
        compiler_params=pltpu.CompilerParams(
            dimension_semantics=("arbitrary", "arbitrary"), vmem_limit_bytes=VMEM_LIMIT),
        name="gdn_fwd" if finalize else "gdn_bwd",
    )(*args)


def _gla_kernel(*refs, reverse, finalize):
    if finalize:
        (q_ref, k_ref, v_ref, lo_ref, r_ref, ob_ref, up_ref, ab_ref, g_ref, s0_ref,
         o_ref, sfin_ref, s_scr) = refs
    else:
        q_ref, k_ref, v_ref, lo_ref, up_ref, ab_ref, s0_ref, o_ref, sfin_ref, s_scr = refs
    n = pl.program_id(1)

    @pl.when(n == 0)
    def _():
        s_scr[...] = s0_ref[0]

    tb = q_ref.shape[1]
    gate = _dot(lo_ref[0].astype(BF16), up_ref[...].astype(BF16)) + ab_ref[...]
    log_f = jax.nn.log_sigmoid(gate) / GLA_TAU
    incl, _, eye = _chunk_masks(reverse)
    tri = incl.astype(F32)
    last = 0 if reverse else CHUNK - 1
    nchunk = tb // CHUNK
    order = range(nchunk - 1, -1, -1) if reverse else range(nchunk)
    for c in order:
        sl = slice(c * CHUNK, (c + 1) * CHUNK)
        cum = _dot(tri, log_f[sl], HI)
        total = cum[last:last + 1]
        q_dec = q_ref[0, sl, :] * GLA_DK ** -0.5 * jnp.exp(cum)
        k = k_ref[0, sl, :]
        k_inv = (k * jnp.exp(-cum)).astype(BF16)
        k_end = (k * jnp.exp(total - cum)).astype(BF16)
        q_dec = q_dec.astype(BF16)
        dec_row = jnp.exp(total)
        for h in range(GLA_HEADS):
            kc = slice(h * GLA_DK, (h + 1) * GLA_DK)
            vc = slice(h * GLA_DV, (h + 1) * GLA_DV)
            vb = v_ref[0, sl, vc].astype(BF16)
            att = jnp.where(incl, _dot_nt(q_dec[:, kc], k_inv[:, kc]), 0.0)
            s = s_scr[h]
            o = _dot(att.astype(BF16), vb) + _dot(q_dec[:, kc], s.astype(BF16))
            dec_col = jnp.sum(eye * dec_row[:, kc], axis=1, keepdims=True)
            s_scr[h] = dec_col * s + _dot_tn(k_end[:, kc], vb)
            if finalize:
                o = o + ob_ref[0, sl, vc]
                o = o * lax.rsqrt(jnp.mean(o * o, axis=-1, keepdims=True) + EPS) * g_ref[...]
                o = o * _silu(r_ref[0, sl, vc])
            o_ref[0, sl, vc] = o.astype(o_ref.dtype)

    @pl.when(n == pl.num_programs(1) - 1)
    def _():
        sfin_ref[0] = s_scr[...]


def _gla_pass(proj, o_other, s0, lp, *, dirn):
    b, t, _ = proj.shape
    tb = min(TOKEN_BLOCK, t)
    assert t % tb == 0
    nblk = t // tb
    reverse = dirn == 1
    finalize = o_other is not None
    tok = (lambda i, n: (i, nblk - 1 - n)) if reverse else (lambda i, n: (i, n))
    seg = lambda name: pl.BlockSpec((1, tb, SEG[name][2]), lambda i, n: tok(i, n) + (_col_block(name),))
    full = lambda a: pl.BlockSpec(a.shape, lambda i, n: (0,) * a.ndim)
    hk = GLA_HEADS * GLA_DK
    up = jnp.pad(lp['gla_a_up'][dirn], ((0, LANES - GLA_LR), (0, 0)))
    ab = lp['gla_a_b'][dirn].reshape(1, hk)
    g = lp['gla_norm_g'].reshape(1, GLA_DV)
    state_spec = pl.BlockSpec((1, GLA_HEADS, GLA_DK, GLA_DV), lambda i, n: (i, 0, 0, 0))
    if finalize:
        args = (proj, proj, proj, proj, proj, o_other, up, ab, g, s0)
        in_specs = [seg('g_q'), seg('g_k'), seg('g_v'), seg('g_lo'), seg('g_r'),
                    pl.BlockSpec((1, tb, GROUP), lambda i, n: tok(i, n) + (0,)),
                    full(up), full(ab), full(g), state_spec]
    else:
        args = (proj, proj, proj, proj, up, ab, s0)
        in_specs = [seg('g_q'), seg('g_k'), seg('g_v'), seg('g_lo'), full(up), full(ab), state_spec]
    return pl.pallas_call(
        functools.partial(_gla_kernel, reverse=reverse, finalize=finalize),
        grid=(b, nblk),
        in_specs=in_specs,
        out_specs=[pl.BlockSpec((1, tb, GROUP), lambda i, n: tok(i, n) + (0,)), state_spec],
        out_shape=[jax.ShapeDtypeStruct((b, t, GROUP), BF16 if finalize else F32),
                   jax.ShapeDtypeStruct((b, GLA_HEADS, GLA_DK, GLA_DV), F32)],
        scratch_shapes=[pltpu.VMEM((GLA_HEADS, GLA_DK, GLA_DV), F32)],
        compiler_params=pltpu.CompilerParams(
            dimension_semantics=("arbitrary", "arbitrary"), vmem_limit_bytes=VMEM_LIMIT),
        name="gla_fwd" if finalize else "gla_bwd",
    )(*args)


def _two_direction_group(pass_fn, state_shape, proj_c, proj_x, **kw):
    zero = jnp.zeros((proj_x[0].shape[0],) + state_shape, F32)
    ob_c, sb_c = pass_fn(proj_c[0], None, zero, dirn=1, **proj_c[1], **kw)
    ob_x, _ = pass_fn(proj_x[0], None, sb_c, dirn=1, **proj_x[1], **kw)
    y_c, sf_c = pass_fn(proj_c[0], ob_c, zero, dirn=0, **proj_c[1], **kw)
    y_x, _ = pass_fn(proj_x[0], ob_x, sf_c, dirn=0, **proj_x[1], **kw)
    return y_c, y_x


def _gla_group_pallas(proj_c, proj_x, lp):
    return _two_direction_group(_gla_pass, (GLA_HEADS, GLA_DK, GLA_DV), (proj_c, {}), (proj_x, {}), lp=lp)


def _gdn_group_pallas(proj_c, proj_x, lp):
    return _two_direction_group(_gdn_pass, (GDN_HEADS, GDN_DK, GDN_DV),
                                (proj_c, dict(row_len=proj_c.shape[1])), (proj_x, dict(row_len=GRID_W)), lp=lp)


def _gla_group(proj, init, lp):
    g_q, g_k, g_v, g_r, g_lo = (_seg(proj, n) for n in ('g_q', 'g_k', 'g_v', 'g_r', 'g_lo'))
    q = _heads(g_q, GLA_HEADS) * GLA_DK ** -0.5
    k = _heads(g_k, GLA_HEADS)
    v = _heads(g_v, GLA_HEADS)
    log_f = [_heads(jax.nn.log_sigmoid(g_lo @ lp['gla_a_up'][i] + lp['gla_a_b'][i]) / GLA_TAU, GLA_HEADS)
             for i in range(2)]
    o_f, s_f = _gla_scan(q, k, v, log_f[0], init[0])
    o_b, s_b = _gla_scan(_flip(q), _flip(k), _flip(v), _flip(log_f[1]), init[1])
    y = _rms_heads(o_f + _flip(o_b), lp['gla_norm_g']) * jax.nn.silu(_heads(g_r, GLA_HEADS))
    return y.reshape(y.shape[:2] + (GROUP,)), (s_f, s_b)


def _gdn_group(proj, row_len, init, lp):
    b, t = proj.shape[:2]
    d_qkv, d_z, d_ab = (_seg(proj, n) for n in ('d_qkv', 'd_z', 'd_ab'))
    d_a, d_b = d_ab[..., :2 * GDN_HEADS], d_ab[..., 2 * GDN_HEADS:]
    qkv = jax.nn.silu(_conv3_rows(d_qkv, lp['gdn_conv'], row_len))
    q, k, v = (_heads(a, GDN_HEADS) for a in jnp.split(qkv, 3, axis=-1))
    q, k = _l2norm(q), _l2norm(k)
    log_a = -jnp.exp(lp['gdn_a_log']) * jax.nn.softplus(d_a.reshape(b, t, 2, GDN_HEADS) + lp['gdn_dt_bias'])
    beta = jax.nn.sigmoid(d_b.reshape(b, t, 2, GDN_HEADS))
    o_f, s_f = _gdn_scan(q, k, v, log_a[:, :, 0], beta[:, :, 0], init[0])
    o_b, s_b = _gdn_scan(_flip(q), _flip(k), _flip(v), _flip(log_a[:, :, 1]), _flip(beta[:, :, 1]), init[1])
    y = _rms_heads(o_f + _flip(o_b), lp['gdn_norm_g']) * jax.nn.silu(_heads(d_z, GDN_HEADS))
    return y.reshape(b, t, GROUP), (s_f, s_b)


def _sc_group(proj, row_len, lp):
    c_b, c_c, c_h = (_seg(proj, n) for n in ('c_b', 'c_c', 'c_h'))
    return c_b * _conv3_rows(c_c * c_h, lp['sc_conv'], row_len)


def _mixers(proj_c, proj_x, lp):
    b = proj_x.shape[0]
    z = lambda h, d1, d2: jnp.zeros((b, h, d1, d2), F32)
    gla0 = (z(GLA_HEADS, GLA_DK, GLA_DV),) * 2
    gdn0 = (z(GDN_HEADS, GDN_DK, GDN_DV),) * 2
    tc = proj_c.shape[1]
    y_gla_c, y_gla_x = _gla_group_pallas(proj_c, proj_x, lp)
    y_gdn_c, y_gdn_x = _gdn_group_pallas(proj_c, proj_x, lp)
    y_sc_c = _sc_group(proj_c, tc, lp)
    y_sc_x = _sc_group(proj_x, GRID_W, lp)
    y_rw_c, y_rw_x = _rw_group(proj_c, proj_x, lp)
    y_c = jnp.concatenate([y_gla_c, y_gdn_c, y_sc_c, y_rw_c], axis=-1).astype(BF16)
    y_x = jnp.concatenate([y_gla_x, y_gdn_x, y_sc_x, y_rw_x], axis=-1).astype(BF16)
    return y_c, y_x


def kernel(x, c, ctx, c_ctx, ada_w, ada_b, norm1_g, norm2_g, w_in, w_out, gla_a_up, gla_a_b, gla_norm_g, gdn_conv, gdn_a_log, gdn_dt_bias, gdn_norm_g, sc_conv, rw_mu_rkv, rw_mu_wa, rw_w0, rw_w2, rw_a0, rw_a2, rw_g2, rw_kk, rw_ka, rw_rk, rw_gn_w, rw_gn_b, ffn_w_gu, ffn_w_down, final_g):
    nb, t, d = x.shape
    tc = ctx.shape[1]
    depth = w_in.shape[0]
    assert d == D_MODEL and t % 512 == 0 and tc % CHUNK == 0 and nb + 1 <= SUBLANES

    cond = jnp.concatenate([c, c_ctx[None, :], jnp.zeros((SUBLANES - nb - 1, d), F32)], axis=0)
    mod = _modulation(cond, ada_w, ada_b)

    tm_x = 512
    tm_c = tc
    h_x, h_c = x, ctx
    for l in range(depth):
        lp = dict(gla_a_up=gla_a_up[l], gla_a_b=gla_a_b[l],
                  gla_norm_g=gla_norm_g[l], gdn_conv=gdn_conv[l], gdn_a_log=gdn_a_log[l],
                  gdn_dt_bias=gdn_dt_bias[l], gdn_norm_g=gdn_norm_g[l], sc_conv=sc_conv[l],
                  rw_mu_rkv=rw_mu_rkv[l], rw_mu_wa=rw_mu_wa[l], rw_w0=rw_w0[l], rw_w2=rw_w2[l],
                  rw_a0=rw_a0[l], rw_a2=rw_a2[l], rw_g2=rw_g2[l], rw_kk=rw_kk[l], rw_ka=rw_ka[l],
                  rw_rk=rw_rk[l], rw_gn_w=rw_gn_w[l], rw_gn_b=rw_gn_b[l])
        m_x = [mod[l, :nb, i * d:(i + 1) * d][:, None, :] for i in range(6)]
        m_c = [jnp.broadcast_to(mod[l, nb, i * d:(i + 1) * d][None, None, :], (nb, 1, d)) for i in range(6)]
        w_in_l = _pad_w_in(w_in[l])
        w_out_l = jnp.concatenate([w_out[l][:3 * GROUP], w_out[l][3 * GROUP:][_rw_channel_perm()]],
                                  axis=0).astype(BF16)
        w_gu_l = ffn_w_gu[l].astype(BF16)
        w_dn_l = ffn_w_down[l].astype(BF16)

        proj_c = _norm_proj(h_c, norm1_g[l], m_c[1], m_c[0], w_in_l, tm_c, 1024)
        proj_x = _norm_proj(h_x, norm1_g[l], m_x[1], m_x[0], w_in_l, 2 * tm_x, 1024)
        y_c, y_x = _mixers(proj_c, proj_x, lp)

        h_x = _proj_residual(y_x, w_out_l, h_x, m_x[2], 2 * tm_x, 1024)
        a_x = _norm_swiglu(h_x, norm2_g[l], m_x[4], m_x[3], w_gu_l, 2 * tm_x, 512)
        h_x = _proj_residual(a_x, w_dn_l, h_x, m_x[5], tm_x, 512)
        if l < depth - 1:
            h_c = _proj_residual(y_c, w_out_l, h_c, m_c[2], tm_c, 512)
            a_c = _norm_swiglu(h_c, norm2_g[l], m_c[4], m_c[3], w_gu_l, tm_c, 512)
            h_c = _proj_residual(a_c, w_dn_l, h_c, m_c[5], tm_c, 512)
    return _final_norm(h_x, final_g, tm_x)
```

```python
import functools
import math

import numpy as np
import jax
import jax.numpy as jnp
from jax import lax
from jax.experimental import pallas as pl
from jax.experimental.pallas import tpu as pltpu

F32 = jnp.float32
BF16 = jnp.bfloat16

D_MODEL = 2048
GROUP = D_MODEL // 4
CHUNK = 64
EPS = 1e-6
GRID_W = 64

GLA_HEADS = 4
GLA_DK = 64
GLA_DV = 128
GLA_LR = 16
GLA_TAU = 16.0
GDN_HEADS = 4
GDN_DK = 128
GDN_DV = 128
RW_HEAD = 64
RW_HEADS = GROUP // RW_HEAD
RW_DECAY_LR = 32
RW_A_LR = 32
RW_GATE_LR = 96
RW_DECAY_SCALE = math.exp(-0.5)
RW_GN_EPS = 64e-5
FFN = -(-(8 * D_MODEL) // (3 * 256)) * 256

SUBLANES = 8
LANES = 128
VMEM_LIMIT = 48 * 1024 * 1024

_SRC_SEGS = (
    ('g_q', 256), ('g_k', 256), ('g_v', 512), ('g_r', 512), ('g_lo', GLA_LR),
    ('d_qkv', 1536), ('d_z', 512), ('d_ab', 16),
    ('c_b', 512), ('c_c', 512), ('c_h', 512),
    ('r_rkv', 1536), ('r_wa', 64), ('r_g', RW_GATE_LR),
)
_DST_ORDER = ('g_q', 'g_k', 'g_v', 'g_r', 'd_qkv', 'r_rkv', 'd_z', 'c_b', 'c_c', 'c_h',
              'g_lo', 'd_ab', 'r_wa', 'r_g')


def _round_up(n, m):
    return -(-n // m) * m


def _seg_layout():
    width = dict(_SRC_SEGS)
    src, o = {}, 0
    for name, wd in _SRC_SEGS:
        src[name] = o
        o += wd
    dst, o = {}, 0
    for name in _DST_ORDER:
        pw = _round_up(width[name], LANES)
        assert o % pw == 0, name
        dst[name] = (o, width[name], pw)
        o += pw
    return src, dst, o


SEG_SRC, SEG, PROJ_PAD = _seg_layout()


def _pad_w_in(w):
    parts = []
    for name in _DST_ORDER:
        _, wd, pw = SEG[name]
        seg = w[:, SEG_SRC[name]:SEG_SRC[name] + wd]
        if name == 'r_rkv':
            perm = _rw_channel_perm()
            seg = jnp.concatenate([seg[:, i * GROUP:(i + 1) * GROUP][:, perm] for i in range(3)], axis=1)
        if pw != wd:
            seg = jnp.pad(seg, ((0, 0), (0, pw - wd)))
        parts.append(seg)
    return jnp.concatenate(parts, axis=1).astype(BF16)


def _col_block(name):
    o, _, pw = SEG[name]
    return o // pw


def _mod_kernel(c_ref, w_ref, b_ref, o_ref):
    x = c_ref[...]
    x = (x * jax.nn.sigmoid(x)).astype(BF16)
    o_ref[0] = jnp.dot(x, w_ref[0].astype(BF16), preferred_element_type=F32) + b_ref[0]


def _modulation(cond, ada_w, ada_b):
    nl, d, n = ada_w.shape
    tn = 1024
    return pl.pallas_call(
        _mod_kernel,
        grid=(nl, n // tn),
        in_specs=[pl.BlockSpec((SUBLANES, d), lambda l, j: (0, 0)),
                  pl.BlockSpec((1, d, tn), lambda l, j: (l, 0, j)),
                  pl.BlockSpec((1, 1, tn), lambda l, j: (l, 0, j))],
        out_specs=pl.BlockSpec((1, SUBLANES, tn), lambda l, j: (l, 0, j)),
        out_shape=jax.ShapeDtypeStruct((nl, SUBLANES, n), F32),
        compiler_params=pltpu.CompilerParams(
            dimension_semantics=("arbitrary", "arbitrary"), vmem_limit_bytes=VMEM_LIMIT),
        name="modulation",
    )(cond, ada_w, ada_b.reshape(nl, 1, n))


def _modnorm_rows(h_ref, g_ref, sc_ref, sh_ref):
    x = h_ref[0]
    y = x * lax.rsqrt(jnp.mean(x * x, axis=-1, keepdims=True) + EPS) * g_ref[...]
    return (y * (1.0 + sc_ref[0]) + sh_ref[0]).astype(BF16)


def _norm_proj_kernel(h_ref, g_ref, sc_ref, sh_ref, w_ref, o_ref, a_scr):
    @pl.when(pl.program_id(2) == 0)
    def _():
        a_scr[...] = _modnorm_rows(h_ref, g_ref, sc_ref, sh_ref)

    o_ref[0] = jnp.dot(a_scr[...], w_ref[...], preferred_element_type=F32).astype(o_ref.dtype)


def _norm_proj(h, g, scale, shift, w, tm, tn):
    b, t, d = h.shape
    n = w.shape[1]
    return pl.pallas_call(
        _norm_proj_kernel,
        grid=(b, t // tm, n // tn),
        in_specs=[pl.BlockSpec((1, tm, d), lambda i, m, j: (i, m, 0)),
                  pl.BlockSpec((1, d), lambda i, m, j: (0, 0)),
                  pl.BlockSpec((1, 1, d), lambda i, m, j: (i, 0, 0)),
                  pl.BlockSpec((1, 1, d), lambda i, m, j: (i, 0, 0)),
                  pl.BlockSpec((d, tn), lambda i, m, j: (0, j))],
        out_specs=pl.BlockSpec((1, tm, tn), lambda i, m, j: (i, m, j)),
        out_shape=jax.ShapeDtypeStruct((b, t, n), F32),
        scratch_shapes=[pltpu.VMEM((tm, d), BF16)],
        compiler_params=pltpu.CompilerParams(
            dimension_semantics=("arbitrary", "arbitrary", "arbitrary"), vmem_limit_bytes=VMEM_LIMIT),
        name="norm_proj",
    )(h, g.reshape(1, d), scale, shift, w)


def _norm_swiglu_kernel(h_ref, g_ref, sc_ref, sh_ref, wg_ref, wu_ref, o_ref, a_scr):
    @pl.when(pl.program_id(2) == 0)
    def _():
        a_scr[...] = _modnorm_rows(h_ref, g_ref, sc_ref, sh_ref)

    a = a_scr[...]
    gate = jnp.dot(a, wg_ref[...], preferred_element_type=F32)
    up = jnp.dot(a, wu_ref[...], preferred_element_type=F32)
    o_ref[0] = (gate * jax.nn.sigmoid(gate) * up).astype(o_ref.dtype)


def _norm_swiglu(h, g, scale, shift, w_gu, tm, tn):
    b, t, d = h.shape
    f = w_gu.shape[1] // 2
    nj = f // tn
    return pl.pallas_call(
        _norm_swiglu_kernel,
        grid=(b, t // tm, nj),
        in_specs=[pl.BlockSpec((1, tm, d), lambda i, m, j: (i, m, 0)),
                  pl.BlockSpec((1, d), lambda i, m, j: (0, 0)),
                  pl.BlockSpec((1, 1, d), lambda i, m, j: (i, 0, 0)),
                  pl.BlockSpec((1, 1, d), lambda i, m, j: (i, 0, 0)),
                  pl.BlockSpec((d, tn), lambda i, m, j: (0, j)),
                  pl.BlockSpec((d, tn), lambda i, m, j: (0, j + nj))],
        out_specs=pl.BlockSpec((1, tm, tn), lambda i, m, j: (i, m, j)),
        out_shape=jax.ShapeDtypeStruct((b, t, f), BF16),
        scratch_shapes=[pltpu.VMEM((tm, d), BF16)],
        compiler_params=pltpu.CompilerParams(
            dimension_semantics=("arbitrary", "arbitrary", "arbitrary"), vmem_limit_bytes=VMEM_LIMIT),
        name="norm_swiglu",
    )(h, g.reshape(1, d), scale, shift, w_gu, w_gu)


def _proj_residual_kernel(a_ref, w_ref, h_ref, gate_ref, o_ref):
    o_ref[0] = h_ref[0] + gate_ref[0] * jnp.dot(a_ref[0], w_ref[...], preferred_element_type=F32)


def _proj_residual(a, w, h, gate, tm, tn):
    b, t, k = a.shape
    d = w.shape[1]
    return pl.pallas_call(
        _proj_residual_kernel,
        grid=(b, t // tm, d // tn),
        in_specs=[pl.BlockSpec((1, tm, k), lambda i, m, j: (i, m, 0)),
                  pl.BlockSpec((k, tn), lambda i, m, j: (0, j)),
                  pl.BlockSpec((1, tm, tn), lambda i, m, j: (i, m, j)),
                  pl.BlockSpec((1, 1, tn), lambda i, m, j: (i, 0, j))],
        out_specs=pl.BlockSpec((1, tm, tn), lambda i, m, j: (i, m, j)),
        out_shape=jax.ShapeDtypeStruct((b, t, d), F32),
        compiler_params=pltpu.CompilerParams(
            dimension_semantics=("arbitrary", "arbitrary", "arbitrary"), vmem_limit_bytes=VMEM_LIMIT),
        name="proj_residual",
    )(a, w, h, gate)


def _groups_residual_kernel(*refs):
    *a_refs, w_ref, h_ref, gate_ref, o_ref = refs
    k = a_refs[0].shape[2]
    acc = jnp.dot(a_refs[0][0], w_ref[0:k, :], preferred_element_type=F32)
    for i, a_ref in enumerate(a_refs[1:], start=1):
        acc = acc + jnp.dot(a_ref[0], w_ref[i * k:(i + 1) * k, :], preferred_element_type=F32)
    o_ref[0] = h_ref[0] + gate_ref[0] * acc


def _groups_residual(groups, w, h, gate, tm, tn):
    b, t, k = groups[0].shape
    d = w.shape[1]
    assert len(groups) * k == w.shape[0]
    return pl.pallas_call(
        _groups_residual_kernel,
        grid=(b, t // tm, d // tn),
        in_specs=[pl.BlockSpec((1, tm, k), lambda i, m, j: (i, m, 0)) for _ in groups]
        + [pl.BlockSpec((w.shape[0], tn), lambda i, m, j: (0, j)),
           pl.BlockSpec((1, tm, tn), lambda i, m, j: (i, m, j)),
           pl.BlockSpec((1, 1, tn), lambda i, m, j: (i, 0, j))],
        out_specs=pl.BlockSpec((1, tm, tn), lambda i, m, j: (i, m, j)),
        out_shape=jax.ShapeDtypeStruct((b, t, d), F32),
        compiler_params=pltpu.CompilerParams(
            dimension_semantics=("arbitrary", "arbitrary", "arbitrary"), vmem_limit_bytes=VMEM_LIMIT),
        name="groups_residual",
    )(*groups, w, h, gate)


def _final_norm_kernel(h_ref, g_ref, o_ref):
    x = h_ref[0]
    o_ref[0] = x * lax.rsqrt(jnp.mean(x * x, axis=-1, keepdims=True) + EPS) * g_ref[...]


def _final_norm(h, g, tm):
    b, t, d = h.shape
    return pl.pallas_call(
        _final_norm_kernel,
        grid=(b, t // tm),
        in_specs=[pl.BlockSpec((1, tm, d), lambda i, m: (i, m, 0)),
                  pl.BlockSpec((1, d), lambda i, m: (0, 0))],
        out_specs=pl.BlockSpec((1, tm, d), lambda i, m: (i, m, 0)),
        out_shape=jax.ShapeDtypeStruct((b, t, d), F32),
        compiler_params=pltpu.CompilerParams(
            dimension_semantics=("arbitrary", "arbitrary"), vmem_limit_bytes=VMEM_LIMIT),
        name="final_norm",
    )(h, g.reshape(1, d))


RW_CHAINS = 32
RW_KLO = LANES // RW_CHAINS
RW_KQ = RW_HEAD // RW_KLO
RW_VM = RW_HEAD // SUBLANES
RW_VGROUP = 4
RW_STEP_BLOCK = 16
OP_W, OP_B, OP_K, OP_R, OP_KK = range(5)


def _lane_group_sum(p):
    out = p
    for g in range(1, RW_KLO):
        out = out + pltpu.roll(p, g * RW_CHAINS, 1)
    return out


def _rwkv_kernel(opsf_ref, opsb_ref, vf_ref, vb_ref, s0_ref, yf_ref, yb_ref, sfin_ref,
                 s_ref, saf_ref, sab_ref, pb_ref):
    tb = opsf_ref.shape[0]
    dirs = ((opsf_ref, vf_ref, yf_ref), (opsb_ref, vb_ref, yb_ref))

    @pl.when(pl.program_id(0) == 0)
    def _():
        s_ref[...] = s0_ref[...]

    zero = jnp.zeros((SUBLANES, LANES), F32)
    groups = [range(m0, m0 + RW_VGROUP) for m0 in range(0, RW_VM, RW_VGROUP)]

    for d, (ops_ref, _, _) in enumerate(dirs):
        first = 0 if d == 0 else tb - 1
        for ms in groups:
            acc = {m: zero for m in ms}
            for q in range(RW_KQ):
                kk = ops_ref[first, OP_KK, q:q + 1, :]
                for m in ms:
                    acc[m] = acc[m] + s_ref[d, q, m] * kk
            for m in ms:
                if d == 0:
                    saf_ref[m] = _lane_group_sum(acc[m])
                else:
                    pb_ref[m] = acc[m]

    def advance(d, i, sa_ref, next_ref, reduce_next):
        ops_ref, v_ref, y_ref = dirs[d]
        t = i if d == 0 else tb - 1 - i
        tn = jnp.minimum(i + 1, tb - 1) if d == 0 else jnp.maximum(tb - 2 - i, 0)
        for ms in groups:
            sa = {m: sa_ref[m] for m in ms}
            vt = {m: v_ref[t, m] for m in ms}
            acc_y = {m: zero for m in ms}
            acc_s = {m: zero for m in ms}
            for q in range(RW_KQ):
                w = ops_ref[t, OP_W, q:q + 1, :]
                b = ops_ref[t, OP_B, q:q + 1, :]
                kt = ops_ref[t, OP_K, q:q + 1, :]
                r = ops_ref[t, OP_R, q:q + 1, :]
                kkn = ops_ref[tn, OP_KK, q:q + 1, :]
                for m in ms:
                    s = s_ref[d, q, m] * w - sa[m] * b + vt[m] * kt
                    s_ref[d, q, m] = s
                    acc_y[m] = acc_y[m] + s * r
                    acc_s[m] = acc_s[m] + s * kkn
            for m in ms:
                y_ref[t, m] = acc_y[m]
                next_ref[m] = _lane_group_sum(acc_s[m]) if reduce_next else acc_s[m]

    def step(i, carry):
        for m in range(RW_VM):
            sab_ref[m] = _lane_group_sum(pb_ref[m])
        advance(0, i, saf_ref, saf_ref, True)
        advance(1, i, sab_ref, pb_ref, False)
        return carry

    lax.fori_loop(0, tb, step, 0)

    @pl.when(pl.program_id(0) == pl.num_programs(0) - 1)
    def _():
        sfin_ref[...] = s_ref[...]


RW_STATE_SHAPE = (2, RW_KQ, RW_VM, SUBLANES, LANES)


def _rwkv_scan(ops_f, ops_b, v_f, v_b, s0):
    t = ops_f.shape[0]
    tb = RW_STEP_BLOCK
    nblk = t // tb
    fwd = lambda i: (i, 0, 0, 0)
    bwd = lambda i: (nblk - 1 - i, 0, 0, 0)
    ops_block = (tb, 5, RW_KQ, LANES)
    v_block = (tb, RW_VM, SUBLANES, LANES)
    y_shape = jax.ShapeDtypeStruct((t, RW_VM, SUBLANES, LANES), F32)
    state_spec = pl.BlockSpec(RW_STATE_SHAPE, lambda i: (0,) * len(RW_STATE_SHAPE))
    return pl.pallas_call(
        _rwkv_kernel,
        grid=(nblk,),
        in_specs=[pl.BlockSpec(ops_block, fwd), pl.BlockSpec(ops_block, bwd),
                  pl.BlockSpec(v_block, fwd), pl.BlockSpec(v_block, bwd), state_spec],
        out_specs=[pl.BlockSpec(v_block, fwd), pl.BlockSpec(v_block, bwd), state_spec],
        out_shape=[y_shape, y_shape, jax.ShapeDtypeStruct(RW_STATE_SHAPE, F32)],
        scratch_shapes=[pltpu.VMEM(RW_STATE_SHAPE, F32)]
        + [pltpu.VMEM((RW_VM, SUBLANES, LANES), F32)] * 3,
        compiler_params=pltpu.CompilerParams(
            dimension_semantics=("arbitrary",), vmem_limit_bytes=VMEM_LIMIT),
        name="rwkv_scan",
    )(ops_f, ops_b, v_f, v_b, s0)


RW_PREP_BLOCK = 128
RW_PREP_VMEM_LIMIT = 56 * 1024 * 1024
RW_NPIECE = 3


def _rw_channel_perm():
    return np.arange(GROUP).reshape(RW_HEADS, RW_HEAD).T.reshape(-1)


def _rw_relayout_matrices(nb):
    slab = LANES
    ch_per_slab = slab // RW_HEADS
    pk = np.zeros((nb, RW_NPIECE, slab, ch_per_slab // RW_KLO, RW_KLO, nb, RW_HEADS), np.float32)
    pv = np.zeros((nb, RW_NPIECE, slab, ch_per_slab, RW_KLO, nb, RW_HEADS), np.float32)
    for b in range(nb):
        for ch in range(ch_per_slab):
            for h in range(RW_HEADS):
                pk[b, :, ch * RW_HEADS + h, ch // RW_KLO, ch % RW_KLO, b, h] = 1.0
                pv[b, :, ch * RW_HEADS + h, ch, :, b, h] = 1.0
    rows = nb * RW_NPIECE * slab
    return (jnp.asarray(pk.reshape(rows, -1), BF16), jnp.asarray(pv.reshape(rows, -1), BF16))


def _split3(x):
    hi = x.astype(BF16)
    r1 = x - hi.astype(F32)
    mid = r1.astype(BF16)
    lo = (r1 - mid.astype(F32)).astype(BF16)
    return hi, mid, lo


def _rw_prep_kernel(rkv_ref, wa_ref, rkv_nb_ref, wa_nb_ref, mu_rkv_ref, mu_wa_ref, w0_ref, w2_ref,
                    a0_ref, a2_ref, kk_ref, ka_ref, rk_ref, hsum_ref, pk_ref, pv_ref,
                    ops_ref, v_ref, bonus_ref, *, backward):
    nb, tb, _ = rkv_ref.shape
    n = pl.program_id(0)
    edge = (n == pl.num_programs(0) - 1) if backward else (n == 0)
    nb_row = SUBLANES - 1 if not backward else 0
    row = lax.broadcasted_iota(jnp.int32, (tb, 1), 0)
    edge_row = (row == tb - 1) if backward else (row == 0)

    def shifted(x, neighbour):
        inner = pltpu.roll(x, tb - 1 if backward else 1, 0)
        outer = jnp.where(edge, 0.0, neighbour)
        return jnp.where(edge_row, outer, inner)

    def head_sum(z):
        hi, lo = _split_bf16(z)
        return _dot(hi, hsum_ref[...]) + _dot(lo, hsum_ref[...])

    feats = []
    for b in range(nb):
        x = rkv_ref[b]
        x = x + (shifted(x, rkv_nb_ref[b, nb_row:nb_row + 1, :]) - x) * mu_rkv_ref[...]
        xw = wa_ref[b]
        xw = xw + (shifted(xw, wa_nb_ref[b, nb_row:nb_row + 1, :]) - xw) * mu_wa_ref[...]
        r, k, v = x[:, :GROUP], x[:, GROUP:2 * GROUP], x[:, 2 * GROUP:]
        decay = jnp.exp(-RW_DECAY_SCALE * jax.nn.sigmoid(
            w0_ref[...] + _dot(jnp.tanh(xw).astype(BF16), w2_ref[...])))
        a = jax.nn.sigmoid(a0_ref[...] + _dot(xw.astype(BF16), a2_ref[...]))
        kq = k * kk_ref[...]
        kk = kq * lax.rsqrt(head_sum(kq * kq) + EPS)
        kt = k * (1.0 + (a - 1.0) * ka_ref[...])
        bonus_ref[b] = head_sum(r * kt * rk_ref[...]) * v
        feats.append([decay, kk * a, kt, r, kk, v])

    nslab = GROUP // LANES
    kw = pk_ref.shape[1]
    vw = pv_ref.shape[1]
    for j in range(6):
        pieces = [_split3(feats[b][j]) for b in range(nb)]
        lhs = jnp.concatenate(
            [jnp.concatenate([p[:, g * LANES:(g + 1) * LANES] for bp in pieces for p in bp], axis=1)
             for g in range(nslab)], axis=0)
        if j < 5:
            out = _dot(lhs, pk_ref[...])
            for g in range(nslab):
                ops_ref[:, (j * nslab + g) * kw:(j * nslab + g + 1) * kw] = out[g * tb:(g + 1) * tb]
        else:
            out = _dot(lhs, pv_ref[...])
            for g in range(nslab):
                v_ref[:, g * vw:(g + 1) * vw] = out[g * tb:(g + 1) * tb]


def _rw_prep(proj, lp, dirn):
    nb, t, _ = proj.shape
    tb = min(RW_PREP_BLOCK, t)
    assert t % tb == 0 and nb * RW_HEADS == RW_CHAINS
    nblk = t // tb
    backward = dirn == 1
    per8 = tb // SUBLANES
    last8 = t // SUBLANES - 1
    if backward:
        nb_idx = lambda n: jnp.minimum((n + 1) * per8, last8)
    else:
        nb_idx = lambda n: jnp.maximum(n * per8 - 1, 0)
    perm = _rw_channel_perm()
    lane_row = lambda v: v.reshape(1, -1)
    pad_rows = lambda w, lo: jnp.zeros((LANES, GROUP), F32).at[lo:lo + w.shape[0]].set(w)
    mu_rkv = lane_row(jnp.concatenate([lp['rw_mu_rkv'][dirn][i * GROUP:(i + 1) * GROUP][perm] for i in range(3)]))
    mu_wa = lane_row(jnp.pad(lp['rw_mu_wa'][dirn], (0, LANES - RW_DECAY_LR - RW_A_LR)))
    w0 = lane_row(lp['rw_w0'][dirn][perm])
    w2 = pad_rows(lp['rw_w2'][dirn][:, perm], 0).astype(BF16)
    a0 = lane_row(lp['rw_a0'][dirn][perm])
    a2 = pad_rows(lp['rw_a2'][dirn][:, perm], RW_DECAY_LR).astype(BF16)
    k_k = lane_row(lp['rw_kk'][perm])
    k_a = lane_row(lp['rw_ka'][perm])
    r_k = lane_row(lp['rw_rk'].reshape(-1)[perm])
    lane = np.arange(GROUP)
    hsum = jnp.asarray((lane[:, None] % RW_HEADS) == (lane[None, :] % RW_HEADS), BF16)
    pk, pv = _rw_relayout_matrices(nb)
    params = (mu_rkv, mu_wa, w0, w2, a0, a2, k_k, k_a, r_k, hsum, pk, pv)
    full = lambda a: pl.BlockSpec(a.shape, lambda n: (0,) * a.ndim)
    ops_w = 5 * RW_KQ * LANES
    v_w = RW_VM * SUBLANES * LANES
    ops, vt, bonus = pl.pallas_call(
        functools.partial(_rw_prep_kernel, backward=backward),
        grid=(nblk,),
        in_specs=[pl.BlockSpec((nb, tb, 3 * GROUP), lambda n: (0, n, _col_block('r_rkv'))),
                  pl.BlockSpec((nb, tb, LANES), lambda n: (0, n, _col_block('r_wa'))),
                  pl.BlockSpec((nb, SUBLANES, 3 * GROUP), lambda n: (0, nb_idx(n), _col_block('r_rkv'))),
                  pl.BlockSpec((nb, SUBLANES, LANES), lambda n: (0, nb_idx(n), _col_block('r_wa')))]
        + [full(p) for p in params],
        out_specs=[pl.BlockSpec((tb, ops_w), lambda n: (n, 0)),
                   pl.BlockSpec((tb, v_w), lambda n: (n, 0)),
                   pl.BlockSpec((nb, tb, GROUP), lambda n: (0, n, 0))],
        out_shape=[jax.ShapeDtypeStruct((t, ops_w), F32), jax.ShapeDtypeStruct((t, v_w), F32),
                   jax.ShapeDtypeStruct((nb, t, GROUP), F32)],
        compiler_params=pltpu.CompilerParams(
            dimension_semantics=("arbitrary",), vmem_limit_bytes=RW_PREP_VMEM_LIMIT),
        name="rw_prep",
    )(proj, proj, proj, proj, *params)
    return ops.reshape(t, 5, RW_KQ, LANES), vt.reshape(t, RW_VM, SUBLANES, LANES), bonus


def _heads(t, n_heads):
    return t.reshape(t.shape[:-1] + (n_heads, t.shape[-1] // n_heads))


def _flip(t):
    return jnp.flip(t, axis=1)


def _l2norm(x):
    return x * lax.rsqrt(jnp.sum(x * x, axis=-1, keepdims=True) + EPS)


def _rw_gather_matrix(nb):
    vs = LANES // RW_HEADS
    q = np.zeros((RW_NPIECE, vs, RW_KLO, nb, RW_HEADS, nb, vs, RW_HEADS), np.float32)
    for b in range(nb):
        for v in range(vs):
            for h in range(RW_HEADS):
                q[:, v, :, b, h, b, v, h] = 1.0
    return jnp.asarray(q.reshape(RW_NPIECE * vs * LANES, nb * LANES), BF16)


def _rw_post_kernel(yf_ref, yb_ref, bf_ref, bb_ref, g_ref, gnw_ref, gnb_ref, g2_ref, hsum_ref, q_ref, o_ref):
    nb = bf_ref.shape[0]
    pieces = _split3(yf_ref[...] + yb_ref[...])
    slab = (LANES // RW_HEADS) * LANES
    nslab = GROUP // LANES
    token_major = [_dot(jnp.concatenate([p[:, g * slab:(g + 1) * slab] for p in pieces], axis=1), q_ref[...])
                   for g in range(nslab)]

    def head_sum(z):
        hi, lo = _split_bf16(z)
        return _dot(hi, hsum_ref[...]) + _dot(lo, hsum_ref[...])

    for b in range(nb):
        yy = jnp.concatenate([tm[:, b * LANES:(b + 1) * LANES] for tm in token_major], axis=1)
        d = yy - head_sum(yy) * (1.0 / RW_HEAD)
        var = head_sum(d * d) * (1.0 / RW_HEAD)
        y_rw = d * lax.rsqrt(var + RW_GN_EPS) * gnw_ref[...] + gnb_ref[...] + (bf_ref[b] + bb_ref[b])
        gate = _dot(jax.nn.sigmoid(g_ref[b]).astype(BF16), g2_ref[...])
        o_ref[b] = (y_rw * gate).astype(o_ref.dtype)


def _rw_post(y_f, y_b, bonus_f, bonus_b, proj, lp):
    nb, t, _ = bonus_f.shape
    tb = min(RW_PREP_BLOCK, t)
    perm = _rw_channel_perm()
    yw = RW_VM * SUBLANES * LANES
    gnw = lp['rw_gn_w'][perm].reshape(1, GROUP)
    gnb = lp['rw_gn_b'][perm].reshape(1, GROUP)
    g2 = jnp.zeros((LANES, GROUP), F32).at[:RW_GATE_LR].set(lp['rw_g2'][:, perm]).astype(BF16)
    lane = np.arange(GROUP)
    hsum = jnp.asarray((lane[:, None] % RW_HEADS) == (lane[None, :] % RW_HEADS), BF16)
    q = _rw_gather_matrix(nb)
    params = (gnw, gnb, g2, hsum, q)
    full = lambda a: pl.BlockSpec(a.shape, lambda n: (0,) * a.ndim)
    tok = pl.BlockSpec((nb, tb, GROUP), lambda n: (0, n, 0))
    return pl.pallas_call(
        _rw_post_kernel,
        grid=(t // tb,),
        in_specs=[pl.BlockSpec((tb, yw), lambda n: (n, 0)), pl.BlockSpec((tb, yw), lambda n: (n, 0)), tok, tok,
                  pl.BlockSpec((nb, tb, LANES), lambda n: (0, n, _col_block('r_g')))] + [full(p) for p in params],
        out_specs=tok,
        out_shape=jax.ShapeDtypeStruct((nb, t, GROUP), BF16),
        compiler_params=pltpu.CompilerParams(
            dimension_semantics=("arbitrary",), vmem_limit_bytes=RW_PREP_VMEM_LIMIT),
        name="rw_post",
    )(y_f.reshape(t, yw), y_b.reshape(t, yw), bonus_f, bonus_b, proj, *params)


def _rw_group(proj_c, proj_x, lp):
    s = jnp.zeros(RW_STATE_SHAPE, F32)
    outs = []
    for proj in (proj_c, proj_x):
        ops_f, v_f, bonus_f = _rw_prep(proj, lp, 0)
        ops_b, v_b, bonus_b = _rw_prep(proj, lp, 1)
        y_f, y_b, s = _rwkv_scan(ops_f, ops_b, v_f, v_b, s)
        outs.append(_rw_post(y_f, y_b, bonus_f, bonus_b, proj, lp))
    return outs


def _to_chunks(t):
    return t.reshape((t.shape[0], t.shape[1] // CHUNK, CHUNK) + t.shape[2:])


def _rms_heads(o, g):
    return o * lax.rsqrt(jnp.mean(o * o, axis=-1, keepdims=True) + EPS) * g


def _conv3_rows(x, w, row_len):
    b, t, ch = x.shape
    rows = t // row_len
    xp = jnp.pad(x.reshape(b, rows, row_len, ch), ((0, 0), (0, 0), (1, 1), (0, 0)))
    y = xp[:, :, :-2] * w[0] + xp[:, :, 1:-1] * w[1] + xp[:, :, 2:] * w[2]
    return y.reshape(b, t, ch)


def _gla_scan(q, k, v, log_f, s0):
    b, t, h, dv = v.shape
    q, k, v, log_f = (_to_chunks(a) for a in (q, k, v, log_f))
    cum = jnp.cumsum(log_f, axis=2)
    last = cum[:, :, -1:]
    q_dec = q * jnp.exp(cum)
    k_inv = k * jnp.exp(-cum)
    k_end = k * jnp.exp(last - cum)
    lower = jnp.tril(jnp.ones((CHUNK, CHUNK), bool))
    att = jnp.where(lower, jnp.einsum('bnihd,bnjhd->bnhij', q_dec, k_inv), 0.0)
    o_intra = jnp.einsum('bnhij,bnjhv->bnihv', att, v)
    d_state = jnp.einsum('bnjhd,bnjhv->nbhdv', k_end, v)
    chunk_decay = jnp.moveaxis(jnp.exp(last[:, :, 0]), 1, 0)

    def step(s, inp):
        dec, ds = inp
        return dec[..., None] * s + ds, s

    s_fin, s_start = lax.scan(step, s0, (chunk_decay, d_state))
    o_inter = jnp.einsum('bnihd,nbhdv->bnihv', q_dec, s_start)
    return (o_intra + o_inter).reshape(b, t, h, dv), s_fin


def _gdn_scan(q, k, v, log_a, beta, s0):
    b, t, h, dv = v.shape
    dk = q.shape[-1]
    hm = lambda a: jnp.moveaxis(_to_chunks(a), 3, 2)
    q, k, v, log_a, beta = (hm(a) for a in (q, k, v, log_a, beta))
    q = q * dk ** -0.5
    cum = jnp.cumsum(log_a, axis=-1)
    lower = jnp.tril(jnp.ones((CHUNK, CHUNK), bool))
    strict = jnp.tril(jnp.ones((CHUNK, CHUNK), bool), -1)
    decay = jnp.exp(jnp.where(lower, cum[..., :, None] - cum[..., None, :], -jnp.inf))
    kk = jnp.einsum('bnhid,bnhjd->bnhij', k, k)
    lmat = jnp.where(strict, beta[..., None] * kk * decay, 0.0) + jnp.eye(CHUNK, dtype=kk.dtype)
    rhs = jnp.concatenate([v * beta[..., None], k * (beta * jnp.exp(cum))[..., None]], axis=-1)
    sol = lax.linalg.triangular_solve(lmat, rhs, left_side=True, lower=True, unit_diagonal=True)
    u, w = sol[..., :dv], sol[..., dv:]
    a_qk = jnp.einsum('bnhid,bnhjd->bnhij', q, k) * decay
    k_end = k * jnp.exp(cum[..., -1:] - cum)[..., None]
    chunk_decay = jnp.exp(cum[..., -1])
    cm = lambda a: jnp.moveaxis(a, 1, 0)

    def step(s, inp):
        u_i, w_i, ke_i, dec_i = inp
        v_new = u_i - jnp.einsum('bhcd,bhdv->bhcv', w_i, s)
        s_next = dec_i[..., None, None] * s + jnp.einsum('bhcd,bhcv->bhdv', ke_i, v_new)
        return s_next, (s, v_new)

    s_fin, (s_start, v_new) = lax.scan(step, s0, (cm(u), cm(w), cm(k_end), cm(chunk_decay)))
    o = (jnp.einsum('bnhcd,nbhdv->bnhcv', q * jnp.exp(cum)[..., None], s_start)
         + jnp.einsum('bnhij,nbhjv->bnhiv', a_qk, v_new))
    return jnp.moveaxis(o, 2, 3).reshape(b, t, h, dv), s_fin


def _seg(proj, name):
    o, wd, _ = SEG[name]
    return proj[..., o:o + wd]


TOKEN_BLOCK = 256
HI = lax.Precision.HIGHEST


def _dot(a, b, precision=None):
    return jnp.dot(a, b, preferred_element_type=F32, precision=precision)


def _dot_nt(a, b):
    return lax.dot_general(a, b, (((1,), (1,)), ((), ())), preferred_element_type=F32)


def _dot_tn(a, b):
    return lax.dot_general(a, b, (((0,), (0,)), ((), ())), preferred_element_type=F32)


def _split_bf16(x):
    hi = x.astype(BF16)
    return hi, (x - hi.astype(F32)).astype(BF16)


def _dot3(a_hi, a_lo, b_hi, b_lo):
    return _dot(a_hi, b_hi) + (_dot(a_hi, b_lo) + _dot(a_lo, b_hi))


def _silu(x):
    return x * jax.nn.sigmoid(x)


def _chunk_masks(reverse):
    ri = lax.broadcasted_iota(jnp.int32, (CHUNK, CHUNK), 0)
    ci = lax.broadcasted_iota(jnp.int32, (CHUNK, CHUNK), 1)
    incl = (ci >= ri) if reverse else (ci <= ri)
    strict = (ci > ri) if reverse else (ci < ri)
    return incl, strict, (ri == ci).astype(F32)


def _conv3_block(x, cw, row_len):
    tb = x.shape[0]
    pos = lax.broadcasted_iota(jnp.int32, (tb, 1), 0) % row_len
    x_prev = jnp.where(pos == 0, 0.0, pltpu.roll(x, 1, 0))
    x_next = jnp.where(pos == row_len - 1, 0.0, pltpu.roll(x, tb - 1, 0))
    return x_prev * cw[0:1] + x * cw[1:2] + x_next * cw[2:3]


def _gdn_kernel(*refs, reverse, finalize, row_len, dirn):
    if finalize:
        (qkv_ref, ab_ref, z_ref, ob_ref, cw_ref, alog_ref, dtb_ref, g_ref, s0_ref,
         o_ref, sfin_ref, s_scr) = refs
    else:
        qkv_ref, ab_ref, cw_ref, alog_ref, dtb_ref, s0_ref, o_ref, sfin_ref, s_scr = refs
    n = pl.program_id(1)

    @pl.when(n == 0)
    def _():
        s_scr[...] = s0_ref[0]

    tb = qkv_ref.shape[1]
    qkv = _silu(_conv3_block(qkv_ref[0], cw_ref[...], row_len))
    ab = ab_ref[0]
    log_a = -jnp.exp(alog_ref[...]) * jax.nn.softplus(ab + dtb_ref[...])
    beta = jax.nn.sigmoid(ab)
    incl, strict, eye = _chunk_masks(reverse)
    tri = incl.astype(F32)
    last = 0 if reverse else CHUNK - 1
    hd = GDN_DK
    nchunk = tb // CHUNK
    order = list(range(nchunk - 1, -1, -1) if reverse else range(nchunk))
    rows = lambda c: slice(c * CHUNK, (c + 1) * CHUNK)
    cum_all = _dot(tri, jnp.concatenate([log_a[rows(c)] for c in range(nchunk)], axis=1), HI)

    items = [(c, h) for c in order for h in range(GDN_HEADS)]
    pre = {}
    for c, h in items:
        col = dirn * GDN_HEADS + h
        cum = cum_all[:, c * LANES + col:c * LANES + col + 1]
        bet = beta[rows(c), 2 * GDN_HEADS + col:2 * GDN_HEADS + col + 1]
        q = _l2norm(qkv[rows(c), h * hd:(h + 1) * hd]) * hd ** -0.5
        k = _l2norm(qkv[rows(c), GROUP + h * hd:GROUP + (h + 1) * hd])
        v = qkv[rows(c), 2 * GROUP + h * hd:2 * GROUP + (h + 1) * hd]
        cum_row = jnp.sum(eye * cum, axis=0, keepdims=True)
        decay = jnp.exp(jnp.where(incl, cum - cum_row, -jnp.inf))
        kb = k.astype(BF16)
        a = jnp.where(strict, bet * _dot_nt(kb, kb) * decay, 0.0)
        ecum = jnp.exp(cum)
        total = cum[last:last + 1]
        pre[c, h] = dict(
            x=-a, rhs=jnp.concatenate([v * bet, k * (bet * ecum)], axis=1),
            a_qk=(_dot_nt(q.astype(BF16), kb) * decay).astype(BF16),
            qe=(q * ecum).astype(BF16), k_end=(k * jnp.exp(total - cum)).astype(BF16),
            dec=jnp.exp(total))
    for p in pre.values():
        p['inv'] = eye + p['x']
    for level in range(6):
        for p in pre.values():
            x_hi, x_lo = _split_bf16(p['x'])
            if level == 0:
                p['x'] = _dot3(x_hi, x_lo, x_hi, x_lo)
            elif level < 5:
                i_hi, i_lo = _split_bf16(p['inv'])
                both = _dot3(jnp.concatenate([x_hi, i_hi], axis=0), jnp.concatenate([x_lo, i_lo], axis=0),
                             x_hi, x_lo)
                p['x'] = both[:CHUNK]
                p['inv'] = p['inv'] + both[CHUNK:]
            else:
                i_hi, i_lo = _split_bf16(p['inv'])
                p['inv'] = p['inv'] + _dot3(i_hi, i_lo, x_hi, x_lo)
    for p in pre.values():
        i_hi, i_lo = _split_bf16(p['inv'])
        r_hi, r_lo = _split_bf16(p['rhs'])
        sol = _dot3(i_hi, i_lo, r_hi, r_lo)
        p['u'], p['w'] = sol[:, :GDN_DV], sol[:, GDN_DV:].astype(BF16)

    for c, h in items:
        p = pre[c, h]
        s = s_scr[h]
        sb = s.astype(BF16)
        v_new = p['u'] - _dot(p['w'], sb)
        vb = v_new.astype(BF16)
        o = _dot(p['qe'], sb) + _dot(p['a_qk'], vb)
        s_scr[h] = p['dec'] * s + _dot_tn(p['k_end'], vb)
        cols = slice(h * GDN_DV, (h + 1) * GDN_DV)
        if finalize:
            o = o + ob_ref[0, rows(c), cols]
            o = o * lax.rsqrt(jnp.mean(o * o, axis=-1, keepdims=True) + EPS) * g_ref[...]
            o = o * _silu(z_ref[0, rows(c), cols])
        o_ref[0, rows(c), cols] = o.astype(o_ref.dtype)

    @pl.when(n == pl.num_programs(1) - 1)
    def _():
        sfin_ref[0] = s_scr[...]


def _gdn_pass(proj, o_other, s0, lp, *, dirn, row_len):
    b, t, _ = proj.shape
    tb = min(TOKEN_BLOCK, t)
    assert t % tb == 0 and tb % row_len == 0
    nblk = t // tb
    reverse = dirn == 1
    finalize = o_other is not None
    tok = (lambda i, n: (i, nblk - 1 - n)) if reverse else (lambda i, n: (i, n))
    seg = lambda name, wd: pl.BlockSpec((1, tb, wd), lambda i, n: tok(i, n) + (_col_block(name),))
    full = lambda a: pl.BlockSpec(a.shape, lambda i, n: (0,) * a.ndim)
    lane_row = lambda vals: jnp.pad(vals.reshape(1, -1), ((0, 0), (0, LANES - vals.size)))
    cw = lp['gdn_conv']
    alog = lane_row(lp['gdn_a_log'])
    dtb = lane_row(lp['gdn_dt_bias'])
    g = lp['gdn_norm_g'].reshape(1, GDN_DV)
    state_spec = pl.BlockSpec((1, GDN_HEADS, GDN_DK, GDN_DV), lambda i, n: (i, 0, 0, 0))
    if finalize:
        args = (proj, proj, proj, o_other, cw, alog, dtb, g, s0)
        in_specs = [seg('d_qkv', 3 * GROUP), seg('d_ab', LANES), seg('d_z', GROUP),
                    pl.BlockSpec((1, tb, GROUP), lambda i, n: tok(i, n) + (0,)),
                    full(cw), full(alog), full(dtb), full(g), state_spec]
    else:
        args = (proj, proj, cw, alog, dtb, s0)
        in_specs = [seg('d_qkv', 3 * GROUP), seg('d_ab', LANES), full(cw), full(alog), full(dtb), state_spec]
    return pl.pallas_call(
        functools.partial(_gdn_kernel, reverse=reverse, finalize=finalize, row_len=row_len, dirn=dirn),
        grid=(b, nblk),
        in_specs=in_specs,
        out_specs=[pl.BlockSpec((1, tb, GROUP), lambda i, n: tok(i, n) + (0,)), state_spec],
        out_shape=[jax.ShapeDtypeStruct((b, t, GROUP), BF16 if finalize else F32),
                   jax.ShapeDtypeStruct((b, GDN_HEADS, GDN_DK, GDN_DV), F32)],
        scratch_shapes=[pltpu.VMEM((GDN_HEADS, GDN_DK, GDN_DV), F32)],
        compiler_params=pltpu.CompilerParams(
            dimension_semantics=("arbitrary", "arbitrary"), vmem_limit_bytes=VMEM_LIMIT),
        name="gdn_fwd" if finalize else "gdn_bwd",
    )(*args)


def _gla_kernel(*refs, reverse, finalize):
    if finalize:
        (q_ref, k_ref, v_ref, lo_ref, r_ref, ob_ref, up_ref, ab_ref, g_ref, s0_ref,
         o_ref, sfin_ref, s_scr) = refs
    else:
        q_ref, k_ref, v_ref, lo_ref, up_ref, ab_ref, s0_ref, o_ref, sfin_ref, s_scr = refs
    n = pl.program_id(1)

    @pl.when(n == 0)
    def _():
        s_scr[...] = s0_ref[0]

    tb = q_ref.shape[1]
    gate = _dot(lo_ref[0].astype(BF16), up_ref[...].astype(BF16)) + ab_ref[...]
    log_f = jax.nn.log_sigmoid(gate) / GLA_TAU
    incl, _, eye = _chunk_masks(reverse)
    tri = incl.astype(F32)
    last = 0 if reverse else CHUNK - 1
    nchunk = tb // CHUNK
    order = range(nchunk - 1, -1, -1) if reverse else range(nchunk)
    for c in order:
        sl = slice(c * CHUNK, (c + 1) * CHUNK)
        cum = _dot(tri, log_f[sl], HI)
        total = cum[last:last + 1]
        q_dec = q_ref[0, sl, :] * GLA_DK ** -0.5 * jnp.exp(cum)
        k = k_ref[0, sl, :]
        k_inv = (k * jnp.exp(-cum)).astype(BF16)
        k_end = (k * jnp.exp(total - cum)).astype(BF16)
        q_dec = q_dec.astype(BF16)
        dec_row = jnp.exp(total)
        for h in range(GLA_HEADS):
            kc = slice(h * GLA_DK, (h + 1) * GLA_DK)
            vc = slice(h * GLA_DV, (h + 1) * GLA_DV)
            vb = v_ref[0, sl, vc].astype(BF16)
            att = jnp.where(incl, _dot_nt(q_dec[:, kc], k_inv[:, kc]), 0.0)
            s = s_scr[h]
            o = _dot(att.astype(BF16), vb) + _dot(q_dec[:, kc], s.astype(BF16))
            dec_col = jnp.sum(eye * dec_row[:, kc], axis=1, keepdims=True)
            s_scr[h] = dec_col * s + _dot_tn(k_end[:, kc], vb)
            if finalize:
                o = o + ob_ref[0, sl, vc]
                o = o * lax.rsqrt(jnp.mean(o * o, axis=-1, keepdims=True) + EPS) * g_ref[...]
                o = o * _silu(r_ref[0, sl, vc])
            o_ref[0, sl, vc] = o.astype(o_ref.dtype)

    @pl.when(n == pl.num_programs(1) - 1)
    def _():
        sfin_ref[0] = s_scr[...]


def _gla_pass(proj, o_other, s0, lp, *, dirn):
    b, t, _ = proj.shape
    tb = min(TOKEN_BLOCK, t)
    assert t % tb == 0
    nblk = t // tb
    reverse = dirn == 1
    finalize = o_other is not None
    tok = (lambda i, n: (i, nblk - 1 - n)) if reverse else (lambda i, n: (i, n))
    seg = lambda name: pl.BlockSpec((1, tb, SEG[name][2]), lambda i, n: tok(i, n) + (_col_block(name),))
    full = lambda a: pl.BlockSpec(a.shape, lambda i, n: (0,) * a.ndim)
    hk = GLA_HEADS * GLA_DK
    up = jnp.pad(lp['gla_a_up'][dirn], ((0, LANES - GLA_LR), (0, 0)))
    ab = lp['gla_a_b'][dirn].reshape(1, hk)
    g = lp['gla_norm_g'].reshape(1, GLA_DV)
    state_spec = pl.BlockSpec((1, GLA_HEADS, GLA_DK, GLA_DV), lambda i, n: (i, 0, 0, 0))
    if finalize:
        args = (proj, proj, proj, proj, proj, o_other, up, ab, g, s0)
        in_specs = [seg('g_q'), seg('g_k'), seg('g_v'), seg('g_lo'), seg('g_r'),
                    pl.BlockSpec((1, tb, GROUP), lambda i, n: tok(i, n) + (0,)),
                    full(up), full(ab), full(g), state_spec]
    else:
        args = (proj, proj, proj, proj, up, ab, s0)
        in_specs = [seg('g_q'), seg('g_k'), seg('g_v'), seg('g_lo'), full(up), full(ab), state_spec]
    return pl.pallas_call(
        functools.partial(_gla_kernel, reverse=reverse, finalize=finalize),
        grid=(b, nblk),
        in_specs=in_specs,
        out_specs=[pl.BlockSpec((1, tb, GROUP), lambda i, n: tok(i, n) + (0,)), state_spec],
        out_shape=[jax.ShapeDtypeStruct((b, t, GROUP), BF16 if finalize else F32),
                   jax.ShapeDtypeStruct((b, GLA_HEADS, GLA_DK, GLA_DV), F32)],
        scratch_shapes=[pltpu.VMEM((GLA_HEADS, GLA_DK, GLA_DV), F32)],
        compiler_params=pltpu.CompilerParams(
            dimension_semantics=("arbitrary", "arbitrary"), vmem_limit_bytes=VMEM_LIMIT),
        name="gla_fwd" if finalize else "gla_bwd",
    )(*args)


def _two_direction_group(pass_fn, state_shape, proj_c, proj_x, **kw):
    zero = jnp.zeros((proj_x[0].shape[0],) + state_shape, F32)
    ob_c, sb_c = pass_fn(proj_c[0], None, zero, dirn=1, **proj_c[1], **kw)
    ob_x, _ = pass_fn(proj_x[0], None, sb_c, dirn=1, **proj_x[1], **kw)
    y_c, sf_c = pass_fn(proj_c[0], ob_c, zero, dirn=0, **proj_c[1], **kw)
    y_x, _ = pass_fn(proj_x[0], ob_x, sf_c, dirn=0, **proj_x[1], **kw)
    return y_c, y_x


def _gla_group_pallas(proj_c, proj_x, lp):
    return _two_direction_group(_gla_pass, (GLA_HEADS, GLA_DK, GLA_DV), (proj_c, {}), (proj_x, {}), lp=lp)


def _gdn_group_pallas(proj_c, proj_x, lp):
    return _two_direction_group(_gdn_pass, (GDN_HEADS, GDN_DK, GDN_DV),
                                (proj_c, dict(row_len=proj_c.shape[1])), (proj_x, dict(row_len=GRID_W)), lp=lp)


def _gla_group(proj, init, lp):
    g_q, g_k, g_v, g_r, g_lo = (_seg(proj, n) for n in ('g_q', 'g_k', 'g_v', 'g_r', 'g_lo'))
    q = _heads(g_q, GLA_HEADS) * GLA_DK ** -0.5
    k = _heads(g_k, GLA_HEADS)
    v = _heads(g_v, GLA_HEADS)
    log_f = [_heads(jax.nn.log_sigmoid(g_lo @ lp['gla_a_up'][i] + lp['gla_a_b'][i]) / GLA_TAU, GLA_HEADS)
             for i in range(2)]
    o_f, s_f = _gla_scan(q, k, v, log_f[0], init[0])
    o_b, s_b = _gla_scan(_flip(q), _flip(k), _flip(v), _flip(log_f[1]), init[1])
    y = _rms_heads(o_f + _flip(o_b), lp['gla_norm_g']) * jax.nn.silu(_heads(g_r, GLA_HEADS))
    return y.reshape(y.shape[:2] + (GROUP,)), (s_f, s_b)


def _gdn_group(proj, row_len, init, lp):
    b, t = proj.shape[:2]
    d_qkv, d_z, d_ab = (_seg(proj, n) for n in ('d_qkv', 'd_z', 'd_ab'))
    d_a, d_b = d_ab[..., :2 * GDN_HEADS], d_ab[..., 2 * GDN_HEADS:]
    qkv = jax.nn.silu(_conv3_rows(d_qkv, lp['gdn_conv'], row_len))
    q, k, v = (_heads(a, GDN_HEADS) for a in jnp.split(qkv, 3, axis=-1))
    q, k = _l2norm(q), _l2norm(k)
    log_a = -jnp.exp(lp['gdn_a_log']) * jax.nn.softplus(d_a.reshape(b, t, 2, GDN_HEADS) + lp['gdn_dt_bias'])
    beta = jax.nn.sigmoid(d_b.reshape(b, t, 2, GDN_HEADS))
    o_f, s_f = _gdn_scan(q, k, v, log_a[:, :, 0], beta[:, :, 0], init[0])
    o_b, s_b = _gdn_scan(_flip(q), _flip(k), _flip(v), _flip(log_a[:, :, 1]), _flip(beta[:, :, 1]), init[1])
    y = _rms_heads(o_f + _flip(o_b), lp['gdn_norm_g']) * jax.nn.silu(_heads(d_z, GDN_HEADS))
    return y.reshape(b, t, GROUP), (s_f, s_b)


def _sc_kernel(b_ref, c_ref, h_ref, cw_ref, o_ref, *, row_len):
    o_ref[0] = (b_ref[0] * _conv3_block(c_ref[0] * h_ref[0], cw_ref[...], row_len)).astype(o_ref.dtype)


def _sc_group(proj, row_len, lp):
    b, t, _ = proj.shape
    tb = min(TOKEN_BLOCK, t)
    assert t % tb == 0 and tb % row_len == 0
    seg = lambda name: pl.BlockSpec((1, tb, GROUP), lambda i, n: (i, n, _col_block(name)))
    cw = lp['sc_conv']
    return pl.pallas_call(
        functools.partial(_sc_kernel, row_len=row_len),
        grid=(b, t // tb),
        in_specs=[seg('c_b'), seg('c_c'), seg('c_h'), pl.BlockSpec(cw.shape, lambda i, n: (0, 0))],
        out_specs=pl.BlockSpec((1, tb, GROUP), lambda i, n: (i, n, 0)),
        out_shape=jax.ShapeDtypeStruct((b, t, GROUP), BF16),
        compiler_params=pltpu.CompilerParams(
            dimension_semantics=("arbitrary", "arbitrary"), vmem_limit_bytes=VMEM_LIMIT),
        name="short_conv",
    )(proj, proj, proj, cw)


def _mixers(proj_c, proj_x, lp):
    tc = proj_c.shape[1]
    y_gla_c, y_gla_x = _gla_group_pallas(proj_c, proj_x, lp)
    y_gdn_c, y_gdn_x = _gdn_group_pallas(proj_c, proj_x, lp)
    y_sc_c = _sc_group(proj_c, tc, lp)
    y_sc_x = _sc_group(proj_x, GRID_W, lp)
    y_rw_c, y_rw_x = _rw_group(proj_c, proj_x, lp)
    return (y_gla_c, y_gdn_c, y_sc_c, y_rw_c), (y_gla_x, y_gdn_x, y_sc_x, y_rw_x)


def kernel(x, c, ctx, c_ctx, ada_w, ada_b, norm1_g, norm2_g, w_in, w_out, gla_a_up, gla_a_b, gla_norm_g, gdn_conv, gdn_a_log, gdn_dt_bias, gdn_norm_g, sc_conv, rw_mu_rkv, rw_mu_wa, rw_w0, rw_w2, rw_a0, rw_a2, rw_g2, rw_kk, rw_ka, rw_rk, rw_gn_w, rw_gn_b, ffn_w_gu, ffn_w_down, final_g):
    nb, t, d = x.shape
    tc = ctx.shape[1]
    depth = w_in.shape[0]
    assert d == D_MODEL and t % 512 == 0 and tc % CHUNK == 0 and nb + 1 <= SUBLANES

    cond = jnp.concatenate([c, c_ctx[None, :], jnp.zeros((SUBLANES - nb - 1, d), F32)], axis=0)
    mod = _modulation(cond, ada_w, ada_b)

    tm_x = 512
    tm_c = tc
    h_x, h_c = x, ctx
    for l in range(depth):
        lp = dict(gla_a_up=gla_a_up[l], gla_a_b=gla_a_b[l],
                  gla_norm_g=gla_norm_g[l], gdn_conv=gdn_conv[l], gdn_a_log=gdn_a_log[l],
                  gdn_dt_bias=gdn_dt_bias[l], gdn_norm_g=gdn_norm_g[l], sc_conv=sc_conv[l],
                  rw_mu_rkv=rw_mu_rkv[l], rw_mu_wa=rw_mu_wa[l], rw_w0=rw_w0[l], rw_w2=rw_w2[l],
                  rw_a0=rw_a0[l], rw_a2=rw_a2[l], rw_g2=rw_g2[l], rw_kk=rw_kk[l], rw_ka=rw_ka[l],
                  rw_rk=rw_rk[l], rw_gn_w=rw_gn_w[l], rw_gn_b=rw_gn_b[l])
        m_x = [mod[l, :nb, i * d:(i + 1) * d][:, None, :] for i in range(6)]
        m_c = [jnp.broadcast_to(mod[l, nb, i * d:(i + 1) * d][None, None, :], (nb, 1, d)) for i in range(6)]
        w_in_l = _pad_w_in(w_in[l])
        w_out_l = jnp.concatenate([w_out[l][:3 * GROUP], w_out[l][3 * GROUP:][_rw_channel_perm()]],
                                  axis=0).astype(BF16)
        w_gu_l = ffn_w_gu[l].astype(BF16)
        w_dn_l = ffn_w_down[l].astype(BF16)

        proj_c = _norm_proj(h_c, norm1_g[l], m_c[1], m_c[0], w_in_l, tm_c, 1024)
        proj_x = _norm_proj(h_x, norm1_g[l], m_x[1], m_x[0], w_in_l, 2 * tm_x, 1024)
        y_c, y_x = _mixers(proj_c, proj_x, lp)

        h_x = _groups_residual(y_x, w_out_l, h_x, m_x[2], 2 * tm_x, 1024)
        a_x = _norm_swiglu(h_x, norm2_g[l], m_x[4], m_x[3], w_gu_l, 2 * tm_x, 512)
        h_x = _proj_residual(a_x, w_dn_l, h_x, m_x[5], tm_x, 512)
        if l < depth - 1:
            h_c = _groups_residual(y_c, w_out_l, h_c, m_c[2], tm_c, 512)
            a_c = _norm_swiglu(h_c, norm2_g[l], m_c[4], m_c[3], w_gu_l, tm_c, 512)
            h_c = _proj_residual(a_c, w_dn_l, h_c, m_c[5], tm_c, 512)
    return _final_norm(h_x, final_g, tm_x)
```

```python
import functools
import math

import numpy as np
import jax
import jax.numpy as jnp
from jax import lax
from jax.experimental import pallas as pl
from jax.experimental.pallas import tpu as pltpu

F32 = jnp.float32
BF16 = jnp.bfloat16

D_MODEL = 2048
GROUP = D_MODEL // 4
CHUNK = 64
EPS = 1e-6
GRID_W = 64

GLA_HEADS = 4
GLA_DK = 64
GLA_DV = 128
GLA_LR = 16
GLA_TAU = 16.0
GDN_HEADS = 4
GDN_DK = 128
GDN_DV = 128
RW_HEAD = 64
RW_HEADS = GROUP // RW_HEAD
RW_DECAY_LR = 32
RW_A_LR = 32
RW_GATE_LR = 96
RW_DECAY_SCALE = math.exp(-0.5)
RW_GN_EPS = 64e-5
FFN = -(-(8 * D_MODEL) // (3 * 256)) * 256

SUBLANES = 8
LANES = 128
VMEM_LIMIT = 48 * 1024 * 1024

_SRC_SEGS = (
    ('g_q', 256), ('g_k', 256), ('g_v', 512), ('g_r', 512), ('g_lo', GLA_LR),
    ('d_qkv', 1536), ('d_z', 512), ('d_ab', 16),
    ('c_b', 512), ('c_c', 512), ('c_h', 512),
    ('r_rkv', 1536), ('r_wa', 64), ('r_g', RW_GATE_LR),
)
_DST_ORDER = ('g_q', 'g_k', 'g_v', 'g_r', 'd_qkv', 'r_rkv', 'd_z', 'c_b', 'c_c', 'c_h',
              'g_lo', 'd_ab', 'r_wa', 'r_g')


def _round_up(n, m):
    return -(-n // m) * m


def _seg_layout():
    width = dict(_SRC_SEGS)
    src, o = {}, 0
    for name, wd in _SRC_SEGS:
        src[name] = o
        o += wd
    dst, o = {}, 0
    for name in _DST_ORDER:
        pw = _round_up(width[name], LANES)
        assert o % pw == 0, name
        dst[name] = (o, width[name], pw)
        o += pw
    return src, dst, o


SEG_SRC, SEG, PROJ_PAD = _seg_layout()


def _pad_w_in(w):
    parts = []
    for name in _DST_ORDER:
        _, wd, pw = SEG[name]
        seg = w[:, SEG_SRC[name]:SEG_SRC[name] + wd]
        if name == 'r_rkv':
            perm = _rw_channel_perm()
            seg = jnp.concatenate([seg[:, i * GROUP:(i + 1) * GROUP][:, perm] for i in range(3)], axis=1)
        if pw != wd:
            seg = jnp.pad(seg, ((0, 0), (0, pw - wd)))
        parts.append(seg)
    return jnp.concatenate(parts, axis=1).astype(BF16)


def _col_block(name):
    o, _, pw = SEG[name]
    return o // pw


def _mod_kernel(c_ref, w_ref, b_ref, o_ref):
    x = c_ref[...]
    x = (x * jax.nn.sigmoid(x)).astype(BF16)
    o_ref[0] = jnp.dot(x, w_ref[0].astype(BF16), preferred_element_type=F32) + b_ref[0]


def _modulation(cond, ada_w, ada_b):
    nl, d, n = ada_w.shape
    tn = 1024
    return pl.pallas_call(
        _mod_kernel,
        grid=(nl, n // tn),
        in_specs=[pl.BlockSpec((SUBLANES, d), lambda l, j: (0, 0)),
                  pl.BlockSpec((1, d, tn), lambda l, j: (l, 0, j)),
                  pl.BlockSpec((1, 1, tn), lambda l, j: (l, 0, j))],
        out_specs=pl.BlockSpec((1, SUBLANES, tn), lambda l, j: (l, 0, j)),
        out_shape=jax.ShapeDtypeStruct((nl, SUBLANES, n), F32),
        compiler_params=pltpu.CompilerParams(
            dimension_semantics=("arbitrary", "arbitrary"), vmem_limit_bytes=VMEM_LIMIT),
        name="modulation",
    )(cond, ada_w, ada_b.reshape(nl, 1, n))


def _modnorm_rows(h_ref, g_ref, sc_ref, sh_ref):
    x = h_ref[0]
    y = x * lax.rsqrt(jnp.mean(x * x, axis=-1, keepdims=True) + EPS) * g_ref[...]
    return (y * (1.0 + sc_ref[0]) + sh_ref[0]).astype(BF16)


def _norm_proj_kernel(h_ref, g_ref, sc_ref, sh_ref, w_ref, o_ref, a_scr):
    @pl.when(pl.program_id(2) == 0)
    def _():
        a_scr[...] = _modnorm_rows(h_ref, g_ref, sc_ref, sh_ref)

    o_ref[0] = jnp.dot(a_scr[...], w_ref[...], preferred_element_type=F32).astype(o_ref.dtype)


def _norm_proj(h, g, scale, shift, w, tm, tn):
    b, t, d = h.shape
    n = w.shape[1]
    return pl.pallas_call(
        _norm_proj_kernel,
        grid=(b, t // tm, n // tn),
        in_specs=[pl.BlockSpec((1, tm, d), lambda i, m, j: (i, m, 0)),
                  pl.BlockSpec((1, d), lambda i, m, j: (0, 0)),
                  pl.BlockSpec((1, 1, d), lambda i, m, j: (i, 0, 0)),
                  pl.BlockSpec((1, 1, d), lambda i, m, j: (i, 0, 0)),
                  pl.BlockSpec((d, tn), lambda i, m, j: (0, j))],
        out_specs=pl.BlockSpec((1, tm, tn), lambda i, m, j: (i, m, j)),
        out_shape=jax.ShapeDtypeStruct((b, t, n), F32),
        scratch_shapes=[pltpu.VMEM((tm, d), BF16)],
        compiler_params=pltpu.CompilerParams(
            dimension_semantics=("arbitrary", "arbitrary", "arbitrary"), vmem_limit_bytes=VMEM_LIMIT),
        name="norm_proj",
    )(h, g.reshape(1, d), scale, shift, w)


def _norm_swiglu_kernel(h_ref, g_ref, sc_ref, sh_ref, wg_ref, wu_ref, o_ref, a_scr):
    @pl.when(pl.program_id(2) == 0)
    def _():
        a_scr[...] = _modnorm_rows(h_ref, g_ref, sc_ref, sh_ref)

    a = a_scr[...]
    gate = jnp.dot(a, wg_ref[...], preferred_element_type=F32)
    up = jnp.dot(a, wu_ref[...], preferred_element_type=F32)
    o_ref[0] = (gate * jax.nn.sigmoid(gate) * up).astype(o_ref.dtype)


def _norm_swiglu(h, g, scale, shift, w_gu, tm, tn):
    b, t, d = h.shape
    f = w_gu.shape[1] // 2
    nj = f // tn
    return pl.pallas_call(
        _norm_swiglu_kernel,
        grid=(b, t // tm, nj),
        in_specs=[pl.BlockSpec((1, tm, d), lambda i, m, j: (i, m, 0)),
                  pl.BlockSpec((1, d), lambda i, m, j: (0, 0)),
                  pl.BlockSpec((1, 1, d), lambda i, m, j: (i, 0, 0)),
                  pl.BlockSpec((1, 1, d), lambda i, m, j: (i, 0, 0)),
                  pl.BlockSpec((d, tn), lambda i, m, j: (0, j)),
                  pl.BlockSpec((d, tn), lambda i, m, j: (0, j + nj))],
        out_specs=pl.BlockSpec((1, tm, tn), lambda i, m, j: (i, m, j)),
        out_shape=jax.ShapeDtypeStruct((b, t, f), BF16),
        scratch_shapes=[pltpu.VMEM((tm, d), BF16)],
        compiler_params=pltpu.CompilerParams(
            dimension_semantics=("arbitrary", "arbitrary", "arbitrary"), vmem_limit_bytes=VMEM_LIMIT),
        name="norm_swiglu",
    )(h, g.reshape(1, d), scale, shift, w_gu, w_gu)


def _proj_residual_kernel(a_ref, w_ref, h_ref, gate_ref, o_ref):
    o_ref[0] = h_ref[0] + gate_ref[0] * jnp.dot(a_ref[0], w_ref[...], preferred_element_type=F32)


def _proj_residual(a, w, h, gate, tm, tn):
    b, t, k = a.shape
    d = w.shape[1]
    return pl.pallas_call(
        _proj_residual_kernel,
        grid=(b, t // tm, d // tn),
        in_specs=[pl.BlockSpec((1, tm, k), lambda i, m, j: (i, m, 0)),
                  pl.BlockSpec((k, tn), lambda i, m, j: (0, j)),
                  pl.BlockSpec((1, tm, tn), lambda i, m, j: (i, m, j)),
                  pl.BlockSpec((1, 1, tn), lambda i, m, j: (i, 0, j))],
        out_specs=pl.BlockSpec((1, tm, tn), lambda i, m, j: (i, m, j)),
        out_shape=jax.ShapeDtypeStruct((b, t, d), F32),
        compiler_params=pltpu.CompilerParams(
            dimension_semantics=("arbitrary", "arbitrary", "arbitrary"), vmem_limit_bytes=VMEM_LIMIT),
        name="proj_residual",
    )(a, w, h, gate)


def _groups_residual_kernel(*refs):
    *a_refs, w_ref, h_ref, gate_ref, o_ref = refs
    k = a_refs[0].shape[2]
    acc = jnp.dot(a_refs[0][0], w_ref[0:k, :], preferred_element_type=F32)
    for i, a_ref in enumerate(a_refs[1:], start=1):
        acc = acc + jnp.dot(a_ref[0], w_ref[i * k:(i + 1) * k, :], preferred_element_type=F32)
    o_ref[0] = h_ref[0] + gate_ref[0] * acc


def _groups_residual(groups, w, h, gate, tm, tn):
    b, t, k = groups[0].shape
    d = w.shape[1]
    assert len(groups) * k == w.shape[0]
    return pl.pallas_call(
        _groups_residual_kernel,
        grid=(b, t // tm, d // tn),
        in_specs=[pl.BlockSpec((1, tm, k), lambda i, m, j: (i, m, 0)) for _ in groups]
        + [pl.BlockSpec((w.shape[0], tn), lambda i, m, j: (0, j)),
           pl.BlockSpec((1, tm, tn), lambda i, m, j: (i, m, j)),
           pl.BlockSpec((1, 1, tn), lambda i, m, j: (i, 0, j))],
        out_specs=pl.BlockSpec((1, tm, tn), lambda i, m, j: (i, m, j)),
        out_shape=jax.ShapeDtypeStruct((b, t, d), F32),
        compiler_params=pltpu.CompilerParams(
            dimension_semantics=("arbitrary", "arbitrary", "arbitrary"), vmem_limit_bytes=VMEM_LIMIT),
        name="groups_residual",
    )(*groups, w, h, gate)


def _final_norm_kernel(h_ref, g_ref, o_ref):
    x = h_ref[0]
    o_ref[0] = x * lax.rsqrt(jnp.mean(x * x, axis=-1, keepdims=True) + EPS) * g_ref[...]


def _final_norm(h, g, tm):
    b, t, d = h.shape
    return pl.pallas_call(
        _final_norm_kernel,
        grid=(b, t // tm),
        in_specs=[pl.BlockSpec((1, tm, d), lambda i, m: (i, m, 0)),
                  pl.BlockSpec((1, d), lambda i, m: (0, 0))],
        out_specs=pl.BlockSpec((1, tm, d), lambda i, m: (i, m, 0)),
        out_shape=jax.ShapeDtypeStruct((b, t, d), F32),
        compiler_params=pltpu.CompilerParams(
            dimension_semantics=("arbitrary", "arbitrary"), vmem_limit_bytes=VMEM_LIMIT),
        name="final_norm",
    )(h, g.reshape(1, d))


RW_CHAINS = 32
RW_KLO = LANES // RW_CHAINS
RW_KQ = RW_HEAD // RW_KLO
RW_VM = RW_HEAD // SUBLANES
RW_VGROUP = 4
RW_STEP_BLOCK = 16
OP_W, OP_B, OP_K, OP_R, OP_KK = range(5)


def _lane_group_sum(p):
    out = p
    for g in range(1, RW_KLO):
        out = out + pltpu.roll(p, g * RW_CHAINS, 1)
    return out


def _rwkv_kernel(opsf_ref, opsb_ref, vf_ref, vb_ref, s0_ref, yf_ref, yb_ref, sfin_ref,
                 s_ref, saf_ref, sab_ref, pb_ref):
    tb = opsf_ref.shape[0]
    dirs = ((opsf_ref, vf_ref, yf_ref), (opsb_ref, vb_ref, yb_ref))

    def row(ops_ref, t, j, q):
        lane0 = (j * RW_KQ + q) * LANES
        return ops_ref[t:t + 1, lane0:lane0 + LANES]

    @pl.when(pl.program_id(0) == 0)
    def _():
        s_ref[...] = s0_ref[...]

    zero = jnp.zeros((SUBLANES, LANES), F32)
    groups = [range(m0, m0 + RW_VGROUP) for m0 in range(0, RW_VM, RW_VGROUP)]

    for d, (ops_ref, _, _) in enumerate(dirs):
        first = 0 if d == 0 else tb - 1
        for ms in groups:
            acc = {m: zero for m in ms}
            for q in range(RW_KQ):
                kk = row(ops_ref, first, OP_KK, q)
                for m in ms:
                    acc[m] = acc[m] + s_ref[d, q, m] * kk
            for m in ms:
                if d == 0:
                    saf_ref[m] = _lane_group_sum(acc[m])
                else:
                    pb_ref[m] = acc[m]

    def advance(d, i, sa_ref, next_ref, reduce_next):
        ops_ref, v_ref, y_ref = dirs[d]
        t = i if d == 0 else tb - 1 - i
        tn = min(i + 1, tb - 1) if d == 0 else max(tb - 2 - i, 0)
        for ms in groups:
            sa = {m: sa_ref[m] for m in ms}
            vt = {m: v_ref[t, m] for m in ms}
            acc_y = {m: zero for m in ms}
            acc_s = {m: zero for m in ms}
            for q in range(RW_KQ):
                w = row(ops_ref, t, OP_W, q)
                b = row(ops_ref, t, OP_B, q)
                kt = row(ops_ref, t, OP_K, q)
                r = row(ops_ref, t, OP_R, q)
                kkn = row(ops_ref, tn, OP_KK, q)
                for m in ms:
                    s = s_ref[d, q, m] * w - sa[m] * b + vt[m] * kt
                    s_ref[d, q, m] = s
                    acc_y[m] = acc_y[m] + s * r
                    acc_s[m] = acc_s[m] + s * kkn
            for m in ms:
                y_ref[t, m] = acc_y[m]
                next_ref[m] = _lane_group_sum(acc_s[m]) if reduce_next else acc_s[m]

    for i in range(tb):
        for m in range(RW_VM):
            sab_ref[m] = _lane_group_sum(pb_ref[m])
        advance(0, i, saf_ref, saf_ref, True)
        advance(1, i, sab_ref, pb_ref, False)

    @pl.when(pl.program_id(0) == pl.num_programs(0) - 1)
    def _():
        sfin_ref[...] = s_ref[...]


RW_STATE_SHAPE = (2, RW_KQ, RW_VM, SUBLANES, LANES)


def _rwkv_scan(ops_f, ops_b, v_f, v_b, s0):
    t = ops_f.shape[0]
    tb = RW_STEP_BLOCK
    nblk = t // tb
    fwd = lambda i: (i, 0, 0, 0)
    bwd = lambda i: (nblk - 1 - i, 0, 0, 0)
    ops_block = (tb, 5 * RW_KQ * LANES)
    v_block = (tb, RW_VM, SUBLANES, LANES)
    y_shape = jax.ShapeDtypeStruct((t, RW_VM, SUBLANES, LANES), F32)
    state_spec = pl.BlockSpec(RW_STATE_SHAPE, lambda i: (0,) * len(RW_STATE_SHAPE))
    return pl.pallas_call(
        _rwkv_kernel,
        grid=(nblk,),
        in_specs=[pl.BlockSpec(ops_block, lambda i: (i, 0)), pl.BlockSpec(ops_block, lambda i: (nblk - 1 - i, 0)),
                  pl.BlockSpec(v_block, fwd), pl.BlockSpec(v_block, bwd), state_spec],
        out_specs=[pl.BlockSpec(v_block, fwd), pl.BlockSpec(v_block, bwd), state_spec],
        out_shape=[y_shape, y_shape, jax.ShapeDtypeStruct(RW_STATE_SHAPE, F32)],
        scratch_shapes=[pltpu.VMEM(RW_STATE_SHAPE, F32)]
        + [pltpu.VMEM((RW_VM, SUBLANES, LANES), F32)] * 3,
        compiler_params=pltpu.CompilerParams(
            dimension_semantics=("arbitrary",), vmem_limit_bytes=VMEM_LIMIT),
        name="rwkv_scan",
    )(ops_f, ops_b, v_f, v_b, s0)


RW_PREP_BLOCK = 128
RW_PREP_VMEM_LIMIT = 56 * 1024 * 1024
RW_NPIECE = 3


def _rw_channel_perm():
    return np.arange(GROUP).reshape(RW_HEADS, RW_HEAD).T.reshape(-1)


def _rw_relayout_matrices(nb):
    slab = LANES
    ch_per_slab = slab // RW_HEADS
    pk = np.zeros((nb, RW_NPIECE, slab, ch_per_slab // RW_KLO, RW_KLO, nb, RW_HEADS), np.float32)
    pv = np.zeros((nb, RW_NPIECE, slab, ch_per_slab, RW_KLO, nb, RW_HEADS), np.float32)
    for b in range(nb):
        for ch in range(ch_per_slab):
            for h in range(RW_HEADS):
                pk[b, :, ch * RW_HEADS + h, ch // RW_KLO, ch % RW_KLO, b, h] = 1.0
                pv[b, :, ch * RW_HEADS + h, ch, :, b, h] = 1.0
    rows = nb * RW_NPIECE * slab
    return (jnp.asarray(pk.reshape(rows, -1), BF16), jnp.asarray(pv.reshape(rows, -1), BF16))


def _split3(x):
    hi = x.astype(BF16)
    r1 = x - hi.astype(F32)
    mid = r1.astype(BF16)
    lo = (r1 - mid.astype(F32)).astype(BF16)
    return hi, mid, lo


def _rw_prep_kernel(rkv_ref, wa_ref, rkv_nb_ref, wa_nb_ref, mu_rkv_ref, mu_wa_ref, w0_ref, w2_ref,
                    a0_ref, a2_ref, kk_ref, ka_ref, rk_ref, hsum_ref, pk_ref, pv_ref,
                    ops_ref, v_ref, bonus_ref, *, backward):
    nb, tb, _ = rkv_ref.shape
    n = pl.program_id(0)
    edge = (n == pl.num_programs(0) - 1) if backward else (n == 0)
    nb_row = SUBLANES - 1 if not backward else 0
    row = lax.broadcasted_iota(jnp.int32, (tb, 1), 0)
    edge_row = (row == tb - 1) if backward else (row == 0)

    def shifted(x, neighbour):
        inner = pltpu.roll(x, tb - 1 if backward else 1, 0)
        outer = jnp.where(edge, 0.0, neighbour)
        return jnp.where(edge_row, outer, inner)

    def head_sum(z):
        hi, lo = _split_bf16(z)
        return _dot(hi, hsum_ref[...]) + _dot(lo, hsum_ref[...])

    feats = []
    for b in range(nb):
        x = rkv_ref[b]
        x = x + (shifted(x, rkv_nb_ref[b, nb_row:nb_row + 1, :]) - x) * mu_rkv_ref[...]
        xw = wa_ref[b]
        xw = xw + (shifted(xw, wa_nb_ref[b, nb_row:nb_row + 1, :]) - xw) * mu_wa_ref[...]
        r, k, v = x[:, :GROUP], x[:, GROUP:2 * GROUP], x[:, 2 * GROUP:]
        decay = jnp.exp(-RW_DECAY_SCALE * jax.nn.sigmoid(
            w0_ref[...] + _dot(jnp.tanh(xw).astype(BF16), w2_ref[...])))
        a = jax.nn.sigmoid(a0_ref[...] + _dot(xw.astype(BF16), a2_ref[...]))
        kq = k * kk_ref[...]
        kk = kq * lax.rsqrt(head_sum(kq * kq) + EPS)
        kt = k * (1.0 + (a - 1.0) * ka_ref[...])
        bonus_ref[b] = head_sum(r * kt * rk_ref[...]) * v
        feats.append([decay, kk * a, kt, r, kk, v])

    nslab = GROUP // LANES
    kw = pk_ref.shape[1]
    vw = pv_ref.shape[1]
    for j in range(6):
        pieces = [_split3(feats[b][j]) for b in range(nb)]
        lhs = jnp.concatenate(
            [jnp.concatenate([p[:, g * LANES:(g + 1) * LANES] for bp in pieces for p in bp], axis=1)
             for g in range(nslab)], axis=0)
        if j < 5:
            out = _dot(lhs, pk_ref[...])
            for g in range(nslab):
                ops_ref[:, (j * nslab + g) * kw:(j * nslab + g + 1) * kw] = out[g * tb:(g + 1) * tb]
        else:
            out = _dot(lhs, pv_ref[...])
            for g in range(nslab):
                v_ref[:, g * vw:(g + 1) * vw] = out[g * tb:(g + 1) * tb]


def _rw_prep(proj, lp, dirn):
    nb, t, _ = proj.shape
    tb = min(RW_PREP_BLOCK, t)
    assert t % tb == 0 and nb * RW_HEADS == RW_CHAINS
    nblk = t // tb
    backward = dirn == 1
    per8 = tb // SUBLANES
    last8 = t // SUBLANES - 1
    if backward:
        nb_idx = lambda n: jnp.minimum((n + 1) * per8, last8)
    else:
        nb_idx = lambda n: jnp.maximum(n * per8 - 1, 0)
    perm = _rw_channel_perm()
    lane_row = lambda v: v.reshape(1, -1)
    pad_rows = lambda w, lo: jnp.zeros((LANES, GROUP), F32).at[lo:lo + w.shape[0]].set(w)
    mu_rkv = lane_row(jnp.concatenate([lp['rw_mu_rkv'][dirn][i * GROUP:(i + 1) * GROUP][perm] for i in range(3)]))
    mu_wa = lane_row(jnp.pad(lp['rw_mu_wa'][dirn], (0, LANES - RW_DECAY_LR - RW_A_LR)))
    w0 = lane_row(lp['rw_w0'][dirn][perm])
    w2 = pad_rows(lp['rw_w2'][dirn][:, perm], 0).astype(BF16)
    a0 = lane_row(lp['rw_a0'][dirn][perm])
    a2 = pad_rows(lp['rw_a2'][dirn][:, perm], RW_DECAY_LR).astype(BF16)
    k_k = lane_row(lp['rw_kk'][perm])
    k_a = lane_row(lp['rw_ka'][perm])
    r_k = lane_row(lp['rw_rk'].reshape(-1)[perm])
    lane = np.arange(GROUP)
    hsum = jnp.asarray((lane[:, None] % RW_HEADS) == (lane[None, :] % RW_HEADS), BF16)
    pk, pv = _rw_relayout_matrices(nb)
    params = (mu_rkv, mu_wa, w0, w2, a0, a2, k_k, k_a, r_k, hsum, pk, pv)
    full = lambda a: pl.BlockSpec(a.shape, lambda n: (0,) * a.ndim)
    ops_w = 5 * RW_KQ * LANES
    v_w = RW_VM * SUBLANES * LANES
    ops, vt, bonus = pl.pallas_call(
        functools.partial(_rw_prep_kernel, backward=backward),
        grid=(nblk,),
        in_specs=[pl.BlockSpec((nb, tb, 3 * GROUP), lambda n: (0, n, _col_block('r_rkv'))),
                  pl.BlockSpec((nb, tb, LANES), lambda n: (0, n, _col_block('r_wa'))),
                  pl.BlockSpec((nb, SUBLANES, 3 * GROUP), lambda n: (0, nb_idx(n), _col_block('r_rkv'))),
                  pl.BlockSpec((nb, SUBLANES, LANES), lambda n: (0, nb_idx(n), _col_block('r_wa')))]
        + [full(p) for p in params],
        out_specs=[pl.BlockSpec((tb, ops_w), lambda n: (n, 0)),
                   pl.BlockSpec((tb, v_w), lambda n: (n, 0)),
                   pl.BlockSpec((nb, tb, GROUP), lambda n: (0, n, 0))],
        out_shape=[jax.ShapeDtypeStruct((t, ops_w), F32), jax.ShapeDtypeStruct((t, v_w), F32),
                   jax.ShapeDtypeStruct((nb, t, GROUP), F32)],
        compiler_params=pltpu.CompilerParams(
            dimension_semantics=("arbitrary",), vmem_limit_bytes=RW_PREP_VMEM_LIMIT),
        name="rw_prep",
    )(proj, proj, proj, proj, *params)
    return ops, vt.reshape(t, RW_VM, SUBLANES, LANES), bonus


def _heads(t, n_heads):
    return t.reshape(t.shape[:-1] + (n_heads, t.shape[-1] // n_heads))


def _flip(t):
    return jnp.flip(t, axis=1)


def _l2norm(x):
    return x * lax.rsqrt(jnp.sum(x * x, axis=-1, keepdims=True) + EPS)


def _rw_gather_matrix(nb):
    vs = LANES // RW_HEADS
    q = np.zeros((RW_NPIECE, vs, RW_KLO, nb, RW_HEADS, nb, vs, RW_HEADS), np.float32)
    for b in range(nb):
        for v in range(vs):
            for h in range(RW_HEADS):
                q[:, v, :, b, h, b, v, h] = 1.0
    return jnp.asarray(q.reshape(RW_NPIECE * vs * LANES, nb * LANES), BF16)


def _rw_post_kernel(yf_ref, yb_ref, bf_ref, bb_ref, g_ref, gnw_ref, gnb_ref, g2_ref, hsum_ref, q_ref, o_ref):
    nb = bf_ref.shape[0]
    pieces = _split3(yf_ref[...] + yb_ref[...])
    slab = (LANES // RW_HEADS) * LANES
    nslab = GROUP // LANES
    token_major = [_dot(jnp.concatenate([p[:, g * slab:(g + 1) * slab] for p in pieces], axis=1), q_ref[...])
                   for g in range(nslab)]

    def head_sum(z):
        hi, lo = _split_bf16(z)
        return _dot(hi, hsum_ref[...]) + _dot(lo, hsum_ref[...])

    for b in range(nb):
        yy = jnp.concatenate([tm[:, b * LANES:(b + 1) * LANES] for tm in token_major], axis=1)
        d = yy - head_sum(yy) * (1.0 / RW_HEAD)
        var = head_sum(d * d) * (1.0 / RW_HEAD)
        y_rw = d * lax.rsqrt(var + RW_GN_EPS) * gnw_ref[...] + gnb_ref[...] + (bf_ref[b] + bb_ref[b])
        gate = _dot(jax.nn.sigmoid(g_ref[b]).astype(BF16), g2_ref[...])
        o_ref[b] = (y_rw * gate).astype(o_ref.dtype)


def _rw_post(y_f, y_b, bonus_f, bonus_b, proj, lp):
    nb, t, _ = bonus_f.shape
    tb = min(RW_PREP_BLOCK, t)
    perm = _rw_channel_perm()
    yw = RW_VM * SUBLANES * LANES
    gnw = lp['rw_gn_w'][perm].reshape(1, GROUP)
    gnb = lp['rw_gn_b'][perm].reshape(1, GROUP)
    g2 = jnp.zeros((LANES, GROUP), F32).at[:RW_GATE_LR].set(lp['rw_g2'][:, perm]).astype(BF16)
    lane = np.arange(GROUP)
    hsum = jnp.asarray((lane[:, None] % RW_HEADS) == (lane[None, :] % RW_HEADS), BF16)
    q = _rw_gather_matrix(nb)
    params = (gnw, gnb, g2, hsum, q)
    full = lambda a: pl.BlockSpec(a.shape, lambda n: (0,) * a.ndim)
    tok = pl.BlockSpec((nb, tb, GROUP), lambda n: (0, n, 0))
    return pl.pallas_call(
        _rw_post_kernel,
        grid=(t // tb,),
        in_specs=[pl.BlockSpec((tb, yw), lambda n: (n, 0)), pl.BlockSpec((tb, yw), lambda n: (n, 0)), tok, tok,
                  pl.BlockSpec((nb, tb, LANES), lambda n: (0, n, _col_block('r_g')))] + [full(p) for p in params],
        out_specs=tok,
        out_shape=jax.ShapeDtypeStruct((nb, t, GROUP), BF16),
        compiler_params=pltpu.CompilerParams(
            dimension_semantics=("arbitrary",), vmem_limit_bytes=RW_PREP_VMEM_LIMIT),
        name="rw_post",
    )(y_f.reshape(t, yw), y_b.reshape(t, yw), bonus_f, bonus_b, proj, *params)


def _rw_group(proj_c, proj_x, lp):
    s = jnp.zeros(RW_STATE_SHAPE, F32)
    outs = []
    for proj in (proj_c, proj_x):
        ops_f, v_f, bonus_f = _rw_prep(proj, lp, 0)
        ops_b, v_b, bonus_b = _rw_prep(proj, lp, 1)
        y_f, y_b, s = _rwkv_scan(ops_f, ops_b, v_f, v_b, s)
        outs.append(_rw_post(y_f, y_b, bonus_f, bonus_b, proj, lp))
    return outs


def _to_chunks(t):
    return t.reshape((t.shape[0], t.shape[1] // CHUNK, CHUNK) + t.shape[2:])


def _rms_heads(o, g):
    return o * lax.rsqrt(jnp.mean(o * o, axis=-1, keepdims=True) + EPS) * g


def _conv3_rows(x, w, row_len):
    b, t, ch = x.shape
    rows = t // row_len
    xp = jnp.pad(x.reshape(b, rows, row_len, ch), ((0, 0), (0, 0), (1, 1), (0, 0)))
    y = xp[:, :, :-2] * w[0] + xp[:, :, 1:-1] * w[1] + xp[:, :, 2:] * w[2]
    return y.reshape(b, t, ch)


def _gla_scan(q, k, v, log_f, s0):
    b, t, h, dv = v.shape
    q, k, v, log_f = (_to_chunks(a) for a in (q, k, v, log_f))
    cum = jnp.cumsum(log_f, axis=2)
    last = cum[:, :, -1:]
    q_dec = q * jnp.exp(cum)
    k_inv = k * jnp.exp(-cum)
    k_end = k * jnp.exp(last - cum)
    lower = jnp.tril(jnp.ones((CHUNK, CHUNK), bool))
    att = jnp.where(lower, jnp.einsum('bnihd,bnjhd->bnhij', q_dec, k_inv), 0.0)
    o_intra = jnp.einsum('bnhij,bnjhv->bnihv', att, v)
    d_state = jnp.einsum('bnjhd,bnjhv->nbhdv', k_end, v)
    chunk_decay = jnp.moveaxis(jnp.exp(last[:, :, 0]), 1, 0)

    def step(s, inp):
        dec, ds = inp
        return dec[..., None] * s + ds, s

    s_fin, s_start = lax.scan(step, s0, (chunk_decay, d_state))
    o_inter = jnp.einsum('bnihd,nbhdv->bnihv', q_dec, s_start)
    return (o_intra + o_inter).reshape(b, t, h, dv), s_fin


def _gdn_scan(q, k, v, log_a, beta, s0):
    b, t, h, dv = v.shape
    dk = q.shape[-1]
    hm = lambda a: jnp.moveaxis(_to_chunks(a), 3, 2)
    q, k, v, log_a, beta = (hm(a) for a in (q, k, v, log_a, beta))
    q = q * dk ** -0.5
    cum = jnp.cumsum(log_a, axis=-1)
    lower = jnp.tril(jnp.ones((CHUNK, CHUNK), bool))
    strict = jnp.tril(jnp.ones((CHUNK, CHUNK), bool), -1)
    decay = jnp.exp(jnp.where(lower, cum[..., :, None] - cum[..., None, :], -jnp.inf))
    kk = jnp.einsum('bnhid,bnhjd->bnhij', k, k)
    lmat = jnp.where(strict, beta[..., None] * kk * decay, 0.0) + jnp.eye(CHUNK, dtype=kk.dtype)
    rhs = jnp.concatenate([v * beta[..., None], k * (beta * jnp.exp(cum))[..., None]], axis=-1)
    sol = lax.linalg.triangular_solve(lmat, rhs, left_side=True, lower=True, unit_diagonal=True)
    u, w = sol[..., :dv], sol[..., dv:]
    a_qk = jnp.einsum('bnhid,bnhjd->bnhij', q, k) * decay
    k_end = k * jnp.exp(cum[..., -1:] - cum)[..., None]
    chunk_decay = jnp.exp(cum[..., -1])
    cm = lambda a: jnp.moveaxis(a, 1, 0)

    def step(s, inp):
        u_i, w_i, ke_i, dec_i = inp
        v_new = u_i - jnp.einsum('bhcd,bhdv->bhcv', w_i, s)
        s_next = dec_i[..., None, None] * s + jnp.einsum('bhcd,bhcv->bhdv', ke_i, v_new)
        return s_next, (s, v_new)

    s_fin, (s_start, v_new) = lax.scan(step, s0, (cm(u), cm(w), cm(k_end), cm(chunk_decay)))
    o = (jnp.einsum('bnhcd,nbhdv->bnhcv', q * jnp.exp(cum)[..., None], s_start)
         + jnp.einsum('bnhij,nbhjv->bnhiv', a_qk, v_new))
    return jnp.moveaxis(o, 2, 3).reshape(b, t, h, dv), s_fin


def _seg(proj, name):
    o, wd, _ = SEG[name]
    return proj[..., o:o + wd]


TOKEN_BLOCK = 256
HI = lax.Precision.HIGHEST


def _dot(a, b, precision=None):
    return jnp.dot(a, b, preferred_element_type=F32, precision=precision)


def _dot_nt(a, b):
    return lax.dot_general(a, b, (((1,), (1,)), ((), ())), preferred_element_type=F32)


def _dot_tn(a, b):
    return lax.dot_general(a, b, (((0,), (0,)), ((), ())), preferred_element_type=F32)


def _split_bf16(x):
    hi = x.astype(BF16)
    return hi, (x - hi.astype(F32)).astype(BF16)


def _dot3(a_hi, a_lo, b_hi, b_lo):
    return _dot(a_hi, b_hi) + (_dot(a_hi, b_lo) + _dot(a_lo, b_hi))


def _silu(x):
    return x * jax.nn.sigmoid(x)


def _chunk_masks(reverse):
    ri = lax.broadcasted_iota(jnp.int32, (CHUNK, CHUNK), 0)
    ci = lax.broadcasted_iota(jnp.int32, (CHUNK, CHUNK), 1)
    incl = (ci >= ri) if reverse else (ci <= ri)
    strict = (ci > ri) if reverse else (ci < ri)
    return incl, strict, (ri == ci).astype(F32)


def _conv3_block(x, cw, row_len):
    tb = x.shape[0]
    pos = lax.broadcasted_iota(jnp.int32, (tb, 1), 0) % row_len
    x_prev = jnp.where(pos == 0, 0.0, pltpu.roll(x, 1, 0))
    x_next = jnp.where(pos == row_len - 1, 0.0, pltpu.roll(x, tb - 1, 0))
    return x_prev * cw[0:1] + x * cw[1:2] + x_next * cw[2:3]


def _gdn_kernel(*refs, reverse, finalize, row_len, dirn):
    if finalize:
        (qkv_ref, ab_ref, z_ref, ob_ref, cw_ref, alog_ref, dtb_ref, g_ref, s0_ref,
         o_ref, sfin_ref, s_scr) = refs
    else:
        qkv_ref, ab_ref, cw_ref, alog_ref, dtb_ref, s0_ref, o_ref, sfin_ref, s_scr = refs
    n = pl.program_id(1)

    @pl.when(n == 0)
    def _():
        s_scr[...] = s0_ref[0]

    tb = qkv_ref.shape[1]
    qkv = _silu(_conv3_block(qkv_ref[0], cw_ref[...], row_len))
    ab = ab_ref[0]
    log_a = -jnp.exp(alog_ref[...]) * jax.nn.softplus(ab + dtb_ref[...])
    beta = jax.nn.sigmoid(ab)
    incl, strict, eye = _chunk_masks(reverse)
    tri = incl.astype(F32)
    last = 0 if reverse else CHUNK - 1
    hd = GDN_DK
    nchunk = tb // CHUNK
    order = list(range(nchunk - 1, -1, -1) if reverse else range(nchunk))
    rows = lambda c: slice(c * CHUNK, (c + 1) * CHUNK)
    cum_all = _dot(tri, jnp.concatenate([log_a[rows(c)] for c in range(nchunk)], axis=1), HI)

    items = [(c, h) for c in order for h in range(GDN_HEADS)]
    pre = {}
    for c, h in items:
        col = dirn * GDN_HEADS + h
        cum = cum_all[:, c * LANES + col:c * LANES + col + 1]
        bet = beta[rows(c), 2 * GDN_HEADS + col:2 * GDN_HEADS + col + 1]
        q = _l2norm(qkv[rows(c), h * hd:(h + 1) * hd]) * hd ** -0.5
        k = _l2norm(qkv[rows(c), GROUP + h * hd:GROUP + (h + 1) * hd])
        v = qkv[rows(c), 2 * GROUP + h * hd:2 * GROUP + (h + 1) * hd]
        cum_row = jnp.sum(eye * cum, axis=0, keepdims=True)
        decay = jnp.exp(jnp.where(incl, cum - cum_row, -jnp.inf))
        kb = k.astype(BF16)
        a = jnp.where(strict, bet * _dot_nt(kb, kb) * decay, 0.0)
        ecum = jnp.exp(cum)
        total = cum[last:last + 1]
        pre[c, h] = dict(
            x=-a, rhs=jnp.concatenate([v * bet, k * (bet * ecum)], axis=1),
            a_qk=(_dot_nt(q.astype(BF16), kb) * decay).astype(BF16),
            qe=(q * ecum).astype(BF16), k_end=(k * jnp.exp(total - cum)).astype(BF16),
            dec=jnp.exp(total))
    for p in pre.values():
        p['inv'] = eye + p['x']
    for level in range(6):
        for p in pre.values():
            x_hi, x_lo = _split_bf16(p['x'])
            if level == 0:
                p['x'] = _dot3(x_hi, x_lo, x_hi, x_lo)
            elif level < 5:
                i_hi, i_lo = _split_bf16(p['inv'])
                both = _dot3(jnp.concatenate([x_hi, i_hi], axis=0), jnp.concatenate([x_lo, i_lo], axis=0),
                             x_hi, x_lo)
                p['x'] = both[:CHUNK]
                p['inv'] = p['inv'] + both[CHUNK:]
            else:
                i_hi, i_lo = _split_bf16(p['inv'])
                p['inv'] = p['inv'] + _dot3(i_hi, i_lo, x_hi, x_lo)
    for p in pre.values():
        i_hi, i_lo = _split_bf16(p['inv'])
        r_hi, r_lo = _split_bf16(p['rhs'])
        sol = _dot3(i_hi, i_lo, r_hi, r_lo)
        p['u'], p['w'] = sol[:, :GDN_DV], sol[:, GDN_DV:].astype(BF16)

    for c, h in items:
        p = pre[c, h]
        s = s_scr[h]
        sb = s.astype(BF16)
        v_new = p['u'] - _dot(p['w'], sb)
        vb = v_new.astype(BF16)
        o = _dot(p['qe'], sb) + _dot(p['a_qk'], vb)
        s_scr[h] = p['dec'] * s + _dot_tn(p['k_end'], vb)
        cols = slice(h * GDN_DV, (h + 1) * GDN_DV)
        if finalize:
            o = o + ob_ref[0, rows(c), cols]
            o = o * lax.rsqrt(jnp.mean(o * o, axis=-1, keepdims=True) + EPS) * g_ref[...]
            o = o * _silu(z_ref[0, rows(c), cols])
        o_ref[0, rows(c), cols] = o.astype(o_ref.dtype)

    @pl.when(n == pl.num_programs(1) - 1)
    def _():
        sfin_ref[0] = s_scr[...]


def _gdn_pass(proj, o_other, s0, lp, *, dirn, row_len):
    b, t, _ = proj.shape
    tb = min(TOKEN_BLOCK, t)
    assert t % tb == 0 and tb % row_len == 0
    nblk = t // tb
    reverse = dirn == 1
    finalize = o_other is not None
    tok = (lambda i, n: (i, nblk - 1 - n)) if reverse else (lambda i, n: (i, n))
    seg = lambda name, wd: pl.BlockSpec((1, tb, wd), lambda i, n: tok(i, n) + (_col_block(name),))
    full = lambda a: pl.BlockSpec(a.shape, lambda i, n: (0,) * a.ndim)
    lane_row = lambda vals: jnp.pad(vals.reshape(1, -1), ((0, 0), (0, LANES - vals.size)))
    cw = lp['gdn_conv']
    alog = lane_row(lp['gdn_a_log'])
    dtb = lane_row(lp['gdn_dt_bias'])
    g = lp['gdn_norm_g'].reshape(1, GDN_DV)
    state_spec = pl.BlockSpec((1, GDN_HEADS, GDN_DK, GDN_DV), lambda i, n: (i, 0, 0, 0))
    if finalize:
        args = (proj, proj, proj, o_other, cw, alog, dtb, g, s0)
        in_specs = [seg('d_qkv', 3 * GROUP), seg('d_ab', LANES), seg('d_z', GROUP),
                    pl.BlockSpec((1, tb, GROUP), lambda i, n: tok(i, n) + (0,)),
                    full(cw), full(alog), full(dtb), full(g), state_spec]
    else:
        args = (proj, proj, cw, alog, dtb, s0)
        in_specs = [seg('d_qkv', 3 * GROUP), seg('d_ab', LANES), full(cw), full(alog), full(dtb), state_spec]
    return pl.pallas_call(
        functools.partial(_gdn_kernel, reverse=reverse, finalize=finalize, row_len=row_len, dirn=dirn),
        grid=(b, nblk),
        in_specs=in_specs,
        out_specs=[pl.BlockSpec((1, tb, GROUP), lambda i, n: tok(i, n) + (0,)), state_spec],
        out_shape=[jax.ShapeDtypeStruct((b, t, GROUP), BF16 if finalize else F32),
                   jax.ShapeDtypeStruct((b, GDN_HEADS, GDN_DK, GDN_DV), F32)],
        scratch_shapes=[pltpu.VMEM((GDN_HEADS, GDN_DK, GDN_DV), F32)],
        compiler_params=pltpu.CompilerParams(
            dimension_semantics=("arbitrary", "arbitrary"), vmem_limit_bytes=VMEM_LIMIT),
        name="gdn_fwd" if finalize else "gdn_bwd",
    )(*args)


def _gla_kernel(*refs, reverse, finalize):
    if finalize:
        (q_ref, k_ref, v_ref, lo_ref, r_ref, ob_ref, up_ref, ab_ref, g_ref, s0_ref,
         o_ref, sfin_ref, s_scr) = refs
    else:
        q_ref, k_ref, v_ref, lo_ref, up_ref, ab_ref, s0_ref, o_ref, sfin_ref, s_scr = refs
    n = pl.program_id(1)

    @pl.when(n == 0)
    def _():
        s_scr[...] = s0_ref[0]

    tb = q_ref.shape[1]
    gate = _dot(lo_ref[0].astype(BF16), up_ref[...].astype(BF16)) + ab_ref[...]
    log_f = jax.nn.log_sigmoid(gate) / GLA_TAU
    incl, _, eye = _chunk_masks(reverse)
    tri = incl.astype(F32)
    last = 0 if reverse else CHUNK - 1
    nchunk = tb // CHUNK
    order = range(nchunk - 1, -1, -1) if reverse else range(nchunk)
    for c in order:
        sl = slice(c * CHUNK, (c + 1) * CHUNK)
        cum = _dot(tri, log_f[sl], HI)
        total = cum[last:last + 1]
        q_dec = q_ref[0, sl, :] * GLA_DK ** -0.5 * jnp.exp(cum)
        k = k_ref[0, sl, :]
        k_inv = (k * jnp.exp(-cum)).astype(BF16)
        k_end = (k * jnp.exp(total - cum)).astype(BF16)
        q_dec = q_dec.astype(BF16)
        dec_row = jnp.exp(total)
        for h in range(GLA_HEADS):
            kc = slice(h * GLA_DK, (h + 1) * GLA_DK)
            vc = slice(h * GLA_DV, (h + 1) * GLA_DV)
            vb = v_ref[0, sl, vc].astype(BF16)
            att = jnp.where(incl, _dot_nt(q_dec[:, kc], k_inv[:, kc]), 0.0)
            s = s_scr[h]
            o = _dot(att.astype(BF16), vb) + _dot(q_dec[:, kc], s.astype(BF16))
            dec_col = jnp.sum(eye * dec_row[:, kc], axis=1, keepdims=True)
            s_scr[h] = dec_col * s + _dot_tn(k_end[:, kc], vb)
            if finalize:
                o = o + ob_ref[0, sl, vc]
                o = o * lax.rsqrt(jnp.mean(o * o, axis=-1, keepdims=True) + EPS) * g_ref[...]
                o = o * _silu(r_ref[0, sl, vc])
            o_ref[0, sl, vc] = o.astype(o_ref.dtype)

    @pl.when(n == pl.num_programs(1) - 1)
    def _():
        sfin_ref[0] = s_scr[...]


def _gla_pass(proj, o_other, s0, lp, *, dirn):
    b, t, _ = proj.shape
    tb = min(TOKEN_BLOCK, t)
    assert t % tb == 0
    nblk = t // tb
    reverse = dirn == 1
    finalize = o_other is not None
    tok = (lambda i, n: (i, nblk - 1 - n)) if reverse else (lambda i, n: (i, n))
    seg = lambda name: pl.BlockSpec((1, tb, SEG[name][2]), lambda i, n: tok(i, n) + (_col_block(name),))
    full = lambda a: pl.BlockSpec(a.shape, lambda i, n: (0,) * a.ndim)
    hk = GLA_HEADS * GLA_DK
    up = jnp.pad(lp['gla_a_up'][dirn], ((0, LANES - GLA_LR), (0, 0)))
    ab = lp['gla_a_b'][dirn].reshape(1, hk)
    g = lp['gla_norm_g'].reshape(1, GLA_DV)
    state_spec = pl.BlockSpec((1, GLA_HEADS, GLA_DK, GLA_DV), lambda i, n: (i, 0, 0, 0))
    if finalize:
        args = (proj, proj, proj, proj, proj, o_other, up, ab, g, s0)
        in_specs = [seg('g_q'), seg('g_k'), seg('g_v'), seg('g_lo'), seg('g_r'),
                    pl.BlockSpec((1, tb, GROUP), lambda i, n: tok(i, n) + (0,)),
                    full(up), full(ab), full(g), state_spec]
    else:
        args = (proj, proj, proj, proj, up, ab, s0)
        in_specs = [seg('g_q'), seg('g_k'), seg('g_v'), seg('g_lo'), full(up), full(ab), state_spec]
    return pl.pallas_call(
        functools.partial(_gla_kernel, reverse=reverse, finalize=finalize),
        grid=(b, nblk),
        in_specs=in_specs,
        out_specs=[pl.BlockSpec((1, tb, GROUP), lambda i, n: tok(i, n) + (0,)), state_spec],
        out_shape=[jax.ShapeDtypeStruct((b, t, GROUP), BF16 if finalize else F32),
                   jax.ShapeDtypeStruct((b, GLA_HEADS, GLA_DK, GLA_DV), F32)],
        scratch_shapes=[pltpu.VMEM((GLA_HEADS, GLA_DK, GLA_DV), F32)],
        compiler_params=pltpu.CompilerParams(
            dimension_semantics=("arbitrary", "arbitrary"), vmem_limit_bytes=VMEM_LIMIT),
        name="gla_fwd" if finalize else "gla_bwd",
    )(*args)


def _two_direction_group(pass_fn, state_shape, proj_c, proj_x, **kw):
    zero = jnp.zeros((proj_x[0].shape[0],) + state_shape, F32)
    ob_c, sb_c = pass_fn(proj_c[0], None, zero, dirn=1, **proj_c[1], **kw)
    ob_x, _ = pass_fn(proj_x[0], None, sb_c, dirn=1, **proj_x[1], **kw)
    y_c, sf_c = pass_fn(proj_c[0], ob_c, zero, dirn=0, **proj_c[1], **kw)
    y_x, _ = pass_fn(proj_x[0], ob_x, sf_c, dirn=0, **proj_x[1], **kw)
    return y_c, y_x


def _gla_group_pallas(proj_c, proj_x, lp):
    return _two_direction_group(_gla_pass, (GLA_HEADS, GLA_DK, GLA_DV), (proj_c, {}), (proj_x, {}), lp=lp)


def _gdn_group_pallas(proj_c, proj_x, lp):
    return _two_direction_group(_gdn_pass, (GDN_HEADS, GDN_DK, GDN_DV),
                                (proj_c, dict(row_len=proj_c.shape[1])), (proj_x, dict(row_len=GRID_W)), lp=lp)


def _gla_group(proj, init, lp):
    g_q, g_k, g_v, g_r, g_lo = (_seg(proj, n) for n in ('g_q', 'g_k', 'g_v', 'g_r', 'g_lo'))
    q = _heads(g_q, GLA_HEADS) * GLA_DK ** -0.5
    k = _heads(g_k, GLA_HEADS)
    v = _heads(g_v, GLA_HEADS)
    log_f = [_heads(jax.nn.log_sigmoid(g_lo @ lp['gla_a_up'][i] + lp['gla_a_b'][i]) / GLA_TAU, GLA_HEADS)
             for i in range(2)]
    o_f, s_f = _gla_scan(q, k, v, log_f[0], init[0])
    o_b, s_b = _gla_scan(_flip(q), _flip(k), _flip(v), _flip(log_f[1]), init[1])
    y = _rms_heads(o_f + _flip(o_b), lp['gla_norm_g']) * jax.nn.silu(_heads(g_r, GLA_HEADS))
    return y.reshape(y.shape[:2] + (GROUP,)), (s_f, s_b)


def _gdn_group(proj, row_len, init, lp):
    b, t = proj.shape[:2]
    d_qkv, d_z, d_ab = (_seg(proj, n) for n in ('d_qkv', 'd_z', 'd_ab'))
    d_a, d_b = d_ab[..., :2 * GDN_HEADS], d_ab[..., 2 * GDN_HEADS:]
    qkv = jax.nn.silu(_conv3_rows(d_qkv, lp['gdn_conv'], row_len))
    q, k, v = (_heads(a, GDN_HEADS) for a in jnp.split(qkv, 3, axis=-1))
    q, k = _l2norm(q), _l2norm(k)
    log_a = -jnp.exp(lp['gdn_a_log']) * jax.nn.softplus(d_a.reshape(b, t, 2, GDN_HEADS) + lp['gdn_dt_bias'])
    beta = jax.nn.sigmoid(d_b.reshape(b, t, 2, GDN_HEADS))
    o_f, s_f = _gdn_scan(q, k, v, log_a[:, :, 0], beta[:, :, 0], init[0])
    o_b, s_b = _gdn_scan(_flip(q), _flip(k), _flip(v), _flip(log_a[:, :, 1]), _flip(beta[:, :, 1]), init[1])
    y = _rms_heads(o_f + _flip(o_b), lp['gdn_norm_g']) * jax.nn.silu(_heads(d_z, GDN_HEADS))
    return y.reshape(b, t, GROUP), (s_f, s_b)


def _sc_kernel(b_ref, c_ref, h_ref, cw_ref, o_ref, *, row_len):
    o_ref[0] = (b_ref[0] * _conv3_block(c_ref[0] * h_ref[0], cw_ref[...], row_len)).astype(o_ref.dtype)


def _sc_group(proj, row_len, lp):
    b, t, _ = proj.shape
    tb = min(TOKEN_BLOCK, t)
    assert t % tb == 0 and tb % row_len == 0
    seg = lambda name: pl.BlockSpec((1, tb, GROUP), lambda i, n: (i, n, _col_block(name)))
    cw = lp['sc_conv']
    return pl.pallas_call(
        functools.partial(_sc_kernel, row_len=row_len),
        grid=(b, t // tb),
        in_specs=[seg('c_b'), seg('c_c'), seg('c_h'), pl.BlockSpec(cw.shape, lambda i, n: (0, 0))],
        out_specs=pl.BlockSpec((1, tb, GROUP), lambda i, n: (i, n, 0)),
        out_shape=jax.ShapeDtypeStruct((b, t, GROUP), BF16),
        compiler_params=pltpu.CompilerParams(
            dimension_semantics=("arbitrary", "arbitrary"), vmem_limit_bytes=VMEM_LIMIT),
        name="short_conv",
    )(proj, proj, proj, cw)


def _mixers(proj_c, proj_x, lp):
    tc = proj_c.shape[1]
    y_gla_c, y_gla_x = _gla_group_pallas(proj_c, proj_x, lp)
    y_gdn_c, y_gdn_x = _gdn_group_pallas(proj_c, proj_x, lp)
    y_sc_c = _sc_group(proj_c, tc, lp)
    y_sc_x = _sc_group(proj_x, GRID_W, lp)
    y_rw_c, y_rw_x = _rw_group(proj_c, proj_x, lp)
    return (y_gla_c, y_gdn_c, y_sc_c, y_rw_c), (y_gla_x, y_gdn_x, y_sc_x, y_rw_x)


def kernel(x, c, ctx, c_ctx, ada_w, ada_b, norm1_g, norm2_g, w_in, w_out, gla_a_up, gla_a_b, gla_norm_g, gdn_conv, gdn_a_log, gdn_dt_bias, gdn_norm_g, sc_conv, rw_mu_rkv, rw_mu_wa, rw_w0, rw_w2, rw_a0, rw_a2, rw_g2, rw_kk, rw_ka, rw_rk, rw_gn_w, rw_gn_b, ffn_w_gu, ffn_w_down, final_g):
    nb, t, d = x.shape
    tc = ctx.shape[1]
    depth = w_in.shape[0]
    assert d == D_MODEL and t % 512 == 0 and tc % CHUNK == 0 and nb + 1 <= SUBLANES

    cond = jnp.concatenate([c, c_ctx[None, :], jnp.zeros((SUBLANES - nb - 1, d), F32)], axis=0)
    mod = _modulation(cond, ada_w, ada_b)

    tm_x = 512
    tm_c = tc
    h_x, h_c = x, ctx
    for l in range(depth):
        lp = dict(gla_a_up=gla_a_up[l], gla_a_b=gla_a_b[l],
                  gla_norm_g=gla_norm_g[l], gdn_conv=gdn_conv[l], gdn_a_log=gdn_a_log[l],
                  gdn_dt_bias=gdn_dt_bias[l], gdn_norm_g=gdn_norm_g[l], sc_conv=sc_conv[l],
                  rw_mu_rkv=rw_mu_rkv[l], rw_mu_wa=rw_mu_wa[l], rw_w0=rw_w0[l], rw_w2=rw_w2[l],
                  rw_a0=rw_a0[l], rw_a2=rw_a2[l], rw_g2=rw_g2[l], rw_kk=rw_kk[l], rw_ka=rw_ka[l],
                  rw_rk=rw_rk[l], rw_gn_w=rw_gn_w[l], rw_gn_b=rw_gn_b[l])
        m_x = [mod[l, :nb, i * d:(i + 1) * d][:, None, :] for i in range(6)]
        m_c = [jnp.broadcast_to(mod[l, nb, i * d:(i + 1) * d][None, None, :], (nb, 1, d)) for i in range(6)]
        w_in_l = _pad_w_in(w_in[l])
        w_out_l = jnp.concatenate([w_out[l][:3 * GROUP], w_out[l][3 * GROUP:][_rw_channel_perm()]],
                                  axis=0).astype(BF16)
        w_gu_l = ffn_w_gu[l].astype(BF16)
        w_dn_l = ffn_w_down[l].astype(BF16)

        proj_c = _norm_proj(h_c, norm1_g[l], m_c[1], m_c[0], w_in_l, tm_c, 1024)
        proj_x = _norm_proj(h_x, norm1_g[l], m_x[1], m_x[0], w_in_l, 2 * tm_x, 1024)
        y_c, y_x = _mixers(proj_c, proj_x, lp)

        h_x = _groups_residual(y_x, w_out_l, h_x, m_x[2], 2 * tm_x, 1024)
        a_x = _norm_swiglu(h_x, norm2_g[l], m_x[4], m_x[3], w_gu_l, 2 * tm_x, 512)
        h_x = _proj_residual(a_x, w_dn_l, h_x, m_x[5], tm_x, 512)
        if l < depth - 1:
            h_c = _groups_residual(y_c, w_out_l, h_c, m_c[2], tm_c, 512)
            a_c = _norm_swiglu(h_c, norm2_g[l], m_c[4], m_c[3], w_gu_l, tm_c, 512)
            h_c = _proj_residual(a_c, w_dn_l, h_c, m_c[5], tm_c, 512)
    return _final_norm(h_x, final_g, tm_x)
```

```python
import functools
import math

import numpy as np
import jax
import jax.numpy as jnp
from jax import lax
from jax.experimental import pallas as pl
from jax.experimental.pallas import tpu as pltpu

F32 = jnp.float32
BF16 = jnp.bfloat16

D_MODEL = 2048
GROUP = D_MODEL // 4
CHUNK = 64
EPS = 1e-6
GRID_W = 64

GLA_HEADS = 4
GLA_DK = 64
GLA_DV = 128
GLA_LR = 16
GLA_TAU = 16.0
GDN_HEADS = 4
GDN_DK = 128
GDN_DV = 128
RW_HEAD = 64
RW_HEADS = GROUP // RW_HEAD
RW_DECAY_LR = 32
RW_A_LR = 32
RW_GATE_LR = 96
RW_DECAY_SCALE = math.exp(-0.5)
RW_GN_EPS = 64e-5
FFN = -(-(8 * D_MODEL) // (3 * 256)) * 256

SUBLANES = 8
LANES = 128
VMEM_LIMIT = 56 * 1024 * 1024

_SRC_SEGS = (
    ('g_q', 256), ('g_k', 256), ('g_v', 512), ('g_r', 512), ('g_lo', GLA_LR),
    ('d_qkv', 1536), ('d_z', 512), ('d_ab', 16),
    ('c_b', 512), ('c_c', 512), ('c_h', 512),
    ('r_rkv', 1536), ('r_wa', 64), ('r_g', RW_GATE_LR),
)
_DST_ORDER = ('g_q', 'g_k', 'g_v', 'g_r', 'd_qkv', 'r_rkv', 'd_z', 'c_b', 'c_c', 'c_h',
              'g_lo', 'd_ab', 'r_wa', 'r_g')


def _round_up(n, m):
    return -(-n // m) * m


def _seg_layout():
    width = dict(_SRC_SEGS)
    src, o = {}, 0
    for name, wd in _SRC_SEGS:
        src[name] = o
        o += wd
    dst, o = {}, 0
    for name in _DST_ORDER:
        pw = _round_up(width[name], LANES)
        assert o % pw == 0, name
        dst[name] = (o, width[name], pw)
        o += pw
    return src, dst, o


SEG_SRC, SEG, PROJ_PAD = _seg_layout()


def _pad_w_in(w):
    parts = []
    for name in _DST_ORDER:
        _, wd, pw = SEG[name]
        seg = w[:, SEG_SRC[name]:SEG_SRC[name] + wd]
        if name == 'r_rkv':
            perm = _rw_channel_perm()
            seg = jnp.concatenate([seg[:, i * GROUP:(i + 1) * GROUP][:, perm] for i in range(3)], axis=1)
        if pw != wd:
            seg = jnp.pad(seg, ((0, 0), (0, pw - wd)))
        parts.append(seg)
    return jnp.concatenate(parts, axis=1).astype(BF16)


def _col_block(name):
    o, _, pw = SEG[name]
    return o // pw


def _mod_kernel(c_ref, w_ref, b_ref, o_ref):
    x = c_ref[...]
    x = (x * jax.nn.sigmoid(x)).astype(BF16)
    o_ref[0] = jnp.dot(x, w_ref[0].astype(BF16), preferred_element_type=F32) + b_ref[0]


def _modulation(cond, ada_w, ada_b):
    nl, d, n = ada_w.shape
    tn = 1024
    return pl.pallas_call(
        _mod_kernel,
        grid=(nl, n // tn),
        in_specs=[pl.BlockSpec((SUBLANES, d), lambda l, j: (0, 0)),
                  pl.BlockSpec((1, d, tn), lambda l, j: (l, 0, j)),
                  pl.BlockSpec((1, 1, tn), lambda l, j: (l, 0, j))],
        out_specs=pl.BlockSpec((1, SUBLANES, tn), lambda l, j: (l, 0, j)),
        out_shape=jax.ShapeDtypeStruct((nl, SUBLANES, n), F32),
        compiler_params=pltpu.CompilerParams(
            dimension_semantics=("arbitrary", "arbitrary"), vmem_limit_bytes=VMEM_LIMIT),
        name="modulation",
    )(cond, ada_w, ada_b.reshape(nl, 1, n))


def _modnorm_rows(h_ref, g_ref, sc_ref, sh_ref):
    x = h_ref[0]
    y = x * lax.rsqrt(jnp.mean(x * x, axis=-1, keepdims=True) + EPS) * g_ref[...]
    return (y * (1.0 + sc_ref[0]) + sh_ref[0]).astype(BF16)


def _norm_proj_kernel(h_ref, g_ref, sc_ref, sh_ref, w_ref, o_ref, a_scr):
    @pl.when(pl.program_id(2) == 0)
    def _():
        a_scr[...] = _modnorm_rows(h_ref, g_ref, sc_ref, sh_ref)

    o_ref[0] = jnp.dot(a_scr[...], w_ref[...], preferred_element_type=F32).astype(o_ref.dtype)


def _norm_proj(h, g, scale, shift, w, tm, tn):
    b, t, d = h.shape
    n = w.shape[1]
    return pl.pallas_call(
        _norm_proj_kernel,
        grid=(b, t // tm, n // tn),
        in_specs=[pl.BlockSpec((1, tm, d), lambda i, m, j: (i, m, 0)),
                  pl.BlockSpec((1, d), lambda i, m, j: (0, 0)),
                  pl.BlockSpec((1, 1, d), lambda i, m, j: (i, 0, 0)),
                  pl.BlockSpec((1, 1, d), lambda i, m, j: (i, 0, 0)),
                  pl.BlockSpec((d, tn), lambda i, m, j: (0, j))],
        out_specs=pl.BlockSpec((1, tm, tn), lambda i, m, j: (i, m, j)),
        out_shape=jax.ShapeDtypeStruct((b, t, n), F32),
        scratch_shapes=[pltpu.VMEM((tm, d), BF16)],
        compiler_params=pltpu.CompilerParams(
            dimension_semantics=("arbitrary", "arbitrary", "arbitrary"), vmem_limit_bytes=VMEM_LIMIT),
        name="norm_proj",
    )(h, g.reshape(1, d), scale, shift, w)


def _norm_swiglu_kernel(h_ref, g_ref, sc_ref, sh_ref, wg_ref, wu_ref, o_ref, a_scr):
    @pl.when(pl.program_id(2) == 0)
    def _():
        a_scr[...] = _modnorm_rows(h_ref, g_ref, sc_ref, sh_ref)

    a = a_scr[...]
    gate = jnp.dot(a, wg_ref[...], preferred_element_type=F32)
    up = jnp.dot(a, wu_ref[...], preferred_element_type=F32)
    o_ref[0] = (gate * jax.nn.sigmoid(gate) * up).astype(o_ref.dtype)


def _norm_swiglu(h, g, scale, shift, w_gu, tm, tn):
    b, t, d = h.shape
    f = w_gu.shape[1] // 2
    nj = f // tn
    return pl.pallas_call(
        _norm_swiglu_kernel,
        grid=(b, t // tm, nj),
        in_specs=[pl.BlockSpec((1, tm, d), lambda i, m, j: (i, m, 0)),
                  pl.BlockSpec((1, d), lambda i, m, j: (0, 0)),
                  pl.BlockSpec((1, 1, d), lambda i, m, j: (i, 0, 0)),
                  pl.BlockSpec((1, 1, d), lambda i, m, j: (i, 0, 0)),
                  pl.BlockSpec((d, tn), lambda i, m, j: (0, j)),
                  pl.BlockSpec((d, tn), lambda i, m, j: (0, j + nj))],
        out_specs=pl.BlockSpec((1, tm, tn), lambda i, m, j: (i, m, j)),
        out_shape=jax.ShapeDtypeStruct((b, t, f), BF16),
        scratch_shapes=[pltpu.VMEM((tm, d), BF16)],
        compiler_params=pltpu.CompilerParams(
            dimension_semantics=("arbitrary", "arbitrary", "arbitrary"), vmem_limit_bytes=VMEM_LIMIT),
        name="norm_swiglu",
    )(h, g.reshape(1, d), scale, shift, w_gu, w_gu)


def _proj_residual_kernel(a_ref, w_ref, h_ref, gate_ref, o_ref):
    o_ref[0] = h_ref[0] + gate_ref[0] * jnp.dot(a_ref[0], w_ref[...], preferred_element_type=F32)


def _proj_residual(a, w, h, gate, tm, tn):
    b, t, k = a.shape
    d = w.shape[1]
    return pl.pallas_call(
        _proj_residual_kernel,
        grid=(b, t // tm, d // tn),
        in_specs=[pl.BlockSpec((1, tm, k), lambda i, m, j: (i, m, 0)),
                  pl.BlockSpec((k, tn), lambda i, m, j: (0, j)),
                  pl.BlockSpec((1, tm, tn), lambda i, m, j: (i, m, j)),
                  pl.BlockSpec((1, 1, tn), lambda i, m, j: (i, 0, j))],
        out_specs=pl.BlockSpec((1, tm, tn), lambda i, m, j: (i, m, j)),
        out_shape=jax.ShapeDtypeStruct((b, t, d), F32),
        compiler_params=pltpu.CompilerParams(
            dimension_semantics=("arbitrary", "arbitrary", "arbitrary"), vmem_limit_bytes=VMEM_LIMIT),
        name="proj_residual",
    )(a, w, h, gate)


def _groups_residual_kernel(*refs):
    *a_refs, w_ref, h_ref, gate_ref, o_ref = refs
    k = a_refs[0].shape[2]
    acc = jnp.dot(a_refs[0][0], w_ref[0:k, :], preferred_element_type=F32)
    for i, a_ref in enumerate(a_refs[1:], start=1):
        acc = acc + jnp.dot(a_ref[0], w_ref[i * k:(i + 1) * k, :], preferred_element_type=F32)
    o_ref[0] = h_ref[0] + gate_ref[0] * acc


def _groups_residual(groups, w, h, gate, tm, tn):
    b, t, k = groups[0].shape
    d = w.shape[1]
    assert len(groups) * k == w.shape[0]
    return pl.pallas_call(
        _groups_residual_kernel,
        grid=(b, t // tm, d // tn),
        in_specs=[pl.BlockSpec((1, tm, k), lambda i, m, j: (i, m, 0)) for _ in groups]
        + [pl.BlockSpec((w.shape[0], tn), lambda i, m, j: (0, j)),
           pl.BlockSpec((1, tm, tn), lambda i, m, j: (i, m, j)),
           pl.BlockSpec((1, 1, tn), lambda i, m, j: (i, 0, j))],
        out_specs=pl.BlockSpec((1, tm, tn), lambda i, m, j: (i, m, j)),
        out_shape=jax.ShapeDtypeStruct((b, t, d), F32),
        compiler_params=pltpu.CompilerParams(
            dimension_semantics=("arbitrary", "arbitrary", "arbitrary"), vmem_limit_bytes=VMEM_LIMIT),
        name="groups_residual",
    )(*groups, w, h, gate)


def _final_norm_kernel(h_ref, g_ref, o_ref):
    x = h_ref[0]
    o_ref[0] = x * lax.rsqrt(jnp.mean(x * x, axis=-1, keepdims=True) + EPS) * g_ref[...]


def _final_norm(h, g, tm):
    b, t, d = h.shape
    return pl.pallas_call(
        _final_norm_kernel,
        grid=(b, t // tm),
        in_specs=[pl.BlockSpec((1, tm, d), lambda i, m: (i, m, 0)),
                  pl.BlockSpec((1, d), lambda i, m: (0, 0))],
        out_specs=pl.BlockSpec((1, tm, d), lambda i, m: (i, m, 0)),
        out_shape=jax.ShapeDtypeStruct((b, t, d), F32),
        compiler_params=pltpu.CompilerParams(
            dimension_semantics=("arbitrary", "arbitrary"), vmem_limit_bytes=VMEM_LIMIT),
        name="final_norm",
    )(h, g.reshape(1, d))


RW_CHAINS = 32
RW_KLO = LANES // RW_CHAINS
RW_KQ = RW_HEAD // RW_KLO
RW_VM = RW_HEAD // SUBLANES
RW_VGROUP = 4
RW_STEP_BLOCK = 16
OP_W, OP_B, OP_K, OP_R, OP_KK = range(5)


def _lane_group_sum(p):
    out = p
    for g in range(1, RW_KLO):
        out = out + pltpu.roll(p, g * RW_CHAINS, 1)
    return out


def _rwkv_kernel(opsf_ref, opsb_ref, vf_ref, vb_ref, s0_ref, yf_ref, yb_ref, sfin_ref,
                 s_ref, saf_ref, sab_ref, pb_ref):
    tb = opsf_ref.shape[0]
    dirs = ((opsf_ref, vf_ref, yf_ref), (opsb_ref, vb_ref, yb_ref))

    def row(ops_ref, t, j, q):
        lane0 = (j * RW_KQ + q) * LANES
        return ops_ref[t:t + 1, lane0:lane0 + LANES]

    @pl.when(pl.program_id(0) == 0)
    def _():
        s_ref[...] = s0_ref[...]

    zero = jnp.zeros((SUBLANES, LANES), F32)
    groups = [range(m0, m0 + RW_VGROUP) for m0 in range(0, RW_VM, RW_VGROUP)]

    for d, (ops_ref, _, _) in enumerate(dirs):
        first = 0 if d == 0 else tb - 1
        for ms in groups:
            acc = {m: zero for m in ms}
            for q in range(RW_KQ):
                kk = row(ops_ref, first, OP_KK, q)
                for m in ms:
                    acc[m] = acc[m] + s_ref[d, q, m] * kk
            for m in ms:
                if d == 0:
                    saf_ref[m] = _lane_group_sum(acc[m])
                else:
                    pb_ref[m] = acc[m]

    def advance(d, i, sa_ref, next_ref, reduce_next):
        ops_ref, v_ref, y_ref = dirs[d]
        t = i if d == 0 else tb - 1 - i
        tn = min(i + 1, tb - 1) if d == 0 else max(tb - 2 - i, 0)
        for ms in groups:
            sa = {m: sa_ref[m] for m in ms}
            vt = {m: v_ref[t, m] for m in ms}
            acc_y = {m: zero for m in ms}
            acc_s = {m: zero for m in ms}
            for q in range(RW_KQ):
                w = row(ops_ref, t, OP_W, q)
                b = row(ops_ref, t, OP_B, q)
                kt = row(ops_ref, t, OP_K, q)
                r = row(ops_ref, t, OP_R, q)
                kkn = row(ops_ref, tn, OP_KK, q)
                for m in ms:
                    s = s_ref[d, q, m] * w - sa[m] * b + vt[m] * kt
                    s_ref[d, q, m] = s
                    acc_y[m] = acc_y[m] + s * r
                    acc_s[m] = acc_s[m] + s * kkn
            for m in ms:
                y_ref[t, m] = acc_y[m]
                next_ref[m] = _lane_group_sum(acc_s[m]) if reduce_next else acc_s[m]

    for i in range(tb):
        for m in range(RW_VM):
            sab_ref[m] = _lane_group_sum(pb_ref[m])
        advance(0, i, saf_ref, saf_ref, True)
        advance(1, i, sab_ref, pb_ref, False)

    @pl.when(pl.program_id(0) == pl.num_programs(0) - 1)
    def _():
        sfin_ref[...] = s_ref[...]


RW_STATE_SHAPE = (2, RW_KQ, RW_VM, SUBLANES, LANES)


def _rwkv_scan(ops_f, ops_b, v_f, v_b, s0):
    t = ops_f.shape[0]
    tb = RW_STEP_BLOCK
    nblk = t // tb
    fwd = lambda i: (i, 0, 0, 0)
    bwd = lambda i: (nblk - 1 - i, 0, 0, 0)
    ops_block = (tb, 5 * RW_KQ * LANES)
    v_block = (tb, RW_VM, SUBLANES, LANES)
    y_shape = jax.ShapeDtypeStruct((t, RW_VM, SUBLANES, LANES), F32)
    state_spec = pl.BlockSpec(RW_STATE_SHAPE, lambda i: (0,) * len(RW_STATE_SHAPE))
    return pl.pallas_call(
        _rwkv_kernel,
        grid=(nblk,),
        in_specs=[pl.BlockSpec(ops_block, lambda i: (i, 0)), pl.BlockSpec(ops_block, lambda i: (nblk - 1 - i, 0)),
                  pl.BlockSpec(v_block, fwd), pl.BlockSpec(v_block, bwd), state_spec],
        out_specs=[pl.BlockSpec(v_block, fwd), pl.BlockSpec(v_block, bwd), state_spec],
        out_shape=[y_shape, y_shape, jax.ShapeDtypeStruct(RW_STATE_SHAPE, F32)],
        scratch_shapes=[pltpu.VMEM(RW_STATE_SHAPE, F32)]
        + [pltpu.VMEM((RW_VM, SUBLANES, LANES), F32)] * 3,
        compiler_params=pltpu.CompilerParams(
            dimension_semantics=("arbitrary",), vmem_limit_bytes=VMEM_LIMIT),
        name="rwkv_scan",
    )(ops_f, ops_b, v_f, v_b, s0)


RW_PREP_BLOCK = 128
RW_PREP_VMEM_LIMIT = 56 * 1024 * 1024
RW_NPIECE = 3


def _rw_channel_perm():
    return np.arange(GROUP).reshape(RW_HEADS, RW_HEAD).T.reshape(-1)


def _rw_relayout_matrices(nb):
    slab = LANES
    ch_per_slab = slab // RW_HEADS
    pk = np.zeros((nb, RW_NPIECE, slab, ch_per_slab // RW_KLO, RW_KLO, nb, RW_HEADS), np.float32)
    pv = np.zeros((nb, RW_NPIECE, slab, ch_per_slab, RW_KLO, nb, RW_HEADS), np.float32)
    for b in range(nb):
        for ch in range(ch_per_slab):
            for h in range(RW_HEADS):
                pk[b, :, ch * RW_HEADS + h, ch // RW_KLO, ch % RW_KLO, b, h] = 1.0
                pv[b, :, ch * RW_HEADS + h, ch, :, b, h] = 1.0
    rows = nb * RW_NPIECE * slab
    return (jnp.asarray(pk.reshape(rows, -1), BF16), jnp.asarray(pv.reshape(rows, -1), BF16))


def _split3(x):
    hi = x.astype(BF16)
    r1 = x - hi.astype(F32)
    mid = r1.astype(BF16)
    lo = (r1 - mid.astype(F32)).astype(BF16)
    return hi, mid, lo


def _rw_prep_kernel(rkv_ref, wa_ref, rkv_nb_ref, wa_nb_ref, mu_rkv_ref, mu_wa_ref, w0_ref, w2_ref,
                    a0_ref, a2_ref, kk_ref, ka_ref, rk_ref, hsum_ref, pk_ref, pv_ref,
                    ops_ref, v_ref, bonus_ref, *, backward):
    nb, tb, _ = rkv_ref.shape
    n = pl.program_id(0)
    edge = (n == pl.num_programs(0) - 1) if backward else (n == 0)
    nb_row = SUBLANES - 1 if not backward else 0
    row = lax.broadcasted_iota(jnp.int32, (tb, 1), 0)
    edge_row = (row == tb - 1) if backward else (row == 0)

    def shifted(x, neighbour):
        inner = pltpu.roll(x, tb - 1 if backward else 1, 0)
        outer = jnp.where(edge, 0.0, neighbour)
        return jnp.where(edge_row, outer, inner)

    def head_sum(z):
        hi, lo = _split_bf16(z)
        return _dot(hi, hsum_ref[...]) + _dot(lo, hsum_ref[...])

    feats = []
    for b in range(nb):
        x = rkv_ref[b]
        x = x + (shifted(x, rkv_nb_ref[b, nb_row:nb_row + 1, :]) - x) * mu_rkv_ref[...]
        xw = wa_ref[b]
        xw = xw + (shifted(xw, wa_nb_ref[b, nb_row:nb_row + 1, :]) - xw) * mu_wa_ref[...]
        r, k, v = x[:, :GROUP], x[:, GROUP:2 * GROUP], x[:, 2 * GROUP:]
        decay = jnp.exp(-RW_DECAY_SCALE * jax.nn.sigmoid(
            w0_ref[...] + _dot(jnp.tanh(xw).astype(BF16), w2_ref[...])))
        a = jax.nn.sigmoid(a0_ref[...] + _dot(xw.astype(BF16), a2_ref[...]))
        kq = k * kk_ref[...]
        kk = kq * lax.rsqrt(head_sum(kq * kq) + EPS)
        kt = k * (1.0 + (a - 1.0) * ka_ref[...])
        bonus_ref[b] = head_sum(r * kt * rk_ref[...]) * v
        feats.append([decay, kk * a, kt, r, kk, v])

    nslab = GROUP // LANES
    kw = pk_ref.shape[1]
    vw = pv_ref.shape[1]
    for j in range(6):
        pieces = [_split3(feats[b][j]) for b in range(nb)]
        lhs = jnp.concatenate(
            [jnp.concatenate([p[:, g * LANES:(g + 1) * LANES] for bp in pieces for p in bp], axis=1)
             for g in range(nslab)], axis=0)
        if j < 5:
            out = _dot(lhs, pk_ref[...])
            for g in range(nslab):
                ops_ref[:, (j * nslab + g) * kw:(j * nslab + g + 1) * kw] = out[g * tb:(g + 1) * tb]
        else:
            out = _dot(lhs, pv_ref[...])
            for g in range(nslab):
                v_ref[:, g * vw:(g + 1) * vw] = out[g * tb:(g + 1) * tb]


def _rw_prep(proj, lp, dirn):
    nb, t, _ = proj.shape
    tb = min(RW_PREP_BLOCK, t)
    assert t % tb == 0 and nb * RW_HEADS == RW_CHAINS
    nblk = t // tb
    backward = dirn == 1
    per8 = tb // SUBLANES
    last8 = t // SUBLANES - 1
    if backward:
        nb_idx = lambda n: jnp.minimum((n + 1) * per8, last8)
    else:
        nb_idx = lambda n: jnp.maximum(n * per8 - 1, 0)
    perm = _rw_channel_perm()
    lane_row = lambda v: v.reshape(1, -1)
    pad_rows = lambda w, lo: jnp.zeros((LANES, GROUP), F32).at[lo:lo + w.shape[0]].set(w)
    mu_rkv = lane_row(jnp.concatenate([lp['rw_mu_rkv'][dirn][i * GROUP:(i + 1) * GROUP][perm] for i in range(3)]))
    mu_wa = lane_row(jnp.pad(lp['rw_mu_wa'][dirn], (0, LANES - RW_DECAY_LR - RW_A_LR)))
    w0 = lane_row(lp['rw_w0'][dirn][perm])
    w2 = pad_rows(lp['rw_w2'][dirn][:, perm], 0).astype(BF16)
    a0 = lane_row(lp['rw_a0'][dirn][perm])
    a2 = pad_rows(lp['rw_a2'][dirn][:, perm], RW_DECAY_LR).astype(BF16)
    k_k = lane_row(lp['rw_kk'][perm])
    k_a = lane_row(lp['rw_ka'][perm])
    r_k = lane_row(lp['rw_rk'].reshape(-1)[perm])
    lane = np.arange(GROUP)
    hsum = jnp.asarray((lane[:, None] % RW_HEADS) == (lane[None, :] % RW_HEADS), BF16)
    pk, pv = _rw_relayout_matrices(nb)
    params = (mu_rkv, mu_wa, w0, w2, a0, a2, k_k, k_a, r_k, hsum, pk, pv)
    full = lambda a: pl.BlockSpec(a.shape, lambda n: (0,) * a.ndim)
    ops_w = 5 * RW_KQ * LANES
    v_w = RW_VM * SUBLANES * LANES
    ops, vt, bonus = pl.pallas_call(
        functools.partial(_rw_prep_kernel, backward=backward),
        grid=(nblk,),
        in_specs=[pl.BlockSpec((nb, tb, 3 * GROUP), lambda n: (0, n, _col_block('r_rkv'))),
                  pl.BlockSpec((nb, tb, LANES), lambda n: (0, n, _col_block('r_wa'))),
                  pl.BlockSpec((nb, SUBLANES, 3 * GROUP), lambda n: (0, nb_idx(n), _col_block('r_rkv'))),
                  pl.BlockSpec((nb, SUBLANES, LANES), lambda n: (0, nb_idx(n), _col_block('r_wa')))]
        + [full(p) for p in params],
        out_specs=[pl.BlockSpec((tb, ops_w), lambda n: (n, 0)),
                   pl.BlockSpec((tb, v_w), lambda n: (n, 0)),
                   pl.BlockSpec((nb, tb, GROUP), lambda n: (0, n, 0))],
        out_shape=[jax.ShapeDtypeStruct((t, ops_w), F32), jax.ShapeDtypeStruct((t, v_w), F32),
                   jax.ShapeDtypeStruct((nb, t, GROUP), F32)],
        compiler_params=pltpu.CompilerParams(
            dimension_semantics=("arbitrary",), vmem_limit_bytes=RW_PREP_VMEM_LIMIT),
        name="rw_prep",
    )(proj, proj, proj, proj, *params)
    return ops, vt.reshape(t, RW_VM, SUBLANES, LANES), bonus


def _l2norm(x):
    return x * lax.rsqrt(jnp.sum(x * x, axis=-1, keepdims=True) + EPS)


def _rw_gather_matrix(nb):
    vs = LANES // RW_HEADS
    q = np.zeros((RW_NPIECE, vs, RW_KLO, nb, RW_HEADS, nb, vs, RW_HEADS), np.float32)
    for b in range(nb):
        for v in range(vs):
            for h in range(RW_HEADS):
                q[:, v, :, b, h, b, v, h] = 1.0
    return jnp.asarray(q.reshape(RW_NPIECE * vs * LANES, nb * LANES), BF16)


def _rw_post_kernel(yf_ref, yb_ref, bf_ref, bb_ref, g_ref, gnw_ref, gnb_ref, g2_ref, hsum_ref, q_ref, o_ref):
    nb = bf_ref.shape[0]
    pieces = _split3(yf_ref[...] + yb_ref[...])
    slab = (LANES // RW_HEADS) * LANES
    nslab = GROUP // LANES
    token_major = [_dot(jnp.concatenate([p[:, g * slab:(g + 1) * slab] for p in pieces], axis=1), q_ref[...])
                   for g in range(nslab)]

    def head_sum(z):
        hi, lo = _split_bf16(z)
        return _dot(hi, hsum_ref[...]) + _dot(lo, hsum_ref[...])

    for b in range(nb):
        yy = jnp.concatenate([tm[:, b * LANES:(b + 1) * LANES] for tm in token_major], axis=1)
        d = yy - head_sum(yy) * (1.0 / RW_HEAD)
        var = head_sum(d * d) * (1.0 / RW_HEAD)
        y_rw = d * lax.rsqrt(var + RW_GN_EPS) * gnw_ref[...] + gnb_ref[...] + (bf_ref[b] + bb_ref[b])
        gate = _dot(jax.nn.sigmoid(g_ref[b]).astype(BF16), g2_ref[...])
        o_ref[b] = (y_rw * gate).astype(o_ref.dtype)


def _rw_post(y_f, y_b, bonus_f, bonus_b, proj, lp):
    nb, t, _ = bonus_f.shape
    tb = min(RW_PREP_BLOCK, t)
    perm = _rw_channel_perm()
    yw = RW_VM * SUBLANES * LANES
    gnw = lp['rw_gn_w'][perm].reshape(1, GROUP)
    gnb = lp['rw_gn_b'][perm].reshape(1, GROUP)
    g2 = jnp.zeros((LANES, GROUP), F32).at[:RW_GATE_LR].set(lp['rw_g2'][:, perm]).astype(BF16)
    lane = np.arange(GROUP)
    hsum = jnp.asarray((lane[:, None] % RW_HEADS) == (lane[None, :] % RW_HEADS), BF16)
    q = _rw_gather_matrix(nb)
    params = (gnw, gnb, g2, hsum, q)
    full = lambda a: pl.BlockSpec(a.shape, lambda n: (0,) * a.ndim)
    tok = pl.BlockSpec((nb, tb, GROUP), lambda n: (0, n, 0))
    return pl.pallas_call(
        _rw_post_kernel,
        grid=(t // tb,),
        in_specs=[pl.BlockSpec((tb, yw), lambda n: (n, 0)), pl.BlockSpec((tb, yw), lambda n: (n, 0)), tok, tok,
                  pl.BlockSpec((nb, tb, LANES), lambda n: (0, n, _col_block('r_g')))] + [full(p) for p in params],
        out_specs=tok,
        out_shape=jax.ShapeDtypeStruct((nb, t, GROUP), BF16),
        compiler_params=pltpu.CompilerParams(
            dimension_semantics=("arbitrary",), vmem_limit_bytes=RW_PREP_VMEM_LIMIT),
        name="rw_post",
    )(y_f.reshape(t, yw), y_b.reshape(t, yw), bonus_f, bonus_b, proj, *params)


def _rw_group(proj_c, proj_x, lp):
    s = jnp.zeros(RW_STATE_SHAPE, F32)
    outs = []
    for proj in (proj_c, proj_x):
        ops_f, v_f, bonus_f = _rw_prep(proj, lp, 0)
        ops_b, v_b, bonus_b = _rw_prep(proj, lp, 1)
        y_f, y_b, s = _rwkv_scan(ops_f, ops_b, v_f, v_b, s)
        outs.append(_rw_post(y_f, y_b, bonus_f, bonus_b, proj, lp))
    return outs


TOKEN_BLOCK = 256
HI = lax.Precision.HIGHEST


def _dot(a, b, precision=None):
    return jnp.dot(a, b, preferred_element_type=F32, precision=precision)


def _dot_nt(a, b):
    return lax.dot_general(a, b, (((1,), (1,)), ((), ())), preferred_element_type=F32)


def _dot_tn(a, b):
    return lax.dot_general(a, b, (((0,), (0,)), ((), ())), preferred_element_type=F32)


def _split_bf16(x):
    hi = x.astype(BF16)
    return hi, (x - hi.astype(F32)).astype(BF16)


def _dot3(a_hi, a_lo, b_hi, b_lo):
    return _dot(a_hi, b_hi) + (_dot(a_hi, b_lo) + _dot(a_lo, b_hi))


def _silu(x):
    return x * jax.nn.sigmoid(x)


def _chunk_masks(reverse):
    ri = lax.broadcasted_iota(jnp.int32, (CHUNK, CHUNK), 0)
    ci = lax.broadcasted_iota(jnp.int32, (CHUNK, CHUNK), 1)
    incl = (ci >= ri) if reverse else (ci <= ri)
    strict = (ci > ri) if reverse else (ci < ri)
    return incl, strict, (ri == ci).astype(F32)


def _conv3_block(x, cw, row_len):
    tb = x.shape[0]
    pos = lax.broadcasted_iota(jnp.int32, (tb, 1), 0) % row_len
    x_prev = jnp.where(pos == 0, 0.0, pltpu.roll(x, 1, 0))
    x_next = jnp.where(pos == row_len - 1, 0.0, pltpu.roll(x, tb - 1, 0))
    return x_prev * cw[0:1] + x * cw[1:2] + x_next * cw[2:3]


def _gdn_kernel(*refs, reverse, finalize, row_len, dirn):
    if finalize:
        (qkv_ref, ab_ref, z_ref, ob_ref, cw_ref, alog_ref, dtb_ref, g_ref, s0_ref,
         o_ref, sfin_ref, s_scr) = refs
    else:
        qkv_ref, ab_ref, cw_ref, alog_ref, dtb_ref, s0_ref, o_ref, sfin_ref, s_scr = refs
    n = pl.program_id(1)

    @pl.when(n == 0)
    def _():
        s_scr[...] = s0_ref[0]

    tb = qkv_ref.shape[1]
    qkv = _silu(_conv3_block(qkv_ref[0], cw_ref[...], row_len))
    ab = ab_ref[0]
    log_a = -jnp.exp(alog_ref[...]) * jax.nn.softplus(ab + dtb_ref[...])
    beta = jax.nn.sigmoid(ab)
    incl, strict, eye = _chunk_masks(reverse)
    tri = incl.astype(F32)
    last = 0 if reverse else CHUNK - 1
    hd = GDN_DK
    nchunk = tb // CHUNK
    order = list(range(nchunk - 1, -1, -1) if reverse else range(nchunk))
    rows = lambda c: slice(c * CHUNK, (c + 1) * CHUNK)
    cum_all = _dot(tri, jnp.concatenate([log_a[rows(c)] for c in range(nchunk)], axis=1), HI)

    items = [(c, h) for c in order for h in range(GDN_HEADS)]
    pre = {}
    for c, h in items:
        col = dirn * GDN_HEADS + h
        cum = cum_all[:, c * LANES + col:c * LANES + col + 1]
        bet = beta[rows(c), 2 * GDN_HEADS + col:2 * GDN_HEADS + col + 1]
        q = _l2norm(qkv[rows(c), h * hd:(h + 1) * hd]) * hd ** -0.5
        k = _l2norm(qkv[rows(c), GROUP + h * hd:GROUP + (h + 1) * hd])
        v = qkv[rows(c), 2 * GROUP + h * hd:2 * GROUP + (h + 1) * hd]
        cum_row = jnp.sum(eye * cum, axis=0, keepdims=True)
        decay = jnp.exp(jnp.where(incl, cum - cum_row, -jnp.inf))
        kb = k.astype(BF16)
        a = jnp.where(strict, bet * _dot_nt(kb, kb) * decay, 0.0)
        ecum = jnp.exp(cum)
        total = cum[last:last + 1]
        pre[c, h] = dict(
            x=-a, rhs=jnp.concatenate([v * bet, k * (bet * ecum)], axis=1),
            a_qk=(_dot_nt(q.astype(BF16), kb) * decay).astype(BF16),
            qe=(q * ecum).astype(BF16), k_end=(k * jnp.exp(total - cum)).astype(BF16),
            dec=jnp.exp(total))
    for p in pre.values():
        p['inv'] = eye + p['x']
    for level in range(6):
        for p in pre.values():
            x_hi, x_lo = _split_bf16(p['x'])
            if level == 0:
                p['x'] = _dot3(x_hi, x_lo, x_hi, x_lo)
            elif level < 5:
                i_hi, i_lo = _split_bf16(p['inv'])
                both = _dot3(jnp.concatenate([x_hi, i_hi], axis=0), jnp.concatenate([x_lo, i_lo], axis=0),
                             x_hi, x_lo)
                p['x'] = both[:CHUNK]
                p['inv'] = p['inv'] + both[CHUNK:]
            else:
                i_hi, i_lo = _split_bf16(p['inv'])
                p['inv'] = p['inv'] + _dot3(i_hi, i_lo, x_hi, x_lo)
    for p in pre.values():
        i_hi, i_lo = _split_bf16(p['inv'])
        r_hi, r_lo = _split_bf16(p['rhs'])
        sol = _dot3(i_hi, i_lo, r_hi, r_lo)
        p['u'], p['w'] = sol[:, :GDN_DV], sol[:, GDN_DV:].astype(BF16)

    for c, h in items:
        p = pre[c, h]
        s = s_scr[h]
        sb = s.astype(BF16)
        v_new = p['u'] - _dot(p['w'], sb)
        vb = v_new.astype(BF16)
        o = _dot(p['qe'], sb) + _dot(p['a_qk'], vb)
        s_scr[h] = p['dec'] * s + _dot_tn(p['k_end'], vb)
        cols = slice(h * GDN_DV, (h + 1) * GDN_DV)
        if finalize:
            o = o + ob_ref[0, rows(c), cols]
            o = o * lax.rsqrt(jnp.mean(o * o, axis=-1, keepdims=True) + EPS) * g_ref[...]
            o = o * _silu(z_ref[0, rows(c), cols])
        o_ref[0, rows(c), cols] = o.astype(o_ref.dtype)

    @pl.when(n == pl.num_programs(1) - 1)
    def _():
        sfin_ref[0] = s_scr[...]


def _gdn_pass(proj, o_other, s0, lp, *, dirn, row_len):
    b, t, _ = proj.shape
    tb = min(TOKEN_BLOCK, t)
    assert t % tb == 0 and tb % row_len == 0
    nblk = t // tb
    reverse = dirn == 1
    finalize = o_other is not None
    tok = (lambda i, n: (i, nblk - 1 - n)) if reverse else (lambda i, n: (i, n))
    seg = lambda name, wd: pl.BlockSpec((1, tb, wd), lambda i, n: tok(i, n) + (_col_block(name),))
    full = lambda a: pl.BlockSpec(a.shape, lambda i, n: (0,) * a.ndim)
    lane_row = lambda vals: jnp.pad(vals.reshape(1, -1), ((0, 0), (0, LANES - vals.size)))
    cw = lp['gdn_conv']
    alog = lane_row(lp['gdn_a_log'])
    dtb = lane_row(lp['gdn_dt_bias'])
    g = lp['gdn_norm_g'].reshape(1, GDN_DV)
    state_spec = pl.BlockSpec((1, GDN_HEADS, GDN_DK, GDN_DV), lambda i, n: (i, 0, 0, 0))
    if finalize:
        args = (proj, proj, proj, o_other, cw, alog, dtb, g, s0)
        in_specs = [seg('d_qkv', 3 * GROUP), seg('d_ab', LANES), seg('d_z', GROUP),
                    pl.BlockSpec((1, tb, GROUP), lambda i, n: tok(i, n) + (0,)),
                    full(cw), full(alog), full(dtb), full(g), state_spec]
    else:
        args = (proj, proj, cw, alog, dtb, s0)
        in_specs = [seg('d_qkv', 3 * GROUP), seg('d_ab', LANES), full(cw), full(alog), full(dtb), state_spec]
    return pl.pallas_call(
        functools.partial(_gdn_kernel, reverse=reverse, finalize=finalize, row_len=row_len, dirn=dirn),
        grid=(b, nblk),
        in_specs=in_specs,
        out_specs=[pl.BlockSpec((1, tb, GROUP), lambda i, n: tok(i, n) + (0,)), state_spec],
        out_shape=[jax.ShapeDtypeStruct((b, t, GROUP), BF16 if finalize else F32),
                   jax.ShapeDtypeStruct((b, GDN_HEADS, GDN_DK, GDN_DV), F32)],
        scratch_shapes=[pltpu.VMEM((GDN_HEADS, GDN_DK, GDN_DV), F32)],
        compiler_params=pltpu.CompilerParams(
            dimension_semantics=("arbitrary", "arbitrary"), vmem_limit_bytes=VMEM_LIMIT),
        name="gdn_fwd" if finalize else "gdn_bwd",
    )(*args)


def _gla_kernel(*refs, reverse, finalize):
    if finalize:
        (q_ref, k_ref, v_ref, lo_ref, r_ref, ob_ref, up_ref, ab_ref, g_ref, s0_ref,
         o_ref, sfin_ref, s_scr) = refs
    else:
        q_ref, k_ref, v_ref, lo_ref, up_ref, ab_ref, s0_ref, o_ref, sfin_ref, s_scr = refs
    n = pl.program_id(1)

    @pl.when(n == 0)
    def _():
        s_scr[...] = s0_ref[0]

    tb = q_ref.shape[1]
    gate = _dot(lo_ref[0].astype(BF16), up_ref[...].astype(BF16)) + ab_ref[...]
    log_f = jax.nn.log_sigmoid(gate) / GLA_TAU
    incl, _, eye = _chunk_masks(reverse)
    tri = incl.astype(F32)
    last = 0 if reverse else CHUNK - 1
    nchunk = tb // CHUNK
    order = range(nchunk - 1, -1, -1) if reverse else range(nchunk)
    for c in order:
        sl = slice(c * CHUNK, (c + 1) * CHUNK)
        cum = _dot(tri, log_f[sl], HI)
        total = cum[last:last + 1]
        q_dec = q_ref[0, sl, :] * GLA_DK ** -0.5 * jnp.exp(cum)
        k = k_ref[0, sl, :]
        k_inv = (k * jnp.exp(-cum)).astype(BF16)
        k_end = (k * jnp.exp(total - cum)).astype(BF16)
        q_dec = q_dec.astype(BF16)
        dec_row = jnp.exp(total)
        for h in range(GLA_HEADS):
            kc = slice(h * GLA_DK, (h + 1) * GLA_DK)
            vc = slice(h * GLA_DV, (h + 1) * GLA_DV)
            vb = v_ref[0, sl, vc].astype(BF16)
            att = jnp.where(incl, _dot_nt(q_dec[:, kc], k_inv[:, kc]), 0.0)
            s = s_scr[h]
            o = _dot(att.astype(BF16), vb) + _dot(q_dec[:, kc], s.astype(BF16))
            dec_col = jnp.sum(eye * dec_row[:, kc], axis=1, keepdims=True)
            s_scr[h] = dec_col * s + _dot_tn(k_end[:, kc], vb)
            if finalize:
                o = o + ob_ref[0, sl, vc]
                o = o * lax.rsqrt(jnp.mean(o * o, axis=-1, keepdims=True) + EPS) * g_ref[...]
                o = o * _silu(r_ref[0, sl, vc])
            o_ref[0, sl, vc] = o.astype(o_ref.dtype)

    @pl.when(n == pl.num_programs(1) - 1)
    def _():
        sfin_ref[0] = s_scr[...]


def _gla_pass(proj, o_other, s0, lp, *, dirn):
    b, t, _ = proj.shape
    tb = min(TOKEN_BLOCK, t)
    assert t % tb == 0
    nblk = t // tb
    reverse = dirn == 1
    finalize = o_other is not None
    tok = (lambda i, n: (i, nblk - 1 - n)) if reverse else (lambda i, n: (i, n))
    seg = lambda name: pl.BlockSpec((1, tb, SEG[name][2]), lambda i, n: tok(i, n) + (_col_block(name),))
    full = lambda a: pl.BlockSpec(a.shape, lambda i, n: (0,) * a.ndim)
    hk = GLA_HEADS * GLA_DK
    up = jnp.pad(lp['gla_a_up'][dirn], ((0, LANES - GLA_LR), (0, 0)))
    ab = lp['gla_a_b'][dirn].reshape(1, hk)
    g = lp['gla_norm_g'].reshape(1, GLA_DV)
    state_spec = pl.BlockSpec((1, GLA_HEADS, GLA_DK, GLA_DV), lambda i, n: (i, 0, 0, 0))
    if finalize:
        args = (proj, proj, proj, proj, proj, o_other, up, ab, g, s0)
        in_specs = [seg('g_q'), seg('g_k'), seg('g_v'), seg('g_lo'), seg('g_r'),
                    pl.BlockSpec((1, tb, GROUP), lambda i, n: tok(i, n) + (0,)),
                    full(up), full(ab), full(g), state_spec]
    else:
        args = (proj, proj, proj, proj, up, ab, s0)
        in_specs = [seg('g_q'), seg('g_k'), seg('g_v'), seg('g_lo'), full(up), full(ab), state_spec]
    return pl.pallas_call(
        functools.partial(_gla_kernel, reverse=reverse, finalize=finalize),
        grid=(b, nblk),
        in_specs=in_specs,
        out_specs=[pl.BlockSpec((1, tb, GROUP), lambda i, n: tok(i, n) + (0,)), state_spec],
        out_shape=[jax.ShapeDtypeStruct((b, t, GROUP), BF16 if finalize else F32),
                   jax.ShapeDtypeStruct((b, GLA_HEADS, GLA_DK, GLA_DV), F32)],
        scratch_shapes=[pltpu.VMEM((GLA_HEADS, GLA_DK, GLA_DV), F32)],
        compiler_params=pltpu.CompilerParams(
            dimension_semantics=("arbitrary", "arbitrary"), vmem_limit_bytes=VMEM_LIMIT),
        name="gla_fwd" if finalize else "gla_bwd",
    )(*args)


def _two_direction_group(pass_fn, state_shape, proj_c, proj_x, **kw):
    zero = jnp.zeros((proj_x[0].shape[0],) + state_shape, F32)
    ob_c, sb_c = pass_fn(proj_c[0], None, zero, dirn=1, **proj_c[1], **kw)
    ob_x, _ = pass_fn(proj_x[0], None, sb_c, dirn=1, **proj_x[1], **kw)
    y_c, sf_c = pass_fn(proj_c[0], ob_c, zero, dirn=0, **proj_c[1], **kw)
    y_x, _ = pass_fn(proj_x[0], ob_x, sf_c, dirn=0, **proj_x[1], **kw)
    return y_c, y_x


def _gla_group_pallas(proj_c, proj_x, lp):
    return _two_direction_group(_gla_pass, (GLA_HEADS, GLA_DK, GLA_DV), (proj_c, {}), (proj_x, {}), lp=lp)


def _gdn_group_pallas(proj_c, proj_x, lp):
    return _two_direction_group(_gdn_pass, (GDN_HEADS, GDN_DK, GDN_DV),
                                (proj_c, dict(row_len=proj_c.shape[1])), (proj_x, dict(row_len=GRID_W)), lp=lp)


def _sc_kernel(b_ref, c_ref, h_ref, cw_ref, o_ref, *, row_len):
    o_ref[0] = (b_ref[0] * _conv3_block(c_ref[0] * h_ref[0], cw_ref[...], row_len)).astype(o_ref.dtype)


def _sc_group(proj, row_len, lp):
    b, t, _ = proj.shape
    tb = min(TOKEN_BLOCK, t)
    assert t % tb == 0 and tb % row_len == 0
    seg = lambda name: pl.BlockSpec((1, tb, GROUP), lambda i, n: (i, n, _col_block(name)))
    cw = lp['sc_conv']
    return pl.pallas_call(
        functools.partial(_sc_kernel, row_len=row_len),
        grid=(b, t // tb),
        in_specs=[seg('c_b'), seg('c_c'), seg('c_h'), pl.BlockSpec(cw.shape, lambda i, n: (0, 0))],
        out_specs=pl.BlockSpec((1, tb, GROUP), lambda i, n: (i, n, 0)),
        out_shape=jax.ShapeDtypeStruct((b, t, GROUP), BF16),
        compiler_params=pltpu.CompilerParams(
            dimension_semantics=("arbitrary", "arbitrary"), vmem_limit_bytes=VMEM_LIMIT),
        name="short_conv",
    )(proj, proj, proj, cw)


def _mixers(proj_c, proj_x, lp):
    tc = proj_c.shape[1]
    y_gla_c, y_gla_x = _gla_group_pallas(proj_c, proj_x, lp)
    y_gdn_c, y_gdn_x = _gdn_group_pallas(proj_c, proj_x, lp)
    y_sc_c = _sc_group(proj_c, tc, lp)
    y_sc_x = _sc_group(proj_x, GRID_W, lp)
    y_rw_c, y_rw_x = _rw_group(proj_c, proj_x, lp)
    return (y_gla_c, y_gdn_c, y_sc_c, y_rw_c), (y_gla_x, y_gdn_x, y_sc_x, y_rw_x)


def kernel(x, c, ctx, c_ctx, ada_w, ada_b, norm1_g, norm2_g, w_in, w_out, gla_a_up, gla_a_b, gla_norm_g, gdn_conv, gdn_a_log, gdn_dt_bias, gdn_norm_g, sc_conv, rw_mu_rkv, rw_mu_wa, rw_w0, rw_w2, rw_a0, rw_a2, rw_g2, rw_kk, rw_ka, rw_rk, rw_gn_w, rw_gn_b, ffn_w_gu, ffn_w_down, final_g):
    nb, t, d = x.shape
    tc = ctx.shape[1]
    depth = w_in.shape[0]
    assert d == D_MODEL and t % 512 == 0 and tc % CHUNK == 0 and nb + 1 <= SUBLANES

    cond = jnp.concatenate([c, c_ctx[None, :], jnp.zeros((SUBLANES - nb - 1, d), F32)], axis=0)
    mod = _modulation(cond, ada_w, ada_b)

    tm_x = 512
    tm_c = tc
    h_x, h_c = x, ctx
    for l in range(depth):
        lp = dict(gla_a_up=gla_a_up[l], gla_a_b=gla_a_b[l],
                  gla_norm_g=gla_norm_g[l], gdn_conv=gdn_conv[l], gdn_a_log=gdn_a_log[l],
                  gdn_dt_bias=gdn_dt_bias[l], gdn_norm_g=gdn_norm_g[l], sc_conv=sc_conv[l],
                  rw_mu_rkv=rw_mu_rkv[l], rw_mu_wa=rw_mu_wa[l], rw_w0=rw_w0[l], rw_w2=rw_w2[l],
                  rw_a0=rw_a0[l], rw_a2=rw_a2[l], rw_g2=rw_g2[l], rw_kk=rw_kk[l], rw_ka=rw_ka[l],
                  rw_rk=rw_rk[l], rw_gn_w=rw_gn_w[l], rw_gn_b=rw_gn_b[l])
        m_x = [mod[l, :nb, i * d:(i + 1) * d][:, None, :] for i in range(6)]
        m_c = [jnp.broadcast_to(mod[l, nb, i * d:(i + 1) * d][None, None, :], (nb, 1, d)) for i in range(6)]
        w_in_l = _pad_w_in(w_in[l])
        w_out_l = jnp.concatenate([w_out[l][:3 * GROUP], w_out[l][3 * GROUP:][_rw_channel_perm()]],
                                  axis=0).astype(BF16)
        w_gu_l = ffn_w_gu[l].astype(BF16)
        w_dn_l = ffn_w_down[l].astype(BF16)

        proj_c = _norm_proj(h_c, norm1_g[l], m_c[1], m_c[0], w_in_l, tm_c, 1024)
        proj_x = _norm_proj(h_x, norm1_g[l], m_x[1], m_x[0], w_in_l, 2 * tm_x, 1024)
        y_c, y_x = _mixers(proj_c, proj_x, lp)

        h_x = _groups_residual(y_x, w_out_l, h_x, m_x[2], 2 * tm_x, 1024)
        a_x = _norm_swiglu(h_x, norm2_g[l], m_x[4], m_x[3], w_gu_l, 2 * tm_x, 512)
        h_x = _proj_residual(a_x, w_dn_l, h_x, m_x[5], 2 * tm_x, 512)
        if l < depth - 1:
            h_c = _groups_residual(y_c, w_out_l, h_c, m_c[2], tm_c, 512)
            a_c = _norm_swiglu(h_c, norm2_g[l], m_c[4], m_c[3], w_gu_l, tm_c, 512)
            h_c = _proj_residual(a_c, w_dn_l, h_c, m_c[5], tm_c, 512)
    return _final_norm(h_x, final_g, tm_x)
```

```python
import functools
import math

import numpy as np
import jax
import jax.numpy as jnp
from jax import lax
from jax.experimental import pallas as pl
from jax.experimental.pallas import tpu as pltpu

F32 = jnp.float32
BF16 = jnp.bfloat16

D_MODEL = 2048
GROUP = D_MODEL // 4
CHUNK = 64
EPS = 1e-6
GRID_W = 64

GLA_HEADS = 4
GLA_DK = 64
GLA_DV = 128
GLA_LR = 16
GLA_TAU = 16.0
GDN_HEADS = 4
GDN_DK = 128
GDN_DV = 128
RW_HEAD = 64
RW_HEADS = GROUP // RW_HEAD
RW_DECAY_LR = 32
RW_A_LR = 32
RW_GATE_LR = 96
RW_DECAY_SCALE = math.exp(-0.5)
RW_GN_EPS = 64e-5
FFN = -(-(8 * D_MODEL) // (3 * 256)) * 256

SUBLANES = 8
LANES = 128
VMEM_LIMIT = 56 * 1024 * 1024

_SRC_SEGS = (
    ('g_q', 256), ('g_k', 256), ('g_v', 512), ('g_r', 512), ('g_lo', GLA_LR),
    ('d_qkv', 1536), ('d_z', 512), ('d_ab', 16),
    ('c_b', 512), ('c_c', 512), ('c_h', 512),
    ('r_rkv', 1536), ('r_wa', 64), ('r_g', RW_GATE_LR),
)
_DST_ORDER = ('g_q', 'g_k', 'g_v', 'g_r', 'd_qkv', 'r_rkv', 'd_z', 'c_b', 'c_c', 'c_h',
              'g_lo', 'd_ab', 'r_wa', 'r_g')


def _round_up(n, m):
    return -(-n // m) * m


def _seg_layout():
    width = dict(_SRC_SEGS)
    src, o = {}, 0
    for name, wd in _SRC_SEGS:
        src[name] = o
        o += wd
    dst, o = {}, 0
    for name in _DST_ORDER:
        pw = _round_up(width[name], LANES)
        assert o % pw == 0, name
        dst[name] = (o, width[name], pw)
        o += pw
    return src, dst, o


SEG_SRC, SEG, PROJ_PAD = _seg_layout()


def _pad_w_in(w):
    parts = []
    for name in _DST_ORDER:
        _, wd, pw = SEG[name]
        seg = w[:, SEG_SRC[name]:SEG_SRC[name] + wd]
        if name == 'r_rkv':
            perm = _rw_channel_perm()
            seg = jnp.concatenate([seg[:, i * GROUP:(i + 1) * GROUP][:, perm] for i in range(3)], axis=1)
        if pw != wd:
            seg = jnp.pad(seg, ((0, 0), (0, pw - wd)))
        parts.append(seg)
    return jnp.concatenate(parts, axis=1).astype(BF16)


def _col_block(name):
    o, _, pw = SEG[name]
    return o // pw


def _mod_kernel(c_ref, w_ref, b_ref, o_ref):
    x = c_ref[...]
    x = (x * jax.nn.sigmoid(x)).astype(BF16)
    o_ref[0] = jnp.dot(x, w_ref[0].astype(BF16), preferred_element_type=F32) + b_ref[0]


def _modulation(cond, ada_w, ada_b):
    nl, d, n = ada_w.shape
    tn = 1024
    return pl.pallas_call(
        _mod_kernel,
        grid=(nl, n // tn),
        in_specs=[pl.BlockSpec((SUBLANES, d), lambda l, j: (0, 0)),
                  pl.BlockSpec((1, d, tn), lambda l, j: (l, 0, j)),
                  pl.BlockSpec((1, 1, tn), lambda l, j: (l, 0, j))],
        out_specs=pl.BlockSpec((1, SUBLANES, tn), lambda l, j: (l, 0, j)),
        out_shape=jax.ShapeDtypeStruct((nl, SUBLANES, n), F32),
        compiler_params=pltpu.CompilerParams(
            dimension_semantics=("arbitrary", "arbitrary"), vmem_limit_bytes=VMEM_LIMIT),
        name="modulation",
    )(cond, ada_w, ada_b.reshape(nl, 1, n))


def _modnorm_rows(h_ref, g_ref, sc_ref, sh_ref):
    x = h_ref[0]
    y = x * lax.rsqrt(jnp.mean(x * x, axis=-1, keepdims=True) + EPS) * g_ref[...]
    return (y * (1.0 + sc_ref[0]) + sh_ref[0]).astype(BF16)


def _norm_proj_kernel(h_ref, g_ref, sc_ref, sh_ref, w_ref, o_ref, a_scr):
    @pl.when(pl.program_id(2) == 0)
    def _():
        a_scr[...] = _modnorm_rows(h_ref, g_ref, sc_ref, sh_ref)

    o_ref[0] = jnp.dot(a_scr[...], w_ref[...], preferred_element_type=F32).astype(o_ref.dtype)


def _norm_proj(h, g, scale, shift, w, tm, tn):
    b, t, d = h.shape
    n = w.shape[1]
    return pl.pallas_call(
        _norm_proj_kernel,
        grid=(b, t // tm, n // tn),
        in_specs=[pl.BlockSpec((1, tm, d), lambda i, m, j: (i, m, 0)),
                  pl.BlockSpec((1, d), lambda i, m, j: (0, 0)),
                  pl.BlockSpec((1, 1, d), lambda i, m, j: (i, 0, 0)),
                  pl.BlockSpec((1, 1, d), lambda i, m, j: (i, 0, 0)),
                  pl.BlockSpec((d, tn), lambda i, m, j: (0, j))],
        out_specs=pl.BlockSpec((1, tm, tn), lambda i, m, j: (i, m, j)),
        out_shape=jax.ShapeDtypeStruct((b, t, n), F32),
        scratch_shapes=[pltpu.VMEM((tm, d), BF16)],
        compiler_params=pltpu.CompilerParams(
            dimension_semantics=("arbitrary", "arbitrary", "arbitrary"), vmem_limit_bytes=VMEM_LIMIT),
        name="norm_proj",
    )(h, g.reshape(1, d), scale, shift, w)


def _norm_swiglu_kernel(h_ref, g_ref, sc_ref, sh_ref, wg_ref, wu_ref, o_ref, a_scr):
    @pl.when(pl.program_id(2) == 0)
    def _():
        a_scr[...] = _modnorm_rows(h_ref, g_ref, sc_ref, sh_ref)

    a = a_scr[...]
    gate = jnp.dot(a, wg_ref[...], preferred_element_type=F32)
    up = jnp.dot(a, wu_ref[...], preferred_element_type=F32)
    o_ref[0] = (gate * jax.nn.sigmoid(gate) * up).astype(o_ref.dtype)


def _norm_swiglu(h, g, scale, shift, w_gu, tm, tn):
    b, t, d = h.shape
    f = w_gu.shape[1] // 2
    nj = f // tn
    return pl.pallas_call(
        _norm_swiglu_kernel,
        grid=(b, t // tm, nj),
        in_specs=[pl.BlockSpec((1, tm, d), lambda i, m, j: (i, m, 0)),
                  pl.BlockSpec((1, d), lambda i, m, j: (0, 0)),
                  pl.BlockSpec((1, 1, d), lambda i, m, j: (i, 0, 0)),
                  pl.BlockSpec((1, 1, d), lambda i, m, j: (i, 0, 0)),
                  pl.BlockSpec((d, tn), lambda i, m, j: (0, j)),
                  pl.BlockSpec((d, tn), lambda i, m, j: (0, j + nj))],
        out_specs=pl.BlockSpec((1, tm, tn), lambda i, m, j: (i, m, j)),
        out_shape=jax.ShapeDtypeStruct((b, t, f), BF16),
        scratch_shapes=[pltpu.VMEM((tm, d), BF16)],
        compiler_params=pltpu.CompilerParams(
            dimension_semantics=("arbitrary", "arbitrary", "arbitrary"), vmem_limit_bytes=VMEM_LIMIT),
        name="norm_swiglu",
    )(h, g.reshape(1, d), scale, shift, w_gu, w_gu)


def _proj_residual_kernel(a_ref, w_ref, h_ref, gate_ref, o_ref):
    o_ref[0] = h_ref[0] + gate_ref[0] * jnp.dot(a_ref[0], w_ref[...], preferred_element_type=F32)


def _proj_residual(a, w, h, gate, tm, tn):
    b, t, k = a.shape
    d = w.shape[1]
    return pl.pallas_call(
        _proj_residual_kernel,
        grid=(b, t // tm, d // tn),
        in_specs=[pl.BlockSpec((1, tm, k), lambda i, m, j: (i, m, 0)),
                  pl.BlockSpec((k, tn), lambda i, m, j: (0, j)),
                  pl.BlockSpec((1, tm, tn), lambda i, m, j: (i, m, j)),
                  pl.BlockSpec((1, 1, tn), lambda i, m, j: (i, 0, j))],
        out_specs=pl.BlockSpec((1, tm, tn), lambda i, m, j: (i, m, j)),
        out_shape=jax.ShapeDtypeStruct((b, t, d), F32),
        compiler_params=pltpu.CompilerParams(
            dimension_semantics=("arbitrary", "arbitrary", "arbitrary"), vmem_limit_bytes=VMEM_LIMIT),
        name="proj_residual",
    )(a, w, h, gate)


def _groups_residual_kernel(*refs):
    *a_refs, w_ref, h_ref, gate_ref, o_ref = refs
    k = a_refs[0].shape[2]
    acc = jnp.dot(a_refs[0][0], w_ref[0:k, :], preferred_element_type=F32)
    for i, a_ref in enumerate(a_refs[1:], start=1):
        acc = acc + jnp.dot(a_ref[0], w_ref[i * k:(i + 1) * k, :], preferred_element_type=F32)
    o_ref[0] = h_ref[0] + gate_ref[0] * acc


def _groups_residual(groups, w, h, gate, tm, tn):
    b, t, k = groups[0].shape
    d = w.shape[1]
    assert len(groups) * k == w.shape[0]
    return pl.pallas_call(
        _groups_residual_kernel,
        grid=(b, t // tm, d // tn),
        in_specs=[pl.BlockSpec((1, tm, k), lambda i, m, j: (i, m, 0)) for _ in groups]
        + [pl.BlockSpec((w.shape[0], tn), lambda i, m, j: (0, j)),
           pl.BlockSpec((1, tm, tn), lambda i, m, j: (i, m, j)),
           pl.BlockSpec((1, 1, tn), lambda i, m, j: (i, 0, j))],
        out_specs=pl.BlockSpec((1, tm, tn), lambda i, m, j: (i, m, j)),
        out_shape=jax.ShapeDtypeStruct((b, t, d), F32),
        compiler_params=pltpu.CompilerParams(
            dimension_semantics=("arbitrary", "arbitrary", "arbitrary"), vmem_limit_bytes=VMEM_LIMIT),
        name="groups_residual",
    )(*groups, w, h, gate)


def _final_norm_kernel(h_ref, g_ref, o_ref):
    x = h_ref[0]
    o_ref[0] = x * lax.rsqrt(jnp.mean(x * x, axis=-1, keepdims=True) + EPS) * g_ref[...]


def _final_norm(h, g, tm):
    b, t, d = h.shape
    return pl.pallas_call(
        _final_norm_kernel,
        grid=(b, t // tm),
        in_specs=[pl.BlockSpec((1, tm, d), lambda i, m: (i, m, 0)),
                  pl.BlockSpec((1, d), lambda i, m: (0, 0))],
        out_specs=pl.BlockSpec((1, tm, d), lambda i, m: (i, m, 0)),
        out_shape=jax.ShapeDtypeStruct((b, t, d), F32),
        compiler_params=pltpu.CompilerParams(
            dimension_semantics=("arbitrary", "arbitrary"), vmem_limit_bytes=VMEM_LIMIT),
        name="final_norm",
    )(h, g.reshape(1, d))


RW_CHAINS = 32
RW_KLO = LANES // RW_CHAINS
RW_KQ = RW_HEAD // RW_KLO
RW_VM = RW_HEAD // SUBLANES
RW_VGROUP = 4
RW_STEP_BLOCK = 16
OP_W, OP_B, OP_K, OP_R, OP_KK = range(5)


def _lane_group_sum(p):
    out = p
    for g in range(1, RW_KLO):
        out = out + pltpu.roll(p, g * RW_CHAINS, 1)
    return out


def _rwkv_kernel(opsf_ref, opsb_ref, vf_ref, vb_ref, s0_ref, yf_ref, yb_ref, sfin_ref,
                 s_ref, saf_ref, sab_ref, pb_ref):
    tb = opsf_ref.shape[0]
    dirs = ((opsf_ref, vf_ref, yf_ref), (opsb_ref, vb_ref, yb_ref))

    def row(ops_ref, t, j, q):
        lane0 = (j * RW_KQ + q) * LANES
        return ops_ref[t:t + 1, lane0:lane0 + LANES]

    @pl.when(pl.program_id(0) == 0)
    def _():
        s_ref[...] = s0_ref[...]

    zero = jnp.zeros((SUBLANES, LANES), F32)
    groups = [range(m0, m0 + RW_VGROUP) for m0 in range(0, RW_VM, RW_VGROUP)]

    for d, (ops_ref, _, _) in enumerate(dirs):
        first = 0 if d == 0 else tb - 1
        for ms in groups:
            acc = {m: zero for m in ms}
            for q in range(RW_KQ):
                kk = row(ops_ref, first, OP_KK, q)
                for m in ms:
                    acc[m] = acc[m] + s_ref[d, q, m] * kk
            for m in ms:
                if d == 0:
                    saf_ref[m] = _lane_group_sum(acc[m])
                else:
                    pb_ref[m] = acc[m]

    def advance(d, i, sa_ref, next_ref, reduce_next):
        ops_ref, v_ref, y_ref = dirs[d]
        t = i if d == 0 else tb - 1 - i
        tn = min(i + 1, tb - 1) if d == 0 else max(tb - 2 - i, 0)
        for ms in groups:
            sa = {m: sa_ref[m] for m in ms}
            vt = {m: v_ref[t, m] for m in ms}
            acc_y = {m: zero for m in ms}
            acc_s = {m: zero for m in ms}
            for q in range(RW_KQ):
                w = row(ops_ref, t, OP_W, q)
                b = row(ops_ref, t, OP_B, q)
                kt = row(ops_ref, t, OP_K, q)
                r = row(ops_ref, t, OP_R, q)
                kkn = row(ops_ref, tn, OP_KK, q)
                for m in ms:
                    s = s_ref[d, q, m] * w - sa[m] * b + vt[m] * kt
                    s_ref[d, q, m] = s
                    acc_y[m] = acc_y[m] + s * r
                    acc_s[m] = acc_s[m] + s * kkn
            for m in ms:
                y_ref[t, m] = acc_y[m]
                next_ref[m] = _lane_group_sum(acc_s[m]) if reduce_next else acc_s[m]

    for i in range(tb):
        for m in range(RW_VM):
            sab_ref[m] = _lane_group_sum(pb_ref[m])
        advance(0, i, saf_ref, saf_ref, True)
        advance(1, i, sab_ref, pb_ref, False)

    @pl.when(pl.program_id(0) == pl.num_programs(0) - 1)
    def _():
        sfin_ref[...] = s_ref[...]


RW_STATE_SHAPE = (2, RW_KQ, RW_VM, SUBLANES, LANES)


def _rwkv_scan(ops_f, ops_b, v_f, v_b, s0):
    t = ops_f.shape[0]
    tb = RW_STEP_BLOCK
    nblk = t // tb
    fwd = lambda i: (i, 0, 0, 0)
    bwd = lambda i: (nblk - 1 - i, 0, 0, 0)
    ops_block = (tb, 5 * RW_KQ * LANES)
    v_block = (tb, RW_VM, SUBLANES, LANES)
    y_shape = jax.ShapeDtypeStruct((t, RW_VM, SUBLANES, LANES), F32)
    state_spec = pl.BlockSpec(RW_STATE_SHAPE, lambda i: (0,) * len(RW_STATE_SHAPE))
    return pl.pallas_call(
        _rwkv_kernel,
        grid=(nblk,),
        in_specs=[pl.BlockSpec(ops_block, lambda i: (i, 0)), pl.BlockSpec(ops_block, lambda i: (nblk - 1 - i, 0)),
                  pl.BlockSpec(v_block, fwd), pl.BlockSpec(v_block, bwd), state_spec],
        out_specs=[pl.BlockSpec(v_block, fwd), pl.BlockSpec(v_block, bwd), state_spec],
        out_shape=[y_shape, y_shape, jax.ShapeDtypeStruct(RW_STATE_SHAPE, F32)],
        scratch_shapes=[pltpu.VMEM(RW_STATE_SHAPE, F32)]
        + [pltpu.VMEM((RW_VM, SUBLANES, LANES), F32)] * 3,
        compiler_params=pltpu.CompilerParams(
            dimension_semantics=("arbitrary",), vmem_limit_bytes=VMEM_LIMIT),
        name="rwkv_scan",
    )(ops_f, ops_b, v_f, v_b, s0)


RW_PREP_BLOCK = 128
RW_PREP_VMEM_LIMIT = 56 * 1024 * 1024
RW_NPIECE = 3


def _rw_channel_perm():
    return np.arange(GROUP).reshape(RW_HEADS, RW_HEAD).T.reshape(-1)


def _rw_relayout_matrices(nb):
    slab = LANES
    ch_per_slab = slab // RW_HEADS
    pk = np.zeros((nb, RW_NPIECE, slab, ch_per_slab // RW_KLO, RW_KLO, nb, RW_HEADS), np.float32)
    pv = np.zeros((nb, RW_NPIECE, slab, ch_per_slab, RW_KLO, nb, RW_HEADS), np.float32)
    for b in range(nb):
        for ch in range(ch_per_slab):
            for h in range(RW_HEADS):
                pk[b, :, ch * RW_HEADS + h, ch // RW_KLO, ch % RW_KLO, b, h] = 1.0
                pv[b, :, ch * RW_HEADS + h, ch, :, b, h] = 1.0
    rows = nb * RW_NPIECE * slab
    return (jnp.asarray(pk.reshape(rows, -1), BF16), jnp.asarray(pv.reshape(rows, -1), BF16))


def _split3(x):
    hi = x.astype(BF16)
    r1 = x - hi.astype(F32)
    mid = r1.astype(BF16)
    lo = (r1 - mid.astype(F32)).astype(BF16)
    return hi, mid, lo


def _rw_prep_kernel(rkv_ref, wa_ref, rkv_nb_ref, wa_nb_ref, mu_rkv_ref, mu_wa_ref, w0_ref, w2_ref,
                    a0_ref, a2_ref, kk_ref, ka_ref, rk_ref, hsum_ref, pk_ref, pv_ref,
                    ops_ref, v_ref, bonus_ref, *, backward):
    nb, tb, _ = rkv_ref.shape
    n = pl.program_id(0)
    edge = (n == pl.num_programs(0) - 1) if backward else (n == 0)
    nb_row = SUBLANES - 1 if not backward else 0
    row = lax.broadcasted_iota(jnp.int32, (tb, 1), 0)
    edge_row = (row == tb - 1) if backward else (row == 0)

    def shifted(x, neighbour):
        inner = pltpu.roll(x, tb - 1 if backward else 1, 0)
        outer = jnp.where(edge, 0.0, neighbour)
        return jnp.where(edge_row, outer, inner)

    def head_sum(z):
        hi, lo = _split_bf16(z)
        return _dot(hi, hsum_ref[...]) + _dot(lo, hsum_ref[...])

    feats = []
    for b in range(nb):
        x = rkv_ref[b]
        x = x + (shifted(x, rkv_nb_ref[b, nb_row:nb_row + 1, :]) - x) * mu_rkv_ref[...]
        xw = wa_ref[b]
        xw = xw + (shifted(xw, wa_nb_ref[b, nb_row:nb_row + 1, :]) - xw) * mu_wa_ref[...]
        r, k, v = x[:, :GROUP], x[:, GROUP:2 * GROUP], x[:, 2 * GROUP:]
        decay = jnp.exp(-RW_DECAY_SCALE * jax.nn.sigmoid(
            w0_ref[...] + _dot(jnp.tanh(xw).astype(BF16), w2_ref[...])))
        a = jax.nn.sigmoid(a0_ref[...] + _dot(xw.astype(BF16), a2_ref[...]))
        kq = k * kk_ref[...]
        kk = kq * lax.rsqrt(head_sum(kq * kq) + EPS)
        kt = k * (1.0 + (a - 1.0) * ka_ref[...])
        bonus_ref[b] = head_sum(r * kt * rk_ref[...]) * v
        feats.append([decay, kk * a, kt, r, kk, v])

    nslab = GROUP // LANES
    kw = pk_ref.shape[1]
    vw = pv_ref.shape[1]
    for j in range(6):
        pieces = [_split3(feats[b][j]) for b in range(nb)]
        lhs = jnp.concatenate(
            [jnp.concatenate([p[:, g * LANES:(g + 1) * LANES] for bp in pieces for p in bp], axis=1)
             for g in range(nslab)], axis=0)
        if j < 5:
            out = _dot(lhs, pk_ref[...])
            for g in range(nslab):
                ops_ref[:, (j * nslab + g) * kw:(j * nslab + g + 1) * kw] = out[g * tb:(g + 1) * tb]
        else:
            out = _dot(lhs, pv_ref[...])
            for g in range(nslab):
                rows = vw // LANES
                v_ref[:, g * rows:(g + 1) * rows, :] = out[g * tb:(g + 1) * tb].reshape(tb, rows, LANES)


def _rw_prep(proj, lp, dirn):
    nb, t, _ = proj.shape
    tb = min(RW_PREP_BLOCK, t)
    assert t % tb == 0 and nb * RW_HEADS == RW_CHAINS
    nblk = t // tb
    backward = dirn == 1
    per8 = tb // SUBLANES
    last8 = t // SUBLANES - 1
    if backward:
        nb_idx = lambda n: jnp.minimum((n + 1) * per8, last8)
    else:
        nb_idx = lambda n: jnp.maximum(n * per8 - 1, 0)
    perm = _rw_channel_perm()
    lane_row = lambda v: v.reshape(1, -1)
    pad_rows = lambda w, lo: jnp.zeros((LANES, GROUP), F32).at[lo:lo + w.shape[0]].set(w)
    mu_rkv = lane_row(jnp.concatenate([lp['rw_mu_rkv'][dirn][i * GROUP:(i + 1) * GROUP][perm] for i in range(3)]))
    mu_wa = lane_row(jnp.pad(lp['rw_mu_wa'][dirn], (0, LANES - RW_DECAY_LR - RW_A_LR)))
    w0 = lane_row(lp['rw_w0'][dirn][perm])
    w2 = pad_rows(lp['rw_w2'][dirn][:, perm], 0).astype(BF16)
    a0 = lane_row(lp['rw_a0'][dirn][perm])
    a2 = pad_rows(lp['rw_a2'][dirn][:, perm], RW_DECAY_LR).astype(BF16)
    k_k = lane_row(lp['rw_kk'][perm])
    k_a = lane_row(lp['rw_ka'][perm])
    r_k = lane_row(lp['rw_rk'].reshape(-1)[perm])
    lane = np.arange(GROUP)
    hsum = jnp.asarray((lane[:, None] % RW_HEADS) == (lane[None, :] % RW_HEADS), BF16)
    pk, pv = _rw_relayout_matrices(nb)
    params = (mu_rkv, mu_wa, w0, w2, a0, a2, k_k, k_a, r_k, hsum, pk, pv)
    full = lambda a: pl.BlockSpec(a.shape, lambda n: (0,) * a.ndim)
    ops_w = 5 * RW_KQ * LANES
    v_w = RW_VM * SUBLANES * LANES
    ops, vt, bonus = pl.pallas_call(
        functools.partial(_rw_prep_kernel, backward=backward),
        grid=(nblk,),
        in_specs=[pl.BlockSpec((nb, tb, 3 * GROUP), lambda n: (0, n, _col_block('r_rkv'))),
                  pl.BlockSpec((nb, tb, LANES), lambda n: (0, n, _col_block('r_wa'))),
                  pl.BlockSpec((nb, SUBLANES, 3 * GROUP), lambda n: (0, nb_idx(n), _col_block('r_rkv'))),
                  pl.BlockSpec((nb, SUBLANES, LANES), lambda n: (0, nb_idx(n), _col_block('r_wa')))]
        + [full(p) for p in params],
        out_specs=[pl.BlockSpec((tb, ops_w), lambda n: (n, 0)),
                   pl.BlockSpec((tb, RW_HEAD, LANES), lambda n: (n, 0, 0)),
                   pl.BlockSpec((nb, tb, GROUP), lambda n: (0, n, 0))],
        out_shape=[jax.ShapeDtypeStruct((t, ops_w), F32), jax.ShapeDtypeStruct((t, RW_HEAD, LANES), F32),
                   jax.ShapeDtypeStruct((nb, t, GROUP), F32)],
        compiler_params=pltpu.CompilerParams(
            dimension_semantics=("arbitrary",), vmem_limit_bytes=RW_PREP_VMEM_LIMIT),
        name="rw_prep",
    )(proj, proj, proj, proj, *params)
    return ops, vt.reshape(t, RW_VM, SUBLANES, LANES), bonus


def _l2norm(x):
    return x * lax.rsqrt(jnp.sum(x * x, axis=-1, keepdims=True) + EPS)


def _rw_gather_matrix(nb):
    vs = LANES // RW_HEADS
    q = np.zeros((RW_NPIECE, vs, RW_KLO, nb, RW_HEADS, nb, vs, RW_HEADS), np.float32)
    for b in range(nb):
        for v in range(vs):
            for h in range(RW_HEADS):
                q[:, v, :, b, h, b, v, h] = 1.0
    return jnp.asarray(q.reshape(RW_NPIECE * vs * LANES, nb * LANES), BF16)


def _rw_post_kernel(yf_ref, yb_ref, bf_ref, bb_ref, g_ref, gnw_ref, gnb_ref, g2_ref, hsum_ref, q_ref, o_ref):
    nb = bf_ref.shape[0]
    tb = yf_ref.shape[0]
    pieces = _split3((yf_ref[...] + yb_ref[...]).reshape(tb, RW_HEAD * LANES))
    slab = (LANES // RW_HEADS) * LANES
    nslab = GROUP // LANES
    token_major = [_dot(jnp.concatenate([p[:, g * slab:(g + 1) * slab] for p in pieces], axis=1), q_ref[...])
                   for g in range(nslab)]

    def head_sum(z):
        hi, lo = _split_bf16(z)
        return _dot(hi, hsum_ref[...]) + _dot(lo, hsum_ref[...])

    for b in range(nb):
        yy = jnp.concatenate([tm[:, b * LANES:(b + 1) * LANES] for tm in token_major], axis=1)
        d = yy - head_sum(yy) * (1.0 / RW_HEAD)
        var = head_sum(d * d) * (1.0 / RW_HEAD)
        y_rw = d * lax.rsqrt(var + RW_GN_EPS) * gnw_ref[...] + gnb_ref[...] + (bf_ref[b] + bb_ref[b])
        gate = _dot(jax.nn.sigmoid(g_ref[b]).astype(BF16), g2_ref[...])
        o_ref[b] = (y_rw * gate).astype(o_ref.dtype)


def _rw_post(y_f, y_b, bonus_f, bonus_b, proj, lp):
    nb, t, _ = bonus_f.shape
    tb = min(RW_PREP_BLOCK, t)
    perm = _rw_channel_perm()
    yw = RW_VM * SUBLANES * LANES
    gnw = lp['rw_gn_w'][perm].reshape(1, GROUP)
    gnb = lp['rw_gn_b'][perm].reshape(1, GROUP)
    g2 = jnp.zeros((LANES, GROUP), F32).at[:RW_GATE_LR].set(lp['rw_g2'][:, perm]).astype(BF16)
    lane = np.arange(GROUP)
    hsum = jnp.asarray((lane[:, None] % RW_HEADS) == (lane[None, :] % RW_HEADS), BF16)
    q = _rw_gather_matrix(nb)
    params = (gnw, gnb, g2, hsum, q)
    full = lambda a: pl.BlockSpec(a.shape, lambda n: (0,) * a.ndim)
    tok = pl.BlockSpec((nb, tb, GROUP), lambda n: (0, n, 0))
    return pl.pallas_call(
        _rw_post_kernel,
        grid=(t // tb,),
        in_specs=[pl.BlockSpec((tb, RW_HEAD, LANES), lambda n: (n, 0, 0)),
                  pl.BlockSpec((tb, RW_HEAD, LANES), lambda n: (n, 0, 0)), tok, tok,
                  pl.BlockSpec((nb, tb, LANES), lambda n: (0, n, _col_block('r_g')))] + [full(p) for p in params],
        out_specs=tok,
        out_shape=jax.ShapeDtypeStruct((nb, t, GROUP), BF16),
        compiler_params=pltpu.CompilerParams(
            dimension_semantics=("arbitrary",), vmem_limit_bytes=RW_PREP_VMEM_LIMIT),
        name="rw_post",
    )(y_f.reshape(t, RW_HEAD, LANES), y_b.reshape(t, RW_HEAD, LANES), bonus_f, bonus_b, proj, *params)


def _rw_group(proj_c, proj_x, lp):
    s = jnp.zeros(RW_STATE_SHAPE, F32)
    outs = []
    for proj in (proj_c, proj_x):
        ops_f, v_f, bonus_f = _rw_prep(proj, lp, 0)
        ops_b, v_b, bonus_b = _rw_prep(proj, lp, 1)
        y_f, y_b, s = _rwkv_scan(ops_f, ops_b, v_f, v_b, s)
        outs.append(_rw_post(y_f, y_b, bonus_f, bonus_b, proj, lp))
    return outs


TOKEN_BLOCK = 256
HI = lax.Precision.HIGHEST


def _dot(a, b, precision=None):
    return jnp.dot(a, b, preferred_element_type=F32, precision=precision)


def _dot_nt(a, b):
    return lax.dot_general(a, b, (((1,), (1,)), ((), ())), preferred_element_type=F32)


def _dot_tn(a, b):
    return lax.dot_general(a, b, (((0,), (0,)), ((), ())), preferred_element_type=F32)


def _split_bf16(x):
    hi = x.astype(BF16)
    return hi, (x - hi.astype(F32)).astype(BF16)


def _dot3(a_hi, a_lo, b_hi, b_lo):
    return _dot(a_hi, b_hi) + (_dot(a_hi, b_lo) + _dot(a_lo, b_hi))


def _silu(x):
    return x * jax.nn.sigmoid(x)


def _chunk_masks(reverse):
    ri = lax.broadcasted_iota(jnp.int32, (CHUNK, CHUNK), 0)
    ci = lax.broadcasted_iota(jnp.int32, (CHUNK, CHUNK), 1)
    incl = (ci >= ri) if reverse else (ci <= ri)
    strict = (ci > ri) if reverse else (ci < ri)
    return incl, strict, (ri == ci).astype(F32)


def _conv3_block(x, cw, row_len):
    tb = x.shape[0]
    pos = lax.broadcasted_iota(jnp.int32, (tb, 1), 0) % row_len
    x_prev = jnp.where(pos == 0, 0.0, pltpu.roll(x, 1, 0))
    x_next = jnp.where(pos == row_len - 1, 0.0, pltpu.roll(x, tb - 1, 0))
    return x_prev * cw[0:1] + x * cw[1:2] + x_next * cw[2:3]


def _gdn_kernel(*refs, reverse, finalize, row_len, dirn):
    if finalize:
        (qkv_ref, ab_ref, z_ref, ob_ref, cw_ref, alog_ref, dtb_ref, g_ref, s0_ref,
         o_ref, sfin_ref, s_scr) = refs
    else:
        qkv_ref, ab_ref, cw_ref, alog_ref, dtb_ref, s0_ref, o_ref, sfin_ref, s_scr = refs
    n = pl.program_id(1)

    @pl.when(n == 0)
    def _():
        s_scr[...] = s0_ref[0]

    tb = qkv_ref.shape[1]
    qkv = _silu(_conv3_block(qkv_ref[0], cw_ref[...], row_len))
    ab = ab_ref[0]
    log_a = -jnp.exp(alog_ref[...]) * jax.nn.softplus(ab + dtb_ref[...])
    beta = jax.nn.sigmoid(ab)
    incl, strict, eye = _chunk_masks(reverse)
    tri = incl.astype(F32)
    last = 0 if reverse else CHUNK - 1
    hd = GDN_DK
    nchunk = tb // CHUNK
    order = list(range(nchunk - 1, -1, -1) if reverse else range(nchunk))
    rows = lambda c: slice(c * CHUNK, (c + 1) * CHUNK)
    cum_all = _dot(tri, jnp.concatenate([log_a[rows(c)] for c in range(nchunk)], axis=1), HI)

    items = [(c, h) for c in order for h in range(GDN_HEADS)]
    pre = {}
    for c, h in items:
        col = dirn * GDN_HEADS + h
        cum = cum_all[:, c * LANES + col:c * LANES + col + 1]
        bet = beta[rows(c), 2 * GDN_HEADS + col:2 * GDN_HEADS + col + 1]
        q = _l2norm(qkv[rows(c), h * hd:(h + 1) * hd]) * hd ** -0.5
        k = _l2norm(qkv[rows(c), GROUP + h * hd:GROUP + (h + 1) * hd])
        v = qkv[rows(c), 2 * GROUP + h * hd:2 * GROUP + (h + 1) * hd]
        cum_row = jnp.sum(eye * cum, axis=0, keepdims=True)
        decay = jnp.exp(jnp.where(incl, cum - cum_row, -jnp.inf))
        kb = k.astype(BF16)
        a = jnp.where(strict, bet * _dot_nt(kb, kb) * decay, 0.0)
        ecum = jnp.exp(cum)
        total = cum[last:last + 1]
        pre[c, h] = dict(
            x=-a, rhs=jnp.concatenate([v * bet, k * (bet * ecum)], axis=1),
            a_qk=(_dot_nt(q.astype(BF16), kb) * decay).astype(BF16),
            qe=(q * ecum).astype(BF16), k_end=(k * jnp.exp(total - cum)).astype(BF16),
            dec=jnp.exp(total))
    for p in pre.values():
        p['inv'] = eye + p['x']
    for level in range(6):
        for p in pre.values():
            x_hi, x_lo = _split_bf16(p['x'])
            if level == 0:
                p['x'] = _dot3(x_hi, x_lo, x_hi, x_lo)
            elif level < 5:
                i_hi, i_lo = _split_bf16(p['inv'])
                both = _dot3(jnp.concatenate([x_hi, i_hi], axis=0), jnp.concatenate([x_lo, i_lo], axis=0),
                             x_hi, x_lo)
                p['x'] = both[:CHUNK]
                p['inv'] = p['inv'] + both[CHUNK:]
            else:
                i_hi, i_lo = _split_bf16(p['inv'])
                p['inv'] = p['inv'] + _dot3(i_hi, i_lo, x_hi, x_lo)
    for p in pre.values():
        i_hi, i_lo = _split_bf16(p['inv'])
        r_hi, r_lo = _split_bf16(p['rhs'])
        sol = _dot3(i_hi, i_lo, r_hi, r_lo)
        p['u'], p['w'] = sol[:, :GDN_DV], sol[:, GDN_DV:].astype(BF16)

    for c, h in items:
        p = pre[c, h]
        s = s_scr[h]
        sb = s.astype(BF16)
        v_new = p['u'] - _dot(p['w'], sb)
        vb = v_new.astype(BF16)
        o = _dot(p['qe'], sb) + _dot(p['a_qk'], vb)
        s_scr[h] = p['dec'] * s + _dot_tn(p['k_end'], vb)
        cols = slice(h * GDN_DV, (h + 1) * GDN_DV)
        if finalize:
            o = o + ob_ref[0, rows(c), cols]
            o = o * lax.rsqrt(jnp.mean(o * o, axis=-1, keepdims=True) + EPS) * g_ref[...]
            o = o * _silu(z_ref[0, rows(c), cols])
        o_ref[0, rows(c), cols] = o.astype(o_ref.dtype)

    @pl.when(n == pl.num_programs(1) - 1)
    def _():
        sfin_ref[0] = s_scr[...]


def _gdn_pass(proj, o_other, s0, lp, *, dirn, row_len):
    b, t, _ = proj.shape
    tb = min(TOKEN_BLOCK, t)
    assert t % tb == 0 and tb % row_len == 0
    nblk = t // tb
    reverse = dirn == 1
    finalize = o_other is not None
    tok = (lambda i, n: (i, nblk - 1 - n)) if reverse else (lambda i, n: (i, n))
    seg = lambda name, wd: pl.BlockSpec((1, tb, wd), lambda i, n: tok(i, n) + (_col_block(name),))
    full = lambda a: pl.BlockSpec(a.shape, lambda i, n: (0,) * a.ndim)
    lane_row = lambda vals: jnp.pad(vals.reshape(1, -1), ((0, 0), (0, LANES - vals.size)))
    cw = lp['gdn_conv']
    alog = lane_row(lp['gdn_a_log'])
    dtb = lane_row(lp['gdn_dt_bias'])
    g = lp['gdn_norm_g'].reshape(1, GDN_DV)
    state_spec = pl.BlockSpec((1, GDN_HEADS, GDN_DK, GDN_DV), lambda i, n: (i, 0, 0, 0))
    if finalize:
        args = (proj, proj, proj, o_other, cw, alog, dtb, g, s0)
        in_specs = [seg('d_qkv', 3 * GROUP), seg('d_ab', LANES), seg('d_z', GROUP),
                    pl.BlockSpec((1, tb, GROUP), lambda i, n: tok(i, n) + (0,)),
                    full(cw), full(alog), full(dtb), full(g), state_spec]
    else:
        args = (proj, proj, cw, alog, dtb, s0)
        in_specs = [seg('d_qkv', 3 * GROUP), seg('d_ab', LANES), full(cw), full(alog), full(dtb), state_spec]
    return pl.pallas_call(
        functools.partial(_gdn_kernel, reverse=reverse, finalize=finalize, row_len=row_len, dirn=dirn),
        grid=(b, nblk),
        in_specs=in_specs,
        out_specs=[pl.BlockSpec((1, tb, GROUP), lambda i, n: tok(i, n) + (0,)), state_spec],
        out_shape=[jax.ShapeDtypeStruct((b, t, GROUP), BF16 if finalize else F32),
                   jax.ShapeDtypeStruct((b, GDN_HEADS, GDN_DK, GDN_DV), F32)],
        scratch_shapes=[pltpu.VMEM((GDN_HEADS, GDN_DK, GDN_DV), F32)],
        compiler_params=pltpu.CompilerParams(
            dimension_semantics=("arbitrary", "arbitrary"), vmem_limit_bytes=VMEM_LIMIT),
        name="gdn_fwd" if finalize else "gdn_bwd",
    )(*args)


def _gla_kernel(*refs, reverse, finalize):
    if finalize:
        (q_ref, k_ref, v_ref, lo_ref, r_ref, ob_ref, up_ref, ab_ref, g_ref, s0_ref,
         o_ref, sfin_ref, s_scr) = refs
    else:
        q_ref, k_ref, v_ref, lo_ref, up_ref, ab_ref, s0_ref, o_ref, sfin_ref, s_scr = refs
    n = pl.program_id(1)

    @pl.when(n == 0)
    def _():
        s_scr[...] = s0_ref[0]

    tb = q_ref.shape[1]
    gate = _dot(lo_ref[0].astype(BF16), up_ref[...].astype(BF16)) + ab_ref[...]
    log_f = jax.nn.log_sigmoid(gate) / GLA_TAU
    incl, _, eye = _chunk_masks(reverse)
    tri = incl.astype(F32)
    last = 0 if reverse else CHUNK - 1
    nchunk = tb // CHUNK
    order = range(nchunk - 1, -1, -1) if reverse else range(nchunk)
    for c in order:
        sl = slice(c * CHUNK, (c + 1) * CHUNK)
        cum = _dot(tri, log_f[sl], HI)
        total = cum[last:last + 1]
        q_dec = q_ref[0, sl, :] * GLA_DK ** -0.5 * jnp.exp(cum)
        k = k_ref[0, sl, :]
        k_inv = (k * jnp.exp(-cum)).astype(BF16)
        k_end = (k * jnp.exp(total - cum)).astype(BF16)
        q_dec = q_dec.astype(BF16)
        dec_row = jnp.exp(total)
        for h in range(GLA_HEADS):
            kc = slice(h * GLA_DK, (h + 1) * GLA_DK)
            vc = slice(h * GLA_DV, (h + 1) * GLA_DV)
            vb = v_ref[0, sl, vc].astype(BF16)
            att = jnp.where(incl, _dot_nt(q_dec[:, kc], k_inv[:, kc]), 0.0)
            s = s_scr[h]
            o = _dot(att.astype(BF16), vb) + _dot(q_dec[:, kc], s.astype(BF16))
            dec_col = jnp.sum(eye * dec_row[:, kc], axis=1, keepdims=True)
            s_scr[h] = dec_col * s + _dot_tn(k_end[:, kc], vb)
            if finalize:
                o = o + ob_ref[0, sl, vc]
                o = o * lax.rsqrt(jnp.mean(o * o, axis=-1, keepdims=True) + EPS) * g_ref[...]
                o = o * _silu(r_ref[0, sl, vc])
            o_ref[0, sl, vc] = o.astype(o_ref.dtype)

    @pl.when(n == pl.num_programs(1) - 1)
    def _():
        sfin_ref[0] = s_scr[...]


def _gla_pass(proj, o_other, s0, lp, *, dirn):
    b, t, _ = proj.shape
    tb = min(TOKEN_BLOCK, t)
    assert t % tb == 0
    nblk = t // tb
    reverse = dirn == 1
    finalize = o_other is not None
    tok = (lambda i, n: (i, nblk - 1 - n)) if reverse else (lambda i, n: (i, n))
    seg = lambda name: pl.BlockSpec((1, tb, SEG[name][2]), lambda i, n: tok(i, n) + (_col_block(name),))
    full = lambda a: pl.BlockSpec(a.shape, lambda i, n: (0,) * a.ndim)
    hk = GLA_HEADS * GLA_DK
    up = jnp.pad(lp['gla_a_up'][dirn], ((0, LANES - GLA_LR), (0, 0)))
    ab = lp['gla_a_b'][dirn].reshape(1, hk)
    g = lp['gla_norm_g'].reshape(1, GLA_DV)
    state_spec = pl.BlockSpec((1, GLA_HEADS, GLA_DK, GLA_DV), lambda i, n: (i, 0, 0, 0))
    if finalize:
        args = (proj, proj, proj, proj, proj, o_other, up, ab, g, s0)
        in_specs = [seg('g_q'), seg('g_k'), seg('g_v'), seg('g_lo'), seg('g_r'),
                    pl.BlockSpec((1, tb, GROUP), lambda i, n: tok(i, n) + (0,)),
                    full(up), full(ab), full(g), state_spec]
    else:
        args = (proj, proj, proj, proj, up, ab, s0)
        in_specs = [seg('g_q'), seg('g_k'), seg('g_v'), seg('g_lo'), full(up), full(ab), state_spec]
    return pl.pallas_call(
        functools.partial(_gla_kernel, reverse=reverse, finalize=finalize),
        grid=(b, nblk),
        in_specs=in_specs,
        out_specs=[pl.BlockSpec((1, tb, GROUP), lambda i, n: tok(i, n) + (0,)), state_spec],
        out_shape=[jax.ShapeDtypeStruct((b, t, GROUP), BF16 if finalize else F32),
                   jax.ShapeDtypeStruct((b, GLA_HEADS, GLA_DK, GLA_DV), F32)],
        scratch_shapes=[pltpu.VMEM((GLA_HEADS, GLA_DK, GLA_DV), F32)],
        compiler_params=pltpu.CompilerParams(
            dimension_semantics=("arbitrary", "arbitrary"), vmem_limit_bytes=VMEM_LIMIT),
        name="gla_fwd" if finalize else "gla_bwd",
    )(*args)


def _two_direction_group(pass_fn, state_shape, proj_c, proj_x, **kw):
    zero = jnp.zeros((proj_x[0].shape[0],) + state_shape, F32)
    ob_c, sb_c = pass_fn(proj_c[0], None, zero, dirn=1, **proj_c[1], **kw)
    ob_x, _ = pass_fn(proj_x[0], None, sb_c, dirn=1, **proj_x[1], **kw)
    y_c, sf_c = pass_fn(proj_c[0], ob_c, zero, dirn=0, **proj_c[1], **kw)
    y_x, _ = pass_fn(proj_x[0], ob_x, sf_c, dirn=0, **proj_x[1], **kw)
    return y_c, y_x


def _gla_group_pallas(proj_c, proj_x, lp):
    return _two_direction_group(_gla_pass, (GLA_HEADS, GLA_DK, GLA_DV), (proj_c, {}), (proj_x, {}), lp=lp)


def _gdn_group_pallas(proj_c, proj_x, lp):
    return _two_direction_group(_gdn_pass, (GDN_HEADS, GDN_DK, GDN_DV),
                                (proj_c, dict(row_len=proj_c.shape[1])), (proj_x, dict(row_len=GRID_W)), lp=lp)


def _sc_kernel(b_ref, c_ref, h_ref, cw_ref, o_ref, *, row_len):
    o_ref[0] = (b_ref[0] * _conv3_block(c_ref[0] * h_ref[0], cw_ref[...], row_len)).astype(o_ref.dtype)


def _sc_group(proj, row_len, lp):
    b, t, _ = proj.shape
    tb = min(TOKEN_BLOCK, t)
    assert t % tb == 0 and tb % row_len == 0
    seg = lambda name: pl.BlockSpec((1, tb, GROUP), lambda i, n: (i, n, _col_block(name)))
    cw = lp['sc_conv']
    return pl.pallas_call(
        functools.partial(_sc_kernel, row_len=row_len),
        grid=(b, t // tb),
        in_specs=[seg('c_b'), seg('c_c'), seg('c_h'), pl.BlockSpec(cw.shape, lambda i, n: (0, 0))],
        out_specs=pl.BlockSpec((1, tb, GROUP), lambda i, n: (i, n, 0)),
        out_shape=jax.ShapeDtypeStruct((b, t, GROUP), BF16),
        compiler_params=pltpu.CompilerParams(
            dimension_semantics=("arbitrary", "arbitrary"), vmem_limit_bytes=VMEM_LIMIT),
        name="short_conv",
    )(proj, proj, proj, cw)


def _mixers(proj_c, proj_x, lp):
    tc = proj_c.shape[1]
    y_gla_c, y_gla_x = _gla_group_pallas(proj_c, proj_x, lp)
    y_gdn_c, y_gdn_x = _gdn_group_pallas(proj_c, proj_x, lp)
    y_sc_c = _sc_group(proj_c, tc, lp)
    y_sc_x = _sc_group(proj_x, GRID_W, lp)
    y_rw_c, y_rw_x = _rw_group(proj_c, proj_x, lp)
    return (y_gla_c, y_gdn_c, y_sc_c, y_rw_c), (y_gla_x, y_gdn_x, y_sc_x, y_rw_x)


def kernel(x, c, ctx, c_ctx, ada_w, ada_b, norm1_g, norm2_g, w_in, w_out, gla_a_up, gla_a_b, gla_norm_g, gdn_conv, gdn_a_log, gdn_dt_bias, gdn_norm_g, sc_conv, rw_mu_rkv, rw_mu_wa, rw_w0, rw_w2, rw_a0, rw_a2, rw_g2, rw_kk, rw_ka, rw_rk, rw_gn_w, rw_gn_b, ffn_w_gu, ffn_w_down, final_g):
    nb, t, d = x.shape
    tc = ctx.shape[1]
    depth = w_in.shape[0]
    assert d == D_MODEL and t % 512 == 0 and tc % CHUNK == 0 and nb + 1 <= SUBLANES

    cond = jnp.concatenate([c, c_ctx[None, :], jnp.zeros((SUBLANES - nb - 1, d), F32)], axis=0)
    mod = _modulation(cond, ada_w, ada_b)

    tm_x = 512
    tm_c = tc
    h_x, h_c = x, ctx
    for l in range(depth):
        lp = dict(gla_a_up=gla_a_up[l], gla_a_b=gla_a_b[l],
                  gla_norm_g=gla_norm_g[l], gdn_conv=gdn_conv[l], gdn_a_log=gdn_a_log[l],
                  gdn_dt_bias=gdn_dt_bias[l], gdn_norm_g=gdn_norm_g[l], sc_conv=sc_conv[l],
                  rw_mu_rkv=rw_mu_rkv[l], rw_mu_wa=rw_mu_wa[l], rw_w0=rw_w0[l], rw_w2=rw_w2[l],
                  rw_a0=rw_a0[l], rw_a2=rw_a2[l], rw_g2=rw_g2[l], rw_kk=rw_kk[l], rw_ka=rw_ka[l],
                  rw_rk=rw_rk[l], rw_gn_w=rw_gn_w[l], rw_gn_b=rw_gn_b[l])
        m_x = [mod[l, :nb, i * d:(i + 1) * d][:, None, :] for i in range(6)]
        m_c = [jnp.broadcast_to(mod[l, nb, i * d:(i + 1) * d][None, None, :], (nb, 1, d)) for i in range(6)]
        w_in_l = _pad_w_in(w_in[l])
        w_out_l = jnp.concatenate([w_out[l][:3 * GROUP], w_out[l][3 * GROUP:][_rw_channel_perm()]],
                                  axis=0).astype(BF16)
        w_gu_l = ffn_w_gu[l].astype(BF16)
        w_dn_l = ffn_w_down[l].astype(BF16)

        proj_c = _norm_proj(h_c, norm1_g[l], m_c[1], m_c[0], w_in_l, tm_c, 1024)
        proj_x = _norm_proj(h_x, norm1_g[l], m_x[1], m_x[0], w_in_l, 2 * tm_x, 1024)
        y_c, y_x = _mixers(proj_c, proj_x, lp)

        h_x = _groups_residual(y_x, w_out_l, h_x, m_x[2], 2 * tm_x, 1024)
        a_x = _norm_swiglu(h_x, norm2_g[l], m_x[4], m_x[3], w_gu_l, 2 * tm_x, 512)
        h_x = _proj_residual(a_x, w_dn_l, h_x, m_x[5], 2 * tm_x, 512)
        if l < depth - 1:
            h_c = _groups_residual(y_c, w_out_l, h_c, m_c[2], tm_c, 512)
            a_c = _norm_swiglu(h_c, norm2_g[l], m_c[4], m_c[3], w_gu_l, tm_c, 512)
            h_c = _proj_residual(a_c, w_dn_l, h_c, m_c[5], tm_c, 512)
    return _final_norm(h_x, final_g, tm_x)
```

```python
import functools
import math

import numpy as np
import jax
import jax.numpy as jnp
from jax import lax
from jax.experimental import pallas as pl
from jax.experimental.pallas import tpu as pltpu

F32 = jnp.float32
BF16 = jnp.bfloat16

D_MODEL = 2048
GROUP = D_MODEL // 4
CHUNK = 64
EPS = 1e-6
GRID_W = 64

GLA_HEADS = 4
GLA_DK = 64
GLA_DV = 128
GLA_LR = 16
GLA_TAU = 16.0
GDN_HEADS = 4
GDN_DK = 128
GDN_DV = 128
RW_HEAD = 64
RW_HEADS = GROUP // RW_HEAD
RW_DECAY_LR = 32
RW_A_LR = 32
RW_GATE_LR = 96
RW_DECAY_SCALE = math.exp(-0.5)
RW_GN_EPS = 64e-5
FFN = -(-(8 * D_MODEL) // (3 * 256)) * 256

SUBLANES = 8
LANES = 128
VMEM_LIMIT = 56 * 1024 * 1024

_SRC_SEGS = (
    ('g_q', 256), ('g_k', 256), ('g_v', 512), ('g_r', 512), ('g_lo', GLA_LR),
    ('d_qkv', 1536), ('d_z', 512), ('d_ab', 16),
    ('c_b', 512), ('c_c', 512), ('c_h', 512),
    ('r_rkv', 1536), ('r_wa', 64), ('r_g', RW_GATE_LR),
)
_DST_ORDER = ('g_q', 'g_k', 'g_v', 'g_r', 'd_qkv', 'r_rkv', 'd_z', 'c_b', 'c_c', 'c_h',
              'g_lo', 'd_ab', 'r_wa', 'r_g')


def _round_up(n, m):
    return -(-n // m) * m


def _seg_layout():
    width = dict(_SRC_SEGS)
    src, o = {}, 0
    for name, wd in _SRC_SEGS:
        src[name] = o
        o += wd
    dst, o = {}, 0
    for name in _DST_ORDER:
        pw = _round_up(width[name], LANES)
        assert o % pw == 0, name
        dst[name] = (o, width[name], pw)
        o += pw
    return src, dst, o


SEG_SRC, SEG, PROJ_PAD = _seg_layout()


def _pad_w_in(w):
    parts = []
    for name in _DST_ORDER:
        _, wd, pw = SEG[name]
        seg = w[:, SEG_SRC[name]:SEG_SRC[name] + wd]
        if name == 'r_rkv':
            perm = _rw_channel_perm()
            seg = jnp.concatenate([seg[:, i * GROUP:(i + 1) * GROUP][:, perm] for i in range(3)], axis=1)
        if pw != wd:
            seg = jnp.pad(seg, ((0, 0), (0, pw - wd)))
        parts.append(seg)
    return jnp.concatenate(parts, axis=1).astype(BF16)


def _col_block(name):
    o, _, pw = SEG[name]
    return o // pw


def _mod_kernel(c_ref, w_ref, b_ref, o_ref):
    x = c_ref[...]
    x = (x * jax.nn.sigmoid(x)).astype(BF16)
    o_ref[0] = jnp.dot(x, w_ref[0].astype(BF16), preferred_element_type=F32) + b_ref[0]


def _modulation(cond, ada_w, ada_b):
    nl, d, n = ada_w.shape
    tn = 1024
    return pl.pallas_call(
        _mod_kernel,
        grid=(nl, n // tn),
        in_specs=[pl.BlockSpec((SUBLANES, d), lambda l, j: (0, 0)),
                  pl.BlockSpec((1, d, tn), lambda l, j: (l, 0, j)),
                  pl.BlockSpec((1, 1, tn), lambda l, j: (l, 0, j))],
        out_specs=pl.BlockSpec((1, SUBLANES, tn), lambda l, j: (l, 0, j)),
        out_shape=jax.ShapeDtypeStruct((nl, SUBLANES, n), F32),
        compiler_params=pltpu.CompilerParams(
            dimension_semantics=("arbitrary", "arbitrary"), vmem_limit_bytes=VMEM_LIMIT),
        name="modulation",
    )(cond, ada_w, ada_b.reshape(nl, 1, n))


def _modnorm_rows(h_ref, g_ref, sc_ref, sh_ref):
    x = h_ref[0]
    y = x * lax.rsqrt(jnp.mean(x * x, axis=-1, keepdims=True) + EPS) * g_ref[...]
    return (y * (1.0 + sc_ref[0]) + sh_ref[0]).astype(BF16)


def _norm_proj_kernel(h_ref, g_ref, sc_ref, sh_ref, w_ref, o_ref, a_scr):
    @pl.when(pl.program_id(2) == 0)
    def _():
        a_scr[...] = _modnorm_rows(h_ref, g_ref, sc_ref, sh_ref)

    o_ref[0] = jnp.dot(a_scr[...], w_ref[...], preferred_element_type=F32).astype(o_ref.dtype)


def _norm_proj(h, g, scale, shift, w, tm, tn):
    b, t, d = h.shape
    n = w.shape[1]
    return pl.pallas_call(
        _norm_proj_kernel,
        grid=(b, t // tm, n // tn),
        in_specs=[pl.BlockSpec((1, tm, d), lambda i, m, j: (i, m, 0)),
                  pl.BlockSpec((1, d), lambda i, m, j: (0, 0)),
                  pl.BlockSpec((1, 1, d), lambda i, m, j: (i, 0, 0)),
                  pl.BlockSpec((1, 1, d), lambda i, m, j: (i, 0, 0)),
                  pl.BlockSpec((d, tn), lambda i, m, j: (0, j))],
        out_specs=pl.BlockSpec((1, tm, tn), lambda i, m, j: (i, m, j)),
        out_shape=jax.ShapeDtypeStruct((b, t, n), F32),
        scratch_shapes=[pltpu.VMEM((tm, d), BF16)],
        compiler_params=pltpu.CompilerParams(
            dimension_semantics=("arbitrary", "arbitrary", "arbitrary"), vmem_limit_bytes=VMEM_LIMIT),
        name="norm_proj",
    )(h, g.reshape(1, d), scale, shift, w)


def _norm_swiglu_kernel(h_ref, g_ref, sc_ref, sh_ref, wg_ref, wu_ref, o_ref, a_scr):
    @pl.when(pl.program_id(2) == 0)
    def _():
        a_scr[...] = _modnorm_rows(h_ref, g_ref, sc_ref, sh_ref)

    a = a_scr[...]
    gate = jnp.dot(a, wg_ref[...], preferred_element_type=F32)
    up = jnp.dot(a, wu_ref[...], preferred_element_type=F32)
    o_ref[0] = (gate * jax.nn.sigmoid(gate) * up).astype(o_ref.dtype)


def _norm_swiglu(h, g, scale, shift, w_gu, tm, tn):
    b, t, d = h.shape
    f = w_gu.shape[1] // 2
    nj = f // tn
    return pl.pallas_call(
        _norm_swiglu_kernel,
        grid=(b, t // tm, nj),
        in_specs=[pl.BlockSpec((1, tm, d), lambda i, m, j: (i, m, 0)),
                  pl.BlockSpec((1, d), lambda i, m, j: (0, 0)),
                  pl.BlockSpec((1, 1, d), lambda i, m, j: (i, 0, 0)),
                  pl.BlockSpec((1, 1, d), lambda i, m, j: (i, 0, 0)),
                  pl.BlockSpec((d, tn), lambda i, m, j: (0, j)),
                  pl.BlockSpec((d, tn), lambda i, m, j: (0, j + nj))],
        out_specs=pl.BlockSpec((1, tm, tn), lambda i, m, j: (i, m, j)),
        out_shape=jax.ShapeDtypeStruct((b, t, f), BF16),
        scratch_shapes=[pltpu.VMEM((tm, d), BF16)],
        compiler_params=pltpu.CompilerParams(
            dimension_semantics=("arbitrary", "arbitrary", "arbitrary"), vmem_limit_bytes=VMEM_LIMIT),
        name="norm_swiglu",
    )(h, g.reshape(1, d), scale, shift, w_gu, w_gu)


def _proj_residual_kernel(a_ref, w_ref, h_ref, gate_ref, o_ref):
    o_ref[0] = h_ref[0] + gate_ref[0] * jnp.dot(a_ref[0], w_ref[...], preferred_element_type=F32)


def _proj_residual(a, w, h, gate, tm, tn):
    b, t, k = a.shape
    d = w.shape[1]
    return pl.pallas_call(
        _proj_residual_kernel,
        grid=(b, t // tm, d // tn),
        in_specs=[pl.BlockSpec((1, tm, k), lambda i, m, j: (i, m, 0)),
                  pl.BlockSpec((k, tn), lambda i, m, j: (0, j)),
                  pl.BlockSpec((1, tm, tn), lambda i, m, j: (i, m, j)),
                  pl.BlockSpec((1, 1, tn), lambda i, m, j: (i, 0, j))],
        out_specs=pl.BlockSpec((1, tm, tn), lambda i, m, j: (i, m, j)),
        out_shape=jax.ShapeDtypeStruct((b, t, d), F32),
        compiler_params=pltpu.CompilerParams(
            dimension_semantics=("arbitrary", "arbitrary", "arbitrary"), vmem_limit_bytes=VMEM_LIMIT),
        name="proj_residual",
    )(a, w, h, gate)


def _groups_residual_kernel(*refs):
    *a_refs, w_ref, h_ref, gate_ref, o_ref = refs
    k = a_refs[0].shape[2]
    acc = jnp.dot(a_refs[0][0], w_ref[0:k, :], preferred_element_type=F32)
    for i, a_ref in enumerate(a_refs[1:], start=1):
        acc = acc + jnp.dot(a_ref[0], w_ref[i * k:(i + 1) * k, :], preferred_element_type=F32)
    o_ref[0] = h_ref[0] + gate_ref[0] * acc


def _groups_residual(groups, w, h, gate, tm, tn):
    b, t, k = groups[0].shape
    d = w.shape[1]
    assert len(groups) * k == w.shape[0]
    return pl.pallas_call(
        _groups_residual_kernel,
        grid=(b, t // tm, d // tn),
        in_specs=[pl.BlockSpec((1, tm, k), lambda i, m, j: (i, m, 0)) for _ in groups]
        + [pl.BlockSpec((w.shape[0], tn), lambda i, m, j: (0, j)),
           pl.BlockSpec((1, tm, tn), lambda i, m, j: (i, m, j)),
           pl.BlockSpec((1, 1, tn), lambda i, m, j: (i, 0, j))],
        out_specs=pl.BlockSpec((1, tm, tn), lambda i, m, j: (i, m, j)),
        out_shape=jax.ShapeDtypeStruct((b, t, d), F32),
        compiler_params=pltpu.CompilerParams(
            dimension_semantics=("arbitrary", "arbitrary", "arbitrary"), vmem_limit_bytes=VMEM_LIMIT),
        name="groups_residual",
    )(*groups, w, h, gate)


def _final_norm_kernel(h_ref, g_ref, o_ref):
    x = h_ref[0]
    o_ref[0] = x * lax.rsqrt(jnp.mean(x * x, axis=-1, keepdims=True) + EPS) * g_ref[...]


def _final_norm(h, g, tm):
    b, t, d = h.shape
    return pl.pallas_call(
        _final_norm_kernel,
        grid=(b, t // tm),
        in_specs=[pl.BlockSpec((1, tm, d), lambda i, m: (i, m, 0)),
                  pl.BlockSpec((1, d), lambda i, m: (0, 0))],
        out_specs=pl.BlockSpec((1, tm, d), lambda i, m: (i, m, 0)),
        out_shape=jax.ShapeDtypeStruct((b, t, d), F32),
        compiler_params=pltpu.CompilerParams(
            dimension_semantics=("arbitrary", "arbitrary"), vmem_limit_bytes=VMEM_LIMIT),
        name="final_norm",
    )(h, g.reshape(1, d))


RW_CHAINS = 32
RW_KLO = LANES // RW_CHAINS
RW_KQ = RW_HEAD // RW_KLO
RW_VM = RW_HEAD // SUBLANES
RW_VGROUP = 4
RW_STEP_BLOCK = 16
OP_W, OP_B, OP_K, OP_R, OP_KK = range(5)


def _lane_group_sum(p):
    out = p
    for g in range(1, RW_KLO):
        out = out + pltpu.roll(p, g * RW_CHAINS, 1)
    return out


def _rwkv_kernel(opsf_ref, opsb_ref, vf_ref, vb_ref, s0_ref, yf_ref, yb_ref, sfin_ref,
                 s_ref, saf_ref, sab_ref, pb_ref):
    tb = opsf_ref.shape[0]
    dirs = ((opsf_ref, vf_ref, yf_ref), (opsb_ref, vb_ref, yb_ref))

    def row(ops_ref, t, j, q):
        lane0 = (j * RW_KQ + q) * LANES
        return ops_ref[t:t + 1, lane0:lane0 + LANES]

    @pl.when(pl.program_id(0) == 0)
    def _():
        s_ref[...] = s0_ref[...]

    zero = jnp.zeros((SUBLANES, LANES), F32)
    groups = [range(m0, m0 + RW_VGROUP) for m0 in range(0, RW_VM, RW_VGROUP)]

    for d, (ops_ref, _, _) in enumerate(dirs):
        first = 0 if d == 0 else tb - 1
        for ms in groups:
            acc = {m: zero for m in ms}
            for q in range(RW_KQ):
                kk = row(ops_ref, first, OP_KK, q)
                for m in ms:
                    acc[m] = acc[m] + s_ref[d, q, m] * kk
            for m in ms:
                if d == 0:
                    saf_ref[m] = _lane_group_sum(acc[m])
                else:
                    pb_ref[m] = acc[m]

    def advance(d, i, sa_ref, next_ref, reduce_next):
        ops_ref, v_ref, y_ref = dirs[d]
        t = i if d == 0 else tb - 1 - i
        tn = min(i + 1, tb - 1) if d == 0 else max(tb - 2 - i, 0)
        for ms in groups:
            sa = {m: sa_ref[m] for m in ms}
            vt = {m: v_ref[t, m] for m in ms}
            acc_y = {m: zero for m in ms}
            acc_s = {m: zero for m in ms}
            for q in range(RW_KQ):
                w = row(ops_ref, t, OP_W, q)
                b = row(ops_ref, t, OP_B, q)
                kt = row(ops_ref, t, OP_K, q)
                r = row(ops_ref, t, OP_R, q)
                kkn = row(ops_ref, tn, OP_KK, q)
                for m in ms:
                    s = s_ref[d, q, m] * w - sa[m] * b + vt[m] * kt
                    s_ref[d, q, m] = s
                    acc_y[m] = acc_y[m] + s * r
                    acc_s[m] = acc_s[m] + s * kkn
            for m in ms:
                y_ref[t, m] = acc_y[m]
                next_ref[m] = _lane_group_sum(acc_s[m]) if reduce_next else acc_s[m]

    for i in range(tb):
        for m in range(RW_VM):
            sab_ref[m] = _lane_group_sum(pb_ref[m])
        advance(0, i, saf_ref, saf_ref, True)
        advance(1, i, sab_ref, pb_ref, False)

    @pl.when(pl.program_id(0) == pl.num_programs(0) - 1)
    def _():
        sfin_ref[...] = s_ref[...]


RW_STATE_SHAPE = (2, RW_KQ, RW_VM, SUBLANES, LANES)


def _rwkv_scan(ops_f, ops_b, v_f, v_b, s0):
    t = ops_f.shape[0]
    tb = RW_STEP_BLOCK
    nblk = t // tb
    fwd = lambda i: (i, 0, 0, 0)
    bwd = lambda i: (nblk - 1 - i, 0, 0, 0)
    ops_block = (tb, 5 * RW_KQ * LANES)
    v_block = (tb, RW_VM, SUBLANES, LANES)
    y_shape = jax.ShapeDtypeStruct((t, RW_VM, SUBLANES, LANES), F32)
    state_spec = pl.BlockSpec(RW_STATE_SHAPE, lambda i: (0,) * len(RW_STATE_SHAPE))
    return pl.pallas_call(
        _rwkv_kernel,
        grid=(nblk,),
        in_specs=[pl.BlockSpec(ops_block, lambda i: (i, 0)), pl.BlockSpec(ops_block, lambda i: (nblk - 1 - i, 0)),
                  pl.BlockSpec(v_block, fwd), pl.BlockSpec(v_block, bwd), state_spec],
        out_specs=[pl.BlockSpec(v_block, fwd), pl.BlockSpec(v_block, bwd), state_spec],
        out_shape=[y_shape, y_shape, jax.ShapeDtypeStruct(RW_STATE_SHAPE, F32)],
        scratch_shapes=[pltpu.VMEM(RW_STATE_SHAPE, F32)]
        + [pltpu.VMEM((RW_VM, SUBLANES, LANES), F32)] * 3,
        compiler_params=pltpu.CompilerParams(
            dimension_semantics=("arbitrary",), vmem_limit_bytes=VMEM_LIMIT),
        name="rwkv_scan",
    )(ops_f, ops_b, v_f, v_b, s0)


RW_PREP_BLOCK = 128
RW_PREP_VMEM_LIMIT = 56 * 1024 * 1024
RW_NPIECE = 3


def _rw_channel_perm():
    return np.arange(GROUP).reshape(RW_HEADS, RW_HEAD).T.reshape(-1)


def _rw_relayout_matrices(nb):
    slab = LANES
    ch_per_slab = slab // RW_HEADS
    pk = np.zeros((nb, RW_NPIECE, slab, ch_per_slab // RW_KLO, RW_KLO, nb, RW_HEADS), np.float32)
    pv = np.zeros((nb, RW_NPIECE, slab, ch_per_slab, RW_KLO, nb, RW_HEADS), np.float32)
    for b in range(nb):
        for ch in range(ch_per_slab):
            for h in range(RW_HEADS):
                pk[b, :, ch * RW_HEADS + h, ch // RW_KLO, ch % RW_KLO, b, h] = 1.0
                pv[b, :, ch * RW_HEADS + h, ch, :, b, h] = 1.0
    rows = nb * RW_NPIECE * slab
    return (jnp.asarray(pk.reshape(rows, -1), BF16), jnp.asarray(pv.reshape(rows, -1), BF16))


def _split3(x):
    hi = x.astype(BF16)
    r1 = x - hi.astype(F32)
    mid = r1.astype(BF16)
    lo = (r1 - mid.astype(F32)).astype(BF16)
    return hi, mid, lo


def _rw_prep_kernel(rkv_ref, wa_ref, rkv_nb_ref, wa_nb_ref, mu_rkv_ref, mu_wa_ref, w0_ref, w2_ref,
                    a0_ref, a2_ref, kk_ref, ka_ref, rk_ref, hsum_ref, pk_ref, pv_ref,
                    ops_ref, v_ref, bonus_ref, *, backward):
    nb, tb, _ = rkv_ref.shape
    n = pl.program_id(0)
    edge = (n == pl.num_programs(0) - 1) if backward else (n == 0)
    nb_row = SUBLANES - 1 if not backward else 0
    row = lax.broadcasted_iota(jnp.int32, (tb, 1), 0)
    edge_row = (row == tb - 1) if backward else (row == 0)

    def shifted(x, neighbour):
        inner = pltpu.roll(x, tb - 1 if backward else 1, 0)
        outer = jnp.where(edge, 0.0, neighbour)
        return jnp.where(edge_row, outer, inner)

    def head_sum(z):
        hi, lo = _split_bf16(z)
        return _dot(hi, hsum_ref[...]) + _dot(lo, hsum_ref[...])

    feats = []
    for b in range(nb):
        x = rkv_ref[b]
        x = x + (shifted(x, rkv_nb_ref[b, nb_row:nb_row + 1, :]) - x) * mu_rkv_ref[...]
        xw = wa_ref[b]
        xw = xw + (shifted(xw, wa_nb_ref[b, nb_row:nb_row + 1, :]) - xw) * mu_wa_ref[...]
        r, k, v = x[:, :GROUP], x[:, GROUP:2 * GROUP], x[:, 2 * GROUP:]
        decay = jnp.exp(-RW_DECAY_SCALE * jax.nn.sigmoid(
            w0_ref[...] + _dot(jnp.tanh(xw).astype(BF16), w2_ref[...])))
        a = jax.nn.sigmoid(a0_ref[...] + _dot(xw.astype(BF16), a2_ref[...]))
        kq = k * kk_ref[...]
        kk = kq * lax.rsqrt(head_sum(kq * kq) + EPS)
        kt = k * (1.0 + (a - 1.0) * ka_ref[...])
        bonus_ref[b] = head_sum(r * kt * rk_ref[...]) * v
        feats.append([decay, kk * a, kt, r, kk, v])

    nslab = GROUP // LANES
    kw = pk_ref.shape[1]
    vw = pv_ref.shape[1]
    for j in range(6):
        pieces = [_split3(feats[b][j]) for b in range(nb)]
        lhs = jnp.concatenate(
            [jnp.concatenate([p[:, g * LANES:(g + 1) * LANES] for bp in pieces for p in bp], axis=1)
             for g in range(nslab)], axis=0)
        if j < 5:
            out = _dot(lhs, pk_ref[...])
            for g in range(nslab):
                ops_ref[:, (j * nslab + g) * kw:(j * nslab + g + 1) * kw] = out[g * tb:(g + 1) * tb]
        else:
            out = _dot(lhs, pv_ref[...])
            for g in range(nslab):
                rows = vw // LANES
                v_ref[:, g * rows:(g + 1) * rows, :] = out[g * tb:(g + 1) * tb].reshape(tb, rows, LANES)


def _rw_prep(proj, lp, dirn):
    nb, t, _ = proj.shape
    tb = min(RW_PREP_BLOCK, t)
    assert t % tb == 0 and nb * RW_HEADS == RW_CHAINS
    nblk = t // tb
    backward = dirn == 1
    per8 = tb // SUBLANES
    last8 = t // SUBLANES - 1
    if backward:
        nb_idx = lambda n: jnp.minimum((n + 1) * per8, last8)
    else:
        nb_idx = lambda n: jnp.maximum(n * per8 - 1, 0)
    perm = _rw_channel_perm()
    lane_row = lambda v: v.reshape(1, -1)
    pad_rows = lambda w, lo: jnp.zeros((LANES, GROUP), F32).at[lo:lo + w.shape[0]].set(w)
    mu_rkv = lane_row(jnp.concatenate([lp['rw_mu_rkv'][dirn][i * GROUP:(i + 1) * GROUP][perm] for i in range(3)]))
    mu_wa = lane_row(jnp.pad(lp['rw_mu_wa'][dirn], (0, LANES - RW_DECAY_LR - RW_A_LR)))
    w0 = lane_row(lp['rw_w0'][dirn][perm])
    w2 = pad_rows(lp['rw_w2'][dirn][:, perm], 0).astype(BF16)
    a0 = lane_row(lp['rw_a0'][dirn][perm])
    a2 = pad_rows(lp['rw_a2'][dirn][:, perm], RW_DECAY_LR).astype(BF16)
    k_k = lane_row(lp['rw_kk'][perm])
    k_a = lane_row(lp['rw_ka'][perm])
    r_k = lane_row(lp['rw_rk'].reshape(-1)[perm])
    lane = np.arange(GROUP)
    hsum = jnp.asarray((lane[:, None] % RW_HEADS) == (lane[None, :] % RW_HEADS), BF16)
    pk, pv = _rw_relayout_matrices(nb)
    params = (mu_rkv, mu_wa, w0, w2, a0, a2, k_k, k_a, r_k, hsum, pk, pv)
    full = lambda a: pl.BlockSpec(a.shape, lambda n: (0,) * a.ndim)
    ops_w = 5 * RW_KQ * LANES
    v_w = RW_VM * SUBLANES * LANES
    ops, vt, bonus = pl.pallas_call(
        functools.partial(_rw_prep_kernel, backward=backward),
        grid=(nblk,),
        in_specs=[pl.BlockSpec((nb, tb, 3 * GROUP), lambda n: (0, n, _col_block('r_rkv'))),
                  pl.BlockSpec((nb, tb, LANES), lambda n: (0, n, _col_block('r_wa'))),
                  pl.BlockSpec((nb, SUBLANES, 3 * GROUP), lambda n: (0, nb_idx(n), _col_block('r_rkv'))),
                  pl.BlockSpec((nb, SUBLANES, LANES), lambda n: (0, nb_idx(n), _col_block('r_wa')))]
        + [full(p) for p in params],
        out_specs=[pl.BlockSpec((tb, ops_w), lambda n: (n, 0)),
                   pl.BlockSpec((tb, RW_HEAD, LANES), lambda n: (n, 0, 0)),
                   pl.BlockSpec((nb, tb, GROUP), lambda n: (0, n, 0))],
        out_shape=[jax.ShapeDtypeStruct((t, ops_w), F32), jax.ShapeDtypeStruct((t, RW_HEAD, LANES), F32),
                   jax.ShapeDtypeStruct((nb, t, GROUP), F32)],
        compiler_params=pltpu.CompilerParams(
            dimension_semantics=("arbitrary",), vmem_limit_bytes=RW_PREP_VMEM_LIMIT),
        name="rw_prep",
    )(proj, proj, proj, proj, *params)
    return ops, vt.reshape(t, RW_VM, SUBLANES, LANES), bonus


def _l2norm(x):
    return x * lax.rsqrt(jnp.sum(x * x, axis=-1, keepdims=True) + EPS)


def _rw_gather_matrix(nb):
    vs = LANES // RW_HEADS
    q = np.zeros((RW_NPIECE, vs, RW_KLO, nb, RW_HEADS, nb, vs, RW_HEADS), np.float32)
    for b in range(nb):
        for v in range(vs):
            for h in range(RW_HEADS):
                q[:, v, :, b, h, b, v, h] = 1.0
    return jnp.asarray(q.reshape(RW_NPIECE * vs * LANES, nb * LANES), BF16)


def _rw_post_kernel(yf_ref, yb_ref, bf_ref, bb_ref, g_ref, gnw_ref, gnb_ref, g2_ref, hsum_ref, q_ref, o_ref):
    nb = bf_ref.shape[0]
    tb = yf_ref.shape[0]
    pieces = _split3((yf_ref[...] + yb_ref[...]).reshape(tb, RW_HEAD * LANES))
    slab = (LANES // RW_HEADS) * LANES
    nslab = GROUP // LANES
    token_major = [_dot(jnp.concatenate([p[:, g * slab:(g + 1) * slab] for p in pieces], axis=1), q_ref[...])
                   for g in range(nslab)]

    def head_sum(z):
        hi, lo = _split_bf16(z)
        return _dot(hi, hsum_ref[...]) + _dot(lo, hsum_ref[...])

    for b in range(nb):
        yy = jnp.concatenate([tm[:, b * LANES:(b + 1) * LANES] for tm in token_major], axis=1)
        d = yy - head_sum(yy) * (1.0 / RW_HEAD)
        var = head_sum(d * d) * (1.0 / RW_HEAD)
        y_rw = d * lax.rsqrt(var + RW_GN_EPS) * gnw_ref[...] + gnb_ref[...] + (bf_ref[b] + bb_ref[b])
        gate = _dot(jax.nn.sigmoid(g_ref[b]).astype(BF16), g2_ref[...])
        o_ref[b] = (y_rw * gate).astype(o_ref.dtype)


def _rw_post(y_f, y_b, bonus_f, bonus_b, proj, lp):
    nb, t, _ = bonus_f.shape
    tb = min(RW_PREP_BLOCK, t)
    perm = _rw_channel_perm()
    yw = RW_VM * SUBLANES * LANES
    gnw = lp['rw_gn_w'][perm].reshape(1, GROUP)
    gnb = lp['rw_gn_b'][perm].reshape(1, GROUP)
    g2 = jnp.zeros((LANES, GROUP), F32).at[:RW_GATE_LR].set(lp['rw_g2'][:, perm]).astype(BF16)
    lane = np.arange(GROUP)
    hsum = jnp.asarray((lane[:, None] % RW_HEADS) == (lane[None, :] % RW_HEADS), BF16)
    q = _rw_gather_matrix(nb)
    params = (gnw, gnb, g2, hsum, q)
    full = lambda a: pl.BlockSpec(a.shape, lambda n: (0,) * a.ndim)
    tok = pl.BlockSpec((nb, tb, GROUP), lambda n: (0, n, 0))
    return pl.pallas_call(
        _rw_post_kernel,
        grid=(t // tb,),
        in_specs=[pl.BlockSpec((tb, RW_HEAD, LANES), lambda n: (n, 0, 0)),
                  pl.BlockSpec((tb, RW_HEAD, LANES), lambda n: (n, 0, 0)), tok, tok,
                  pl.BlockSpec((nb, tb, LANES), lambda n: (0, n, _col_block('r_g')))] + [full(p) for p in params],
        out_specs=tok,
        out_shape=jax.ShapeDtypeStruct((nb, t, GROUP), BF16),
        compiler_params=pltpu.CompilerParams(
            dimension_semantics=("arbitrary",), vmem_limit_bytes=RW_PREP_VMEM_LIMIT),
        name="rw_post",
    )(y_f.reshape(t, RW_HEAD, LANES), y_b.reshape(t, RW_HEAD, LANES), bonus_f, bonus_b, proj, *params)


def _rw_group(proj_c, proj_x, lp):
    s = jnp.zeros(RW_STATE_SHAPE, F32)
    outs = []
    for proj in (proj_c, proj_x):
        ops_f, v_f, bonus_f = _rw_prep(proj, lp, 0)
        ops_b, v_b, bonus_b = _rw_prep(proj, lp, 1)
        y_f, y_b, s = _rwkv_scan(ops_f, ops_b, v_f, v_b, s)
        outs.append(_rw_post(y_f, y_b, bonus_f, bonus_b, proj, lp))
    return outs


TOKEN_BLOCK = 512
HI = lax.Precision.HIGHEST


def _dot(a, b, precision=None):
    return jnp.dot(a, b, preferred_element_type=F32, precision=precision)


def _dot_nt(a, b):
    return lax.dot_general(a, b, (((1,), (1,)), ((), ())), preferred_element_type=F32)


def _dot_tn(a, b):
    return lax.dot_general(a, b, (((0,), (0,)), ((), ())), preferred_element_type=F32)


def _split_bf16(x):
    hi = x.astype(BF16)
    return hi, (x - hi.astype(F32)).astype(BF16)


def _dot3(a_hi, a_lo, b_hi, b_lo):
    return _dot(a_hi, b_hi) + (_dot(a_hi, b_lo) + _dot(a_lo, b_hi))


def _silu(x):
    return x * jax.nn.sigmoid(x)


def _chunk_masks(reverse):
    ri = lax.broadcasted_iota(jnp.int32, (CHUNK, CHUNK), 0)
    ci = lax.broadcasted_iota(jnp.int32, (CHUNK, CHUNK), 1)
    incl = (ci >= ri) if reverse else (ci <= ri)
    strict = (ci > ri) if reverse else (ci < ri)
    return incl, strict, (ri == ci).astype(F32)


def _conv3_block(x, cw, row_len):
    tb = x.shape[0]
    pos = lax.broadcasted_iota(jnp.int32, (tb, 1), 0) % row_len
    x_prev = jnp.where(pos == 0, 0.0, pltpu.roll(x, 1, 0))
    x_next = jnp.where(pos == row_len - 1, 0.0, pltpu.roll(x, tb - 1, 0))
    return x_prev * cw[0:1] + x * cw[1:2] + x_next * cw[2:3]


def _gdn_kernel(*refs, reverse, finalize, row_len, dirn):
    if finalize:
        (qkv_ref, ab_ref, z_ref, ob_ref, cw_ref, alog_ref, dtb_ref, g_ref, s0_ref,
         o_ref, sfin_ref, s_scr) = refs
    else:
        qkv_ref, ab_ref, cw_ref, alog_ref, dtb_ref, s0_ref, o_ref, sfin_ref, s_scr = refs
    n = pl.program_id(1)

    @pl.when(n == 0)
    def _():
        s_scr[...] = s0_ref[0]

    tb = qkv_ref.shape[1]
    qkv = _silu(_conv3_block(qkv_ref[0], cw_ref[...], row_len))
    ab = ab_ref[0]
    log_a = -jnp.exp(alog_ref[...]) * jax.nn.softplus(ab + dtb_ref[...])
    beta = jax.nn.sigmoid(ab)
    incl, strict, eye = _chunk_masks(reverse)
    tri = incl.astype(F32)
    last = 0 if reverse else CHUNK - 1
    hd = GDN_DK
    nchunk = tb // CHUNK
    order = list(range(nchunk - 1, -1, -1) if reverse else range(nchunk))
    rows = lambda c: slice(c * CHUNK, (c + 1) * CHUNK)
    cum_all = _dot(tri, jnp.concatenate([log_a[rows(c)] for c in range(nchunk)], axis=1), HI)

    items = [(c, h) for c in order for h in range(GDN_HEADS)]
    pre = {}
    for c, h in items:
        col = dirn * GDN_HEADS + h
        cum = cum_all[:, c * LANES + col:c * LANES + col + 1]
        bet = beta[rows(c), 2 * GDN_HEADS + col:2 * GDN_HEADS + col + 1]
        q = _l2norm(qkv[rows(c), h * hd:(h + 1) * hd]) * hd ** -0.5
        k = _l2norm(qkv[rows(c), GROUP + h * hd:GROUP + (h + 1) * hd])
        v = qkv[rows(c), 2 * GROUP + h * hd:2 * GROUP + (h + 1) * hd]
        cum_row = jnp.sum(eye * cum, axis=0, keepdims=True)
        decay = jnp.exp(jnp.where(incl, cum - cum_row, -jnp.inf))
        kb = k.astype(BF16)
        a = jnp.where(strict, bet * _dot_nt(kb, kb) * decay, 0.0)
        ecum = jnp.exp(cum)
        total = cum[last:last + 1]
        pre[c, h] = dict(
            x=-a, rhs=jnp.concatenate([v * bet, k * (bet * ecum)], axis=1),
            a_qk=(_dot_nt(q.astype(BF16), kb) * decay).astype(BF16),
            qe=(q * ecum).astype(BF16), k_end=(k * jnp.exp(total - cum)).astype(BF16),
            dec=jnp.exp(total))
    for p in pre.values():
        p['inv'] = eye + p['x']
    for level in range(6):
        for p in pre.values():
            x_hi, x_lo = _split_bf16(p['x'])
            if level == 0:
                p['x'] = _dot3(x_hi, x_lo, x_hi, x_lo)
            elif level < 5:
                i_hi, i_lo = _split_bf16(p['inv'])
                both = _dot3(jnp.concatenate([x_hi, i_hi], axis=0), jnp.concatenate([x_lo, i_lo], axis=0),
                             x_hi, x_lo)
                p['x'] = both[:CHUNK]
                p['inv'] = p['inv'] + both[CHUNK:]
            else:
                i_hi, i_lo = _split_bf16(p['inv'])
                p['inv'] = p['inv'] + _dot3(i_hi, i_lo, x_hi, x_lo)
    for p in pre.values():
        i_hi, i_lo = _split_bf16(p['inv'])
        r_hi, r_lo = _split_bf16(p['rhs'])
        sol = _dot3(i_hi, i_lo, r_hi, r_lo)
        p['u'], p['w'] = sol[:, :GDN_DV], sol[:, GDN_DV:].astype(BF16)

    for c, h in items:
        p = pre[c, h]
        s = s_scr[h]
        sb = s.astype(BF16)
        v_new = p['u'] - _dot(p['w'], sb)
        vb = v_new.astype(BF16)
        o = _dot(p['qe'], sb) + _dot(p['a_qk'], vb)
        s_scr[h] = p['dec'] * s + _dot_tn(p['k_end'], vb)
        cols = slice(h * GDN_DV, (h + 1) * GDN_DV)
        if finalize:
            o = o + ob_ref[0, rows(c), cols]
            o = o * lax.rsqrt(jnp.mean(o * o, axis=-1, keepdims=True) + EPS) * g_ref[...]
            o = o * _silu(z_ref[0, rows(c), cols])
        o_ref[0, rows(c), cols] = o.astype(o_ref.dtype)

    @pl.when(n == pl.num_programs(1) - 1)
    def _():
        sfin_ref[0] = s_scr[...]


def _gdn_pass(proj, o_other, s0, lp, *, dirn, row_len):
    b, t, _ = proj.shape
    tb = min(TOKEN_BLOCK, t)
    assert t % tb == 0 and tb % row_len == 0
    nblk = t // tb
    reverse = dirn == 1
    finalize = o_other is not None
    tok = (lambda i, n: (i, nblk - 1 - n)) if reverse else (lambda i, n: (i, n))
    seg = lambda name, wd: pl.BlockSpec((1, tb, wd), lambda i, n: tok(i, n) + (_col_block(name),))
    full = lambda a: pl.BlockSpec(a.shape, lambda i, n: (0,) * a.ndim)
    lane_row = lambda vals: jnp.pad(vals.reshape(1, -1), ((0, 0), (0, LANES - vals.size)))
    cw = lp['gdn_conv']
    alog = lane_row(lp['gdn_a_log'])
    dtb = lane_row(lp['gdn_dt_bias'])
    g = lp['gdn_norm_g'].reshape(1, GDN_DV)
    state_spec = pl.BlockSpec((1, GDN_HEADS, GDN_DK, GDN_DV), lambda i, n: (i, 0, 0, 0))
    if finalize:
        args = (proj, proj, proj, o_other, cw, alog, dtb, g, s0)
        in_specs = [seg('d_qkv', 3 * GROUP), seg('d_ab', LANES), seg('d_z', GROUP),
                    pl.BlockSpec((1, tb, GROUP), lambda i, n: tok(i, n) + (0,)),
                    full(cw), full(alog), full(dtb), full(g), state_spec]
    else:
        args = (proj, proj, cw, alog, dtb, s0)
        in_specs = [seg('d_qkv', 3 * GROUP), seg('d_ab', LANES), full(cw), full(alog), full(dtb), state_spec]
    return pl.pallas_call(
        functools.partial(_gdn_kernel, reverse=reverse, finalize=finalize, row_len=row_len, dirn=dirn),
        grid=(b, nblk),
        in_specs=in_specs,
        out_specs=[pl.BlockSpec((1, tb, GROUP), lambda i, n: tok(i, n) + (0,)), state_spec],
        out_shape=[jax.ShapeDtypeStruct((b, t, GROUP), BF16 if finalize else F32),
                   jax.ShapeDtypeStruct((b, GDN_HEADS, GDN_DK, GDN_DV), F32)],
        scratch_shapes=[pltpu.VMEM((GDN_HEADS, GDN_DK, GDN_DV), F32)],
        compiler_params=pltpu.CompilerParams(
            dimension_semantics=("arbitrary", "arbitrary"), vmem_limit_bytes=VMEM_LIMIT),
        name="gdn_fwd" if finalize else "gdn_bwd",
    )(*args)


def _gla_kernel(*refs, reverse, finalize):
    if finalize:
        (q_ref, k_ref, v_ref, lo_ref, r_ref, ob_ref, up_ref, ab_ref, g_ref, s0_ref,
         o_ref, sfin_ref, s_scr) = refs
    else:
        q_ref, k_ref, v_ref, lo_ref, up_ref, ab_ref, s0_ref, o_ref, sfin_ref, s_scr = refs
    n = pl.program_id(1)

    @pl.when(n == 0)
    def _():
        s_scr[...] = s0_ref[0]

    tb = q_ref.shape[1]
    gate = _dot(lo_ref[0].astype(BF16), up_ref[...].astype(BF16)) + ab_ref[...]
    log_f = jax.nn.log_sigmoid(gate) / GLA_TAU
    incl, _, eye = _chunk_masks(reverse)
    tri = incl.astype(F32)
    last = 0 if reverse else CHUNK - 1
    nchunk = tb // CHUNK
    order = range(nchunk - 1, -1, -1) if reverse else range(nchunk)
    for c in order:
        sl = slice(c * CHUNK, (c + 1) * CHUNK)
        cum = _dot(tri, log_f[sl], HI)
        total = cum[last:last + 1]
        q_dec = q_ref[0, sl, :] * GLA_DK ** -0.5 * jnp.exp(cum)
        k = k_ref[0, sl, :]
        k_inv = (k * jnp.exp(-cum)).astype(BF16)
        k_end = (k * jnp.exp(total - cum)).astype(BF16)
        q_dec = q_dec.astype(BF16)
        dec_row = jnp.exp(total)
        for h in range(GLA_HEADS):
            kc = slice(h * GLA_DK, (h + 1) * GLA_DK)
            vc = slice(h * GLA_DV, (h + 1) * GLA_DV)
            vb = v_ref[0, sl, vc].astype(BF16)
            att = jnp.where(incl, _dot_nt(q_dec[:, kc], k_inv[:, kc]), 0.0)
            s = s_scr[h]
            o = _dot(att.astype(BF16), vb) + _dot(q_dec[:, kc], s.astype(BF16))
            dec_col = jnp.sum(eye * dec_row[:, kc], axis=1, keepdims=True)
            s_scr[h] = dec_col * s + _dot_tn(k_end[:, kc], vb)
            if finalize:
                o = o + ob_ref[0, sl, vc]
                o = o * lax.rsqrt(jnp.mean(o * o, axis=-1, keepdims=True) + EPS) * g_ref[...]
                o = o * _silu(r_ref[0, sl, vc])
            o_ref[0, sl, vc] = o.astype(o_ref.dtype)

    @pl.when(n == pl.num_programs(1) - 1)
    def _():
        sfin_ref[0] = s_scr[...]


def _gla_pass(proj, o_other, s0, lp, *, dirn):
    b, t, _ = proj.shape
    tb = min(TOKEN_BLOCK, t)
    assert t % tb == 0
    nblk = t // tb
    reverse = dirn == 1
    finalize = o_other is not None
    tok = (lambda i, n: (i, nblk - 1 - n)) if reverse else (lambda i, n: (i, n))
    seg = lambda name: pl.BlockSpec((1, tb, SEG[name][2]), lambda i, n: tok(i, n) + (_col_block(name),))
    full = lambda a: pl.BlockSpec(a.shape, lambda i, n: (0,) * a.ndim)
    hk = GLA_HEADS * GLA_DK
    up = jnp.pad(lp['gla_a_up'][dirn], ((0, LANES - GLA_LR), (0, 0)))
    ab = lp['gla_a_b'][dirn].reshape(1, hk)
    g = lp['gla_norm_g'].reshape(1, GLA_DV)
    state_spec = pl.BlockSpec((1, GLA_HEADS, GLA_DK, GLA_DV), lambda i, n: (i, 0, 0, 0))
    if finalize:
        args = (proj, proj, proj, proj, proj, o_other, up, ab, g, s0)
        in_specs = [seg('g_q'), seg('g_k'), seg('g_v'), seg('g_lo'), seg('g_r'),
                    pl.BlockSpec((1, tb, GROUP), lambda i, n: tok(i, n) + (0,)),
                    full(up), full(ab), full(g), state_spec]
    else:
        args = (proj, proj, proj, proj, up, ab, s0)
        in_specs = [seg('g_q'), seg('g_k'), seg('g_v'), seg('g_lo'), full(up), full(ab), state_spec]
    return pl.pallas_call(
        functools.partial(_gla_kernel, reverse=reverse, finalize=finalize),
        grid=(b, nblk),
        in_specs=in_specs,
        out_specs=[pl.BlockSpec((1, tb, GROUP), lambda i, n: tok(i, n) + (0,)), state_spec],
        out_shape=[jax.ShapeDtypeStruct((b, t, GROUP), BF16 if finalize else F32),
                   jax.ShapeDtypeStruct((b, GLA_HEADS, GLA_DK, GLA_DV), F32)],
        scratch_shapes=[pltpu.VMEM((GLA_HEADS, GLA_DK, GLA_DV), F32)],
        compiler_params=pltpu.CompilerParams(
            dimension_semantics=("arbitrary", "arbitrary"), vmem_limit_bytes=VMEM_LIMIT),
        name="gla_fwd" if finalize else "gla_bwd",
    )(*args)


def _two_direction_group(pass_fn, state_shape, proj_c, proj_x, **kw):
    zero = jnp.zeros((proj_x[0].shape[0],) + state_shape, F32)
    ob_c, sb_c = pass_fn(proj_c[0], None, zero, dirn=1, **proj_c[1], **kw)
    ob_x, _ = pass_fn(proj_x[0], None, sb_c, dirn=1, **proj_x[1], **kw)
    y_c, sf_c = pass_fn(proj_c[0], ob_c, zero, dirn=0, **proj_c[1], **kw)
    y_x, _ = pass_fn(proj_x[0], ob_x, sf_c, dirn=0, **proj_x[1], **kw)
    return y_c, y_x


def _gla_group_pallas(proj_c, proj_x, lp):
    return _two_direction_group(_gla_pass, (GLA_HEADS, GLA_DK, GLA_DV), (proj_c, {}), (proj_x, {}), lp=lp)


def _gdn_group_pallas(proj_c, proj_x, lp):
    return _two_direction_group(_gdn_pass, (GDN_HEADS, GDN_DK, GDN_DV),
                                (proj_c, dict(row_len=proj_c.shape[1])), (proj_x, dict(row_len=GRID_W)), lp=lp)


def _sc_kernel(b_ref, c_ref, h_ref, cw_ref, o_ref, *, row_len):
    o_ref[0] = (b_ref[0] * _conv3_block(c_ref[0] * h_ref[0], cw_ref[...], row_len)).astype(o_ref.dtype)


def _sc_group(proj, row_len, lp):
    b, t, _ = proj.shape
    tb = min(TOKEN_BLOCK, t)
    assert t % tb == 0 and tb % row_len == 0
    seg = lambda name: pl.BlockSpec((1, tb, GROUP), lambda i, n: (i, n, _col_block(name)))
    cw = lp['sc_conv']
    return pl.pallas_call(
        functools.partial(_sc_kernel, row_len=row_len),
        grid=(b, t // tb),
        in_specs=[seg('c_b'), seg('c_c'), seg('c_h'), pl.BlockSpec(cw.shape, lambda i, n: (0, 0))],
        out_specs=pl.BlockSpec((1, tb, GROUP), lambda i, n: (i, n, 0)),
        out_shape=jax.ShapeDtypeStruct((b, t, GROUP), BF16),
        compiler_params=pltpu.CompilerParams(
            dimension_semantics=("arbitrary", "arbitrary"), vmem_limit_bytes=VMEM_LIMIT),
        name="short_conv",
    )(proj, proj, proj, cw)


def _mixers(proj_c, proj_x, lp):
    tc = proj_c.shape[1]
    y_gla_c, y_gla_x = _gla_group_pallas(proj_c, proj_x, lp)
    y_gdn_c, y_gdn_x = _gdn_group_pallas(proj_c, proj_x, lp)
    y_sc_c = _sc_group(proj_c, tc, lp)
    y_sc_x = _sc_group(proj_x, GRID_W, lp)
    y_rw_c, y_rw_x = _rw_group(proj_c, proj_x, lp)
    return (y_gla_c, y_gdn_c, y_sc_c, y_rw_c), (y_gla_x, y_gdn_x, y_sc_x, y_rw_x)


def kernel(x, c, ctx, c_ctx, ada_w, ada_b, norm1_g, norm2_g, w_in, w_out, gla_a_up, gla_a_b, gla_norm_g, gdn_conv, gdn_a_log, gdn_dt_bias, gdn_norm_g, sc_conv, rw_mu_rkv, rw_mu_wa, rw_w0, rw_w2, rw_a0, rw_a2, rw_g2, rw_kk, rw_ka, rw_rk, rw_gn_w, rw_gn_b, ffn_w_gu, ffn_w_down, final_g):
    nb, t, d = x.shape
    tc = ctx.shape[1]
    depth = w_in.shape[0]
    assert d == D_MODEL and t % 512 == 0 and tc % CHUNK == 0 and nb + 1 <= SUBLANES

    cond = jnp.concatenate([c, c_ctx[None, :], jnp.zeros((SUBLANES - nb - 1, d), F32)], axis=0)
    mod = _modulation(cond, ada_w, ada_b)

    tm_x = 512
    tm_c = tc
    h_x, h_c = x, ctx
    for l in range(depth):
        lp = dict(gla_a_up=gla_a_up[l], gla_a_b=gla_a_b[l],
                  gla_norm_g=gla_norm_g[l], gdn_conv=gdn_conv[l], gdn_a_log=gdn_a_log[l],
                  gdn_dt_bias=gdn_dt_bias[l], gdn_norm_g=gdn_norm_g[l], sc_conv=sc_conv[l],
                  rw_mu_rkv=rw_mu_rkv[l], rw_mu_wa=rw_mu_wa[l], rw_w0=rw_w0[l], rw_w2=rw_w2[l],
                  rw_a0=rw_a0[l], rw_a2=rw_a2[l], rw_g2=rw_g2[l], rw_kk=rw_kk[l], rw_ka=rw_ka[l],
                  rw_rk=rw_rk[l], rw_gn_w=rw_gn_w[l], rw_gn_b=rw_gn_b[l])
        m_x = [mod[l, :nb, i * d:(i + 1) * d][:, None, :] for i in range(6)]
        m_c = [jnp.broadcast_to(mod[l, nb, i * d:(i + 1) * d][None, None, :], (nb, 1, d)) for i in range(6)]
        w_in_l = _pad_w_in(w_in[l])
        w_out_l = jnp.concatenate([w_out[l][:3 * GROUP], w_out[l][3 * GROUP:][_rw_channel_perm()]],
                                  axis=0).astype(BF16)
        w_gu_l = ffn_w_gu[l].astype(BF16)
        w_dn_l = ffn_w_down[l].astype(BF16)

        proj_c = _norm_proj(h_c, norm1_g[l], m_c[1], m_c[0], w_in_l, tm_c, 1024)
        proj_x = _norm_proj(h_x, norm1_g[l], m_x[1], m_x[0], w_in_l, 2 * tm_x, 1024)
        y_c, y_x = _mixers(proj_c, proj_x, lp)

        h_x = _groups_residual(y_x, w_out_l, h_x, m_x[2], 2 * tm_x, 1024)
        a_x = _norm_swiglu(h_x, norm2_g[l], m_x[4], m_x[3], w_gu_l, 2 * tm_x, 512)
        h_x = _proj_residual(a_x, w_dn_l, h_x, m_x[5], 2 * tm_x, 512)
        if l < depth - 1:
            h_c = _groups_residual(y_c, w_out_l, h_c, m_c[2], tm_c, 512)
            a_c = _norm_swiglu(h_c, norm2_g[l], m_c[4], m_c[3], w_gu_l, tm_c, 512)
            h_c = _proj_residual(a_c, w_dn_l, h_c, m_c[5], tm_c, 512)
    return _final_norm(h_x, final_g, tm_x)
```

```python
import functools
import math

import numpy as np
import jax
import jax.numpy as jnp
from jax import lax
from jax.experimental import pallas as pl
from jax.experimental.pallas import tpu as pltpu

F32 = jnp.float32
BF16 = jnp.bfloat16

D_MODEL = 2048
GROUP = D_MODEL // 4
CHUNK = 64
EPS = 1e-6
GRID_W = 64

GLA_HEADS = 4
GLA_DK = 64
GLA_DV = 128
GLA_LR = 16
GLA_TAU = 16.0
GDN_HEADS = 4
GDN_DK = 128
GDN_DV = 128
RW_HEAD = 64
RW_HEADS = GROUP // RW_HEAD
RW_DECAY_LR = 32
RW_A_LR = 32
RW_GATE_LR = 96
RW_DECAY_SCALE = math.exp(-0.5)
RW_GN_EPS = 64e-5
FFN = -(-(8 * D_MODEL) // (3 * 256)) * 256

SUBLANES = 8
LANES = 128
VMEM_LIMIT = 56 * 1024 * 1024

_SRC_SEGS = (
    ('g_q', 256), ('g_k', 256), ('g_v', 512), ('g_r', 512), ('g_lo', GLA_LR),
    ('d_qkv', 1536), ('d_z', 512), ('d_ab', 16),
    ('c_b', 512), ('c_c', 512), ('c_h', 512),
    ('r_rkv', 1536), ('r_wa', 64), ('r_g', RW_GATE_LR),
)
_DST_ORDER = ('g_q', 'g_k', 'g_v', 'g_r', 'd_qkv', 'r_rkv', 'd_z', 'c_b', 'c_c', 'c_h',
              'g_lo', 'd_ab', 'r_wa', 'r_g')


def _round_up(n, m):
    return -(-n // m) * m


def _seg_layout():
    width = dict(_SRC_SEGS)
    src, o = {}, 0
    for name, wd in _SRC_SEGS:
        src[name] = o
        o += wd
    dst, o = {}, 0
    for name in _DST_ORDER:
        pw = _round_up(width[name], LANES)
        assert o % pw == 0, name
        dst[name] = (o, width[name], pw)
        o += pw
    return src, dst, o


SEG_SRC, SEG, PROJ_PAD = _seg_layout()


def _pad_w_in(w):
    parts = []
    for name in _DST_ORDER:
        _, wd, pw = SEG[name]
        seg = w[:, SEG_SRC[name]:SEG_SRC[name] + wd]
        if name == 'r_rkv':
            perm = _rw_channel_perm()
            seg = jnp.concatenate([seg[:, i * GROUP:(i + 1) * GROUP][:, perm] for i in range(3)], axis=1)
        if pw != wd:
            seg = jnp.pad(seg, ((0, 0), (0, pw - wd)))
        parts.append(seg)
    return jnp.concatenate(parts, axis=1).astype(BF16)


def _col_block(name):
    o, _, pw = SEG[name]
    return o // pw


def _mod_kernel(c_ref, w_ref, b_ref, o_ref):
    x = c_ref[...]
    x = (x * jax.nn.sigmoid(x)).astype(BF16)
    o_ref[0] = jnp.dot(x, w_ref[0].astype(BF16), preferred_element_type=F32) + b_ref[0]


def _modulation(cond, ada_w, ada_b):
    nl, d, n = ada_w.shape
    tn = 1024
    return pl.pallas_call(
        _mod_kernel,
        grid=(nl, n // tn),
        in_specs=[pl.BlockSpec((SUBLANES, d), lambda l, j: (0, 0)),
                  pl.BlockSpec((1, d, tn), lambda l, j: (l, 0, j)),
                  pl.BlockSpec((1, 1, tn), lambda l, j: (l, 0, j))],
        out_specs=pl.BlockSpec((1, SUBLANES, tn), lambda l, j: (l, 0, j)),
        out_shape=jax.ShapeDtypeStruct((nl, SUBLANES, n), F32),
        compiler_params=pltpu.CompilerParams(
            dimension_semantics=("arbitrary", "arbitrary"), vmem_limit_bytes=VMEM_LIMIT),
        name="modulation",
    )(cond, ada_w, ada_b.reshape(nl, 1, n))


def _modnorm_rows(h_ref, g_ref, sc_ref, sh_ref):
    x = h_ref[0]
    y = x * lax.rsqrt(jnp.mean(x * x, axis=-1, keepdims=True) + EPS) * g_ref[...]
    return (y * (1.0 + sc_ref[0]) + sh_ref[0]).astype(BF16)


def _norm_proj_kernel(h_ref, g_ref, sc_ref, sh_ref, w_ref, o_ref, a_scr):
    @pl.when(pl.program_id(2) == 0)
    def _():
        a_scr[...] = _modnorm_rows(h_ref, g_ref, sc_ref, sh_ref)

    o_ref[0] = jnp.dot(a_scr[...], w_ref[...], preferred_element_type=F32).astype(o_ref.dtype)


def _norm_proj(h, g, scale, shift, w, tm, tn):
    b, t, d = h.shape
    n = w.shape[1]
    return pl.pallas_call(
        _norm_proj_kernel,
        grid=(b, t // tm, n // tn),
        in_specs=[pl.BlockSpec((1, tm, d), lambda i, m, j: (i, m, 0)),
                  pl.BlockSpec((1, d), lambda i, m, j: (0, 0)),
                  pl.BlockSpec((1, 1, d), lambda i, m, j: (i, 0, 0)),
                  pl.BlockSpec((1, 1, d), lambda i, m, j: (i, 0, 0)),
                  pl.BlockSpec((d, tn), lambda i, m, j: (0, j))],
        out_specs=pl.BlockSpec((1, tm, tn), lambda i, m, j: (i, m, j)),
        out_shape=jax.ShapeDtypeStruct((b, t, n), F32),
        scratch_shapes=[pltpu.VMEM((tm, d), BF16)],
        compiler_params=pltpu.CompilerParams(
            dimension_semantics=("arbitrary", "arbitrary", "arbitrary"), vmem_limit_bytes=VMEM_LIMIT),
        name="norm_proj",
    )(h, g.reshape(1, d), scale, shift, w)


def _norm_swiglu_kernel(h_ref, g_ref, sc_ref, sh_ref, wg_ref, wu_ref, o_ref, a_scr):
    @pl.when(pl.program_id(2) == 0)
    def _():
        a_scr[...] = _modnorm_rows(h_ref, g_ref, sc_ref, sh_ref)

    a = a_scr[...]
    gate = jnp.dot(a, wg_ref[...], preferred_element_type=F32)
    up = jnp.dot(a, wu_ref[...], preferred_element_type=F32)
    o_ref[0] = (gate * jax.nn.sigmoid(gate) * up).astype(o_ref.dtype)


def _norm_swiglu(h, g, scale, shift, w_gu, tm, tn):
    b, t, d = h.shape
    f = w_gu.shape[1] // 2
    nj = f // tn
    return pl.pallas_call(
        _norm_swiglu_kernel,
        grid=(b, t // tm, nj),
        in_specs=[pl.BlockSpec((1, tm, d), lambda i, m, j: (i, m, 0)),
                  pl.BlockSpec((1, d), lambda i, m, j: (0, 0)),
                  pl.BlockSpec((1, 1, d), lambda i, m, j: (i, 0, 0)),
                  pl.BlockSpec((1, 1, d), lambda i, m, j: (i, 0, 0)),
                  pl.BlockSpec((d, tn), lambda i, m, j: (0, j)),
                  pl.BlockSpec((d, tn), lambda i, m, j: (0, j + nj))],
        out_specs=pl.BlockSpec((1, tm, tn), lambda i, m, j: (i, m, j)),
        out_shape=jax.ShapeDtypeStruct((b, t, f), BF16),
        scratch_shapes=[pltpu.VMEM((tm, d), BF16)],
        compiler_params=pltpu.CompilerParams(
            dimension_semantics=("arbitrary", "arbitrary", "arbitrary"), vmem_limit_bytes=VMEM_LIMIT),
        name="norm_swiglu",
    )(h, g.reshape(1, d), scale, shift, w_gu, w_gu)


def _proj_residual_kernel(a_ref, w_ref, h_ref, gate_ref, o_ref):
    o_ref[0] = h_ref[0] + gate_ref[0] * jnp.dot(a_ref[0], w_ref[...], preferred_element_type=F32)


def _proj_residual(a, w, h, gate, tm, tn):
    b, t, k = a.shape
    d = w.shape[1]
    return pl.pallas_call(
        _proj_residual_kernel,
        grid=(b, t // tm, d // tn),
        in_specs=[pl.BlockSpec((1, tm, k), lambda i, m, j: (i, m, 0)),
                  pl.BlockSpec((k, tn), lambda i, m, j: (0, j)),
                  pl.BlockSpec((1, tm, tn), lambda i, m, j: (i, m, j)),
                  pl.BlockSpec((1, 1, tn), lambda i, m, j: (i, 0, j))],
        out_specs=pl.BlockSpec((1, tm, tn), lambda i, m, j: (i, m, j)),
        out_shape=jax.ShapeDtypeStruct((b, t, d), F32),
        compiler_params=pltpu.CompilerParams(
            dimension_semantics=("arbitrary", "arbitrary", "arbitrary"), vmem_limit_bytes=VMEM_LIMIT),
        name="proj_residual",
    )(a, w, h, gate)


def _groups_residual_kernel(*refs):
    *a_refs, w_ref, h_ref, gate_ref, o_ref = refs
    k = a_refs[0].shape[2]
    acc = jnp.dot(a_refs[0][0], w_ref[0:k, :], preferred_element_type=F32)
    for i, a_ref in enumerate(a_refs[1:], start=1):
        acc = acc + jnp.dot(a_ref[0], w_ref[i * k:(i + 1) * k, :], preferred_element_type=F32)
    o_ref[0] = h_ref[0] + gate_ref[0] * acc


def _groups_residual(groups, w, h, gate, tm, tn):
    b, t, k = groups[0].shape
    d = w.shape[1]
    assert len(groups) * k == w.shape[0]
    return pl.pallas_call(
        _groups_residual_kernel,
        grid=(b, t // tm, d // tn),
        in_specs=[pl.BlockSpec((1, tm, k), lambda i, m, j: (i, m, 0)) for _ in groups]
        + [pl.BlockSpec((w.shape[0], tn), lambda i, m, j: (0, j)),
           pl.BlockSpec((1, tm, tn), lambda i, m, j: (i, m, j)),
           pl.BlockSpec((1, 1, tn), lambda i, m, j: (i, 0, j))],
        out_specs=pl.BlockSpec((1, tm, tn), lambda i, m, j: (i, m, j)),
        out_shape=jax.ShapeDtypeStruct((b, t, d), F32),
        compiler_params=pltpu.CompilerParams(
            dimension_semantics=("arbitrary", "arbitrary", "arbitrary"), vmem_limit_bytes=VMEM_LIMIT),
        name="groups_residual",
    )(*groups, w, h, gate)


def _final_norm_kernel(h_ref, g_ref, o_ref):
    x = h_ref[0]
    o_ref[0] = x * lax.rsqrt(jnp.mean(x * x, axis=-1, keepdims=True) + EPS) * g_ref[...]


def _final_norm(h, g, tm):
    b, t, d = h.shape
    return pl.pallas_call(
        _final_norm_kernel,
        grid=(b, t // tm),
        in_specs=[pl.BlockSpec((1, tm, d), lambda i, m: (i, m, 0)),
                  pl.BlockSpec((1, d), lambda i, m: (0, 0))],
        out_specs=pl.BlockSpec((1, tm, d), lambda i, m: (i, m, 0)),
        out_shape=jax.ShapeDtypeStruct((b, t, d), F32),
        compiler_params=pltpu.CompilerParams(
            dimension_semantics=("arbitrary", "arbitrary"), vmem_limit_bytes=VMEM_LIMIT),
        name="final_norm",
    )(h, g.reshape(1, d))


RW_CHAINS = 32
RW_KLO = LANES // RW_CHAINS
RW_KQ = RW_HEAD // RW_KLO
RW_VM = RW_HEAD // SUBLANES
RW_VGROUP = 4
RW_STEP_BLOCK = 32
OP_W, OP_B, OP_K, OP_R, OP_KK = range(5)


def _lane_group_sum(p):
    out = p
    for g in range(1, RW_KLO):
        out = out + pltpu.roll(p, g * RW_CHAINS, 1)
    return out


def _rwkv_kernel(opsf_ref, opsb_ref, vf_ref, vb_ref, s0_ref, yf_ref, yb_ref, sfin_ref,
                 s_ref, saf_ref, sab_ref, pb_ref):
    tb = opsf_ref.shape[0]
    dirs = ((opsf_ref, vf_ref, yf_ref), (opsb_ref, vb_ref, yb_ref))

    def row(ops_ref, t, j, q):
        lane0 = (j * RW_KQ + q) * LANES
        return ops_ref[t:t + 1, lane0:lane0 + LANES]

    @pl.when(pl.program_id(0) == 0)
    def _():
        s_ref[...] = s0_ref[...]

    zero = jnp.zeros((SUBLANES, LANES), F32)
    groups = [range(m0, m0 + RW_VGROUP) for m0 in range(0, RW_VM, RW_VGROUP)]

    for d, (ops_ref, _, _) in enumerate(dirs):
        first = 0 if d == 0 else tb - 1
        for ms in groups:
            acc = {m: zero for m in ms}
            for q in range(RW_KQ):
                kk = row(ops_ref, first, OP_KK, q)
                for m in ms:
                    acc[m] = acc[m] + s_ref[d, q, m] * kk
            for m in ms:
                if d == 0:
                    saf_ref[m] = _lane_group_sum(acc[m])
                else:
                    pb_ref[m] = acc[m]

    def advance(d, i, sa_ref, next_ref, reduce_next):
        ops_ref, v_ref, y_ref = dirs[d]
        t = i if d == 0 else tb - 1 - i
        tn = min(i + 1, tb - 1) if d == 0 else max(tb - 2 - i, 0)
        for ms in groups:
            sa = {m: sa_ref[m] for m in ms}
            vt = {m: v_ref[t, m] for m in ms}
            acc_y = {m: zero for m in ms}
            acc_s = {m: zero for m in ms}
            for q in range(RW_KQ):
                w = row(ops_ref, t, OP_W, q)
                b = row(ops_ref, t, OP_B, q)
                kt = row(ops_ref, t, OP_K, q)
                r = row(ops_ref, t, OP_R, q)
                kkn = row(ops_ref, tn, OP_KK, q)
                for m in ms:
                    s = s_ref[d, q, m] * w - sa[m] * b + vt[m] * kt
                    s_ref[d, q, m] = s
                    acc_y[m] = acc_y[m] + s * r
                    acc_s[m] = acc_s[m] + s * kkn
            for m in ms:
                y_ref[t, m] = acc_y[m]
                next_ref[m] = _lane_group_sum(acc_s[m]) if reduce_next else acc_s[m]

    for i in range(tb):
        for m in range(RW_VM):
            sab_ref[m] = _lane_group_sum(pb_ref[m])
        advance(0, i, saf_ref, saf_ref, True)
        advance(1, i, sab_ref, pb_ref, False)

    @pl.when(pl.program_id(0) == pl.num_programs(0) - 1)
    def _():
        sfin_ref[...] = s_ref[...]


RW_STATE_SHAPE = (2, RW_KQ, RW_VM, SUBLANES, LANES)


def _rwkv_scan(ops_f, ops_b, v_f, v_b, s0):
    t = ops_f.shape[0]
    tb = RW_STEP_BLOCK
    nblk = t // tb
    fwd = lambda i: (i, 0, 0, 0)
    bwd = lambda i: (nblk - 1 - i, 0, 0, 0)
    ops_block = (tb, 5 * RW_KQ * LANES)
    v_block = (tb, RW_VM, SUBLANES, LANES)
    y_shape = jax.ShapeDtypeStruct((t, RW_VM, SUBLANES, LANES), F32)
    state_spec = pl.BlockSpec(RW_STATE_SHAPE, lambda i: (0,) * len(RW_STATE_SHAPE))
    return pl.pallas_call(
        _rwkv_kernel,
        grid=(nblk,),
        in_specs=[pl.BlockSpec(ops_block, lambda i: (i, 0)), pl.BlockSpec(ops_block, lambda i: (nblk - 1 - i, 0)),
                  pl.BlockSpec(v_block, fwd), pl.BlockSpec(v_block, bwd), state_spec],
        out_specs=[pl.BlockSpec(v_block, fwd), pl.BlockSpec(v_block, bwd), state_spec],
        out_shape=[y_shape, y_shape, jax.ShapeDtypeStruct(RW_STATE_SHAPE, F32)],
        scratch_shapes=[pltpu.VMEM(RW_STATE_SHAPE, F32)]
        + [pltpu.VMEM((RW_VM, SUBLANES, LANES), F32)] * 3,
        compiler_params=pltpu.CompilerParams(
            dimension_semantics=("arbitrary",), vmem_limit_bytes=VMEM_LIMIT),
        name="rwkv_scan",
    )(ops_f, ops_b, v_f, v_b, s0)


RW_PREP_BLOCK = 128
RW_PREP_VMEM_LIMIT = 56 * 1024 * 1024
RW_NPIECE = 3


def _rw_channel_perm():
    return np.arange(GROUP).reshape(RW_HEADS, RW_HEAD).T.reshape(-1)


def _rw_relayout_matrices(nb):
    slab = LANES
    ch_per_slab = slab // RW_HEADS
    pk = np.zeros((nb, RW_NPIECE, slab, ch_per_slab // RW_KLO, RW_KLO, nb, RW_HEADS), np.float32)
    pv = np.zeros((nb, RW_NPIECE, slab, ch_per_slab, RW_KLO, nb, RW_HEADS), np.float32)
    for b in range(nb):
        for ch in range(ch_per_slab):
            for h in range(RW_HEADS):
                pk[b, :, ch * RW_HEADS + h, ch // RW_KLO, ch % RW_KLO, b, h] = 1.0
                pv[b, :, ch * RW_HEADS + h, ch, :, b, h] = 1.0
    rows = nb * RW_NPIECE * slab
    return (jnp.asarray(pk.reshape(rows, -1), BF16), jnp.asarray(pv.reshape(rows, -1), BF16))


def _split3(x):
    hi = x.astype(BF16)
    r1 = x - hi.astype(F32)
    mid = r1.astype(BF16)
    lo = (r1 - mid.astype(F32)).astype(BF16)
    return hi, mid, lo


def _rw_prep_kernel(rkv_ref, wa_ref, rkv_nb_ref, wa_nb_ref, mu_rkv_ref, mu_wa_ref, w0_ref, w2_ref,
                    a0_ref, a2_ref, kk_ref, ka_ref, rk_ref, hsum_ref, pk_ref, pv_ref,
                    ops_ref, v_ref, bonus_ref, *, backward):
    nb, tb, _ = rkv_ref.shape
    n = pl.program_id(0)
    edge = (n == pl.num_programs(0) - 1) if backward else (n == 0)
    nb_row = SUBLANES - 1 if not backward else 0
    row = lax.broadcasted_iota(jnp.int32, (tb, 1), 0)
    edge_row = (row == tb - 1) if backward else (row == 0)

    def shifted(x, neighbour):
        inner = pltpu.roll(x, tb - 1 if backward else 1, 0)
        outer = jnp.where(edge, 0.0, neighbour)
        return jnp.where(edge_row, outer, inner)

    def head_sum(z):
        hi, lo = _split_bf16(z)
        return _dot(hi, hsum_ref[...]) + _dot(lo, hsum_ref[...])

    feats = []
    for b in range(nb):
        x = rkv_ref[b]
        x = x + (shifted(x, rkv_nb_ref[b, nb_row:nb_row + 1, :]) - x) * mu_rkv_ref[...]
        xw = wa_ref[b]
        xw = xw + (shifted(xw, wa_nb_ref[b, nb_row:nb_row + 1, :]) - xw) * mu_wa_ref[...]
        r, k, v = x[:, :GROUP], x[:, GROUP:2 * GROUP], x[:, 2 * GROUP:]
        decay = jnp.exp(-RW_DECAY_SCALE * jax.nn.sigmoid(
            w0_ref[...] + _dot(jnp.tanh(xw).astype(BF16), w2_ref[...])))
        a = jax.nn.sigmoid(a0_ref[...] + _dot(xw.astype(BF16), a2_ref[...]))
        kq = k * kk_ref[...]
        kk = kq * lax.rsqrt(head_sum(kq * kq) + EPS)
        kt = k * (1.0 + (a - 1.0) * ka_ref[...])
        bonus_ref[b] = head_sum(r * kt * rk_ref[...]) * v
        feats.append([decay, kk * a, kt, r, kk, v])

    nslab = GROUP // LANES
    kw = pk_ref.shape[1]
    vw = pv_ref.shape[1]
    for j in range(6):
        pieces = [_split3(feats[b][j]) for b in range(nb)]
        lhs = jnp.concatenate(
            [jnp.concatenate([p[:, g * LANES:(g + 1) * LANES] for bp in pieces for p in bp], axis=1)
             for g in range(nslab)], axis=0)
        if j < 5:
            out = _dot(lhs, pk_ref[...])
            for g in range(nslab):
                ops_ref[:, (j * nslab + g) * kw:(j * nslab + g + 1) * kw] = out[g * tb:(g + 1) * tb]
        else:
            out = _dot(lhs, pv_ref[...])
            for g in range(nslab):
                rows = vw // LANES
                v_ref[:, g * rows:(g + 1) * rows, :] = out[g * tb:(g + 1) * tb].reshape(tb, rows, LANES)


def _rw_prep(proj, lp, dirn):
    nb, t, _ = proj.shape
    tb = min(RW_PREP_BLOCK, t)
    assert t % tb == 0 and nb * RW_HEADS == RW_CHAINS
    nblk = t // tb
    backward = dirn == 1
    per8 = tb // SUBLANES
    last8 = t // SUBLANES - 1
    if backward:
        nb_idx = lambda n: jnp.minimum((n + 1) * per8, last8)
    else:
        nb_idx = lambda n: jnp.maximum(n * per8 - 1, 0)
    perm = _rw_channel_perm()
    lane_row = lambda v: v.reshape(1, -1)
    pad_rows = lambda w, lo: jnp.zeros((LANES, GROUP), F32).at[lo:lo + w.shape[0]].set(w)
    mu_rkv = lane_row(jnp.concatenate([lp['rw_mu_rkv'][dirn][i * GROUP:(i + 1) * GROUP][perm] for i in range(3)]))
    mu_wa = lane_row(jnp.pad(lp['rw_mu_wa'][dirn], (0, LANES - RW_DECAY_LR - RW_A_LR)))
    w0 = lane_row(lp['rw_w0'][dirn][perm])
    w2 = pad_rows(lp['rw_w2'][dirn][:, perm], 0).astype(BF16)
    a0 = lane_row(lp['rw_a0'][dirn][perm])
    a2 = pad_rows(lp['rw_a2'][dirn][:, perm], RW_DECAY_LR).astype(BF16)
    k_k = lane_row(lp['rw_kk'][perm])
    k_a = lane_row(lp['rw_ka'][perm])
    r_k = lane_row(lp['rw_rk'].reshape(-1)[perm])
    lane = np.arange(GROUP)
    hsum = jnp.asarray((lane[:, None] % RW_HEADS) == (lane[None, :] % RW_HEADS), BF16)
    pk, pv = _rw_relayout_matrices(nb)
    params = (mu_rkv, mu_wa, w0, w2, a0, a2, k_k, k_a, r_k, hsum, pk, pv)
    full = lambda a: pl.BlockSpec(a.shape, lambda n: (0,) * a.ndim)
    ops_w = 5 * RW_KQ * LANES
    v_w = RW_VM * SUBLANES * LANES
    ops, vt, bonus = pl.pallas_call(
        functools.partial(_rw_prep_kernel, backward=backward),
        grid=(nblk,),
        in_specs=[pl.BlockSpec((nb, tb, 3 * GROUP), lambda n: (0, n, _col_block('r_rkv'))),
                  pl.BlockSpec((nb, tb, LANES), lambda n: (0, n, _col_block('r_wa'))),
                  pl.BlockSpec((nb, SUBLANES, 3 * GROUP), lambda n: (0, nb_idx(n), _col_block('r_rkv'))),
                  pl.BlockSpec((nb, SUBLANES, LANES), lambda n: (0, nb_idx(n), _col_block('r_wa')))]
        + [full(p) for p in params],
        out_specs=[pl.BlockSpec((tb, ops_w), lambda n: (n, 0)),
                   pl.BlockSpec((tb, RW_HEAD, LANES), lambda n: (n, 0, 0)),
                   pl.BlockSpec((nb, tb, GROUP), lambda n: (0, n, 0))],
        out_shape=[jax.ShapeDtypeStruct((t, ops_w), F32), jax.ShapeDtypeStruct((t, RW_HEAD, LANES), F32),
                   jax.ShapeDtypeStruct((nb, t, GROUP), F32)],
        compiler_params=pltpu.CompilerParams(
            dimension_semantics=("arbitrary",), vmem_limit_bytes=RW_PREP_VMEM_LIMIT),
        name="rw_prep",
    )(proj, proj, proj, proj, *params)
    return ops, vt.reshape(t, RW_VM, SUBLANES, LANES), bonus


def _l2norm(x):
    return x * lax.rsqrt(jnp.sum(x * x, axis=-1, keepdims=True) + EPS)


def _rw_gather_matrix(nb):
    vs = LANES // RW_HEADS
    q = np.zeros((RW_NPIECE, vs, RW_KLO, nb, RW_HEADS, nb, vs, RW_HEADS), np.float32)
    for b in range(nb):
        for v in range(vs):
            for h in range(RW_HEADS):
                q[:, v, :, b, h, b, v, h] = 1.0
    return jnp.asarray(q.reshape(RW_NPIECE * vs * LANES, nb * LANES), BF16)


def _rw_post_kernel(yf_ref, yb_ref, bf_ref, bb_ref, g_ref, gnw_ref, gnb_ref, g2_ref, hsum_ref, q_ref, o_ref):
    nb = bf_ref.shape[0]
    tb = yf_ref.shape[0]
    pieces = _split3((yf_ref[...] + yb_ref[...]).reshape(tb, RW_HEAD * LANES))
    slab = (LANES // RW_HEADS) * LANES
    nslab = GROUP // LANES
    token_major = [_dot(jnp.concatenate([p[:, g * slab:(g + 1) * slab] for p in pieces], axis=1), q_ref[...])
                   for g in range(nslab)]

    def head_sum(z):
        hi, lo = _split_bf16(z)
        return _dot(hi, hsum_ref[...]) + _dot(lo, hsum_ref[...])

    for b in range(nb):
        yy = jnp.concatenate([tm[:, b * LANES:(b + 1) * LANES] for tm in token_major], axis=1)
        d = yy - head_sum(yy) * (1.0 / RW_HEAD)
        var = head_sum(d * d) * (1.0 / RW_HEAD)
        y_rw = d * lax.rsqrt(var + RW_GN_EPS) * gnw_ref[...] + gnb_ref[...] + (bf_ref[b] + bb_ref[b])
        gate = _dot(jax.nn.sigmoid(g_ref[b]).astype(BF16), g2_ref[...])
        o_ref[b] = (y_rw * gate).astype(o_ref.dtype)


def _rw_post(y_f, y_b, bonus_f, bonus_b, proj, lp):
    nb, t, _ = bonus_f.shape
    tb = min(RW_PREP_BLOCK, t)
    perm = _rw_channel_perm()
    yw = RW_VM * SUBLANES * LANES
    gnw = lp['rw_gn_w'][perm].reshape(1, GROUP)
    gnb = lp['rw_gn_b'][perm].reshape(1, GROUP)
    g2 = jnp.zeros((LANES, GROUP), F32).at[:RW_GATE_LR].set(lp['rw_g2'][:, perm]).astype(BF16)
    lane = np.arange(GROUP)
    hsum = jnp.asarray((lane[:, None] % RW_HEADS) == (lane[None, :] % RW_HEADS), BF16)
    q = _rw_gather_matrix(nb)
    params = (gnw, gnb, g2, hsum, q)
    full = lambda a: pl.BlockSpec(a.shape, lambda n: (0,) * a.ndim)
    tok = pl.BlockSpec((nb, tb, GROUP), lambda n: (0, n, 0))
    return pl.pallas_call(
        _rw_post_kernel,
        grid=(t // tb,),
        in_specs=[pl.BlockSpec((tb, RW_HEAD, LANES), lambda n: (n, 0, 0)),
                  pl.BlockSpec((tb, RW_HEAD, LANES), lambda n: (n, 0, 0)), tok, tok,
                  pl.BlockSpec((nb, tb, LANES), lambda n: (0, n, _col_block('r_g')))] + [full(p) for p in params],
        out_specs=tok,
        out_shape=jax.ShapeDtypeStruct((nb, t, GROUP), BF16),
        compiler_params=pltpu.CompilerParams(
            dimension_semantics=("arbitrary",), vmem_limit_bytes=RW_PREP_VMEM_LIMIT),
        name="rw_post",
    )(y_f.reshape(t, RW_HEAD, LANES), y_b.reshape(t, RW_HEAD, LANES), bonus_f, bonus_b, proj, *params)


def _rw_group(proj_c, proj_x, lp):
    s = jnp.zeros(RW_STATE_SHAPE, F32)
    outs = []
    for proj in (proj_c, proj_x):
        ops_f, v_f, bonus_f = _rw_prep(proj, lp, 0)
        ops_b, v_b, bonus_b = _rw_prep(proj, lp, 1)
        y_f, y_b, s = _rwkv_scan(ops_f, ops_b, v_f, v_b, s)
        outs.append(_rw_post(y_f, y_b, bonus_f, bonus_b, proj, lp))
    return outs


TOKEN_BLOCK = 512
HI = lax.Precision.HIGHEST


def _dot(a, b, precision=None):
    return jnp.dot(a, b, preferred_element_type=F32, precision=precision)


def _dot_nt(a, b):
    return lax.dot_general(a, b, (((1,), (1,)), ((), ())), preferred_element_type=F32)


def _dot_tn(a, b):
    return lax.dot_general(a, b, (((0,), (0,)), ((), ())), preferred_element_type=F32)


def _split_bf16(x):
    hi = x.astype(BF16)
    return hi, (x - hi.astype(F32)).astype(BF16)


def _dot3(a_hi, a_lo, b_hi, b_lo):
    return _dot(a_hi, b_hi) + (_dot(a_hi, b_lo) + _dot(a_lo, b_hi))


def _silu(x):
    return x * jax.nn.sigmoid(x)


def _chunk_masks(reverse):
    ri = lax.broadcasted_iota(jnp.int32, (CHUNK, CHUNK), 0)
    ci = lax.broadcasted_iota(jnp.int32, (CHUNK, CHUNK), 1)
    incl = (ci >= ri) if reverse else (ci <= ri)
    strict = (ci > ri) if reverse else (ci < ri)
    return incl, strict, (ri == ci).astype(F32)


def _conv3_block(x, cw, row_len):
    tb = x.shape[0]
    pos = lax.broadcasted_iota(jnp.int32, (tb, 1), 0) % row_len
    x_prev = jnp.where(pos == 0, 0.0, pltpu.roll(x, 1, 0))
    x_next = jnp.where(pos == row_len - 1, 0.0, pltpu.roll(x, tb - 1, 0))
    return x_prev * cw[0:1] + x * cw[1:2] + x_next * cw[2:3]


def _gdn_kernel(*refs, reverse, finalize, row_len, dirn):
    if finalize:
        (qkv_ref, ab_ref, z_ref, ob_ref, cw_ref, alog_ref, dtb_ref, g_ref, s0_ref,
         o_ref, sfin_ref, s_scr) = refs
    else:
        qkv_ref, ab_ref, cw_ref, alog_ref, dtb_ref, s0_ref, o_ref, sfin_ref, s_scr = refs
    n = pl.program_id(1)

    @pl.when(n == 0)
    def _():
        s_scr[...] = s0_ref[0]

    tb = qkv_ref.shape[1]
    qkv = _silu(_conv3_block(qkv_ref[0], cw_ref[...], row_len))
    ab = ab_ref[0]
    log_a = -jnp.exp(alog_ref[...]) * jax.nn.softplus(ab + dtb_ref[...])
    beta = jax.nn.sigmoid(ab)
    incl, strict, eye = _chunk_masks(reverse)
    tri = incl.astype(F32)
    last = 0 if reverse else CHUNK - 1
    hd = GDN_DK
    nchunk = tb // CHUNK
    order = list(range(nchunk - 1, -1, -1) if reverse else range(nchunk))
    rows = lambda c: slice(c * CHUNK, (c + 1) * CHUNK)
    cum_all = _dot(tri, jnp.concatenate([log_a[rows(c)] for c in range(nchunk)], axis=1), HI)

    items = [(c, h) for c in order for h in range(GDN_HEADS)]
    pre = {}
    for c, h in items:
        col = dirn * GDN_HEADS + h
        cum = cum_all[:, c * LANES + col:c * LANES + col + 1]
        bet = beta[rows(c), 2 * GDN_HEADS + col:2 * GDN_HEADS + col + 1]
        q = _l2norm(qkv[rows(c), h * hd:(h + 1) * hd]) * hd ** -0.5
        k = _l2norm(qkv[rows(c), GROUP + h * hd:GROUP + (h + 1) * hd])
        v = qkv[rows(c), 2 * GROUP + h * hd:2 * GROUP + (h + 1) * hd]
        cum_row = jnp.sum(eye * cum, axis=0, keepdims=True)
        decay = jnp.exp(jnp.where(incl, cum - cum_row, -jnp.inf))
        kb = k.astype(BF16)
        a = jnp.where(strict, bet * _dot_nt(kb, kb) * decay, 0.0)
        ecum = jnp.exp(cum)
        total = cum[last:last + 1]
        pre[c, h] = dict(
            x=-a, rhs=jnp.concatenate([v * bet, k * (bet * ecum)], axis=1),
            a_qk=(_dot_nt(q.astype(BF16), kb) * decay).astype(BF16),
            qe=(q * ecum).astype(BF16), k_end=(k * jnp.exp(total - cum)).astype(BF16),
            dec=jnp.exp(total))
    for p in pre.values():
        p['inv'] = eye + p['x']
    for level in range(6):
        for p in pre.values():
            x_hi, x_lo = _split_bf16(p['x'])
            if level == 0:
                p['x'] = _dot3(x_hi, x_lo, x_hi, x_lo)
            elif level < 5:
                i_hi, i_lo = _split_bf16(p['inv'])
                both = _dot3(jnp.concatenate([x_hi, i_hi], axis=0), jnp.concatenate([x_lo, i_lo], axis=0),
                             x_hi, x_lo)
                p['x'] = both[:CHUNK]
                p['inv'] = p['inv'] + both[CHUNK:]
            else:
                i_hi, i_lo = _split_bf16(p['inv'])
                p['inv'] = p['inv'] + _dot3(i_hi, i_lo, x_hi, x_lo)
    for p in pre.values():
        i_hi, i_lo = _split_bf16(p['inv'])
        r_hi, r_lo = _split_bf16(p['rhs'])
        sol = _dot3(i_hi, i_lo, r_hi, r_lo)
        p['u'], p['w'] = sol[:, :GDN_DV], sol[:, GDN_DV:].astype(BF16)

    for c, h in items:
        p = pre[c, h]
        s = s_scr[h]
        sb = s.astype(BF16)
        v_new = p['u'] - _dot(p['w'], sb)
        vb = v_new.astype(BF16)
        o = _dot(p['qe'], sb) + _dot(p['a_qk'], vb)
        s_scr[h] = p['dec'] * s + _dot_tn(p['k_end'], vb)
        cols = slice(h * GDN_DV, (h + 1) * GDN_DV)
        if finalize:
            o = o + ob_ref[0, rows(c), cols]
            o = o * lax.rsqrt(jnp.mean(o * o, axis=-1, keepdims=True) + EPS) * g_ref[...]
            o = o * _silu(z_ref[0, rows(c), cols])
        o_ref[0, rows(c), cols] = o.astype(o_ref.dtype)

    @pl.when(n == pl.num_programs(1) - 1)
    def _():
        sfin_ref[0] = s_scr[...]


def _gdn_pass(proj, o_other, s0, lp, *, dirn, row_len):
    b, t, _ = proj.shape
    tb = min(TOKEN_BLOCK, t)
    assert t % tb == 0 and tb % row_len == 0
    nblk = t // tb
    reverse = dirn == 1
    finalize = o_other is not None
    tok = (lambda i, n: (i, nblk - 1 - n)) if reverse else (lambda i, n: (i, n))
    seg = lambda name, wd: pl.BlockSpec((1, tb, wd), lambda i, n: tok(i, n) + (_col_block(name),))
    full = lambda a: pl.BlockSpec(a.shape, lambda i, n: (0,) * a.ndim)
    lane_row = lambda vals: jnp.pad(vals.reshape(1, -1), ((0, 0), (0, LANES - vals.size)))
    cw = lp['gdn_conv']
    alog = lane_row(lp['gdn_a_log'])
    dtb = lane_row(lp['gdn_dt_bias'])
    g = lp['gdn_norm_g'].reshape(1, GDN_DV)
    state_spec = pl.BlockSpec((1, GDN_HEADS, GDN_DK, GDN_DV), lambda i, n: (i, 0, 0, 0))
    if finalize:
        args = (proj, proj, proj, o_other, cw, alog, dtb, g, s0)
        in_specs = [seg('d_qkv', 3 * GROUP), seg('d_ab', LANES), seg('d_z', GROUP),
                    pl.BlockSpec((1, tb, GROUP), lambda i, n: tok(i, n) + (0,)),
                    full(cw), full(alog), full(dtb), full(g), state_spec]
    else:
        args = (proj, proj, cw, alog, dtb, s0)
        in_specs = [seg('d_qkv', 3 * GROUP), seg('d_ab', LANES), full(cw), full(alog), full(dtb), state_spec]
    return pl.pallas_call(
        functools.partial(_gdn_kernel, reverse=reverse, finalize=finalize, row_len=row_len, dirn=dirn),
        grid=(b, nblk),
        in_specs=in_specs,
        out_specs=[pl.BlockSpec((1, tb, GROUP), lambda i, n: tok(i, n) + (0,)), state_spec],
        out_shape=[jax.ShapeDtypeStruct((b, t, GROUP), BF16 if finalize else F32),
                   jax.ShapeDtypeStruct((b, GDN_HEADS, GDN_DK, GDN_DV), F32)],
        scratch_shapes=[pltpu.VMEM((GDN_HEADS, GDN_DK, GDN_DV), F32)],
        compiler_params=pltpu.CompilerParams(
            dimension_semantics=("arbitrary", "arbitrary"), vmem_limit_bytes=VMEM_LIMIT),
        name="gdn_fwd" if finalize else "gdn_bwd",
    )(*args)


def _gla_kernel(*refs, reverse, finalize):
    if finalize:
        (q_ref, k_ref, v_ref, lo_ref, r_ref, ob_ref, up_ref, ab_ref, g_ref, s0_ref,
         o_ref, sfin_ref, s_scr) = refs
    else:
        q_ref, k_ref, v_ref, lo_ref, up_ref, ab_ref, s0_ref, o_ref, sfin_ref, s_scr = refs
    n = pl.program_id(1)

    @pl.when(n == 0)
    def _():
        s_scr[...] = s0_ref[0]

    tb = q_ref.shape[1]
    gate = _dot(lo_ref[0].astype(BF16), up_ref[...].astype(BF16)) + ab_ref[...]
    log_f = jax.nn.log_sigmoid(gate) / GLA_TAU
    incl, _, eye = _chunk_masks(reverse)
    tri = incl.astype(F32)
    last = 0 if reverse else CHUNK - 1
    nchunk = tb // CHUNK
    order = range(nchunk - 1, -1, -1) if reverse else range(nchunk)
    for c in order:
        sl = slice(c * CHUNK, (c + 1) * CHUNK)
        cum = _dot(tri, log_f[sl], HI)
        total = cum[last:last + 1]
        q_dec = q_ref[0, sl, :] * GLA_DK ** -0.5 * jnp.exp(cum)
        k = k_ref[0, sl, :]
        k_inv = (k * jnp.exp(-cum)).astype(BF16)
        k_end = (k * jnp.exp(total - cum)).astype(BF16)
        q_dec = q_dec.astype(BF16)
        dec_row = jnp.exp(total)
        for h in range(GLA_HEADS):
            kc = slice(h * GLA_DK, (h + 1) * GLA_DK)
            vc = slice(h * GLA_DV, (h + 1) * GLA_DV)
            vb = v_ref[0, sl, vc].astype(BF16)
            att = jnp.where(incl, _dot_nt(q_dec[:, kc], k_inv[:, kc]), 0.0)
            s = s_scr[h]
            o = _dot(att.astype(BF16), vb) + _dot(q_dec[:, kc], s.astype(BF16))
            dec_col = jnp.sum(eye * dec_row[:, kc], axis=1, keepdims=True)
            s_scr[h] = dec_col * s + _dot_tn(k_end[:, kc], vb)
            if finalize:
                o = o + ob_ref[0, sl, vc]
                o = o * lax.rsqrt(jnp.mean(o * o, axis=-1, keepdims=True) + EPS) * g_ref[...]
                o = o * _silu(r_ref[0, sl, vc])
            o_ref[0, sl, vc] = o.astype(o_ref.dtype)

    @pl.when(n == pl.num_programs(1) - 1)
    def _():
        sfin_ref[0] = s_scr[...]


def _gla_pass(proj, o_other, s0, lp, *, dirn):
    b, t, _ = proj.shape
    tb = min(TOKEN_BLOCK, t)
    assert t % tb == 0
    nblk = t // tb
    reverse = dirn == 1
    finalize = o_other is not None
    tok = (lambda i, n: (i, nblk - 1 - n)) if reverse else (lambda i, n: (i, n))
    seg = lambda name: pl.BlockSpec((1, tb, SEG[name][2]), lambda i, n: tok(i, n) + (_col_block(name),))
    full = lambda a: pl.BlockSpec(a.shape, lambda i, n: (0,) * a.ndim)
    hk = GLA_HEADS * GLA_DK
    up = jnp.pad(lp['gla_a_up'][dirn], ((0, LANES - GLA_LR), (0, 0)))
    ab = lp['gla_a_b'][dirn].reshape(1, hk)
    g = lp['gla_norm_g'].reshape(1, GLA_DV)
    state_spec = pl.BlockSpec((1, GLA_HEADS, GLA_DK, GLA_DV), lambda i, n: (i, 0, 0, 0))
    if finalize:
        args = (proj, proj, proj, proj, proj, o_other, up, ab, g, s0)
        in_specs = [seg('g_q'), seg('g_k'), seg('g_v'), seg('g_lo'), seg('g_r'),
                    pl.BlockSpec((1, tb, GROUP), lambda i, n: tok(i, n) + (0,)),
                    full(up), full(ab), full(g), state_spec]
    else:
        args = (proj, proj, proj, proj, up, ab, s0)
        in_specs = [seg('g_q'), seg('g_k'), seg('g_v'), seg('g_lo'), full(up), full(ab), state_spec]
    return pl.pallas_call(
        functools.partial(_gla_kernel, reverse=reverse, finalize=finalize),
        grid=(b, nblk),
        in_specs=in_specs,
        out_specs=[pl.BlockSpec((1, tb, GROUP), lambda i, n: tok(i, n) + (0,)), state_spec],
        out_shape=[jax.ShapeDtypeStruct((b, t, GROUP), BF16 if finalize else F32),
                   jax.ShapeDtypeStruct((b, GLA_HEADS, GLA_DK, GLA_DV), F32)],
        scratch_shapes=[pltpu.VMEM((GLA_HEADS, GLA_DK, GLA_DV), F32)],
        compiler_params=pltpu.CompilerParams(
            dimension_semantics=("arbitrary", "arbitrary"), vmem_limit_bytes=VMEM_LIMIT),
        name="gla_fwd" if finalize else "gla_bwd",
    )(*args)


def _two_direction_group(pass_fn, state_shape, proj_c, proj_x, **kw):
    zero = jnp.zeros((proj_x[0].shape[0],) + state_shape, F32)
    ob_c, sb_c = pass_fn(proj_c[0], None, zero, dirn=1, **proj_c[1], **kw)
    ob_x, _ = pass_fn(proj_x[0], None, sb_c, dirn=1, **proj_x[1], **kw)
    y_c, sf_c = pass_fn(proj_c[0], ob_c, zero, dirn=0, **proj_c[1], **kw)
    y_x, _ = pass_fn(proj_x[0], ob_x, sf_c, dirn=0, **proj_x[1], **kw)
    return y_c, y_x


def _gla_group_pallas(proj_c, proj_x, lp):
    return _two_direction_group(_gla_pass, (GLA_HEADS, GLA_DK, GLA_DV), (proj_c, {}), (proj_x, {}), lp=lp)


def _gdn_group_pallas(proj_c, proj_x, lp):
    return _two_direction_group(_gdn_pass, (GDN_HEADS, GDN_DK, GDN_DV),
                                (proj_c, dict(row_len=proj_c.shape[1])), (proj_x, dict(row_len=GRID_W)), lp=lp)


def _sc_kernel(b_ref, c_ref, h_ref, cw_ref, o_ref, *, row_len):
    o_ref[0] = (b_ref[0] * _conv3_block(c_ref[0] * h_ref[0], cw_ref[...], row_len)).astype(o_ref.dtype)


def _sc_group(proj, row_len, lp):
    b, t, _ = proj.shape
    tb = min(TOKEN_BLOCK, t)
    assert t % tb == 0 and tb % row_len == 0
    seg = lambda name: pl.BlockSpec((1, tb, GROUP), lambda i, n: (i, n, _col_block(name)))
    cw = lp['sc_conv']
    return pl.pallas_call(
        functools.partial(_sc_kernel, row_len=row_len),
        grid=(b, t // tb),
        in_specs=[seg('c_b'), seg('c_c'), seg('c_h'), pl.BlockSpec(cw.shape, lambda i, n: (0, 0))],
        out_specs=pl.BlockSpec((1, tb, GROUP), lambda i, n: (i, n, 0)),
        out_shape=jax.ShapeDtypeStruct((b, t, GROUP), BF16),
        compiler_params=pltpu.CompilerParams(
            dimension_semantics=("arbitrary", "arbitrary"), vmem_limit_bytes=VMEM_LIMIT),
        name="short_conv",
    )(proj, proj, proj, cw)


def _mixers(proj_c, proj_x, lp):
    tc = proj_c.shape[1]
    y_gla_c, y_gla_x = _gla_group_pallas(proj_c, proj_x, lp)
    y_gdn_c, y_gdn_x = _gdn_group_pallas(proj_c, proj_x, lp)
    y_sc_c = _sc_group(proj_c, tc, lp)
    y_sc_x = _sc_group(proj_x, GRID_W, lp)
    y_rw_c, y_rw_x = _rw_group(proj_c, proj_x, lp)
    return (y_gla_c, y_gdn_c, y_sc_c, y_rw_c), (y_gla_x, y_gdn_x, y_sc_x, y_rw_x)


def kernel(x, c, ctx, c_ctx, ada_w, ada_b, norm1_g, norm2_g, w_in, w_out, gla_a_up, gla_a_b, gla_norm_g, gdn_conv, gdn_a_log, gdn_dt_bias, gdn_norm_g, sc_conv, rw_mu_rkv, rw_mu_wa, rw_w0, rw_w2, rw_a0, rw_a2, rw_g2, rw_kk, rw_ka, rw_rk, rw_gn_w, rw_gn_b, ffn_w_gu, ffn_w_down, final_g):
    nb, t, d = x.shape
    tc = ctx.shape[1]
    depth = w_in.shape[0]
    assert d == D_MODEL and t % 512 == 0 and tc % CHUNK == 0 and nb + 1 <= SUBLANES

    cond = jnp.concatenate([c, c_ctx[None, :], jnp.zeros((SUBLANES - nb - 1, d), F32)], axis=0)
    mod = _modulation(cond, ada_w, ada_b)

    tm_x = 512
    tm_c = tc
    h_x, h_c = x, ctx
    for l in range(depth):
        lp = dict(gla_a_up=gla_a_up[l], gla_a_b=gla_a_b[l],
                  gla_norm_g=gla_norm_g[l], gdn_conv=gdn_conv[l], gdn_a_log=gdn_a_log[l],
                  gdn_dt_bias=gdn_dt_bias[l], gdn_norm_g=gdn_norm_g[l], sc_conv=sc_conv[l],
                  rw_mu_rkv=rw_mu_rkv[l], rw_mu_wa=rw_mu_wa[l], rw_w0=rw_w0[l], rw_w2=rw_w2[l],
                  rw_a0=rw_a0[l], rw_a2=rw_a2[l], rw_g2=rw_g2[l], rw_kk=rw_kk[l], rw_ka=rw_ka[l],
                  rw_rk=rw_rk[l], rw_gn_w=rw_gn_w[l], rw_gn_b=rw_gn_b[l])
        m_x = [mod[l, :nb, i * d:(i + 1) * d][:, None, :] for i in range(6)]
        m_c = [jnp.broadcast_to(mod[l, nb, i * d:(i + 1) * d][None, None, :], (nb, 1, d)) for i in range(6)]
        w_in_l = _pad_w_in(w_in[l])
        w_out_l = jnp.concatenate([w_out[l][:3 * GROUP], w_out[l][3 * GROUP:][_rw_channel_perm()]],
                                  axis=0).astype(BF16)
        w_gu_l = ffn_w_gu[l].astype(BF16)
        w_dn_l = ffn_w_down[l].astype(BF16)

        proj_c = _norm_proj(h_c, norm1_g[l], m_c[1], m_c[0], w_in_l, tm_c, 1024)
        proj_x = _norm_proj(h_x, norm1_g[l], m_x[1], m_x[0], w_in_l, 2 * tm_x, 1024)
        y_c, y_x = _mixers(proj_c, proj_x, lp)

        h_x = _groups_residual(y_x, w_out_l, h_x, m_x[2], 2 * tm_x, 1024)
        a_x = _norm_swiglu(h_x, norm2_g[l], m_x[4], m_x[3], w_gu_l, 2 * tm_x, 512)
        h_x = _proj_residual(a_x, w_dn_l, h_x, m_x[5], 2 * tm_x, 512)
        if l < depth - 1:
            h_c = _groups_residual(y_c, w_out_l, h_c, m_c[2], tm_c, 512)
            a_c = _norm_swiglu(h_c, norm2_g[l], m_c[4], m_c[3], w_gu_l, tm_c, 512)
            h_c = _proj_residual(a_c, w_dn_l, h_c, m_c[5], tm_c, 512)
    return _final_norm(h_x, final_g, tm_x)
```

```python
import functools
import math

import numpy as np
import jax
import jax.numpy as jnp
from jax import lax
from jax.experimental import pallas as pl
from jax.experimental.pallas import tpu as pltpu

F32 = jnp.float32
BF16 = jnp.bfloat16

D_MODEL = 2048
GROUP = D_MODEL // 4
CHUNK = 64
EPS = 1e-6
GRID_W = 64

GLA_HEADS = 4
GLA_DK = 64
GLA_DV = 128
GLA_LR = 16
GLA_TAU = 16.0
GDN_HEADS = 4
GDN_DK = 128
GDN_DV = 128
RW_HEAD = 64
RW_HEADS = GROUP // RW_HEAD
RW_DECAY_LR = 32
RW_A_LR = 32
RW_GATE_LR = 96
RW_DECAY_SCALE = math.exp(-0.5)
RW_GN_EPS = 64e-5
FFN = -(-(8 * D_MODEL) // (3 * 256)) * 256

SUBLANES = 8
LANES = 128
VMEM_LIMIT = 56 * 1024 * 1024

_SRC_SEGS = (
    ('g_q', 256), ('g_k', 256), ('g_v', 512), ('g_r', 512), ('g_lo', GLA_LR),
    ('d_qkv', 1536), ('d_z', 512), ('d_ab', 16),
    ('c_b', 512), ('c_c', 512), ('c_h', 512),
    ('r_rkv', 1536), ('r_wa', 64), ('r_g', RW_GATE_LR),
)
_DST_ORDER = ('g_q', 'g_k', 'g_v', 'g_r', 'd_qkv', 'r_rkv', 'd_z', 'c_b', 'c_c', 'c_h',
              'g_lo', 'd_ab', 'r_wa', 'r_g')


def _round_up(n, m):
    return -(-n // m) * m


def _seg_layout():
    width = dict(_SRC_SEGS)
    src, o = {}, 0
    for name, wd in _SRC_SEGS:
        src[name] = o
        o += wd
    dst, o = {}, 0
    for name in _DST_ORDER:
        pw = _round_up(width[name], LANES)
        assert o % pw == 0, name
        dst[name] = (o, width[name], pw)
        o += pw
    return src, dst, o


SEG_SRC, SEG, PROJ_PAD = _seg_layout()


def _pad_w_in(w):
    parts = []
    for name in _DST_ORDER:
        _, wd, pw = SEG[name]
        seg = w[:, SEG_SRC[name]:SEG_SRC[name] + wd]
        if name == 'r_rkv':
            perm = _rw_channel_perm()
            seg = jnp.concatenate([seg[:, i * GROUP:(i + 1) * GROUP][:, perm] for i in range(3)], axis=1)
        if pw != wd:
            seg = jnp.pad(seg, ((0, 0), (0, pw - wd)))
        parts.append(seg)
    return jnp.concatenate(parts, axis=1).astype(BF16)


def _col_block(name):
    o, _, pw = SEG[name]
    return o // pw


def _mod_kernel(c_ref, w_ref, b_ref, o_ref):
    x = c_ref[...]
    x = (x * jax.nn.sigmoid(x)).astype(BF16)
    o_ref[0] = jnp.dot(x, w_ref[0].astype(BF16), preferred_element_type=F32) + b_ref[0]


def _modulation(cond, ada_w, ada_b):
    nl, d, n = ada_w.shape
    tn = 1024
    return pl.pallas_call(
        _mod_kernel,
        grid=(nl, n // tn),
        in_specs=[pl.BlockSpec((SUBLANES, d), lambda l, j: (0, 0)),
                  pl.BlockSpec((1, d, tn), lambda l, j: (l, 0, j)),
                  pl.BlockSpec((1, 1, tn), lambda l, j: (l, 0, j))],
        out_specs=pl.BlockSpec((1, SUBLANES, tn), lambda l, j: (l, 0, j)),
        out_shape=jax.ShapeDtypeStruct((nl, SUBLANES, n), F32),
        compiler_params=pltpu.CompilerParams(
            dimension_semantics=("arbitrary", "arbitrary"), vmem_limit_bytes=VMEM_LIMIT),
        name="modulation",
    )(cond, ada_w, ada_b.reshape(nl, 1, n))


def _modnorm_rows(h_ref, g_ref, sc_ref, sh_ref):
    x = h_ref[0]
    y = x * lax.rsqrt(jnp.mean(x * x, axis=-1, keepdims=True) + EPS) * g_ref[...]
    return (y * (1.0 + sc_ref[0]) + sh_ref[0]).astype(BF16)


def _norm_proj_kernel(h_ref, g_ref, sc_ref, sh_ref, w_ref, o_ref, a_scr):
    @pl.when(pl.program_id(2) == 0)
    def _():
        a_scr[...] = _modnorm_rows(h_ref, g_ref, sc_ref, sh_ref)

    o_ref[0] = jnp.dot(a_scr[...], w_ref[...], preferred_element_type=F32).astype(o_ref.dtype)


def _norm_proj(h, g, scale, shift, w, tm, tn):
    b, t, d = h.shape
    n = w.shape[1]
    return pl.pallas_call(
        _norm_proj_kernel,
        grid=(b, t // tm, n // tn),
        in_specs=[pl.BlockSpec((1, tm, d), lambda i, m, j: (i, m, 0)),
                  pl.BlockSpec((1, d), lambda i, m, j: (0, 0)),
                  pl.BlockSpec((1, 1, d), lambda i, m, j: (i, 0, 0)),
                  pl.BlockSpec((1, 1, d), lambda i, m, j: (i, 0, 0)),
                  pl.BlockSpec((d, tn), lambda i, m, j: (0, j))],
        out_specs=pl.BlockSpec((1, tm, tn), lambda i, m, j: (i, m, j)),
        out_shape=jax.ShapeDtypeStruct((b, t, n), F32),
        scratch_shapes=[pltpu.VMEM((tm, d), BF16)],
        compiler_params=pltpu.CompilerParams(
            dimension_semantics=("arbitrary", "arbitrary", "arbitrary"), vmem_limit_bytes=VMEM_LIMIT),
        name="norm_proj",
    )(h, g.reshape(1, d), scale, shift, w)


def _norm_swiglu_kernel(h_ref, g_ref, sc_ref, sh_ref, wg_ref, wu_ref, o_ref, a_scr):
    @pl.when(pl.program_id(2) == 0)
    def _():
        a_scr[...] = _modnorm_rows(h_ref, g_ref, sc_ref, sh_ref)

    a = a_scr[...]
    gate = jnp.dot(a, wg_ref[...], preferred_element_type=F32)
    up = jnp.dot(a, wu_ref[...], preferred_element_type=F32)
    o_ref[0] = (gate * jax.nn.sigmoid(gate) * up).astype(o_ref.dtype)


def _norm_swiglu(h, g, scale, shift, w_gu, tm, tn):
    b, t, d = h.shape
    f = w_gu.shape[1] // 2
    nj = f // tn
    return pl.pallas_call(
        _norm_swiglu_kernel,
        grid=(b, t // tm, nj),
        in_specs=[pl.BlockSpec((1, tm, d), lambda i, m, j: (i, m, 0)),
                  pl.BlockSpec((1, d), lambda i, m, j: (0, 0)),
                  pl.BlockSpec((1, 1, d), lambda i, m, j: (i, 0, 0)),
                  pl.BlockSpec((1, 1, d), lambda i, m, j: (i, 0, 0)),
                  pl.BlockSpec((d, tn), lambda i, m, j: (0, j)),
                  pl.BlockSpec((d, tn), lambda i, m, j: (0, j + nj))],
        out_specs=pl.BlockSpec((1, tm, tn), lambda i, m, j: (i, m, j)),
        out_shape=jax.ShapeDtypeStruct((b, t, f), BF16),
        scratch_shapes=[pltpu.VMEM((tm, d), BF16)],
        compiler_params=pltpu.CompilerParams(
            dimension_semantics=("arbitrary", "arbitrary", "arbitrary"), vmem_limit_bytes=VMEM_LIMIT),
        name="norm_swiglu",
    )(h, g.reshape(1, d), scale, shift, w_gu, w_gu)


def _proj_residual_kernel(a_ref, w_ref, h_ref, gate_ref, o_ref):
    o_ref[0] = h_ref[0] + gate_ref[0] * jnp.dot(a_ref[0], w_ref[...], preferred_element_type=F32)


def _proj_residual(a, w, h, gate, tm, tn):
    b, t, k = a.shape
    d = w.shape[1]
    return pl.pallas_call(
        _proj_residual_kernel,
        grid=(b, t // tm, d // tn),
        in_specs=[pl.BlockSpec((1, tm, k), lambda i, m, j: (i, m, 0)),
                  pl.BlockSpec((k, tn), lambda i, m, j: (0, j)),
                  pl.BlockSpec((1, tm, tn), lambda i, m, j: (i, m, j)),
                  pl.BlockSpec((1, 1, tn), lambda i, m, j: (i, 0, j))],
        out_specs=pl.BlockSpec((1, tm, tn), lambda i, m, j: (i, m, j)),
        out_shape=jax.ShapeDtypeStruct((b, t, d), F32),
        compiler_params=pltpu.CompilerParams(
            dimension_semantics=("arbitrary", "arbitrary", "arbitrary"), vmem_limit_bytes=VMEM_LIMIT),
        name="proj_residual",
    )(a, w, h, gate)


def _groups_residual_kernel(*refs):
    *a_refs, w_ref, h_ref, gate_ref, o_ref = refs
    k = a_refs[0].shape[2]
    acc = jnp.dot(a_refs[0][0], w_ref[0:k, :], preferred_element_type=F32)
    for i, a_ref in enumerate(a_refs[1:], start=1):
        acc = acc + jnp.dot(a_ref[0], w_ref[i * k:(i + 1) * k, :], preferred_element_type=F32)
    o_ref[0] = h_ref[0] + gate_ref[0] * acc


def _groups_residual(groups, w, h, gate, tm, tn):
    b, t, k = groups[0].shape
    d = w.shape[1]
    assert len(groups) * k == w.shape[0]
    return pl.pallas_call(
        _groups_residual_kernel,
        grid=(b, t // tm, d // tn),
        in_specs=[pl.BlockSpec((1, tm, k), lambda i, m, j: (i, m, 0)) for _ in groups]
        + [pl.BlockSpec((w.shape[0], tn), lambda i, m, j: (0, j)),
           pl.BlockSpec((1, tm, tn), lambda i, m, j: (i, m, j)),
           pl.BlockSpec((1, 1, tn), lambda i, m, j: (i, 0, j))],
        out_specs=pl.BlockSpec((1, tm, tn), lambda i, m, j: (i, m, j)),
        out_shape=jax.ShapeDtypeStruct((b, t, d), F32),
        compiler_params=pltpu.CompilerParams(
            dimension_semantics=("arbitrary", "arbitrary", "arbitrary"), vmem_limit_bytes=VMEM_LIMIT),
        name="groups_residual",
    )(*groups, w, h, gate)


def _final_norm_kernel(h_ref, g_ref, o_ref):
    x = h_ref[0]
    o_ref[0] = x * lax.rsqrt(jnp.mean(x * x, axis=-1, keepdims=True) + EPS) * g_ref[...]


def _final_norm(h, g, tm):
    b, t, d = h.shape
    return pl.pallas_call(
        _final_norm_kernel,
        grid=(b, t // tm),
        in_specs=[pl.BlockSpec((1, tm, d), lambda i, m: (i, m, 0)),
                  pl.BlockSpec((1, d), lambda i, m: (0, 0))],
        out_specs=pl.BlockSpec((1, tm, d), lambda i, m: (i, m, 0)),
        out_shape=jax.ShapeDtypeStruct((b, t, d), F32),
        compiler_params=pltpu.CompilerParams(
            dimension_semantics=("arbitrary", "arbitrary"), vmem_limit_bytes=VMEM_LIMIT),
        name="final_norm",
    )(h, g.reshape(1, d))


RW_CHAINS = 32
RW_KLO = LANES // RW_CHAINS
RW_KQ = RW_HEAD // RW_KLO
RW_VM = RW_HEAD // SUBLANES
RW_VGROUP = 4
RW_STEP_BLOCK = 16
OP_W, OP_B, OP_K, OP_R, OP_KK = range(5)


def _lane_group_sum(p):
    out = p
    for g in range(1, RW_KLO):
        out = out + pltpu.roll(p, g * RW_CHAINS, 1)
    return out


def _rwkv_kernel(opsf_ref, opsb_ref, vf_ref, vb_ref, s0_ref, yf_ref, yb_ref, sfin_ref,
                 s_ref, saf_ref, sab_ref, pb_ref):
    tb = opsf_ref.shape[0]
    dirs = ((opsf_ref, vf_ref, yf_ref), (opsb_ref, vb_ref, yb_ref))

    def row(ops_ref, t, j, q):
        lane0 = (j * RW_KQ + q) * LANES
        return ops_ref[t:t + 1, lane0:lane0 + LANES]

    @pl.when(pl.program_id(0) == 0)
    def _():
        s_ref[...] = s0_ref[...]

    zero = jnp.zeros((SUBLANES, LANES), F32)
    groups = [range(m0, m0 + RW_VGROUP) for m0 in range(0, RW_VM, RW_VGROUP)]

    for d, (ops_ref, _, _) in enumerate(dirs):
        first = 0 if d == 0 else tb - 1
        for ms in groups:
            acc = {m: zero for m in ms}
            for q in range(RW_KQ):
                kk = row(ops_ref, first, OP_KK, q)
                for m in ms:
                    acc[m] = acc[m] + s_ref[d, q, m] * kk
            for m in ms:
                if d == 0:
                    saf_ref[m] = _lane_group_sum(acc[m])
                else:
                    pb_ref[m] = acc[m]

    def advance(d, i, sa_ref, next_ref, reduce_next):
        ops_ref, v_ref, y_ref = dirs[d]
        t = i if d == 0 else tb - 1 - i
        tn = min(i + 1, tb - 1) if d == 0 else max(tb - 2 - i, 0)
        for ms in groups:
            sa = {m: sa_ref[m] for m in ms}
            vt = {m: v_ref[t, m] for m in ms}
            acc_y = {m: zero for m in ms}
            acc_s = {m: zero for m in ms}
            for q in range(RW_KQ):
                w = row(ops_ref, t, OP_W, q)
                b = row(ops_ref, t, OP_B, q)
                kt = row(ops_ref, t, OP_K, q)
                r = row(ops_ref, t, OP_R, q)
                kkn = row(ops_ref, tn, OP_KK, q)
                for m in ms:
                    s = s_ref[d, q, m] * w - sa[m] * b + vt[m] * kt
                    s_ref[d, q, m] = s
                    acc_y[m] = acc_y[m] + s * r
                    acc_s[m] = acc_s[m] + s * kkn
            for m in ms:
                y_ref[t, m] = acc_y[m]
                next_ref[m] = _lane_group_sum(acc_s[m]) if reduce_next else acc_s[m]

    for i in range(tb):
        for m in range(RW_VM):
            sab_ref[m] = _lane_group_sum(pb_ref[m])
        advance(0, i, saf_ref, saf_ref, True)
        advance(1, i, sab_ref, pb_ref, False)

    @pl.when(pl.program_id(0) == pl.num_programs(0) - 1)
    def _():
        sfin_ref[...] = s_ref[...]


RW_STATE_SHAPE = (2, RW_KQ, RW_VM, SUBLANES, LANES)


def _rwkv_scan(ops_f, ops_b, v_f, v_b, s0):
    t = ops_f.shape[0]
    tb = RW_STEP_BLOCK
    nblk = t // tb
    fwd = lambda i: (i, 0, 0, 0)
    bwd = lambda i: (nblk - 1 - i, 0, 0, 0)
    ops_block = (tb, 5 * RW_KQ * LANES)
    v_block = (tb, RW_VM, SUBLANES, LANES)
    y_shape = jax.ShapeDtypeStruct((t, RW_VM, SUBLANES, LANES), F32)
    state_spec = pl.BlockSpec(RW_STATE_SHAPE, lambda i: (0,) * len(RW_STATE_SHAPE))
    return pl.pallas_call(
        _rwkv_kernel,
        grid=(nblk,),
        in_specs=[pl.BlockSpec(ops_block, lambda i: (i, 0)), pl.BlockSpec(ops_block, lambda i: (nblk - 1 - i, 0)),
                  pl.BlockSpec(v_block, fwd), pl.BlockSpec(v_block, bwd), state_spec],
        out_specs=[pl.BlockSpec(v_block, fwd), pl.BlockSpec(v_block, bwd), state_spec],
        out_shape=[y_shape, y_shape, jax.ShapeDtypeStruct(RW_STATE_SHAPE, F32)],
        scratch_shapes=[pltpu.VMEM(RW_STATE_SHAPE, F32)]
        + [pltpu.VMEM((RW_VM, SUBLANES, LANES), F32)] * 3,
        compiler_params=pltpu.CompilerParams(
            dimension_semantics=("arbitrary",), vmem_limit_bytes=VMEM_LIMIT),
        name="rwkv_scan",
    )(ops_f, ops_b, v_f, v_b, s0)


RW_PREP_BLOCK = 128
RW_PREP_VMEM_LIMIT = 56 * 1024 * 1024
RW_NPIECE = 3


def _rw_channel_perm():
    return np.arange(GROUP).reshape(RW_HEADS, RW_HEAD).T.reshape(-1)


def _rw_relayout_matrices(nb):
    slab = LANES
    ch_per_slab = slab // RW_HEADS
    pk = np.zeros((nb, RW_NPIECE, slab, ch_per_slab // RW_KLO, RW_KLO, nb, RW_HEADS), np.float32)
    pv = np.zeros((nb, RW_NPIECE, slab, ch_per_slab, RW_KLO, nb, RW_HEADS), np.float32)
    for b in range(nb):
        for ch in range(ch_per_slab):
            for h in range(RW_HEADS):
                pk[b, :, ch * RW_HEADS + h, ch // RW_KLO, ch % RW_KLO, b, h] = 1.0
                pv[b, :, ch * RW_HEADS + h, ch, :, b, h] = 1.0
    rows = nb * RW_NPIECE * slab
    return (jnp.asarray(pk.reshape(rows, -1), BF16), jnp.asarray(pv.reshape(rows, -1), BF16))


def _split3(x):
    hi = x.astype(BF16)
    r1 = x - hi.astype(F32)
    mid = r1.astype(BF16)
    lo = (r1 - mid.astype(F32)).astype(BF16)
    return hi, mid, lo


def _rw_prep_kernel(rkv_ref, wa_ref, rkv_nb_ref, wa_nb_ref, mu_rkv_ref, mu_wa_ref, w0_ref, w2_ref,
                    a0_ref, a2_ref, kk_ref, ka_ref, rk_ref, hsum_ref, pk_ref, pv_ref,
                    ops_ref, v_ref, bonus_ref, *, backward):
    nb, tb, _ = rkv_ref.shape
    n = pl.program_id(0)
    edge = (n == pl.num_programs(0) - 1) if backward else (n == 0)
    nb_row = SUBLANES - 1 if not backward else 0
    row = lax.broadcasted_iota(jnp.int32, (tb, 1), 0)
    edge_row = (row == tb - 1) if backward else (row == 0)

    def shifted(x, neighbour):
        inner = pltpu.roll(x, tb - 1 if backward else 1, 0)
        outer = jnp.where(edge, 0.0, neighbour)
        return jnp.where(edge_row, outer, inner)

    def head_sum(z):
        hi, lo = _split_bf16(z)
        return _dot(hi, hsum_ref[...]) + _dot(lo, hsum_ref[...])

    part = []
    for b in range(nb):
        x = rkv_ref[b]
        x = x + (shifted(x, rkv_nb_ref[b, nb_row:nb_row + 1, :]) - x) * mu_rkv_ref[...]
        xw = wa_ref[b]
        xw = xw + (shifted(xw, wa_nb_ref[b, nb_row:nb_row + 1, :]) - xw) * mu_wa_ref[...]
        r, k, v = x[:, :GROUP], x[:, GROUP:2 * GROUP], x[:, 2 * GROUP:]
        decay = jnp.exp(-RW_DECAY_SCALE * jax.nn.sigmoid(
            w0_ref[...] + _dot(jnp.tanh(xw).astype(BF16), w2_ref[...])))
        a = jax.nn.sigmoid(a0_ref[...] + _dot(xw.astype(BF16), a2_ref[...]))
        kq = k * kk_ref[...]
        kt = k * (1.0 + (a - 1.0) * ka_ref[...])
        part.append((decay, a, kq, kt, r, v))
    sums = head_sum(jnp.concatenate([p[2] * p[2] for p in part]
                                    + [p[4] * p[3] * rk_ref[...] for p in part], axis=0))
    feats = []
    for b, (decay, a, kq, kt, r, v) in enumerate(part):
        kk = kq * lax.rsqrt(sums[b * tb:(b + 1) * tb] + EPS)
        bonus_ref[b] = sums[(nb + b) * tb:(nb + b + 1) * tb] * v
        feats.append([decay, kk * a, kt, r, kk, v])

    nslab = GROUP // LANES
    kw = pk_ref.shape[1]
    vw = pv_ref.shape[1]
    for j in range(6):
        pieces = [_split3(feats[b][j]) for b in range(nb)]
        lhs = jnp.concatenate(
            [jnp.concatenate([p[:, g * LANES:(g + 1) * LANES] for bp in pieces for p in bp], axis=1)
             for g in range(nslab)], axis=0)
        if j < 5:
            out = _dot(lhs, pk_ref[...])
            for g in range(nslab):
                ops_ref[:, (j * nslab + g) * kw:(j * nslab + g + 1) * kw] = out[g * tb:(g + 1) * tb]
        else:
            out = _dot(lhs, pv_ref[...])
            for g in range(nslab):
                rows = vw // LANES
                v_ref[:, g * rows:(g + 1) * rows, :] = out[g * tb:(g + 1) * tb].reshape(tb, rows, LANES)


def _rw_prep(proj, lp, dirn):
    nb, t, _ = proj.shape
    tb = min(RW_PREP_BLOCK, t)
    assert t % tb == 0 and nb * RW_HEADS == RW_CHAINS
    nblk = t // tb
    backward = dirn == 1
    per8 = tb // SUBLANES
    last8 = t // SUBLANES - 1
    if backward:
        nb_idx = lambda n: jnp.minimum((n + 1) * per8, last8)
    else:
        nb_idx = lambda n: jnp.maximum(n * per8 - 1, 0)
    perm = _rw_channel_perm()
    lane_row = lambda v: v.reshape(1, -1)
    pad_rows = lambda w, lo: jnp.zeros((LANES, GROUP), F32).at[lo:lo + w.shape[0]].set(w)
    mu_rkv = lane_row(jnp.concatenate([lp['rw_mu_rkv'][dirn][i * GROUP:(i + 1) * GROUP][perm] for i in range(3)]))
    mu_wa = lane_row(jnp.pad(lp['rw_mu_wa'][dirn], (0, LANES - RW_DECAY_LR - RW_A_LR)))
    w0 = lane_row(lp['rw_w0'][dirn][perm])
    w2 = pad_rows(lp['rw_w2'][dirn][:, perm], 0).astype(BF16)
    a0 = lane_row(lp['rw_a0'][dirn][perm])
    a2 = pad_rows(lp['rw_a2'][dirn][:, perm], RW_DECAY_LR).astype(BF16)
    k_k = lane_row(lp['rw_kk'][perm])
    k_a = lane_row(lp['rw_ka'][perm])
    r_k = lane_row(lp['rw_rk'].reshape(-1)[perm])
    lane = np.arange(GROUP)
    hsum = jnp.asarray((lane[:, None] % RW_HEADS) == (lane[None, :] % RW_HEADS), BF16)
    pk, pv = _rw_relayout_matrices(nb)
    params = (mu_rkv, mu_wa, w0, w2, a0, a2, k_k, k_a, r_k, hsum, pk, pv)
    full = lambda a: pl.BlockSpec(a.shape, lambda n: (0,) * a.ndim)
    ops_w = 5 * RW_KQ * LANES
    ops, vt, bonus = pl.pallas_call(
        functools.partial(_rw_prep_kernel, backward=backward),
        grid=(nblk,),
        in_specs=[pl.BlockSpec((nb, tb, 3 * GROUP), lambda n: (0, n, _col_block('r_rkv'))),
                  pl.BlockSpec((nb, tb, LANES), lambda n: (0, n, _col_block('r_wa'))),
                  pl.BlockSpec((nb, SUBLANES, 3 * GROUP), lambda n: (0, nb_idx(n), _col_block('r_rkv'))),
                  pl.BlockSpec((nb, SUBLANES, LANES), lambda n: (0, nb_idx(n), _col_block('r_wa')))]
        + [full(p) for p in params],
        out_specs=[pl.BlockSpec((tb, ops_w), lambda n: (n, 0)),
                   pl.BlockSpec((tb, RW_HEAD, LANES), lambda n: (n, 0, 0)),
                   pl.BlockSpec((nb, tb, GROUP), lambda n: (0, n, 0))],
        out_shape=[jax.ShapeDtypeStruct((t, ops_w), F32), jax.ShapeDtypeStruct((t, RW_HEAD, LANES), F32),
                   jax.ShapeDtypeStruct((nb, t, GROUP), F32)],
        compiler_params=pltpu.CompilerParams(
            dimension_semantics=("arbitrary",), vmem_limit_bytes=RW_PREP_VMEM_LIMIT),
        name="rw_prep",
    )(proj, proj, proj, proj, *params)
    return ops, vt.reshape(t, RW_VM, SUBLANES, LANES), bonus


def _l2norm(x):
    return x * lax.rsqrt(jnp.sum(x * x, axis=-1, keepdims=True) + EPS)


def _rw_gather_matrix(nb):
    vs = LANES // RW_HEADS
    q = np.zeros((RW_NPIECE, vs, RW_KLO, nb, RW_HEADS, nb, vs, RW_HEADS), np.float32)
    for b in range(nb):
        for v in range(vs):
            for h in range(RW_HEADS):
                q[:, v, :, b, h, b, v, h] = 1.0
    return jnp.asarray(q.reshape(RW_NPIECE * vs * LANES, nb * LANES), BF16)


def _rw_post_kernel(yf_ref, yb_ref, bf_ref, bb_ref, g_ref, gnw_ref, gnb_ref, g2_ref, hsum_ref, q_ref, o_ref):
    nb = bf_ref.shape[0]
    tb = yf_ref.shape[0]
    pieces = _split3((yf_ref[...] + yb_ref[...]).reshape(tb, RW_HEAD * LANES))
    slab = (LANES // RW_HEADS) * LANES
    nslab = GROUP // LANES
    token_major = [_dot(jnp.concatenate([p[:, g * slab:(g + 1) * slab] for p in pieces], axis=1), q_ref[...])
                   for g in range(nslab)]

    def head_sum(z):
        hi, lo = _split_bf16(z)
        return _dot(hi, hsum_ref[...]) + _dot(lo, hsum_ref[...])

    yy = jnp.concatenate([jnp.concatenate([tm[:, b * LANES:(b + 1) * LANES] for tm in token_major], axis=1)
                          for b in range(nb)], axis=0)
    d = yy - head_sum(yy) * (1.0 / RW_HEAD)
    var = head_sum(d * d) * (1.0 / RW_HEAD)
    yn = d * lax.rsqrt(var + RW_GN_EPS) * gnw_ref[...] + gnb_ref[...]
    for b in range(nb):
        y_rw = yn[b * tb:(b + 1) * tb] + (bf_ref[b] + bb_ref[b])
        gate = _dot(jax.nn.sigmoid(g_ref[b]).astype(BF16), g2_ref[...])
        o_ref[b] = (y_rw * gate).astype(o_ref.dtype)


def _rw_post(y_f, y_b, bonus_f, bonus_b, proj, lp):
    nb, t, _ = bonus_f.shape
    tb = min(RW_PREP_BLOCK, t)
    perm = _rw_channel_perm()
    gnw = lp['rw_gn_w'][perm].reshape(1, GROUP)
    gnb = lp['rw_gn_b'][perm].reshape(1, GROUP)
    g2 = jnp.zeros((LANES, GROUP), F32).at[:RW_GATE_LR].set(lp['rw_g2'][:, perm]).astype(BF16)
    lane = np.arange(GROUP)
    hsum = jnp.asarray((lane[:, None] % RW_HEADS) == (lane[None, :] % RW_HEADS), BF16)
    q = _rw_gather_matrix(nb)
    params = (gnw, gnb, g2, hsum, q)
    full = lambda a: pl.BlockSpec(a.shape, lambda n: (0,) * a.ndim)
    tok = pl.BlockSpec((nb, tb, GROUP), lambda n: (0, n, 0))
    return pl.pallas_call(
        _rw_post_kernel,
        grid=(t // tb,),
        in_specs=[pl.BlockSpec((tb, RW_HEAD, LANES), lambda n: (n, 0, 0)),
                  pl.BlockSpec((tb, RW_HEAD, LANES), lambda n: (n, 0, 0)), tok, tok,
                  pl.BlockSpec((nb, tb, LANES), lambda n: (0, n, _col_block('r_g')))] + [full(p) for p in params],
        out_specs=tok,
        out_shape=jax.ShapeDtypeStruct((nb, t, GROUP), BF16),
        compiler_params=pltpu.CompilerParams(
            dimension_semantics=("arbitrary",), vmem_limit_bytes=RW_PREP_VMEM_LIMIT),
        name="rw_post",
    )(y_f.reshape(t, RW_HEAD, LANES), y_b.reshape(t, RW_HEAD, LANES), bonus_f, bonus_b, proj, *params)


def _rw_group(proj_c, proj_x, lp):
    s = jnp.zeros(RW_STATE_SHAPE, F32)
    outs = []
    for proj in (proj_c, proj_x):
        ops_f, v_f, bonus_f = _rw_prep(proj, lp, 0)
        ops_b, v_b, bonus_b = _rw_prep(proj, lp, 1)
        y_f, y_b, s = _rwkv_scan(ops_f, ops_b, v_f, v_b, s)
        outs.append(_rw_post(y_f, y_b, bonus_f, bonus_b, proj, lp))
    return outs


TOKEN_BLOCK = 512
HI = lax.Precision.HIGHEST


def _dot(a, b, precision=None):
    return jnp.dot(a, b, preferred_element_type=F32, precision=precision)


def _dot_nt(a, b):
    return lax.dot_general(a, b, (((1,), (1,)), ((), ())), preferred_element_type=F32)


def _dot_tn(a, b):
    return lax.dot_general(a, b, (((0,), (0,)), ((), ())), preferred_element_type=F32)


def _split_bf16(x):
    hi = x.astype(BF16)
    return hi, (x - hi.astype(F32)).astype(BF16)


def _dot3(a_hi, a_lo, b_hi, b_lo):
    return _dot(a_hi, b_hi) + (_dot(a_hi, b_lo) + _dot(a_lo, b_hi))


def _silu(x):
    return x * jax.nn.sigmoid(x)


def _chunk_masks(reverse):
    ri = lax.broadcasted_iota(jnp.int32, (CHUNK, CHUNK), 0)
    ci = lax.broadcasted_iota(jnp.int32, (CHUNK, CHUNK), 1)
    incl = (ci >= ri) if reverse else (ci <= ri)
    strict = (ci > ri) if reverse else (ci < ri)
    return incl, strict, (ri == ci).astype(F32)


def _conv3_block(x, cw, row_len):
    tb = x.shape[0]
    pos = lax.broadcasted_iota(jnp.int32, (tb, 1), 0) % row_len
    x_prev = jnp.where(pos == 0, 0.0, pltpu.roll(x, 1, 0))
    x_next = jnp.where(pos == row_len - 1, 0.0, pltpu.roll(x, tb - 1, 0))
    return x_prev * cw[0:1] + x * cw[1:2] + x_next * cw[2:3]


def _gdn_kernel(*refs, reverse, finalize, row_len, dirn):
    if finalize:
        (qkv_ref, ab_ref, z_ref, ob_ref, cw_ref, alog_ref, dtb_ref, g_ref, s0_ref,
         o_ref, sfin_ref, s_scr) = refs
    else:
        qkv_ref, ab_ref, cw_ref, alog_ref, dtb_ref, s0_ref, o_ref, sfin_ref, s_scr = refs
    n = pl.program_id(1)

    @pl.when(n == 0)
    def _():
        s_scr[...] = s0_ref[0]

    tb = qkv_ref.shape[1]
    qkv = _silu(_conv3_block(qkv_ref[0], cw_ref[...], row_len))
    ab = ab_ref[0]
    log_a = -jnp.exp(alog_ref[...]) * jax.nn.softplus(ab + dtb_ref[...])
    beta = jax.nn.sigmoid(ab)
    incl, strict, eye = _chunk_masks(reverse)
    tri = incl.astype(F32)
    last = 0 if reverse else CHUNK - 1
    hd = GDN_DK
    nchunk = tb // CHUNK
    order = list(range(nchunk - 1, -1, -1) if reverse else range(nchunk))
    rows = lambda c: slice(c * CHUNK, (c + 1) * CHUNK)
    cum_all = _dot(tri, jnp.concatenate([log_a[rows(c)] for c in range(nchunk)], axis=1), HI)

    items = [(c, h) for c in order for h in range(GDN_HEADS)]
    pre = {}
    for c, h in items:
        col = dirn * GDN_HEADS + h
        cum = cum_all[:, c * LANES + col:c * LANES + col + 1]
        bet = beta[rows(c), 2 * GDN_HEADS + col:2 * GDN_HEADS + col + 1]
        q = _l2norm(qkv[rows(c), h * hd:(h + 1) * hd]) * hd ** -0.5
        k = _l2norm(qkv[rows(c), GROUP + h * hd:GROUP + (h + 1) * hd])
        v = qkv[rows(c), 2 * GROUP + h * hd:2 * GROUP + (h + 1) * hd]
        cum_row = jnp.sum(eye * cum, axis=0, keepdims=True)
        decay = jnp.exp(jnp.where(incl, cum - cum_row, -jnp.inf))
        kb = k.astype(BF16)
        a = jnp.where(strict, bet * _dot_nt(kb, kb) * decay, 0.0)
        ecum = jnp.exp(cum)
        total = cum[last:last + 1]
        pre[c, h] = dict(
            x=-a, rhs=jnp.concatenate([v * bet, k * (bet * ecum)], axis=1),
            a_qk=(_dot_nt(q.astype(BF16), kb) * decay).astype(BF16),
            qe=(q * ecum).astype(BF16), k_end=(k * jnp.exp(total - cum)).astype(BF16),
            dec=jnp.exp(total))
    for p in pre.values():
        p['inv'] = eye + p['x']
    for level in range(6):
        for p in pre.values():
            x_hi, x_lo = _split_bf16(p['x'])
            if level == 0:
                p['x'] = _dot3(x_hi, x_lo, x_hi, x_lo)
            elif level < 5:
                i_hi, i_lo = _split_bf16(p['inv'])
                both = _dot3(jnp.concatenate([x_hi, i_hi], axis=0), jnp.concatenate([x_lo, i_lo], axis=0),
                             x_hi, x_lo)
                p['x'] = both[:CHUNK]
                p['inv'] = p['inv'] + both[CHUNK:]
            else:
                i_hi, i_lo = _split_bf16(p['inv'])
                p['inv'] = p['inv'] + _dot3(i_hi, i_lo, x_hi, x_lo)
    for p in pre.values():
        i_hi, i_lo = _split_bf16(p['inv'])
        r_hi, r_lo = _split_bf16(p['rhs'])
        sol = _dot3(i_hi, i_lo, r_hi, r_lo)
        p['u'], p['w'] = sol[:, :GDN_DV], sol[:, GDN_DV:].astype(BF16)

    for c, h in items:
        p = pre[c, h]
        s = s_scr[h]
        sb = s.astype(BF16)
        v_new = p['u'] - _dot(p['w'], sb)
        vb = v_new.astype(BF16)
        o = _dot(p['qe'], sb) + _dot(p['a_qk'], vb)
        s_scr[h] = p['dec'] * s + _dot_tn(p['k_end'], vb)
        cols = slice(h * GDN_DV, (h + 1) * GDN_DV)
        if finalize:
            o = o + ob_ref[0, rows(c), cols]
            o = o * lax.rsqrt(jnp.mean(o * o, axis=-1, keepdims=True) + EPS) * g_ref[...]
            o = o * _silu(z_ref[0, rows(c), cols])
        o_ref[0, rows(c), cols] = o.astype(o_ref.dtype)

    @pl.when(n == pl.num_programs(1) - 1)
    def _():
        sfin_ref[0] = s_scr[...]


def _gdn_pass(proj, o_other, s0, lp, *, dirn, row_len):
    b, t, _ = proj.shape
    tb = min(TOKEN_BLOCK, t)
    assert t % tb == 0 and tb % row_len == 0
    nblk = t // tb
    reverse = dirn == 1
    finalize = o_other is not None
    tok = (lambda i, n: (i, nblk - 1 - n)) if reverse else (lambda i, n: (i, n))
    seg = lambda name, wd: pl.BlockSpec((1, tb, wd), lambda i, n: tok(i, n) + (_col_block(name),))
    full = lambda a: pl.BlockSpec(a.shape, lambda i, n: (0,) * a.ndim)
    lane_row = lambda vals: jnp.pad(vals.reshape(1, -1), ((0, 0), (0, LANES - vals.size)))
    cw = lp['gdn_conv']
    alog = lane_row(lp['gdn_a_log'])
    dtb = lane_row(lp['gdn_dt_bias'])
    g = lp['gdn_norm_g'].reshape(1, GDN_DV)
    state_spec = pl.BlockSpec((1, GDN_HEADS, GDN_DK, GDN_DV), lambda i, n: (i, 0, 0, 0))
    if finalize:
        args = (proj, proj, proj, o_other, cw, alog, dtb, g, s0)
        in_specs = [seg('d_qkv', 3 * GROUP), seg('d_ab', LANES), seg('d_z', GROUP),
                    pl.BlockSpec((1, tb, GROUP), lambda i, n: tok(i, n) + (0,)),
                    full(cw), full(alog), full(dtb), full(g), state_spec]
    else:
        args = (proj, proj, cw, alog, dtb, s0)
        in_specs = [seg('d_qkv', 3 * GROUP), seg('d_ab', LANES), full(cw), full(alog), full(dtb), state_spec]
    return pl.pallas_call(
        functools.partial(_gdn_kernel, reverse=reverse, finalize=finalize, row_len=row_len, dirn=dirn),
        grid=(b, nblk),
        in_specs=in_specs,
        out_specs=[pl.BlockSpec((1, tb, GROUP), lambda i, n: tok(i, n) + (0,)), state_spec],
        out_shape=[jax.ShapeDtypeStruct((b, t, GROUP), BF16 if finalize else F32),
                   jax.ShapeDtypeStruct((b, GDN_HEADS, GDN_DK, GDN_DV), F32)],
        scratch_shapes=[pltpu.VMEM((GDN_HEADS, GDN_DK, GDN_DV), F32)],
        compiler_params=pltpu.CompilerParams(
            dimension_semantics=("arbitrary", "arbitrary"), vmem_limit_bytes=VMEM_LIMIT),
        name="gdn_fwd" if finalize else "gdn_bwd",
    )(*args)


def _gla_kernel(*refs, reverse, finalize):
    if finalize:
        (q_ref, k_ref, v_ref, lo_ref, r_ref, ob_ref, up_ref, ab_ref, g_ref, s0_ref,
         o_ref, sfin_ref, s_scr) = refs
    else:
        q_ref, k_ref, v_ref, lo_ref, up_ref, ab_ref, s0_ref, o_ref, sfin_ref, s_scr = refs
    n = pl.program_id(1)

    @pl.when(n == 0)
    def _():
        s_scr[...] = s0_ref[0]

    tb = q_ref.shape[1]
    gate = _dot(lo_ref[0].astype(BF16), up_ref[...].astype(BF16)) + ab_ref[...]
    log_f = jax.nn.log_sigmoid(gate) / GLA_TAU
    incl, _, eye = _chunk_masks(reverse)
    tri = incl.astype(F32)
    last = 0 if reverse else CHUNK - 1
    nchunk = tb // CHUNK
    order = range(nchunk - 1, -1, -1) if reverse else range(nchunk)
    for c in order:
        sl = slice(c * CHUNK, (c + 1) * CHUNK)
        cum = _dot(tri, log_f[sl], HI)
        total = cum[last:last + 1]
        q_dec = q_ref[0, sl, :] * GLA_DK ** -0.5 * jnp.exp(cum)
        k = k_ref[0, sl, :]
        k_inv = (k * jnp.exp(-cum)).astype(BF16)
        k_end = (k * jnp.exp(total - cum)).astype(BF16)
        q_dec = q_dec.astype(BF16)
        dec_row = jnp.exp(total)
        for h in range(GLA_HEADS):
            kc = slice(h * GLA_DK, (h + 1) * GLA_DK)
            vc = slice(h * GLA_DV, (h + 1) * GLA_DV)
            vb = v_ref[0, sl, vc].astype(BF16)
            att = jnp.where(incl, _dot_nt(q_dec[:, kc], k_inv[:, kc]), 0.0)
            s = s_scr[h]
            o = _dot(att.astype(BF16), vb) + _dot(q_dec[:, kc], s.astype(BF16))
            dec_col = jnp.sum(eye * dec_row[:, kc], axis=1, keepdims=True)
            s_scr[h] = dec_col * s + _dot_tn(k_end[:, kc], vb)
            if finalize:
                o = o + ob_ref[0, sl, vc]
                o = o * lax.rsqrt(jnp.mean(o * o, axis=-1, keepdims=True) + EPS) * g_ref[...]
                o = o * _silu(r_ref[0, sl, vc])
            o_ref[0, sl, vc] = o.astype(o_ref.dtype)

    @pl.when(n == pl.num_programs(1) - 1)
    def _():
        sfin_ref[0] = s_scr[...]


def _gla_pass(proj, o_other, s0, lp, *, dirn):
    b, t, _ = proj.shape
    tb = min(TOKEN_BLOCK, t)
    assert t % tb == 0
    nblk = t // tb
    reverse = dirn == 1
    finalize = o_other is not None
    tok = (lambda i, n: (i, nblk - 1 - n)) if reverse else (lambda i, n: (i, n))
    seg = lambda name: pl.BlockSpec((1, tb, SEG[name][2]), lambda i, n: tok(i, n) + (_col_block(name),))
    full = lambda a: pl.BlockSpec(a.shape, lambda i, n: (0,) * a.ndim)
    hk = GLA_HEADS * GLA_DK
    up = jnp.pad(lp['gla_a_up'][dirn], ((0, LANES - GLA_LR), (0, 0)))
    ab = lp['gla_a_b'][dirn].reshape(1, hk)
    g = lp['gla_norm_g'].reshape(1, GLA_DV)
    state_spec = pl.BlockSpec((1, GLA_HEADS, GLA_DK, GLA_DV), lambda i, n: (i, 0, 0, 0))
    if finalize:
        args = (proj, proj, proj, proj, proj, o_other, up, ab, g, s0)
        in_specs = [seg('g_q'), seg('g_k'), seg('g_v'), seg('g_lo'), seg('g_r'),
                    pl.BlockSpec((1, tb, GROUP), lambda i, n: tok(i, n) + (0,)),
                    full(up), full(ab), full(g), state_spec]
    else:
        args = (proj, proj, proj, proj, up, ab, s0)
        in_specs = [seg('g_q'), seg('g_k'), seg('g_v'), seg('g_lo'), full(up), full(ab), state_spec]
    return pl.pallas_call(
        functools.partial(_gla_kernel, reverse=reverse, finalize=finalize),
        grid=(b, nblk),
        in_specs=in_specs,
        out_specs=[pl.BlockSpec((1, tb, GROUP), lambda i, n: tok(i, n) + (0,)), state_spec],
        out_shape=[jax.ShapeDtypeStruct((b, t, GROUP), BF16 if finalize else F32),
                   jax.ShapeDtypeStruct((b, GLA_HEADS, GLA_DK, GLA_DV), F32)],
        scratch_shapes=[pltpu.VMEM((GLA_HEADS, GLA_DK, GLA_DV), F32)],
        compiler_params=pltpu.CompilerParams(
            dimension_semantics=("arbitrary", "arbitrary"), vmem_limit_bytes=VMEM_LIMIT),
        name="gla_fwd" if finalize else "gla_bwd",
    )(*args)


def _two_direction_group(pass_fn, state_shape, proj_c, proj_x, **kw):
    zero = jnp.zeros((proj_x[0].shape[0],) + state_shape, F32)
    ob_c, sb_c = pass_fn(proj_c[0], None, zero, dirn=1, **proj_c[1], **kw)
    ob_x, _ = pass_fn(proj_x[0], None, sb_c, dirn=1, **proj_x[1], **kw)
    y_c, sf_c = pass_fn(proj_c[0], ob_c, zero, dirn=0, **proj_c[1], **kw)
    y_x, _ = pass_fn(proj_x[0], ob_x, sf_c, dirn=0, **proj_x[1], **kw)
    return y_c, y_x


def _gla_group_pallas(proj_c, proj_x, lp):
    return _two_direction_group(_gla_pass, (GLA_HEADS, GLA_DK, GLA_DV), (proj_c, {}), (proj_x, {}), lp=lp)


def _gdn_group_pallas(proj_c, proj_x, lp):
    return _two_direction_group(_gdn_pass, (GDN_HEADS, GDN_DK, GDN_DV),
                                (proj_c, dict(row_len=proj_c.shape[1])), (proj_x, dict(row_len=GRID_W)), lp=lp)


def _sc_kernel(b_ref, c_ref, h_ref, cw_ref, o_ref, *, row_len):
    o_ref[0] = (b_ref[0] * _conv3_block(c_ref[0] * h_ref[0], cw_ref[...], row_len)).astype(o_ref.dtype)


def _sc_group(proj, row_len, lp):
    b, t, _ = proj.shape
    tb = min(TOKEN_BLOCK, t)
    assert t % tb == 0 and tb % row_len == 0
    seg = lambda name: pl.BlockSpec((1, tb, GROUP), lambda i, n: (i, n, _col_block(name)))
    cw = lp['sc_conv']
    return pl.pallas_call(
        functools.partial(_sc_kernel, row_len=row_len),
        grid=(b, t // tb),
        in_specs=[seg('c_b'), seg('c_c'), seg('c_h'), pl.BlockSpec(cw.shape, lambda i, n: (0, 0))],
        out_specs=pl.BlockSpec((1, tb, GROUP), lambda i, n: (i, n, 0)),
        out_shape=jax.ShapeDtypeStruct((b, t, GROUP), BF16),
        compiler_params=pltpu.CompilerParams(
            dimension_semantics=("arbitrary", "arbitrary"), vmem_limit_bytes=VMEM_LIMIT),
        name="short_conv",
    )(proj, proj, proj, cw)


def _mixers(proj_c, proj_x, lp):
    tc = proj_c.shape[1]
    y_gla_c, y_gla_x = _gla_group_pallas(proj_c, proj_x, lp)
    y_gdn_c, y_gdn_x = _gdn_group_pallas(proj_c, proj_x, lp)
    y_sc_c = _sc_group(proj_c, tc, lp)
    y_sc_x = _sc_group(proj_x, GRID_W, lp)
    y_rw_c, y_rw_x = _rw_group(proj_c, proj_x, lp)
    return (y_gla_c, y_gdn_c, y_sc_c, y_rw_c), (y_gla_x, y_gdn_x, y_sc_x, y_rw_x)


def kernel(x, c, ctx, c_ctx, ada_w, ada_b, norm1_g, norm2_g, w_in, w_out, gla_a_up, gla_a_b, gla_norm_g, gdn_conv, gdn_a_log, gdn_dt_bias, gdn_norm_g, sc_conv, rw_mu_rkv, rw_mu_wa, rw_w0, rw_w2, rw_a0, rw_a2, rw_g2, rw_kk, rw_ka, rw_rk, rw_gn_w, rw_gn_b, ffn_w_gu, ffn_w_down, final_g):
    nb, t, d = x.shape
    tc = ctx.shape[1]
    depth = w_in.shape[0]
    assert d == D_MODEL and t % 512 == 0 and tc % CHUNK == 0 and nb + 1 <= SUBLANES

    cond = jnp.concatenate([c, c_ctx[None, :], jnp.zeros((SUBLANES - nb - 1, d), F32)], axis=0)
    mod = _modulation(cond, ada_w, ada_b)

    tm_x = 512
    tm_c = tc
    h_x, h_c = x, ctx
    for l in range(depth):
        lp = dict(gla_a_up=gla_a_up[l], gla_a_b=gla_a_b[l],
                  gla_norm_g=gla_norm_g[l], gdn_conv=gdn_conv[l], gdn_a_log=gdn_a_log[l],
                  gdn_dt_bias=gdn_dt_bias[l], gdn_norm_g=gdn_norm_g[l], sc_conv=sc_conv[l],
                  rw_mu_rkv=rw_mu_rkv[l], rw_mu_wa=rw_mu_wa[l], rw_w0=rw_w0[l], rw_w2=rw_w2[l],
                  rw_a0=rw_a0[l], rw_a2=rw_a2[l], rw_g2=rw_g2[l], rw_kk=rw_kk[l], rw_ka=rw_ka[l],
                  rw_rk=rw_rk[l], rw_gn_w=rw_gn_w[l], rw_gn_b=rw_gn_b[l])
        m_x = [mod[l, :nb, i * d:(i + 1) * d][:, None, :] for i in range(6)]
        m_c = [jnp.broadcast_to(mod[l, nb, i * d:(i + 1) * d][None, None, :], (nb, 1, d)) for i in range(6)]
        w_in_l = _pad_w_in(w_in[l])
        w_out_l = jnp.concatenate([w_out[l][:3 * GROUP], w_out[l][3 * GROUP:][_rw_channel_perm()]],
                                  axis=0).astype(BF16)
        w_gu_l = ffn_w_gu[l].astype(BF16)
        w_dn_l = ffn_w_down[l].astype(BF16)

        proj_c = _norm_proj(h_c, norm1_g[l], m_c[1], m_c[0], w_in_l, tm_c, 1024)
        proj_x = _norm_proj(h_x, norm1_g[l], m_x[1], m_x[0], w_in_l, 2 * tm_x, 1024)
        y_c, y_x = _mixers(proj_c, proj_x, lp)

        h_x = _groups_residual(y_x, w_out_l, h_x, m_x[2], 2 * tm_x, 1024)
        a_x = _norm_swiglu(h_x, norm2_g[l], m_x[4], m_x[3], w_gu_l, 2 * tm_x, 512)
        h_x = _proj_residual(a_x, w_dn_l, h_x, m_x[5], 2 * tm_x, 512)
        if l < depth - 1:
            h_c = _groups_residual(y_c, w_out_l, h_c, m_c[2], tm_c, 512)
            a_c = _norm_swiglu(h_c, norm2_g[l], m_c[4], m_c[3], w_gu_l, tm_c, 512)
            h_c = _proj_residual(a_c, w_dn_l, h_c, m_c[5], tm_c, 512)
    return _final_norm(h_x, final_g, tm_x)
```

```python
import functools
import math

import numpy as np
import jax
import jax.numpy as jnp
from jax import lax
from jax.experimental import pallas as pl
from jax.experimental.pallas import tpu as pltpu

F32 = jnp.float32
BF16 = jnp.bfloat16

D_MODEL = 2048
GROUP = D_MODEL // 4
CHUNK = 64
EPS = 1e-6
GRID_W = 64

GLA_HEADS = 4
GLA_DK = 64
GLA_DV = 128
GLA_LR = 16
GLA_TAU = 16.0
GDN_HEADS = 4
GDN_DK = 128
GDN_DV = 128
RW_HEAD = 64
RW_HEADS = GROUP // RW_HEAD
RW_DECAY_LR = 32
RW_A_LR = 32
RW_GATE_LR = 96
RW_DECAY_SCALE = math.exp(-0.5)
RW_GN_EPS = 64e-5
FFN = -(-(8 * D_MODEL) // (3 * 256)) * 256

SUBLANES = 8
LANES = 128
VMEM_LIMIT = 56 * 1024 * 1024

_SRC_SEGS = (
    ('g_q', 256), ('g_k', 256), ('g_v', 512), ('g_r', 512), ('g_lo', GLA_LR),
    ('d_qkv', 1536), ('d_z', 512), ('d_ab', 16),
    ('c_b', 512), ('c_c', 512), ('c_h', 512),
    ('r_rkv', 1536), ('r_wa', 64), ('r_g', RW_GATE_LR),
)
_DST_ORDER = ('g_q', 'g_k', 'g_v', 'g_r', 'd_qkv', 'r_rkv', 'd_z', 'c_b', 'c_c', 'c_h',
              'g_lo', 'd_ab', 'r_wa', 'r_g')


def _round_up(n, m):
    return -(-n // m) * m


def _seg_layout():
    width = dict(_SRC_SEGS)
    src, o = {}, 0
    for name, wd in _SRC_SEGS:
        src[name] = o
        o += wd
    dst, o = {}, 0
    for name in _DST_ORDER:
        pw = _round_up(width[name], LANES)
        assert o % pw == 0, name
        dst[name] = (o, width[name], pw)
        o += pw
    return src, dst, o


SEG_SRC, SEG, PROJ_PAD = _seg_layout()


def _pad_w_in(w):
    parts = []
    for name in _DST_ORDER:
        _, wd, pw = SEG[name]
        seg = w[:, SEG_SRC[name]:SEG_SRC[name] + wd]
        if name == 'r_rkv':
            perm = _rw_channel_perm()
            seg = jnp.concatenate([seg[:, i * GROUP:(i + 1) * GROUP][:, perm] for i in range(3)], axis=1)
        if pw != wd:
            seg = jnp.pad(seg, ((0, 0), (0, pw - wd)))
        parts.append(seg)
    return jnp.concatenate(parts, axis=1).astype(BF16)


def _col_block(name):
    o, _, pw = SEG[name]
    return o // pw


def _mod_kernel(c_ref, w_ref, b_ref, o_ref):
    x = c_ref[...]
    x = (x * jax.nn.sigmoid(x)).astype(BF16)
    o_ref[0] = jnp.dot(x, w_ref[0].astype(BF16), preferred_element_type=F32) + b_ref[0]


def _modulation(cond, ada_w, ada_b):
    nl, d, n = ada_w.shape
    tn = 1024
    return pl.pallas_call(
        _mod_kernel,
        grid=(nl, n // tn),
        in_specs=[pl.BlockSpec((SUBLANES, d), lambda l, j: (0, 0)),
                  pl.BlockSpec((1, d, tn), lambda l, j: (l, 0, j)),
                  pl.BlockSpec((1, 1, tn), lambda l, j: (l, 0, j))],
        out_specs=pl.BlockSpec((1, SUBLANES, tn), lambda l, j: (l, 0, j)),
        out_shape=jax.ShapeDtypeStruct((nl, SUBLANES, n), F32),
        compiler_params=pltpu.CompilerParams(
            dimension_semantics=("arbitrary", "arbitrary"), vmem_limit_bytes=VMEM_LIMIT),
        name="modulation",
    )(cond, ada_w, ada_b.reshape(nl, 1, n))


def _modnorm_rows(h_ref, g_ref, sc_ref, sh_ref):
    x = h_ref[0]
    y = x * lax.rsqrt(jnp.mean(x * x, axis=-1, keepdims=True) + EPS) * g_ref[...]
    return (y * (1.0 + sc_ref[0]) + sh_ref[0]).astype(BF16)


def _norm_proj_kernel(h_ref, g_ref, sc_ref, sh_ref, w_ref, o_ref, a_scr):
    @pl.when(pl.program_id(2) == 0)
    def _():
        a_scr[...] = _modnorm_rows(h_ref, g_ref, sc_ref, sh_ref)

    o_ref[0] = jnp.dot(a_scr[...], w_ref[...], preferred_element_type=F32).astype(o_ref.dtype)


def _norm_proj(h, g, scale, shift, w, tm, tn):
    b, t, d = h.shape
    n = w.shape[1]
    return pl.pallas_call(
        _norm_proj_kernel,
        grid=(b, t // tm, n // tn),
        in_specs=[pl.BlockSpec((1, tm, d), lambda i, m, j: (i, m, 0)),
                  pl.BlockSpec((1, d), lambda i, m, j: (0, 0)),
                  pl.BlockSpec((1, 1, d), lambda i, m, j: (i, 0, 0)),
                  pl.BlockSpec((1, 1, d), lambda i, m, j: (i, 0, 0)),
                  pl.BlockSpec((d, tn), lambda i, m, j: (0, j))],
        out_specs=pl.BlockSpec((1, tm, tn), lambda i, m, j: (i, m, j)),
        out_shape=jax.ShapeDtypeStruct((b, t, n), F32),
        scratch_shapes=[pltpu.VMEM((tm, d), BF16)],
        compiler_params=pltpu.CompilerParams(
            dimension_semantics=("arbitrary", "arbitrary", "arbitrary"), vmem_limit_bytes=VMEM_LIMIT),
        name="norm_proj",
    )(h, g.reshape(1, d), scale, shift, w)


def _norm_swiglu_kernel(h_ref, g_ref, sc_ref, sh_ref, wg_ref, wu_ref, o_ref, a_scr):
    @pl.when(pl.program_id(2) == 0)
    def _():
        a_scr[...] = _modnorm_rows(h_ref, g_ref, sc_ref, sh_ref)

    a = a_scr[...]
    gate = jnp.dot(a, wg_ref[...], preferred_element_type=F32)
    up = jnp.dot(a, wu_ref[...], preferred_element_type=F32)
    o_ref[0] = (gate * jax.nn.sigmoid(gate) * up).astype(o_ref.dtype)


def _norm_swiglu(h, g, scale, shift, w_gu, tm, tn):
    b, t, d = h.shape
    f = w_gu.shape[1] // 2
    nj = f // tn
    return pl.pallas_call(
        _norm_swiglu_kernel,
        grid=(b, t // tm, nj),
        in_specs=[pl.BlockSpec((1, tm, d), lambda i, m, j: (i, m, 0)),
                  pl.BlockSpec((1, d), lambda i, m, j: (0, 0)),
                  pl.BlockSpec((1, 1, d), lambda i, m, j: (i, 0, 0)),
                  pl.BlockSpec((1, 1, d), lambda i, m, j: (i, 0, 0)),
                  pl.BlockSpec((d, tn), lambda i, m, j: (0, j)),
                  pl.BlockSpec((d, tn), lambda i, m, j: (0, j + nj))],
        out_specs=pl.BlockSpec((1, tm, tn), lambda i, m, j: (i, m, j)),
        out_shape=jax.ShapeDtypeStruct((b, t, f), BF16),
        scratch_shapes=[pltpu.VMEM((tm, d), BF16)],
        compiler_params=pltpu.CompilerParams(
            dimension_semantics=("arbitrary", "arbitrary", "arbitrary"), vmem_limit_bytes=VMEM_LIMIT),
        name="norm_swiglu",
    )(h, g.reshape(1, d), scale, shift, w_gu, w_gu)


def _proj_residual_kernel(a_ref, w_ref, h_ref, gate_ref, o_ref):
    o_ref[0] = h_ref[0] + gate_ref[0] * jnp.dot(a_ref[0], w_ref[...], preferred_element_type=F32)


def _proj_residual(a, w, h, gate, tm, tn):
    b, t, k = a.shape
    d = w.shape[1]
    return pl.pallas_call(
        _proj_residual_kernel,
        grid=(b, t // tm, d // tn),
        in_specs=[pl.BlockSpec((1, tm, k), lambda i, m, j: (i, m, 0)),
                  pl.BlockSpec((k, tn), lambda i, m, j: (0, j)),
                  pl.BlockSpec((1, tm, tn), lambda i, m, j: (i, m, j)),
                  pl.BlockSpec((1, 1, tn), lambda i, m, j: (i, 0, j))],
        out_specs=pl.BlockSpec((1, tm, tn), lambda i, m, j: (i, m, j)),
        out_shape=jax.ShapeDtypeStruct((b, t, d), F32),
        compiler_params=pltpu.CompilerParams(
            dimension_semantics=("arbitrary", "arbitrary", "arbitrary"), vmem_limit_bytes=VMEM_LIMIT),
        name="proj_residual",
    )(a, w, h, gate)


def _groups_residual_kernel(*refs):
    *a_refs, w_ref, h_ref, gate_ref, o_ref = refs
    k = a_refs[0].shape[2]
    acc = jnp.dot(a_refs[0][0], w_ref[0:k, :], preferred_element_type=F32)
    for i, a_ref in enumerate(a_refs[1:], start=1):
        acc = acc + jnp.dot(a_ref[0], w_ref[i * k:(i + 1) * k, :], preferred_element_type=F32)
    o_ref[0] = h_ref[0] + gate_ref[0] * acc


def _groups_residual(groups, w, h, gate, tm, tn):
    b, t, k = groups[0].shape
    d = w.shape[1]
    assert len(groups) * k == w.shape[0]
    return pl.pallas_call(
        _groups_residual_kernel,
        grid=(b, t // tm, d // tn),
        in_specs=[pl.BlockSpec((1, tm, k), lambda i, m, j: (i, m, 0)) for _ in groups]
        + [pl.BlockSpec((w.shape[0], tn), lambda i, m, j: (0, j)),
           pl.BlockSpec((1, tm, tn), lambda i, m, j: (i, m, j)),
           pl.BlockSpec((1, 1, tn), lambda i, m, j: (i, 0, j))],
        out_specs=pl.BlockSpec((1, tm, tn), lambda i, m, j: (i, m, j)),
        out_shape=jax.ShapeDtypeStruct((b, t, d), F32),
        compiler_params=pltpu.CompilerParams(
            dimension_semantics=("arbitrary", "arbitrary", "arbitrary"), vmem_limit_bytes=VMEM_LIMIT),
        name="groups_residual",
    )(*groups, w, h, gate)


def _final_norm_kernel(h_ref, g_ref, o_ref):
    x = h_ref[0]
    o_ref[0] = x * lax.rsqrt(jnp.mean(x * x, axis=-1, keepdims=True) + EPS) * g_ref[...]


def _final_norm(h, g, tm):
    b, t, d = h.shape
    return pl.pallas_call(
        _final_norm_kernel,
        grid=(b, t // tm),
        in_specs=[pl.BlockSpec((1, tm, d), lambda i, m: (i, m, 0)),
                  pl.BlockSpec((1, d), lambda i, m: (0, 0))],
        out_specs=pl.BlockSpec((1, tm, d), lambda i, m: (i, m, 0)),
        out_shape=jax.ShapeDtypeStruct((b, t, d), F32),
        compiler_params=pltpu.CompilerParams(
            dimension_semantics=("arbitrary", "arbitrary"), vmem_limit_bytes=VMEM_LIMIT),
        name="final_norm",
    )(h, g.reshape(1, d))


RW_CHAINS = 32
RW_KLO = LANES // RW_CHAINS
RW_KQ = RW_HEAD // RW_KLO
RW_VM = RW_HEAD // SUBLANES
RW_VGROUP = 4
RW_STEP_BLOCK = 16
OP_W, OP_B, OP_K, OP_R, OP_KK = range(5)


def _lane_group_sum(p):
    out = p
    for g in range(1, RW_KLO):
        out = out + pltpu.roll(p, g * RW_CHAINS, 1)
    return out


def _rwkv_kernel(opsf_ref, opsb_ref, vf_ref, vb_ref, s0_ref, yf_ref, yb_ref, sfin_ref,
                 s_ref, saf_ref, sab_ref, pb_ref):
    tb = opsf_ref.shape[0]
    dirs = ((opsf_ref, vf_ref, yf_ref), (opsb_ref, vb_ref, yb_ref))

    def row(ops_ref, t, j, q):
        lane0 = (j * RW_KQ + q) * LANES
        return ops_ref[t:t + 1, lane0:lane0 + LANES]

    @pl.when(pl.program_id(0) == 0)
    def _():
        s_ref[...] = s0_ref[...]

    zero = jnp.zeros((SUBLANES, LANES), F32)
    groups = [range(m0, m0 + RW_VGROUP) for m0 in range(0, RW_VM, RW_VGROUP)]

    for d, (ops_ref, _, _) in enumerate(dirs):
        first = 0 if d == 0 else tb - 1
        for ms in groups:
            acc = {m: zero for m in ms}
            for q in range(RW_KQ):
                kk = row(ops_ref, first, OP_KK, q)
                for m in ms:
                    acc[m] = acc[m] + s_ref[d, q, m] * kk
            for m in ms:
                if d == 0:
                    saf_ref[m] = _lane_group_sum(acc[m])
                else:
                    pb_ref[m] = acc[m]

    def advance(d, i, sa_ref, next_ref, reduce_next):
        ops_ref, v_ref, y_ref = dirs[d]
        t = i if d == 0 else tb - 1 - i
        tn = min(i + 1, tb - 1) if d == 0 else max(tb - 2 - i, 0)
        for ms in groups:
            sa = {m: sa_ref[m] for m in ms}
            vt = {m: v_ref[t, m] for m in ms}
            acc_y = {m: zero for m in ms}
            acc_s = {m: zero for m in ms}
            for q in range(RW_KQ):
                w = row(ops_ref, t, OP_W, q)
                b = row(ops_ref, t, OP_B, q)
                kt = row(ops_ref, t, OP_K, q)
                r = row(ops_ref, t, OP_R, q)
                kkn = row(ops_ref, tn, OP_KK, q)
                for m in ms:
                    s = s_ref[d, q, m] * w - sa[m] * b + vt[m] * kt
                    s_ref[d, q, m] = s
                    acc_y[m] = acc_y[m] + s * r
                    acc_s[m] = acc_s[m] + s * kkn
            for m in ms:
                y_ref[t, m] = acc_y[m]
                next_ref[m] = _lane_group_sum(acc_s[m]) if reduce_next else acc_s[m]

    for i in range(tb):
        for m in range(RW_VM):
            sab_ref[m] = _lane_group_sum(pb_ref[m])
        advance(0, i, saf_ref, saf_ref, True)
        advance(1, i, sab_ref, pb_ref, False)

    @pl.when(pl.program_id(0) == pl.num_programs(0) - 1)
    def _():
        sfin_ref[...] = s_ref[...]


RW_STATE_SHAPE = (2, RW_KQ, RW_VM, SUBLANES, LANES)


def _rwkv_scan(ops_f, ops_b, v_f, v_b, s0):
    t = ops_f.shape[0]
    tb = RW_STEP_BLOCK
    nblk = t // tb
    fwd = lambda i: (i, 0, 0, 0)
    bwd = lambda i: (nblk - 1 - i, 0, 0, 0)
    ops_block = (tb, 5 * RW_KQ * LANES)
    v_block = (tb, RW_VM, SUBLANES, LANES)
    y_shape = jax.ShapeDtypeStruct((t, RW_VM, SUBLANES, LANES), F32)
    state_spec = pl.BlockSpec(RW_STATE_SHAPE, lambda i: (0,) * len(RW_STATE_SHAPE))
    return pl.pallas_call(
        _rwkv_kernel,
        grid=(nblk,),
        in_specs=[pl.BlockSpec(ops_block, lambda i: (i, 0)), pl.BlockSpec(ops_block, lambda i: (nblk - 1 - i, 0)),
                  pl.BlockSpec(v_block, fwd), pl.BlockSpec(v_block, bwd), state_spec],
        out_specs=[pl.BlockSpec(v_block, fwd), pl.BlockSpec(v_block, bwd), state_spec],
        out_shape=[y_shape, y_shape, jax.ShapeDtypeStruct(RW_STATE_SHAPE, F32)],
        scratch_shapes=[pltpu.VMEM(RW_STATE_SHAPE, F32)]
        + [pltpu.VMEM((RW_VM, SUBLANES, LANES), F32)] * 3,
        compiler_params=pltpu.CompilerParams(
            dimension_semantics=("arbitrary",), vmem_limit_bytes=VMEM_LIMIT),
        name="rwkv_scan",
    )(ops_f, ops_b, v_f, v_b, s0)


RW_PREP_BLOCK = 128
RW_PREP_VMEM_LIMIT = 56 * 1024 * 1024
RW_NPIECE = 3


def _rw_channel_perm():
    return np.arange(GROUP).reshape(RW_HEADS, RW_HEAD).T.reshape(-1)


def _rw_relayout_matrices(nb):
    slab = LANES
    ch_per_slab = slab // RW_HEADS
    pk = np.zeros((nb, RW_NPIECE, slab, ch_per_slab // RW_KLO, RW_KLO, nb, RW_HEADS), np.float32)
    pv = np.zeros((nb, RW_NPIECE, slab, ch_per_slab, RW_KLO, nb, RW_HEADS), np.float32)
    for b in range(nb):
        for ch in range(ch_per_slab):
            for h in range(RW_HEADS):
                pk[b, :, ch * RW_HEADS + h, ch // RW_KLO, ch % RW_KLO, b, h] = 1.0
                pv[b, :, ch * RW_HEADS + h, ch, :, b, h] = 1.0
    rows = nb * RW_NPIECE * slab
    return (jnp.asarray(pk.reshape(rows, -1), BF16), jnp.asarray(pv.reshape(rows, -1), BF16))


def _split3(x):
    hi = x.astype(BF16)
    r1 = x - hi.astype(F32)
    mid = r1.astype(BF16)
    lo = (r1 - mid.astype(F32)).astype(BF16)
    return hi, mid, lo


def _rw_prep_kernel(rkv_ref, wa_ref, rkv_nb_ref, wa_nb_ref, mu_rkv_ref, mu_wa_ref, w0_ref, w2_ref,
                    a0_ref, a2_ref, kk_ref, ka_ref, rk_ref, hsum_ref, pk_ref, pv_ref,
                    ops_ref, v_ref, bonus_ref, *, backward):
    nb, tb, _ = rkv_ref.shape
    n = pl.program_id(0)
    edge = (n == pl.num_programs(0) - 1) if backward else (n == 0)
    nb_row = SUBLANES - 1 if not backward else 0
    row = lax.broadcasted_iota(jnp.int32, (tb, 1), 0)
    edge_row = (row == tb - 1) if backward else (row == 0)

    def shifted(x, neighbour):
        inner = pltpu.roll(x, tb - 1 if backward else 1, 0)
        outer = jnp.where(edge, 0.0, neighbour)
        return jnp.where(edge_row, outer, inner)

    def head_sum(z):
        hi, lo = _split_bf16(z)
        return _dot(hi, hsum_ref[...]) + _dot(lo, hsum_ref[...])

    part = []
    for b in range(nb):
        x = rkv_ref[b]
        x = x + (shifted(x, rkv_nb_ref[b, nb_row:nb_row + 1, :]) - x) * mu_rkv_ref[...]
        xw = wa_ref[b]
        xw = xw + (shifted(xw, wa_nb_ref[b, nb_row:nb_row + 1, :]) - xw) * mu_wa_ref[...]
        r, k, v = x[:, :GROUP], x[:, GROUP:2 * GROUP], x[:, 2 * GROUP:]
        decay = jnp.exp(-RW_DECAY_SCALE * jax.nn.sigmoid(
            w0_ref[...] + _dot(jnp.tanh(xw).astype(BF16), w2_ref[...])))
        a = jax.nn.sigmoid(a0_ref[...] + _dot(xw.astype(BF16), a2_ref[...]))
        kq = k * kk_ref[...]
        kt = k * (1.0 + (a - 1.0) * ka_ref[...])
        part.append((decay, a, kq, kt, r, v))
    sums = head_sum(jnp.concatenate([p[2] * p[2] for p in part]
                                    + [p[4] * p[3] * rk_ref[...] for p in part], axis=0))
    feats = []
    for b, (decay, a, kq, kt, r, v) in enumerate(part):
        kk = kq * lax.rsqrt(sums[b * tb:(b + 1) * tb] + EPS)
        bonus_ref[b] = sums[(nb + b) * tb:(nb + b + 1) * tb] * v
        feats.append([decay, kk * a, kt, r, kk, v])

    nslab = GROUP // LANES
    kw = pk_ref.shape[1]
    vw = pv_ref.shape[1]
    for j in range(6):
        pieces = [_split3(feats[b][j]) for b in range(nb)]
        lhs = jnp.concatenate(
            [jnp.concatenate([p[:, g * LANES:(g + 1) * LANES] for bp in pieces for p in bp], axis=1)
             for g in range(nslab)], axis=0)
        if j < 5:
            out = _dot(lhs, pk_ref[...])
            for g in range(nslab):
                ops_ref[:, (j * nslab + g) * kw:(j * nslab + g + 1) * kw] = out[g * tb:(g + 1) * tb]
        else:
            out = _dot(lhs, pv_ref[...])
            for g in range(nslab):
                rows = vw // LANES
                v_ref[:, g * rows:(g + 1) * rows, :] = out[g * tb:(g + 1) * tb].reshape(tb, rows, LANES)


def _rw_prep(proj, lp, dirn):
    nb, t, _ = proj.shape
    tb = min(RW_PREP_BLOCK, t)
    assert t % tb == 0 and nb * RW_HEADS == RW_CHAINS
    nblk = t // tb
    backward = dirn == 1
    per8 = tb // SUBLANES
    last8 = t // SUBLANES - 1
    if backward:
        nb_idx = lambda n: jnp.minimum((n + 1) * per8, last8)
    else:
        nb_idx = lambda n: jnp.maximum(n * per8 - 1, 0)
    perm = _rw_channel_perm()
    lane_row = lambda v: v.reshape(1, -1)
    pad_rows = lambda w, lo: jnp.zeros((LANES, GROUP), F32).at[lo:lo + w.shape[0]].set(w)
    mu_rkv = lane_row(jnp.concatenate([lp['rw_mu_rkv'][dirn][i * GROUP:(i + 1) * GROUP][perm] for i in range(3)]))
    mu_wa = lane_row(jnp.pad(lp['rw_mu_wa'][dirn], (0, LANES - RW_DECAY_LR - RW_A_LR)))
    w0 = lane_row(lp['rw_w0'][dirn][perm])
    w2 = pad_rows(lp['rw_w2'][dirn][:, perm], 0).astype(BF16)
    a0 = lane_row(lp['rw_a0'][dirn][perm])
    a2 = pad_rows(lp['rw_a2'][dirn][:, perm], RW_DECAY_LR).astype(BF16)
    k_k = lane_row(lp['rw_kk'][perm])
    k_a = lane_row(lp['rw_ka'][perm])
    r_k = lane_row(lp['rw_rk'].reshape(-1)[perm])
    lane = np.arange(GROUP)
    hsum = jnp.asarray((lane[:, None] % RW_HEADS) == (lane[None, :] % RW_HEADS), BF16)
    pk, pv = _rw_relayout_matrices(nb)
    params = (mu_rkv, mu_wa, w0, w2, a0, a2, k_k, k_a, r_k, hsum, pk, pv)
    full = lambda a: pl.BlockSpec(a.shape, lambda n: (0,) * a.ndim)
    ops_w = 5 * RW_KQ * LANES
    ops, vt, bonus = pl.pallas_call(
        functools.partial(_rw_prep_kernel, backward=backward),
        grid=(nblk,),
        in_specs=[pl.BlockSpec((nb, tb, 3 * GROUP), lambda n: (0, n, _col_block('r_rkv'))),
                  pl.BlockSpec((nb, tb, LANES), lambda n: (0, n, _col_block('r_wa'))),
                  pl.BlockSpec((nb, SUBLANES, 3 * GROUP), lambda n: (0, nb_idx(n), _col_block('r_rkv'))),
                  pl.BlockSpec((nb, SUBLANES, LANES), lambda n: (0, nb_idx(n), _col_block('r_wa')))]
        + [full(p) for p in params],
        out_specs=[pl.BlockSpec((tb, ops_w), lambda n: (n, 0)),
                   pl.BlockSpec((tb, RW_HEAD, LANES), lambda n: (n, 0, 0)),
                   pl.BlockSpec((nb, tb, GROUP), lambda n: (0, n, 0))],
        out_shape=[jax.ShapeDtypeStruct((t, ops_w), F32), jax.ShapeDtypeStruct((t, RW_HEAD, LANES), F32),
                   jax.ShapeDtypeStruct((nb, t, GROUP), F32)],
        compiler_params=pltpu.CompilerParams(
            dimension_semantics=("arbitrary",), vmem_limit_bytes=RW_PREP_VMEM_LIMIT),
        name="rw_prep",
    )(proj, proj, proj, proj, *params)
    return ops, vt.reshape(t, RW_VM, SUBLANES, LANES), bonus


def _l2norm(x):
    return x * lax.rsqrt(jnp.sum(x * x, axis=-1, keepdims=True) + EPS)


def _rw_gather_matrix(nb):
    vs = LANES // RW_HEADS
    q = np.zeros((RW_NPIECE, vs, RW_KLO, nb, RW_HEADS, nb, vs, RW_HEADS), np.float32)
    for b in range(nb):
        for v in range(vs):
            for h in range(RW_HEADS):
                q[:, v, :, b, h, b, v, h] = 1.0
    return jnp.asarray(q.reshape(RW_NPIECE * vs * LANES, nb * LANES), BF16)


def _rw_post_kernel(yf_ref, yb_ref, bf_ref, bb_ref, g_ref, gnw_ref, gnb_ref, g2_ref, hsum_ref, q_ref, o_ref):
    nb = bf_ref.shape[0]
    tb = yf_ref.shape[0]
    pieces = _split3((yf_ref[...] + yb_ref[...]).reshape(tb, RW_HEAD * LANES))
    slab = (LANES // RW_HEADS) * LANES
    nslab = GROUP // LANES
    token_major = [_dot(jnp.concatenate([p[:, g * slab:(g + 1) * slab] for p in pieces], axis=1), q_ref[...])
                   for g in range(nslab)]

    def head_sum(z):
        hi, lo = _split_bf16(z)
        return _dot(hi, hsum_ref[...]) + _dot(lo, hsum_ref[...])

    yy = jnp.concatenate([jnp.concatenate([tm[:, b * LANES:(b + 1) * LANES] for tm in token_major], axis=1)
                          for b in range(nb)], axis=0)
    d = yy - head_sum(yy) * (1.0 / RW_HEAD)
    var = head_sum(d * d) * (1.0 / RW_HEAD)
    yn = d * lax.rsqrt(var + RW_GN_EPS) * gnw_ref[...] + gnb_ref[...]
    for b in range(nb):
        y_rw = yn[b * tb:(b + 1) * tb] + (bf_ref[b] + bb_ref[b])
        gate = _dot(jax.nn.sigmoid(g_ref[b]).astype(BF16), g2_ref[...])
        o_ref[b] = (y_rw * gate).astype(o_ref.dtype)


def _rw_post(y_f, y_b, bonus_f, bonus_b, proj, lp):
    nb, t, _ = bonus_f.shape
    tb = min(RW_PREP_BLOCK, t)
    perm = _rw_channel_perm()
    gnw = lp['rw_gn_w'][perm].reshape(1, GROUP)
    gnb = lp['rw_gn_b'][perm].reshape(1, GROUP)
    g2 = jnp.zeros((LANES, GROUP), F32).at[:RW_GATE_LR].set(lp['rw_g2'][:, perm]).astype(BF16)
    lane = np.arange(GROUP)
    hsum = jnp.asarray((lane[:, None] % RW_HEADS) == (lane[None, :] % RW_HEADS), BF16)
    q = _rw_gather_matrix(nb)
    params = (gnw, gnb, g2, hsum, q)
    full = lambda a: pl.BlockSpec(a.shape, lambda n: (0,) * a.ndim)
    tok = pl.BlockSpec((nb, tb, GROUP), lambda n: (0, n, 0))
    return pl.pallas_call(
        _rw_post_kernel,
        grid=(t // tb,),
        in_specs=[pl.BlockSpec((tb, RW_HEAD, LANES), lambda n: (n, 0, 0)),
                  pl.BlockSpec((tb, RW_HEAD, LANES), lambda n: (n, 0, 0)), tok, tok,
                  pl.BlockSpec((nb, tb, LANES), lambda n: (0, n, _col_block('r_g')))] + [full(p) for p in params],
        out_specs=tok,
        out_shape=jax.ShapeDtypeStruct((nb, t, GROUP), BF16),
        compiler_params=pltpu.CompilerParams(
            dimension_semantics=("arbitrary",), vmem_limit_bytes=RW_PREP_VMEM_LIMIT),
        name="rw_post",
    )(y_f.reshape(t, RW_HEAD, LANES), y_b.reshape(t, RW_HEAD, LANES), bonus_f, bonus_b, proj, *params)


def _rw_group(proj_c, proj_x, lp):
    s = jnp.zeros(RW_STATE_SHAPE, F32)
    outs = []
    for proj in (proj_c, proj_x):
        ops_f, v_f, bonus_f = _rw_prep(proj, lp, 0)
        ops_b, v_b, bonus_b = _rw_prep(proj, lp, 1)
        y_f, y_b, s = _rwkv_scan(ops_f, ops_b, v_f, v_b, s)
        outs.append(_rw_post(y_f, y_b, bonus_f, bonus_b, proj, lp))
    return outs


TOKEN_BLOCK = 512
HI = lax.Precision.HIGHEST


def _dot(a, b, precision=None):
    return jnp.dot(a, b, preferred_element_type=F32, precision=precision)


def _dot_nt(a, b):
    return lax.dot_general(a, b, (((1,), (1,)), ((), ())), preferred_element_type=F32)


def _dot_tn(a, b):
    return lax.dot_general(a, b, (((0,), (0,)), ((), ())), preferred_element_type=F32)


def _split_bf16(x):
    hi = x.astype(BF16)
    return hi, (x - hi.astype(F32)).astype(BF16)


def _dot3(a_hi, a_lo, b_hi, b_lo):
    return _dot(a_hi, b_hi) + (_dot(a_hi, b_lo) + _dot(a_lo, b_hi))


def _silu(x):
    return x * jax.nn.sigmoid(x)


def _chunk_masks(reverse):
    ri = lax.broadcasted_iota(jnp.int32, (CHUNK, CHUNK), 0)
    ci = lax.broadcasted_iota(jnp.int32, (CHUNK, CHUNK), 1)
    incl = (ci >= ri) if reverse else (ci <= ri)
    strict = (ci > ri) if reverse else (ci < ri)
    return incl, strict, (ri == ci).astype(F32)


def _conv3_block(x, cw, row_len):
    tb = x.shape[0]
    pos = lax.broadcasted_iota(jnp.int32, (tb, 1), 0) % row_len
    x_prev = jnp.where(pos == 0, 0.0, pltpu.roll(x, 1, 0))
    x_next = jnp.where(pos == row_len - 1, 0.0, pltpu.roll(x, tb - 1, 0))
    return x_prev * cw[0:1] + x * cw[1:2] + x_next * cw[2:3]


def _gdn_kernel(*refs, reverse, finalize, row_len, dirn):
    if finalize:
        (qkv_ref, ab_ref, z_ref, ob_ref, cw_ref, alog_ref, dtb_ref, g_ref, s0_ref,
         o_ref, sfin_ref, s_scr) = refs
    else:
        qkv_ref, ab_ref, cw_ref, alog_ref, dtb_ref, s0_ref, o_ref, sfin_ref, s_scr = refs
    n = pl.program_id(1)

    @pl.when(n == 0)
    def _():
        s_scr[...] = s0_ref[0]

    tb = qkv_ref.shape[1]
    qkv = _silu(_conv3_block(qkv_ref[0], cw_ref[...], row_len))
    ab = ab_ref[0]
    log_a = -jnp.exp(alog_ref[...]) * jax.nn.softplus(ab + dtb_ref[...])
    beta = jax.nn.sigmoid(ab)
    incl, strict, eye = _chunk_masks(reverse)
    tri = incl.astype(F32)
    last = 0 if reverse else CHUNK - 1
    hd = GDN_DK
    nchunk = tb // CHUNK
    order = list(range(nchunk - 1, -1, -1) if reverse else range(nchunk))
    rows = lambda c: slice(c * CHUNK, (c + 1) * CHUNK)
    cum_all = _dot(tri, jnp.concatenate([log_a[rows(c)] for c in range(nchunk)], axis=1), HI)

    items = [(c, h) for c in order for h in range(GDN_HEADS)]
    pre = {}
    for c, h in items:
        col = dirn * GDN_HEADS + h
        cum = cum_all[:, c * LANES + col:c * LANES + col + 1]
        bet = beta[rows(c), 2 * GDN_HEADS + col:2 * GDN_HEADS + col + 1]
        q = _l2norm(qkv[rows(c), h * hd:(h + 1) * hd]) * hd ** -0.5
        k = _l2norm(qkv[rows(c), GROUP + h * hd:GROUP + (h + 1) * hd])
        v = qkv[rows(c), 2 * GROUP + h * hd:2 * GROUP + (h + 1) * hd]
        cum_row = jnp.sum(eye * cum, axis=0, keepdims=True)
        decay = jnp.exp(jnp.where(incl, cum - cum_row, -jnp.inf))
        kb = k.astype(BF16)
        a = jnp.where(strict, bet * _dot_nt(kb, kb) * decay, 0.0)
        ecum = jnp.exp(cum)
        total = cum[last:last + 1]
        pre[c, h] = dict(
            x=-a, rhs=jnp.concatenate([v * bet, k * (bet * ecum)], axis=1),
            a_qk=(_dot_nt(q.astype(BF16), kb) * decay).astype(BF16),
            qe=(q * ecum).astype(BF16), k_end=(k * jnp.exp(total - cum)).astype(BF16),
            dec=jnp.exp(total))
    for p in pre.values():
        p['inv'] = eye + p['x']
    for level in range(6):
        for p in pre.values():
            x_hi, x_lo = _split_bf16(p['x'])
            if level == 0:
                p['x'] = _dot3(x_hi, x_lo, x_hi, x_lo)
            elif level < 5:
                i_hi, i_lo = _split_bf16(p['inv'])
                both = _dot3(jnp.concatenate([x_hi, i_hi], axis=0), jnp.concatenate([x_lo, i_lo], axis=0),
                             x_hi, x_lo)
                p['x'] = both[:CHUNK]
                p['inv'] = p['inv'] + both[CHUNK:]
            else:
                i_hi, i_lo = _split_bf16(p['inv'])
                p['inv'] = p['inv'] + _dot3(i_hi, i_lo, x_hi, x_lo)
    for p in pre.values():
        i_hi, i_lo = _split_bf16(p['inv'])
        r_hi, r_lo = _split_bf16(p['rhs'])
        sol = _dot3(i_hi, i_lo, r_hi, r_lo)
        p['u'], p['w'] = sol[:, :GDN_DV], sol[:, GDN_DV:].astype(BF16)

    for c, h in items:
        p = pre[c, h]
        s = s_scr[h]
        sb = s.astype(BF16)
        v_new = p['u'] - _dot(p['w'], sb)
        vb = v_new.astype(BF16)
        o = _dot(p['qe'], sb) + _dot(p['a_qk'], vb)
        s_scr[h] = p['dec'] * s + _dot_tn(p['k_end'], vb)
        cols = slice(h * GDN_DV, (h + 1) * GDN_DV)
        if finalize:
            o = o + ob_ref[0, rows(c), cols]
            o = o * lax.rsqrt(jnp.mean(o * o, axis=-1, keepdims=True) + EPS) * g_ref[...]
            o = o * _silu(z_ref[0, rows(c), cols])
        o_ref[0, rows(c), cols] = o.astype(o_ref.dtype)

    @pl.when(n == pl.num_programs(1) - 1)
    def _():
        sfin_ref[0] = s_scr[...]


def _gdn_pass(proj, o_other, s0, lp, *, dirn, row_len):
    b, t, _ = proj.shape
    tb = min(TOKEN_BLOCK, t)
    assert t % tb == 0 and tb % row_len == 0
    nblk = t // tb
    reverse = dirn == 1
    finalize = o_other is not None
    tok = (lambda i, n: (i, nblk - 1 - n)) if reverse else (lambda i, n: (i, n))
    seg = lambda name, wd: pl.BlockSpec((1, tb, wd), lambda i, n: tok(i, n) + (_col_block(name),))
    full = lambda a: pl.BlockSpec(a.shape, lambda i, n: (0,) * a.ndim)
    lane_row = lambda vals: jnp.pad(vals.reshape(1, -1), ((0, 0), (0, LANES - vals.size)))
    cw = lp['gdn_conv']
    alog = lane_row(lp['gdn_a_log'])
    dtb = lane_row(lp['gdn_dt_bias'])
    g = lp['gdn_norm_g'].reshape(1, GDN_DV)
    state_spec = pl.BlockSpec((1, GDN_HEADS, GDN_DK, GDN_DV), lambda i, n: (i, 0, 0, 0))
    if finalize:
        args = (proj, proj, proj, o_other, cw, alog, dtb, g, s0)
        in_specs = [seg('d_qkv', 3 * GROUP), seg('d_ab', LANES), seg('d_z', GROUP),
                    pl.BlockSpec((1, tb, GROUP), lambda i, n: tok(i, n) + (0,)),
                    full(cw), full(alog), full(dtb), full(g), state_spec]
    else:
        args = (proj, proj, cw, alog, dtb, s0)
        in_specs = [seg('d_qkv', 3 * GROUP), seg('d_ab', LANES), full(cw), full(alog), full(dtb), state_spec]
    return pl.pallas_call(
        functools.partial(_gdn_kernel, reverse=reverse, finalize=finalize, row_len=row_len, dirn=dirn),
        grid=(b, nblk),
        in_specs=in_specs,
        out_specs=[pl.BlockSpec((1, tb, GROUP), lambda i, n: tok(i, n) + (0,)), state_spec],
        out_shape=[jax.ShapeDtypeStruct((b, t, GROUP), BF16 if finalize else F32),
                   jax.ShapeDtypeStruct((b, GDN_HEADS, GDN_DK, GDN_DV), F32)],
        scratch_shapes=[pltpu.VMEM((GDN_HEADS, GDN_DK, GDN_DV), F32)],
        compiler_params=pltpu.CompilerParams(
            dimension_semantics=("arbitrary", "arbitrary"), vmem_limit_bytes=VMEM_LIMIT),
        name="gdn_fwd" if finalize else "gdn_bwd",
    )(*args)


def _gla_kernel(*refs, reverse, finalize):
    if finalize:
        (q_ref, k_ref, v_ref, lo_ref, r_ref, ob_ref, up_ref, ab_ref, g_ref, s0_ref,
         o_ref, sfin_ref, s_scr) = refs
    else:
        q_ref, k_ref, v_ref, lo_ref, up_ref, ab_ref, s0_ref, o_ref, sfin_ref, s_scr = refs
    n = pl.program_id(1)

    @pl.when(n == 0)
    def _():
        s_scr[...] = s0_ref[0]

    tb = q_ref.shape[1]
    gate = _dot(lo_ref[0].astype(BF16), up_ref[...].astype(BF16)) + ab_ref[...]
    log_f = jax.nn.log_sigmoid(gate) / GLA_TAU
    incl, _, eye = _chunk_masks(reverse)
    tri = incl.astype(F32)
    last = 0 if reverse else CHUNK - 1
    nchunk = tb // CHUNK
    order = range(nchunk - 1, -1, -1) if reverse else range(nchunk)
    for c in order:
        sl = slice(c * CHUNK, (c + 1) * CHUNK)
        cum = _dot(tri, log_f[sl], HI)
        total = cum[last:last + 1]
        q_dec = q_ref[0, sl, :] * GLA_DK ** -0.5 * jnp.exp(cum)
        k = k_ref[0, sl, :]
        k_inv = (k * jnp.exp(-cum)).astype(BF16)
        k_end = (k * jnp.exp(total - cum)).astype(BF16)
        q_dec = q_dec.astype(BF16)
        dec_row = jnp.exp(total)
        for h in range(GLA_HEADS):
            kc = slice(h * GLA_DK, (h + 1) * GLA_DK)
            vc = slice(h * GLA_DV, (h + 1) * GLA_DV)
            vb = v_ref[0, sl, vc].astype(BF16)
            att = jnp.where(incl, _dot_nt(q_dec[:, kc], k_inv[:, kc]), 0.0)
            s = s_scr[h]
            o = _dot(att.astype(BF16), vb) + _dot(q_dec[:, kc], s.astype(BF16))
            dec_col = jnp.sum(eye * dec_row[:, kc], axis=1, keepdims=True)
            s_scr[h] = dec_col * s + _dot_tn(k_end[:, kc], vb)
            if finalize:
                o = o + ob_ref[0, sl, vc]
                o = o * lax.rsqrt(jnp.mean(o * o, axis=-1, keepdims=True) + EPS) * g_ref[...]
                o = o * _silu(r_ref[0, sl, vc])
            o_ref[0, sl, vc] = o.astype(o_ref.dtype)

    @pl.when(n == pl.num_programs(1) - 1)
    def _():
        sfin_ref[0] = s_scr[...]


def _gla_pass(proj, o_other, s0, lp, *, dirn):
    b, t, _ = proj.shape
    tb = min(TOKEN_BLOCK, t)
    assert t % tb == 0
    nblk = t // tb
    reverse = dirn == 1
    finalize = o_other is not None
    tok = (lambda i, n: (i, nblk - 1 - n)) if reverse else (lambda i, n: (i, n))
    seg = lambda name: pl.BlockSpec((1, tb, SEG[name][2]), lambda i, n: tok(i, n) + (_col_block(name),))
    full = lambda a: pl.BlockSpec(a.shape, lambda i, n: (0,) * a.ndim)
    hk = GLA_HEADS * GLA_DK
    up = jnp.pad(lp['gla_a_up'][dirn], ((0, LANES - GLA_LR), (0, 0)))
    ab = lp['gla_a_b'][dirn].reshape(1, hk)
    g = lp['gla_norm_g'].reshape(1, GLA_DV)
    state_spec = pl.BlockSpec((1, GLA_HEADS, GLA_DK, GLA_DV), lambda i, n: (i, 0, 0, 0))
    if finalize:
        args = (proj, proj, proj, proj, proj, o_other, up, ab, g, s0)
        in_specs = [seg('g_q'), seg('g_k'), seg('g_v'), seg('g_lo'), seg('g_r'),
                    pl.BlockSpec((1, tb, GROUP), lambda i, n: tok(i, n) + (0,)),
                    full(up), full(ab), full(g), state_spec]
    else:
        args = (proj, proj, proj, proj, up, ab, s0)
        in_specs = [seg('g_q'), seg('g_k'), seg('g_v'), seg('g_lo'), full(up), full(ab), state_spec]
    return pl.pallas_call(
        functools.partial(_gla_kernel, reverse=reverse, finalize=finalize),
        grid=(b, nblk),
        in_specs=in_specs,
        out_specs=[pl.BlockSpec((1, tb, GROUP), lambda i, n: tok(i, n) + (0,)), state_spec],
        out_shape=[jax.ShapeDtypeStruct((b, t, GROUP), BF16 if finalize else F32),
                   jax.ShapeDtypeStruct((b, GLA_HEADS, GLA_DK, GLA_DV), F32)],
        scratch_shapes=[pltpu.VMEM((GLA_HEADS, GLA_DK, GLA_DV), F32)],
        compiler_params=pltpu.CompilerParams(
            dimension_semantics=("arbitrary", "arbitrary"), vmem_limit_bytes=VMEM_LIMIT),
        name="gla_fwd" if finalize else "gla_bwd",
    )(*args)


def _two_direction_group(pass_fn, state_shape, proj_c, proj_x, **kw):
    zero = jnp.zeros((proj_x[0].shape[0],) + state_shape, F32)
    ob_c, sb_c = pass_fn(proj_c[0], None, zero, dirn=1, **proj_c[1], **kw)
    ob_x, _ = pass_fn(proj_x[0], None, sb_c, dirn=1, **proj_x[1], **kw)
    y_c, sf_c = pass_fn(proj_c[0], ob_c, zero, dirn=0, **proj_c[1], **kw)
    y_x, _ = pass_fn(proj_x[0], ob_x, sf_c, dirn=0, **proj_x[1], **kw)
    return y_c, y_x


def _gla_group_pallas(proj_c, proj_x, lp):
    return _two_direction_group(_gla_pass, (GLA_HEADS, GLA_DK, GLA_DV), (proj_c, {}), (proj_x, {}), lp=lp)


def _gdn_group_pallas(proj_c, proj_x, lp):
    return _two_direction_group(_gdn_pass, (GDN_HEADS, GDN_DK, GDN_DV),
                                (proj_c, dict(row_len=proj_c.shape[1])), (proj_x, dict(row_len=GRID_W)), lp=lp)


def _sc_kernel(b_ref, c_ref, h_ref, cw_ref, o_ref, *, row_len):
    o_ref[0] = (b_ref[0] * _conv3_block(c_ref[0] * h_ref[0], cw_ref[...], row_len)).astype(o_ref.dtype)


def _sc_group(proj, row_len, lp):
    b, t, _ = proj.shape
    tb = min(TOKEN_BLOCK, t)
    assert t % tb == 0 and tb % row_len == 0
    seg = lambda name: pl.BlockSpec((1, tb, GROUP), lambda i, n: (i, n, _col_block(name)))
    cw = lp['sc_conv']
    return pl.pallas_call(
        functools.partial(_sc_kernel, row_len=row_len),
        grid=(b, t // tb),
        in_specs=[seg('c_b'), seg('c_c'), seg('c_h'), pl.BlockSpec(cw.shape, lambda i, n: (0, 0))],
        out_specs=pl.BlockSpec((1, tb, GROUP), lambda i, n: (i, n, 0)),
        out_shape=jax.ShapeDtypeStruct((b, t, GROUP), BF16),
        compiler_params=pltpu.CompilerParams(
            dimension_semantics=("arbitrary", "arbitrary"), vmem_limit_bytes=VMEM_LIMIT),
        name="short_conv",
    )(proj, proj, proj, cw)


def _mixers(proj_c, proj_x, lp):
    tc = proj_c.shape[1]
    y_gla_c, y_gla_x = _gla_group_pallas(proj_c, proj_x, lp)
    y_gdn_c, y_gdn_x = _gdn_group_pallas(proj_c, proj_x, lp)
    y_sc_c = _sc_group(proj_c, tc, lp)
    y_sc_x = _sc_group(proj_x, GRID_W, lp)
    y_rw_c, y_rw_x = _rw_group(proj_c, proj_x, lp)
    return (y_gla_c, y_gdn_c, y_sc_c, y_rw_c), (y_gla_x, y_gdn_x, y_sc_x, y_rw_x)


def kernel(x, c, ctx, c_ctx, ada_w, ada_b, norm1_g, norm2_g, w_in, w_out, gla_a_up, gla_a_b, gla_norm_g, gdn_conv, gdn_a_log, gdn_dt_bias, gdn_norm_g, sc_conv, rw_mu_rkv, rw_mu_wa, rw_w0, rw_w2, rw_a0, rw_a2, rw_g2, rw_kk, rw_ka, rw_rk, rw_gn_w, rw_gn_b, ffn_w_gu, ffn_w_down, final_g):
    nb, t, d = x.shape
    tc = ctx.shape[1]
    depth = w_in.shape[0]
    assert d == D_MODEL and t % 512 == 0 and tc % CHUNK == 0 and nb + 1 <= SUBLANES

    cond = jnp.concatenate([c, c_ctx[None, :], jnp.zeros((SUBLANES - nb - 1, d), F32)], axis=0)
    mod = _modulation(cond, ada_w, ada_b)

    tm_x = 512
    tm_c = tc
    h_x, h_c = x, ctx
    for l in range(depth):
        lp = dict(gla_a_up=gla_a_up[l], gla_a_b=gla_a_b[l],
                  gla_norm_g=gla_norm_g[l], gdn_conv=gdn_conv[l], gdn_a_log=gdn_a_log[l],
                  gdn_dt_bias=gdn_dt_bias[l], gdn_norm_g=gdn_norm_g[l], sc_conv=sc_conv[l],
                  rw_mu_rkv=rw_mu_rkv[l], rw_mu_wa=rw_mu_wa[l], rw_w0=rw_w0[l], rw_w2=rw_w2[l],
                  rw_a0=rw_a0[l], rw_a2=rw_a2[l], rw_g2=rw_g2[l], rw_kk=rw_kk[l], rw_ka=rw_ka[l],
                  rw_rk=rw_rk[l], rw_gn_w=rw_gn_w[l], rw_gn_b=rw_gn_b[l])
        m_x = [mod[l, :nb, i * d:(i + 1) * d][:, None, :] for i in range(6)]
        m_c = [jnp.broadcast_to(mod[l, nb, i * d:(i + 1) * d][None, None, :], (nb, 1, d)) for i in range(6)]
        w_in_l = _pad_w_in(w_in[l])
        w_out_l = jnp.concatenate([w_out[l][:3 * GROUP], w_out[l][3 * GROUP:][_rw_channel_perm()]],
                                  axis=0).astype(BF16)
        w_gu_l = ffn_w_gu[l].astype(BF16)
        w_dn_l = ffn_w_down[l].astype(BF16)

        proj_c = _norm_proj(h_c, norm1_g[l], m_c[1], m_c[0], w_in_l, tm_c, 1024)
        proj_x = _norm_proj(h_x, norm1_g[l], m_x[1], m_x[0], w_in_l, 2 * tm_x, 1024)
        y_c, y_x = _mixers(proj_c, proj_x, lp)

        h_x = _groups_residual(y_x, w_out_l, h_x, m_x[2], 2 * tm_x, 1024)
        a_x = _norm_swiglu(h_x, norm2_g[l], m_x[4], m_x[3], w_gu_l, tm_x, 1408)
        h_x = _proj_residual(a_x, w_dn_l, h_x, m_x[5], 2 * tm_x, 512)
        if l < depth - 1:
            h_c = _groups_residual(y_c, w_out_l, h_c, m_c[2], tm_c, 512)
            a_c = _norm_swiglu(h_c, norm2_g[l], m_c[4], m_c[3], w_gu_l, tm_c, 512)
            h_c = _proj_residual(a_c, w_dn_l, h_c, m_c[5], tm_c, 512)
    return _final_norm(h_x, final_g, tm_x)
```

```python
import functools
import math

import numpy as np
import jax
import jax.numpy as jnp
from jax import lax
from jax.experimental import pallas as pl
from jax.experimental.pallas import tpu as pltpu

F32 = jnp.float32
BF16 = jnp.bfloat16

D_MODEL = 2048
GROUP = D_MODEL // 4
CHUNK = 64
EPS = 1e-6
GRID_W = 64

GLA_HEADS = 4
GLA_DK = 64
GLA_DV = 128
GLA_LR = 16
GLA_TAU = 16.0
GDN_HEADS = 4
GDN_DK = 128
GDN_DV = 128
RW_HEAD = 64
RW_HEADS = GROUP // RW_HEAD
RW_DECAY_LR = 32
RW_A_LR = 32
RW_GATE_LR = 96
RW_DECAY_SCALE = math.exp(-0.5)
RW_GN_EPS = 64e-5
FFN = -(-(8 * D_MODEL) // (3 * 256)) * 256

SUBLANES = 8
LANES = 128
VMEM_LIMIT = 56 * 1024 * 1024

_SRC_SEGS = (
    ('g_q', 256), ('g_k', 256), ('g_v', 512), ('g_r', 512), ('g_lo', GLA_LR),
    ('d_qkv', 1536), ('d_z', 512), ('d_ab', 16),
    ('c_b', 512), ('c_c', 512), ('c_h', 512),
    ('r_rkv', 1536), ('r_wa', 64), ('r_g', RW_GATE_LR),
)
_DST_ORDER = ('g_q', 'g_k', 'g_v', 'g_r', 'd_qkv', 'r_rkv', 'd_z', 'c_b', 'c_c', 'c_h',
              'g_lo', 'd_ab', 'r_wa', 'r_g')


def _round_up(n, m):
    return -(-n // m) * m


def _seg_layout():
    width = dict(_SRC_SEGS)
    src, o = {}, 0
    for name, wd in _SRC_SEGS:
        src[name] = o
        o += wd
    dst, o = {}, 0
    for name in _DST_ORDER:
        pw = _round_up(width[name], LANES)
        assert o % pw == 0, name
        dst[name] = (o, width[name], pw)
        o += pw
    return src, dst, o


SEG_SRC, SEG, PROJ_PAD = _seg_layout()


def _pad_w_in(w):
    parts = []
    for name in _DST_ORDER:
        _, wd, pw = SEG[name]
        seg = w[:, SEG_SRC[name]:SEG_SRC[name] + wd]
        if name == 'r_rkv':
            perm = _rw_channel_perm()
            seg = jnp.concatenate([seg[:, i * GROUP:(i + 1) * GROUP][:, perm] for i in range(3)], axis=1)
        if pw != wd:
            seg = jnp.pad(seg, ((0, 0), (0, pw - wd)))
        parts.append(seg)
    return jnp.concatenate(parts, axis=1).astype(BF16)


def _col_block(name):
    o, _, pw = SEG[name]
    return o // pw


def _mod_kernel(c_ref, w_ref, b_ref, o_ref):
    x = c_ref[...]
    x = (x * jax.nn.sigmoid(x)).astype(BF16)
    o_ref[0] = jnp.dot(x, w_ref[0].astype(BF16), preferred_element_type=F32) + b_ref[0]


def _modulation(cond, ada_w, ada_b):
    nl, d, n = ada_w.shape
    tn = 1024
    return pl.pallas_call(
        _mod_kernel,
        grid=(nl, n // tn),
        in_specs=[pl.BlockSpec((SUBLANES, d), lambda l, j: (0, 0)),
                  pl.BlockSpec((1, d, tn), lambda l, j: (l, 0, j)),
                  pl.BlockSpec((1, 1, tn), lambda l, j: (l, 0, j))],
        out_specs=pl.BlockSpec((1, SUBLANES, tn), lambda l, j: (l, 0, j)),
        out_shape=jax.ShapeDtypeStruct((nl, SUBLANES, n), F32),
        compiler_params=pltpu.CompilerParams(
            dimension_semantics=("arbitrary", "arbitrary"), vmem_limit_bytes=VMEM_LIMIT),
        name="modulation",
    )(cond, ada_w, ada_b.reshape(nl, 1, n))


def _modnorm_rows(h_ref, g_ref, sc_ref, sh_ref):
    x = h_ref[0]
    y = x * lax.rsqrt(jnp.mean(x * x, axis=-1, keepdims=True) + EPS) * g_ref[...]
    return (y * (1.0 + sc_ref[0]) + sh_ref[0]).astype(BF16)


def _norm_proj_kernel(h_ref, g_ref, sc_ref, sh_ref, w_ref, o_ref, a_scr):
    @pl.when(pl.program_id(2) == 0)
    def _():
        a_scr[...] = _modnorm_rows(h_ref, g_ref, sc_ref, sh_ref)

    o_ref[0] = jnp.dot(a_scr[...], w_ref[...], preferred_element_type=F32).astype(o_ref.dtype)


def _norm_proj(h, g, scale, shift, w, tm, tn):
    b, t, d = h.shape
    n = w.shape[1]
    return pl.pallas_call(
        _norm_proj_kernel,
        grid=(b, t // tm, n // tn),
        in_specs=[pl.BlockSpec((1, tm, d), lambda i, m, j: (i, m, 0)),
                  pl.BlockSpec((1, d), lambda i, m, j: (0, 0)),
                  pl.BlockSpec((1, 1, d), lambda i, m, j: (i, 0, 0)),
                  pl.BlockSpec((1, 1, d), lambda i, m, j: (i, 0, 0)),
                  pl.BlockSpec((d, tn), lambda i, m, j: (0, j))],
        out_specs=pl.BlockSpec((1, tm, tn), lambda i, m, j: (i, m, j)),
        out_shape=jax.ShapeDtypeStruct((b, t, n), F32),
        scratch_shapes=[pltpu.VMEM((tm, d), BF16)],
        compiler_params=pltpu.CompilerParams(
            dimension_semantics=("arbitrary", "arbitrary", "arbitrary"), vmem_limit_bytes=VMEM_LIMIT),
        name="norm_proj",
    )(h, g.reshape(1, d), scale, shift, w)


def _norm_swiglu_kernel(h_ref, g_ref, sc_ref, sh_ref, wg_ref, wu_ref, o_ref, a_scr):
    @pl.when(pl.program_id(2) == 0)
    def _():
        a_scr[...] = _modnorm_rows(h_ref, g_ref, sc_ref, sh_ref)

    a = a_scr[...]
    half = wg_ref.shape[1] // 2
    for c in range(2):
        cols = slice(c * half, (c + 1) * half)
        gate = jnp.dot(a, wg_ref[:, cols], preferred_element_type=F32)
        up = jnp.dot(a, wu_ref[:, cols], preferred_element_type=F32)
        o_ref[0, :, cols] = (gate * jax.nn.sigmoid(gate) * up).astype(o_ref.dtype)


def _norm_swiglu(h, g, scale, shift, w_gu, tm, tn):
    b, t, d = h.shape
    f = w_gu.shape[1] // 2
    nj = f // tn
    return pl.pallas_call(
        _norm_swiglu_kernel,
        grid=(b, t // tm, nj),
        in_specs=[pl.BlockSpec((1, tm, d), lambda i, m, j: (i, m, 0)),
                  pl.BlockSpec((1, d), lambda i, m, j: (0, 0)),
                  pl.BlockSpec((1, 1, d), lambda i, m, j: (i, 0, 0)),
                  pl.BlockSpec((1, 1, d), lambda i, m, j: (i, 0, 0)),
                  pl.BlockSpec((d, tn), lambda i, m, j: (0, j)),
                  pl.BlockSpec((d, tn), lambda i, m, j: (0, j + nj))],
        out_specs=pl.BlockSpec((1, tm, tn), lambda i, m, j: (i, m, j)),
        out_shape=jax.ShapeDtypeStruct((b, t, f), BF16),
        scratch_shapes=[pltpu.VMEM((tm, d), BF16)],
        compiler_params=pltpu.CompilerParams(
            dimension_semantics=("arbitrary", "arbitrary", "arbitrary"), vmem_limit_bytes=VMEM_LIMIT),
        name="norm_swiglu",
    )(h, g.reshape(1, d), scale, shift, w_gu, w_gu)


def _proj_residual_kernel(a_ref, w_ref, h_ref, gate_ref, o_ref):
    o_ref[0] = h_ref[0] + gate_ref[0] * jnp.dot(a_ref[0], w_ref[...], preferred_element_type=F32)


def _proj_residual(a, w, h, gate, tm, tn):
    b, t, k = a.shape
    d = w.shape[1]
    return pl.pallas_call(
        _proj_residual_kernel,
        grid=(b, t // tm, d // tn),
        in_specs=[pl.BlockSpec((1, tm, k), lambda i, m, j: (i, m, 0)),
                  pl.BlockSpec((k, tn), lambda i, m, j: (0, j)),
                  pl.BlockSpec((1, tm, tn), lambda i, m, j: (i, m, j)),
                  pl.BlockSpec((1, 1, tn), lambda i, m, j: (i, 0, j))],
        out_specs=pl.BlockSpec((1, tm, tn), lambda i, m, j: (i, m, j)),
        out_shape=jax.ShapeDtypeStruct((b, t, d), F32),
        compiler_params=pltpu.CompilerParams(
            dimension_semantics=("arbitrary", "arbitrary", "arbitrary"), vmem_limit_bytes=VMEM_LIMIT),
        name="proj_residual",
    )(a, w, h, gate)


def _groups_residual_kernel(*refs):
    *a_refs, w_ref, h_ref, gate_ref, o_ref = refs
    k = a_refs[0].shape[2]
    acc = jnp.dot(a_refs[0][0], w_ref[0:k, :], preferred_element_type=F32)
    for i, a_ref in enumerate(a_refs[1:], start=1):
        acc = acc + jnp.dot(a_ref[0], w_ref[i * k:(i + 1) * k, :], preferred_element_type=F32)
    o_ref[0] = h_ref[0] + gate_ref[0] * acc


def _groups_residual(groups, w, h, gate, tm, tn):
    b, t, k = groups[0].shape
    d = w.shape[1]
    assert len(groups) * k == w.shape[0]
    return pl.pallas_call(
        _groups_residual_kernel,
        grid=(b, t // tm, d // tn),
        in_specs=[pl.BlockSpec((1, tm, k), lambda i, m, j: (i, m, 0)) for _ in groups]
        + [pl.BlockSpec((w.shape[0], tn), lambda i, m, j: (0, j)),
           pl.BlockSpec((1, tm, tn), lambda i, m, j: (i, m, j)),
           pl.BlockSpec((1, 1, tn), lambda i, m, j: (i, 0, j))],
        out_specs=pl.BlockSpec((1, tm, tn), lambda i, m, j: (i, m, j)),
        out_shape=jax.ShapeDtypeStruct((b, t, d), F32),
        compiler_params=pltpu.CompilerParams(
            dimension_semantics=("arbitrary", "arbitrary", "arbitrary"), vmem_limit_bytes=VMEM_LIMIT),
        name="groups_residual",
    )(*groups, w, h, gate)


def _final_norm_kernel(h_ref, g_ref, o_ref):
    x = h_ref[0]
    o_ref[0] = x * lax.rsqrt(jnp.mean(x * x, axis=-1, keepdims=True) + EPS) * g_ref[...]


def _final_norm(h, g, tm):
    b, t, d = h.shape
    return pl.pallas_call(
        _final_norm_kernel,
        grid=(b, t // tm),
        in_specs=[pl.BlockSpec((1, tm, d), lambda i, m: (i, m, 0)),
                  pl.BlockSpec((1, d), lambda i, m: (0, 0))],
        out_specs=pl.BlockSpec((1, tm, d), lambda i, m: (i, m, 0)),
        out_shape=jax.ShapeDtypeStruct((b, t, d), F32),
        compiler_params=pltpu.CompilerParams(
            dimension_semantics=("arbitrary", "arbitrary"), vmem_limit_bytes=VMEM_LIMIT),
        name="final_norm",
    )(h, g.reshape(1, d))


RW_CHAINS = 32
RW_KLO = LANES // RW_CHAINS
RW_KQ = RW_HEAD // RW_KLO
RW_VM = RW_HEAD // SUBLANES
RW_VGROUP = 4
RW_STEP_BLOCK = 16
OP_W, OP_B, OP_K, OP_R, OP_KK = range(5)


def _lane_group_sum(p):
    out = p
    for g in range(1, RW_KLO):
        out = out + pltpu.roll(p, g * RW_CHAINS, 1)
    return out


def _rwkv_kernel(opsf_ref, opsb_ref, vf_ref, vb_ref, s0_ref, yf_ref, yb_ref, sfin_ref,
                 s_ref, saf_ref, sab_ref, pb_ref):
    tb = opsf_ref.shape[0]
    dirs = ((opsf_ref, vf_ref, yf_ref), (opsb_ref, vb_ref, yb_ref))

    def row(ops_ref, t, j, q):
        lane0 = (j * RW_KQ + q) * LANES
        return ops_ref[t:t + 1, lane0:lane0 + LANES]

    @pl.when(pl.program_id(0) == 0)
    def _():
        s_ref[...] = s0_ref[...]

    zero = jnp.zeros((SUBLANES, LANES), F32)
    groups = [range(m0, m0 + RW_VGROUP) for m0 in range(0, RW_VM, RW_VGROUP)]

    for d, (ops_ref, _, _) in enumerate(dirs):
        first = 0 if d == 0 else tb - 1
        for ms in groups:
            acc = {m: zero for m in ms}
            for q in range(RW_KQ):
                kk = row(ops_ref, first, OP_KK, q)
                for m in ms:
                    acc[m] = acc[m] + s_ref[d, q, m] * kk
            for m in ms:
                if d == 0:
                    saf_ref[m] = _lane_group_sum(acc[m])
                else:
                    pb_ref[m] = acc[m]

    def advance(d, i, sa_ref, next_ref, reduce_next):
        ops_ref, v_ref, y_ref = dirs[d]
        t = i if d == 0 else tb - 1 - i
        tn = min(i + 1, tb - 1) if d == 0 else max(tb - 2 - i, 0)
        for ms in groups:
            sa = {m: sa_ref[m] for m in ms}
            vt = {m: v_ref[t, m] for m in ms}
            acc_y = {m: zero for m in ms}
            acc_s = {m: zero for m in ms}
            for q in range(RW_KQ):
                w = row(ops_ref, t, OP_W, q)
                b = row(ops_ref, t, OP_B, q)
                kt = row(ops_ref, t, OP_K, q)
                r = row(ops_ref, t, OP_R, q)
                kkn = row(ops_ref, tn, OP_KK, q)
                for m in ms:
                    s = s_ref[d, q, m] * w - sa[m] * b + vt[m] * kt
                    s_ref[d, q, m] = s
                    acc_y[m] = acc_y[m] + s * r
                    acc_s[m] = acc_s[m] + s * kkn
            for m in ms:
                y_ref[t, m] = acc_y[m]
                next_ref[m] = _lane_group_sum(acc_s[m]) if reduce_next else acc_s[m]

    for i in range(tb):
        for m in range(RW_VM):
            sab_ref[m] = _lane_group_sum(pb_ref[m])
        advance(0, i, saf_ref, saf_ref, True)
        advance(1, i, sab_ref, pb_ref, False)

    @pl.when(pl.program_id(0) == pl.num_programs(0) - 1)
    def _():
        sfin_ref[...] = s_ref[...]


RW_STATE_SHAPE = (2, RW_KQ, RW_VM, SUBLANES, LANES)


def _rwkv_scan(ops_f, ops_b, v_f, v_b, s0):
    t = ops_f.shape[0]
    tb = RW_STEP_BLOCK
    nblk = t // tb
    fwd = lambda i: (i, 0, 0, 0)
    bwd = lambda i: (nblk - 1 - i, 0, 0, 0)
    ops_block = (tb, 5 * RW_KQ * LANES)
    v_block = (tb, RW_VM, SUBLANES, LANES)
    y_shape = jax.ShapeDtypeStruct((t, RW_VM, SUBLANES, LANES), F32)
    state_spec = pl.BlockSpec(RW_STATE_SHAPE, lambda i: (0,) * len(RW_STATE_SHAPE))
    return pl.pallas_call(
        _rwkv_kernel,
        grid=(nblk,),
        in_specs=[pl.BlockSpec(ops_block, lambda i: (i, 0)), pl.BlockSpec(ops_block, lambda i: (nblk - 1 - i, 0)),
                  pl.BlockSpec(v_block, fwd), pl.BlockSpec(v_block, bwd), state_spec],
        out_specs=[pl.BlockSpec(v_block, fwd), pl.BlockSpec(v_block, bwd), state_spec],
        out_shape=[y_shape, y_shape, jax.ShapeDtypeStruct(RW_STATE_SHAPE, F32)],
        scratch_shapes=[pltpu.VMEM(RW_STATE_SHAPE, F32)]
        + [pltpu.VMEM((RW_VM, SUBLANES, LANES), F32)] * 3,
        compiler_params=pltpu.CompilerParams(
            dimension_semantics=("arbitrary",), vmem_limit_bytes=VMEM_LIMIT),
        name="rwkv_scan",
    )(ops_f, ops_b, v_f, v_b, s0)


RW_PREP_BLOCK = 128
RW_PREP_VMEM_LIMIT = 56 * 1024 * 1024
RW_NPIECE = 3


def _rw_channel_perm():
    return np.arange(GROUP).reshape(RW_HEADS, RW_HEAD).T.reshape(-1)


def _rw_relayout_matrices(nb):
    slab = LANES
    ch_per_slab = slab // RW_HEADS
    pk = np.zeros((nb, RW_NPIECE, slab, ch_per_slab // RW_KLO, RW_KLO, nb, RW_HEADS), np.float32)
    pv = np.zeros((nb, RW_NPIECE, slab, ch_per_slab, RW_KLO, nb, RW_HEADS), np.float32)
    for b in range(nb):
        for ch in range(ch_per_slab):
            for h in range(RW_HEADS):
                pk[b, :, ch * RW_HEADS + h, ch // RW_KLO, ch % RW_KLO, b, h] = 1.0
                pv[b, :, ch * RW_HEADS + h, ch, :, b, h] = 1.0
    rows = nb * RW_NPIECE * slab
    return (jnp.asarray(pk.reshape(rows, -1), BF16), jnp.asarray(pv.reshape(rows, -1), BF16))


def _split3(x):
    hi = x.astype(BF16)
    r1 = x - hi.astype(F32)
    mid = r1.astype(BF16)
    lo = (r1 - mid.astype(F32)).astype(BF16)
    return hi, mid, lo


def _rw_prep_kernel(rkv_ref, wa_ref, rkv_nb_ref, wa_nb_ref, mu_rkv_ref, mu_wa_ref, w0_ref, w2_ref,
                    a0_ref, a2_ref, kk_ref, ka_ref, rk_ref, hsum_ref, pk_ref, pv_ref,
                    ops_ref, v_ref, bonus_ref, *, backward):
    nb, tb, _ = rkv_ref.shape
    n = pl.program_id(0)
    edge = (n == pl.num_programs(0) - 1) if backward else (n == 0)
    nb_row = SUBLANES - 1 if not backward else 0
    row = lax.broadcasted_iota(jnp.int32, (tb, 1), 0)
    edge_row = (row == tb - 1) if backward else (row == 0)

    def shifted(x, neighbour):
        inner = pltpu.roll(x, tb - 1 if backward else 1, 0)
        outer = jnp.where(edge, 0.0, neighbour)
        return jnp.where(edge_row, outer, inner)

    def head_sum(z):
        hi, lo = _split_bf16(z)
        return _dot(hi, hsum_ref[...]) + _dot(lo, hsum_ref[...])

    part = []
    for b in range(nb):
        x = rkv_ref[b]
        x = x + (shifted(x, rkv_nb_ref[b, nb_row:nb_row + 1, :]) - x) * mu_rkv_ref[...]
        xw = wa_ref[b]
        xw = xw + (shifted(xw, wa_nb_ref[b, nb_row:nb_row + 1, :]) - xw) * mu_wa_ref[...]
        r, k, v = x[:, :GROUP], x[:, GROUP:2 * GROUP], x[:, 2 * GROUP:]
        decay = jnp.exp(-RW_DECAY_SCALE * jax.nn.sigmoid(
            w0_ref[...] + _dot(jnp.tanh(xw).astype(BF16), w2_ref[...])))
        a = jax.nn.sigmoid(a0_ref[...] + _dot(xw.astype(BF16), a2_ref[...]))
        kq = k * kk_ref[...]
        kt = k * (1.0 + (a - 1.0) * ka_ref[...])
        part.append((decay, a, kq, kt, r, v))
    sums = head_sum(jnp.concatenate([p[2] * p[2] for p in part]
                                    + [p[4] * p[3] * rk_ref[...] for p in part], axis=0))
    feats = []
    for b, (decay, a, kq, kt, r, v) in enumerate(part):
        kk = kq * lax.rsqrt(sums[b * tb:(b + 1) * tb] + EPS)
        bonus_ref[b] = sums[(nb + b) * tb:(nb + b + 1) * tb] * v
        feats.append([decay, kk * a, kt, r, kk, v])

    nslab = GROUP // LANES
    kw = pk_ref.shape[1]
    vw = pv_ref.shape[1]
    for j in range(6):
        pieces = [_split3(feats[b][j]) for b in range(nb)]
        lhs = jnp.concatenate(
            [jnp.concatenate([p[:, g * LANES:(g + 1) * LANES] for bp in pieces for p in bp], axis=1)
             for g in range(nslab)], axis=0)
        if j < 5:
            out = _dot(lhs, pk_ref[...])
            for g in range(nslab):
                ops_ref[:, (j * nslab + g) * kw:(j * nslab + g + 1) * kw] = out[g * tb:(g + 1) * tb]
        else:
            out = _dot(lhs, pv_ref[...])
            for g in range(nslab):
                rows = vw // LANES
                v_ref[:, g * rows:(g + 1) * rows, :] = out[g * tb:(g + 1) * tb].reshape(tb, rows, LANES)


def _rw_prep(proj, lp, dirn):
    nb, t, _ = proj.shape
    tb = min(RW_PREP_BLOCK, t)
    assert t % tb == 0 and nb * RW_HEADS == RW_CHAINS
    nblk = t // tb
    backward = dirn == 1
    per8 = tb // SUBLANES
    last8 = t // SUBLANES - 1
    if backward:
        nb_idx = lambda n: jnp.minimum((n + 1) * per8, last8)
    else:
        nb_idx = lambda n: jnp.maximum(n * per8 - 1, 0)
    perm = _rw_channel_perm()
    lane_row = lambda v: v.reshape(1, -1)
    pad_rows = lambda w, lo: jnp.zeros((LANES, GROUP), F32).at[lo:lo + w.shape[0]].set(w)
    mu_rkv = lane_row(jnp.concatenate([lp['rw_mu_rkv'][dirn][i * GROUP:(i + 1) * GROUP][perm] for i in range(3)]))
    mu_wa = lane_row(jnp.pad(lp['rw_mu_wa'][dirn], (0, LANES - RW_DECAY_LR - RW_A_LR)))
    w0 = lane_row(lp['rw_w0'][dirn][perm])
    w2 = pad_rows(lp['rw_w2'][dirn][:, perm], 0).astype(BF16)
    a0 = lane_row(lp['rw_a0'][dirn][perm])
    a2 = pad_rows(lp['rw_a2'][dirn][:, perm], RW_DECAY_LR).astype(BF16)
    k_k = lane_row(lp['rw_kk'][perm])
    k_a = lane_row(lp['rw_ka'][perm])
    r_k = lane_row(lp['rw_rk'].reshape(-1)[perm])
    lane = np.arange(GROUP)
    hsum = jnp.asarray((lane[:, None] % RW_HEADS) == (lane[None, :] % RW_HEADS), BF16)
    pk, pv = _rw_relayout_matrices(nb)
    params = (mu_rkv, mu_wa, w0, w2, a0, a2, k_k, k_a, r_k, hsum, pk, pv)
    full = lambda a: pl.BlockSpec(a.shape, lambda n: (0,) * a.ndim)
    ops_w = 5 * RW_KQ * LANES
    ops, vt, bonus = pl.pallas_call(
        functools.partial(_rw_prep_kernel, backward=backward),
        grid=(nblk,),
        in_specs=[pl.BlockSpec((nb, tb, 3 * GROUP), lambda n: (0, n, _col_block('r_rkv'))),
                  pl.BlockSpec((nb, tb, LANES), lambda n: (0, n, _col_block('r_wa'))),
                  pl.BlockSpec((nb, SUBLANES, 3 * GROUP), lambda n: (0, nb_idx(n), _col_block('r_rkv'))),
                  pl.BlockSpec((nb, SUBLANES, LANES), lambda n: (0, nb_idx(n), _col_block('r_wa')))]
        + [full(p) for p in params],
        out_specs=[pl.BlockSpec((tb, ops_w), lambda n: (n, 0)),
                   pl.BlockSpec((tb, RW_HEAD, LANES), lambda n: (n, 0, 0)),
                   pl.BlockSpec((nb, tb, GROUP), lambda n: (0, n, 0))],
        out_shape=[jax.ShapeDtypeStruct((t, ops_w), F32), jax.ShapeDtypeStruct((t, RW_HEAD, LANES), F32),
                   jax.ShapeDtypeStruct((nb, t, GROUP), F32)],
        compiler_params=pltpu.CompilerParams(
            dimension_semantics=("arbitrary",), vmem_limit_bytes=RW_PREP_VMEM_LIMIT),
        name="rw_prep",
    )(proj, proj, proj, proj, *params)
    return ops, vt.reshape(t, RW_VM, SUBLANES, LANES), bonus


def _l2norm(x):
    return x * lax.rsqrt(jnp.sum(x * x, axis=-1, keepdims=True) + EPS)


def _rw_gather_matrix(nb):
    vs = LANES // RW_HEADS
    q = np.zeros((RW_NPIECE, vs, RW_KLO, nb, RW_HEADS, nb, vs, RW_HEADS), np.float32)
    for b in range(nb):
        for v in range(vs):
            for h in range(RW_HEADS):
                q[:, v, :, b, h, b, v, h] = 1.0
    return jnp.asarray(q.reshape(RW_NPIECE * vs * LANES, nb * LANES), BF16)


def _rw_post_kernel(yf_ref, yb_ref, bf_ref, bb_ref, g_ref, gnw_ref, gnb_ref, g2_ref, hsum_ref, q_ref, o_ref):
    nb = bf_ref.shape[0]
    tb = yf_ref.shape[0]
    pieces = _split3((yf_ref[...] + yb_ref[...]).reshape(tb, RW_HEAD * LANES))
    slab = (LANES // RW_HEADS) * LANES
    nslab = GROUP // LANES
    token_major = [_dot(jnp.concatenate([p[:, g * slab:(g + 1) * slab] for p in pieces], axis=1), q_ref[...])
                   for g in range(nslab)]

    def head_sum(z):
        hi, lo = _split_bf16(z)
        return _dot(hi, hsum_ref[...]) + _dot(lo, hsum_ref[...])

    yy = jnp.concatenate([jnp.concatenate([tm[:, b * LANES:(b + 1) * LANES] for tm in token_major], axis=1)
                          for b in range(nb)], axis=0)
    d = yy - head_sum(yy) * (1.0 / RW_HEAD)
    var = head_sum(d * d) * (1.0 / RW_HEAD)
    yn = d * lax.rsqrt(var + RW_GN_EPS) * gnw_ref[...] + gnb_ref[...]
    for b in range(nb):
        y_rw = yn[b * tb:(b + 1) * tb] + (bf_ref[b] + bb_ref[b])
        gate = _dot(jax.nn.sigmoid(g_ref[b]).astype(BF16), g2_ref[...])
        o_ref[b] = (y_rw * gate).astype(o_ref.dtype)


def _rw_post(y_f, y_b, bonus_f, bonus_b, proj, lp):
    nb, t, _ = bonus_f.shape
    tb = min(RW_PREP_BLOCK, t)
    perm = _rw_channel_perm()
    gnw = lp['rw_gn_w'][perm].reshape(1, GROUP)
    gnb = lp['rw_gn_b'][perm].reshape(1, GROUP)
    g2 = jnp.zeros((LANES, GROUP), F32).at[:RW_GATE_LR].set(lp['rw_g2'][:, perm]).astype(BF16)
    lane = np.arange(GROUP)
    hsum = jnp.asarray((lane[:, None] % RW_HEADS) == (lane[None, :] % RW_HEADS), BF16)
    q = _rw_gather_matrix(nb)
    params = (gnw, gnb, g2, hsum, q)
    full = lambda a: pl.BlockSpec(a.shape, lambda n: (0,) * a.ndim)
    tok = pl.BlockSpec((nb, tb, GROUP), lambda n: (0, n, 0))
    return pl.pallas_call(
        _rw_post_kernel,
        grid=(t // tb,),
        in_specs=[pl.BlockSpec((tb, RW_HEAD, LANES), lambda n: (n, 0, 0)),
                  pl.BlockSpec((tb, RW_HEAD, LANES), lambda n: (n, 0, 0)), tok, tok,
                  pl.BlockSpec((nb, tb, LANES), lambda n: (0, n, _col_block('r_g')))] + [full(p) for p in params],
        out_specs=tok,
        out_shape=jax.ShapeDtypeStruct((nb, t, GROUP), BF16),
        compiler_params=pltpu.CompilerParams(
            dimension_semantics=("arbitrary",), vmem_limit_bytes=RW_PREP_VMEM_LIMIT),
        name="rw_post",
    )(y_f.reshape(t, RW_HEAD, LANES), y_b.reshape(t, RW_HEAD, LANES), bonus_f, bonus_b, proj, *params)


def _rw_group(proj_c, proj_x, lp):
    s = jnp.zeros(RW_STATE_SHAPE, F32)
    outs = []
    for proj in (proj_c, proj_x):
        ops_f, v_f, bonus_f = _rw_prep(proj, lp, 0)
        ops_b, v_b, bonus_b = _rw_prep(proj, lp, 1)
        y_f, y_b, s = _rwkv_scan(ops_f, ops_b, v_f, v_b, s)
        outs.append(_rw_post(y_f, y_b, bonus_f, bonus_b, proj, lp))
    return outs


TOKEN_BLOCK = 512
HI = lax.Precision.HIGHEST


def _dot(a, b, precision=None):
    return jnp.dot(a, b, preferred_element_type=F32, precision=precision)


def _dot_nt(a, b):
    return lax.dot_general(a, b, (((1,), (1,)), ((), ())), preferred_element_type=F32)


def _dot_tn(a, b):
    return lax.dot_general(a, b, (((0,), (0,)), ((), ())), preferred_element_type=F32)


def _split_bf16(x):
    hi = x.astype(BF16)
    return hi, (x - hi.astype(F32)).astype(BF16)


def _dot3(a_hi, a_lo, b_hi, b_lo):
    return _dot(a_hi, b_hi) + (_dot(a_hi, b_lo) + _dot(a_lo, b_hi))


def _silu(x):
    return x * jax.nn.sigmoid(x)


def _chunk_masks(reverse):
    ri = lax.broadcasted_iota(jnp.int32, (CHUNK, CHUNK), 0)
    ci = lax.broadcasted_iota(jnp.int32, (CHUNK, CHUNK), 1)
    incl = (ci >= ri) if reverse else (ci <= ri)
    strict = (ci > ri) if reverse else (ci < ri)
    return incl, strict, (ri == ci).astype(F32)


def _conv3_block(x, cw, row_len):
    tb = x.shape[0]
    pos = lax.broadcasted_iota(jnp.int32, (tb, 1), 0) % row_len
    x_prev = jnp.where(pos == 0, 0.0, pltpu.roll(x, 1, 0))
    x_next = jnp.where(pos == row_len - 1, 0.0, pltpu.roll(x, tb - 1, 0))
    return x_prev * cw[0:1] + x * cw[1:2] + x_next * cw[2:3]


def _gdn_kernel(*refs, reverse, finalize, row_len, dirn):
    if finalize:
        (qkv_ref, ab_ref, z_ref, ob_ref, cw_ref, alog_ref, dtb_ref, g_ref, s0_ref,
         o_ref, sfin_ref, s_scr) = refs
    else:
        qkv_ref, ab_ref, cw_ref, alog_ref, dtb_ref, s0_ref, o_ref, sfin_ref, s_scr = refs
    n = pl.program_id(1)

    @pl.when(n == 0)
    def _():
        s_scr[...] = s0_ref[0]

    tb = qkv_ref.shape[1]
    qkv = _silu(_conv3_block(qkv_ref[0], cw_ref[...], row_len))
    ab = ab_ref[0]
    log_a = -jnp.exp(alog_ref[...]) * jax.nn.softplus(ab + dtb_ref[...])
    beta = jax.nn.sigmoid(ab)
    incl, strict, eye = _chunk_masks(reverse)
    tri = incl.astype(F32)
    last = 0 if reverse else CHUNK - 1
    hd = GDN_DK
    nchunk = tb // CHUNK
    order = list(range(nchunk - 1, -1, -1) if reverse else range(nchunk))
    rows = lambda c: slice(c * CHUNK, (c + 1) * CHUNK)
    cum_all = _dot(tri, jnp.concatenate([log_a[rows(c)] for c in range(nchunk)], axis=1), HI)

    items = [(c, h) for c in order for h in range(GDN_HEADS)]
    pre = {}
    for c, h in items:
        col = dirn * GDN_HEADS + h
        cum = cum_all[:, c * LANES + col:c * LANES + col + 1]
        bet = beta[rows(c), 2 * GDN_HEADS + col:2 * GDN_HEADS + col + 1]
        q = _l2norm(qkv[rows(c), h * hd:(h + 1) * hd]) * hd ** -0.5
        k = _l2norm(qkv[rows(c), GROUP + h * hd:GROUP + (h + 1) * hd])
        v = qkv[rows(c), 2 * GROUP + h * hd:2 * GROUP + (h + 1) * hd]
        cum_row = jnp.sum(eye * cum, axis=0, keepdims=True)
        decay = jnp.exp(jnp.where(incl, cum - cum_row, -jnp.inf))
        kb = k.astype(BF16)
        a = jnp.where(strict, bet * _dot_nt(kb, kb) * decay, 0.0)
        ecum = jnp.exp(cum)
        total = cum[last:last + 1]
        pre[c, h] = dict(
            x=-a, rhs=jnp.concatenate([v * bet, k * (bet * ecum)], axis=1),
            a_qk=(_dot_nt(q.astype(BF16), kb) * decay).astype(BF16),
            qe=(q * ecum).astype(BF16), k_end=(k * jnp.exp(total - cum)).astype(BF16),
            dec=jnp.exp(total))
    for p in pre.values():
        p['inv'] = eye + p['x']
    for level in range(6):
        for p in pre.values():
            x_hi, x_lo = _split_bf16(p['x'])
            if level == 0:
                p['x'] = _dot3(x_hi, x_lo, x_hi, x_lo)
            elif level < 5:
                i_hi, i_lo = _split_bf16(p['inv'])
                both = _dot3(jnp.concatenate([x_hi, i_hi], axis=0), jnp.concatenate([x_lo, i_lo], axis=0),
                             x_hi, x_lo)
                p['x'] = both[:CHUNK]
                p['inv'] = p['inv'] + both[CHUNK:]
            else:
                i_hi, i_lo = _split_bf16(p['inv'])
                p['inv'] = p['inv'] + _dot3(i_hi, i_lo, x_hi, x_lo)
    for p in pre.values():
        i_hi, i_lo = _split_bf16(p['inv'])
        r_hi, r_lo = _split_bf16(p['rhs'])
        sol = _dot3(i_hi, i_lo, r_hi, r_lo)
        p['u'], p['w'] = sol[:, :GDN_DV], sol[:, GDN_DV:].astype(BF16)

    for c, h in items:
        p = pre[c, h]
        s = s_scr[h]
        sb = s.astype(BF16)
        v_new = p['u'] - _dot(p['w'], sb)
        vb = v_new.astype(BF16)
        o = _dot(p['qe'], sb) + _dot(p['a_qk'], vb)
        s_scr[h] = p['dec'] * s + _dot_tn(p['k_end'], vb)
        cols = slice(h * GDN_DV, (h + 1) * GDN_DV)
        if finalize:
            o = o + ob_ref[0, rows(c), cols]
            o = o * lax.rsqrt(jnp.mean(o * o, axis=-1, keepdims=True) + EPS) * g_ref[...]
            o = o * _silu(z_ref[0, rows(c), cols])
        o_ref[0, rows(c), cols] = o.astype(o_ref.dtype)

    @pl.when(n == pl.num_programs(1) - 1)
    def _():
        sfin_ref[0] = s_scr[...]


def _gdn_pass(proj, o_other, s0, lp, *, dirn, row_len):
    b, t, _ = proj.shape
    tb = min(TOKEN_BLOCK, t)
    assert t % tb == 0 and tb % row_len == 0
    nblk = t // tb
    reverse = dirn == 1
    finalize = o_other is not None
    tok = (lambda i, n: (i, nblk - 1 - n)) if reverse else (lambda i, n: (i, n))
    seg = lambda name, wd: pl.BlockSpec((1, tb, wd), lambda i, n: tok(i, n) + (_col_block(name),))
    full = lambda a: pl.BlockSpec(a.shape, lambda i, n: (0,) * a.ndim)
    lane_row = lambda vals: jnp.pad(vals.reshape(1, -1), ((0, 0), (0, LANES - vals.size)))
    cw = lp['gdn_conv']
    alog = lane_row(lp['gdn_a_log'])
    dtb = lane_row(lp['gdn_dt_bias'])
    g = lp['gdn_norm_g'].reshape(1, GDN_DV)
    state_spec = pl.BlockSpec((1, GDN_HEADS, GDN_DK, GDN_DV), lambda i, n: (i, 0, 0, 0))
    if finalize:
        args = (proj, proj, proj, o_other, cw, alog, dtb, g, s0)
        in_specs = [seg('d_qkv', 3 * GROUP), seg('d_ab', LANES), seg('d_z', GROUP),
                    pl.BlockSpec((1, tb, GROUP), lambda i, n: tok(i, n) + (0,)),
                    full(cw), full(alog), full(dtb), full(g), state_spec]
    else:
        args = (proj, proj, cw, alog, dtb, s0)
        in_specs = [seg('d_qkv', 3 * GROUP), seg('d_ab', LANES), full(cw), full(alog), full(dtb), state_spec]
    return pl.pallas_call(
        functools.partial(_gdn_kernel, reverse=reverse, finalize=finalize, row_len=row_len, dirn=dirn),
        grid=(b, nblk),
        in_specs=in_specs,
        out_specs=[pl.BlockSpec((1, tb, GROUP), lambda i, n: tok(i, n) + (0,)), state_spec],
        out_shape=[jax.ShapeDtypeStruct((b, t, GROUP), BF16 if finalize else F32),
                   jax.ShapeDtypeStruct((b, GDN_HEADS, GDN_DK, GDN_DV), F32)],
        scratch_shapes=[pltpu.VMEM((GDN_HEADS, GDN_DK, GDN_DV), F32)],
        compiler_params=pltpu.CompilerParams(
            dimension_semantics=("arbitrary", "arbitrary"), vmem_limit_bytes=VMEM_LIMIT),
        name="gdn_fwd" if finalize else "gdn_bwd",
    )(*args)


def _gla_kernel(*refs, reverse, finalize):
    if finalize:
        (q_ref, k_ref, v_ref, lo_ref, r_ref, ob_ref, up_ref, ab_ref, g_ref, s0_ref,
         o_ref, sfin_ref, s_scr) = refs
    else:
        q_ref, k_ref, v_ref, lo_ref, up_ref, ab_ref, s0_ref, o_ref, sfin_ref, s_scr = refs
    n = pl.program_id(1)

    @pl.when(n == 0)
    def _():
        s_scr[...] = s0_ref[0]

    tb = q_ref.shape[1]
    gate = _dot(lo_ref[0].astype(BF16), up_ref[...].astype(BF16)) + ab_ref[...]
    log_f = jax.nn.log_sigmoid(gate) / GLA_TAU
    incl, _, eye = _chunk_masks(reverse)
    tri = incl.astype(F32)
    last = 0 if reverse else CHUNK - 1
    nchunk = tb // CHUNK
    order = range(nchunk - 1, -1, -1) if reverse else range(nchunk)
    for c in order:
        sl = slice(c * CHUNK, (c + 1) * CHUNK)
        cum = _dot(tri, log_f[sl], HI)
        total = cum[last:last + 1]
        q_dec = q_ref[0, sl, :] * GLA_DK ** -0.5 * jnp.exp(cum)
        k = k_ref[0, sl, :]
        k_inv = (k * jnp.exp(-cum)).astype(BF16)
        k_end = (k * jnp.exp(total - cum)).astype(BF16)
        q_dec = q_dec.astype(BF16)
        dec_row = jnp.exp(total)
        for h in range(GLA_HEADS):
            kc = slice(h * GLA_DK, (h + 1) * GLA_DK)
            vc = slice(h * GLA_DV, (h + 1) * GLA_DV)
            vb = v_ref[0, sl, vc].astype(BF16)
            att = jnp.where(incl, _dot_nt(q_dec[:, kc], k_inv[:, kc]), 0.0)
            s = s_scr[h]
            o = _dot(att.astype(BF16), vb) + _dot(q_dec[:, kc], s.astype(BF16))
            dec_col = jnp.sum(eye * dec_row[:, kc], axis=1, keepdims=True)
            s_scr[h] = dec_col * s + _dot_tn(k_end[:, kc], vb)
            if finalize:
                o = o + ob_ref[0, sl, vc]
                o = o * lax.rsqrt(jnp.mean(o * o, axis=-1, keepdims=True) + EPS) * g_ref[...]
                o = o * _silu(r_ref[0, sl, vc])
            o_ref[0, sl, vc] = o.astype(o_ref.dtype)

    @pl.when(n == pl.num_programs(1) - 1)
    def _():
        sfin_ref[0] = s_scr[...]


def _gla_pass(proj, o_other, s0, lp, *, dirn):
    b, t, _ = proj.shape
    tb = min(TOKEN_BLOCK, t)
    assert t % tb == 0
    nblk = t // tb
    reverse = dirn == 1
    finalize = o_other is not None
    tok = (lambda i, n: (i, nblk - 1 - n)) if reverse else (lambda i, n: (i, n))
    seg = lambda name: pl.BlockSpec((1, tb, SEG[name][2]), lambda i, n: tok(i, n) + (_col_block(name),))
    full = lambda a: pl.BlockSpec(a.shape, lambda i, n: (0,) * a.ndim)
    hk = GLA_HEADS * GLA_DK
    up = jnp.pad(lp['gla_a_up'][dirn], ((0, LANES - GLA_LR), (0, 0)))
    ab = lp['gla_a_b'][dirn].reshape(1, hk)
    g = lp['gla_norm_g'].reshape(1, GLA_DV)
    state_spec = pl.BlockSpec((1, GLA_HEADS, GLA_DK, GLA_DV), lambda i, n: (i, 0, 0, 0))
    if finalize:
        args = (proj, proj, proj, proj, proj, o_other, up, ab, g, s0)
        in_specs = [seg('g_q'), seg('g_k'), seg('g_v'), seg('g_lo'), seg('g_r'),
                    pl.BlockSpec((1, tb, GROUP), lambda i, n: tok(i, n) + (0,)),
                    full(up), full(ab), full(g), state_spec]
    else:
        args = (proj, proj, proj, proj, up, ab, s0)
        in_specs = [seg('g_q'), seg('g_k'), seg('g_v'), seg('g_lo'), full(up), full(ab), state_spec]
    return pl.pallas_call(
        functools.partial(_gla_kernel, reverse=reverse, finalize=finalize),
        grid=(b, nblk),
        in_specs=in_specs,
        out_specs=[pl.BlockSpec((1, tb, GROUP), lambda i, n: tok(i, n) + (0,)), state_spec],
        out_shape=[jax.ShapeDtypeStruct((b, t, GROUP), BF16 if finalize else F32),
                   jax.ShapeDtypeStruct((b, GLA_HEADS, GLA_DK, GLA_DV), F32)],
        scratch_shapes=[pltpu.VMEM((GLA_HEADS, GLA_DK, GLA_DV), F32)],
        compiler_params=pltpu.CompilerParams(
            dimension_semantics=("arbitrary", "arbitrary"), vmem_limit_bytes=VMEM_LIMIT),
        name="gla_fwd" if finalize else "gla_bwd",
    )(*args)


def _two_direction_group(pass_fn, state_shape, proj_c, proj_x, **kw):
    zero = jnp.zeros((proj_x[0].shape[0],) + state_shape, F32)
    ob_c, sb_c = pass_fn(proj_c[0], None, zero, dirn=1, **proj_c[1], **kw)
    ob_x, _ = pass_fn(proj_x[0], None, sb_c, dirn=1, **proj_x[1], **kw)
    y_c, sf_c = pass_fn(proj_c[0], ob_c, zero, dirn=0, **proj_c[1], **kw)
    y_x, _ = pass_fn(proj_x[0], ob_x, sf_c, dirn=0, **proj_x[1], **kw)
    return y_c, y_x


def _gla_group_pallas(proj_c, proj_x, lp):
    return _two_direction_group(_gla_pass, (GLA_HEADS, GLA_DK, GLA_DV), (proj_c, {}), (proj_x, {}), lp=lp)


def _gdn_group_pallas(proj_c, proj_x, lp):
    return _two_direction_group(_gdn_pass, (GDN_HEADS, GDN_DK, GDN_DV),
                                (proj_c, dict(row_len=proj_c.shape[1])), (proj_x, dict(row_len=GRID_W)), lp=lp)


def _sc_kernel(b_ref, c_ref, h_ref, cw_ref, o_ref, *, row_len):
    o_ref[0] = (b_ref[0] * _conv3_block(c_ref[0] * h_ref[0], cw_ref[...], row_len)).astype(o_ref.dtype)


def _sc_group(proj, row_len, lp):
    b, t, _ = proj.shape
    tb = min(TOKEN_BLOCK, t)
    assert t % tb == 0 and tb % row_len == 0
    seg = lambda name: pl.BlockSpec((1, tb, GROUP), lambda i, n: (i, n, _col_block(name)))
    cw = lp['sc_conv']
    return pl.pallas_call(
        functools.partial(_sc_kernel, row_len=row_len),
        grid=(b, t // tb),
        in_specs=[seg('c_b'), seg('c_c'), seg('c_h'), pl.BlockSpec(cw.shape, lambda i, n: (0, 0))],
        out_specs=pl.BlockSpec((1, tb, GROUP), lambda i, n: (i, n, 0)),
        out_shape=jax.ShapeDtypeStruct((b, t, GROUP), BF16),
        compiler_params=pltpu.CompilerParams(
            dimension_semantics=("arbitrary", "arbitrary"), vmem_limit_bytes=VMEM_LIMIT),
        name="short_conv",
    )(proj, proj, proj, cw)


def _mixers(proj_c, proj_x, lp):
    tc = proj_c.shape[1]
    y_gla_c, y_gla_x = _gla_group_pallas(proj_c, proj_x, lp)
    y_gdn_c, y_gdn_x = _gdn_group_pallas(proj_c, proj_x, lp)
    y_sc_c = _sc_group(proj_c, tc, lp)
    y_sc_x = _sc_group(proj_x, GRID_W, lp)
    y_rw_c, y_rw_x = _rw_group(proj_c, proj_x, lp)
    return (y_gla_c, y_gdn_c, y_sc_c, y_rw_c), (y_gla_x, y_gdn_x, y_sc_x, y_rw_x)


def kernel(x, c, ctx, c_ctx, ada_w, ada_b, norm1_g, norm2_g, w_in, w_out, gla_a_up, gla_a_b, gla_norm_g, gdn_conv, gdn_a_log, gdn_dt_bias, gdn_norm_g, sc_conv, rw_mu_rkv, rw_mu_wa, rw_w0, rw_w2, rw_a0, rw_a2, rw_g2, rw_kk, rw_ka, rw_rk, rw_gn_w, rw_gn_b, ffn_w_gu, ffn_w_down, final_g):
    nb, t, d = x.shape
    tc = ctx.shape[1]
    depth = w_in.shape[0]
    assert d == D_MODEL and t % 512 == 0 and tc % CHUNK == 0 and nb + 1 <= SUBLANES

    cond = jnp.concatenate([c, c_ctx[None, :], jnp.zeros((SUBLANES - nb - 1, d), F32)], axis=0)
    mod = _modulation(cond, ada_w, ada_b)

    tm_x = 512
    tm_c = tc
    h_x, h_c = x, ctx
    for l in range(depth):
        lp = dict(gla_a_up=gla_a_up[l], gla_a_b=gla_a_b[l],
                  gla_norm_g=gla_norm_g[l], gdn_conv=gdn_conv[l], gdn_a_log=gdn_a_log[l],
                  gdn_dt_bias=gdn_dt_bias[l], gdn_norm_g=gdn_norm_g[l], sc_conv=sc_conv[l],
                  rw_mu_rkv=rw_mu_rkv[l], rw_mu_wa=rw_mu_wa[l], rw_w0=rw_w0[l], rw_w2=rw_w2[l],
                  rw_a0=rw_a0[l], rw_a2=rw_a2[l], rw_g2=rw_g2[l], rw_kk=rw_kk[l], rw_ka=rw_ka[l],
                  rw_rk=rw_rk[l], rw_gn_w=rw_gn_w[l], rw_gn_b=rw_gn_b[l])
        m_x = [mod[l, :nb, i * d:(i + 1) * d][:, None, :] for i in range(6)]
        m_c = [jnp.broadcast_to(mod[l, nb, i * d:(i + 1) * d][None, None, :], (nb, 1, d)) for i in range(6)]
        w_in_l = _pad_w_in(w_in[l])
        w_out_l = jnp.concatenate([w_out[l][:3 * GROUP], w_out[l][3 * GROUP:][_rw_channel_perm()]],
                                  axis=0).astype(BF16)
        w_gu_l = ffn_w_gu[l].astype(BF16)
        w_dn_l = ffn_w_down[l].astype(BF16)

        proj_c = _norm_proj(h_c, norm1_g[l], m_c[1], m_c[0], w_in_l, tm_c, 1024)
        proj_x = _norm_proj(h_x, norm1_g[l], m_x[1], m_x[0], w_in_l, 2 * tm_x, 1024)
        y_c, y_x = _mixers(proj_c, proj_x, lp)

        h_x = _groups_residual(y_x, w_out_l, h_x, m_x[2], 2 * tm_x, 1024)
        a_x = _norm_swiglu(h_x, norm2_g[l], m_x[4], m_x[3], w_gu_l, 2 * tm_x, 512)
        h_x = _proj_residual(a_x, w_dn_l, h_x, m_x[5], 2 * tm_x, 512)
        if l < depth - 1:
            h_c = _groups_residual(y_c, w_out_l, h_c, m_c[2], tm_c, 512)
            a_c = _norm_swiglu(h_c, norm2_g[l], m_c[4], m_c[3], w_gu_l, tm_c, 512)
            h_c = _proj_residual(a_c, w_dn_l, h_c, m_c[5], tm_c, 512)
    return _final_norm(h_x, final_g, tm_x)
```
